```python
import math
import jax, jax.numpy as jnp
from jax import lax
import numpy as np

D_MODEL = 2048
BATCH = 1
SEQ = 16384
DEPTH = 2

N_BRANCH = 4
BRANCH_W = 1024

POOL_W = BRANCH_W
POOL_WINDOWS = (2, 4, 8, 16)
POOL_GROUPS = len(POOL_WINDOWS)
POOL_GW = POOL_W // POOL_GROUPS

SGU_W = BRANCH_W
SGU_HEADS = 8
SGU_HD = SGU_W // SGU_HEADS
CHUNK = 128

N_Q_HEADS = 16
N_KV_HEADS = 2
Q_PER_KV = N_Q_HEADS // N_KV_HEADS
HEAD_DIM = 64
ATTN_W = N_Q_HEADS * HEAD_DIM
KV_W = N_KV_HEADS * HEAD_DIM
WINDOW = 128
BLOCK = 128
NUM_BUCKETS = 32
MAX_DISTANCE = 128

CONV_W = BRANCH_W
CONV_K = 31

ALPHA = (2 * DEPTH) ** 0.25
BETA = (8 * DEPTH) ** -0.25
LN_EPS = 1e-5

IN_SPLITS = (POOL_W, POOL_W,
             SGU_W, SGU_W, SGU_W,
             ATTN_W, KV_W, KV_W, ATTN_W,
             2 * CONV_W, CONV_W,
             N_BRANCH * D_MODEL)
D_IN = sum(IN_SPLITS)
SPLIT_POINTS = [int(p) for p in np.cumsum(IN_SPLITS)[:-1]]

kernel_name = "hybrid_pool_sgu_swa_conv_gated_deepnorm"


def layer_norm(x, g, b):
    xf = x.astype(jnp.float32)
    mu = jnp.mean(xf, axis=-1, keepdims=True)
    var = jnp.mean(jnp.square(xf - mu), axis=-1, keepdims=True)
    y = (xf - mu) * lax.rsqrt(var + LN_EPS) * g.astype(jnp.float32) + b.astype(jnp.float32)
    return y.astype(x.dtype)


def t5_bucket(n):
    max_exact = NUM_BUCKETS // 2
    nf = jnp.maximum(n, 1).astype(jnp.float32)
    large = max_exact + (jnp.log(nf / max_exact) / math.log(MAX_DISTANCE / max_exact)
                         * (NUM_BUCKETS - max_exact)).astype(jnp.int32)
    large = jnp.minimum(large, NUM_BUCKETS - 1)
    return jnp.where(n < max_exact, n, large)


def band_geometry():
    i = jnp.arange(BLOCK)[:, None]
    j = jnp.arange(2 * BLOCK)[None, :]
    return i + BLOCK - j


def relative_band_bias(rel_bias):
    d = jnp.clip(band_geometry(), 0, WINDOW - 1)
    bias = rel_bias[t5_bucket(d)]
    bias = jnp.transpose(bias, (2, 0, 1)).astype(jnp.float32)
    return bias.reshape(N_KV_HEADS, Q_PER_KV, BLOCK, 2 * BLOCK)


def pool_mixer(xa, w_grp, scale):
    b, s, _ = xa.shape
    xg = xa.reshape(b, s, POOL_GROUPS, POOL_GW)
    cs = jnp.cumsum(xg.astype(jnp.float32), axis=1)
    cs_pad = jnp.concatenate([jnp.zeros_like(cs[:, :1]), cs], axis=1)
    t = jnp.arange(s)[:, None]
    win = jnp.array(POOL_WINDOWS, dtype=jnp.int32)[None, :]
    lo = jnp.maximum(t + 1 - win, 0)
    cnt = jnp.minimum(t + 1, win).astype(jnp.float32)
    gidx = jnp.arange(POOL_GROUPS)[None, :]
    window_sum = cs - cs_pad[:, lo, gidx, :]
    pooled = (window_sum / cnt[None, :, :, None]).astype(xa.dtype)
    mix = pooled - xg
    y = jnp.einsum('bsgc,gcd->bsgd', mix, w_grp).reshape(b, s, POOL_W)
    return y * scale


def spatial_gating(u, v, ln_g, ln_b, w_s, b_s):
    b, s, _ = v.shape
    nc = s // CHUNK
    vn = layer_norm(v, ln_g, ln_b).reshape(b, nc, CHUNK, SGU_HEADS, SGU_HD)
    causal = jnp.tril(jnp.ones((CHUNK, CHUNK), dtype=bool))
    w = jnp.where(causal[None], w_s, jnp.zeros_like(w_s))
    sp = jnp.einsum('hts,bnshd->bnthd', w, vn) + jnp.transpose(b_s)[None, None, :, :, None]
    return u * sp.reshape(b, s, SGU_W)


def sliding_window_attention(q, k, v, sinks, band_bias):
    b, s, _ = q.shape
    nb = s // BLOCK
    qb = q.reshape(b, nb, BLOCK, N_KV_HEADS, Q_PER_KV, HEAD_DIM)
    kb = k.reshape(b, nb, BLOCK, N_KV_HEADS, HEAD_DIM)
    vb = v.reshape(b, nb, BLOCK, N_KV_HEADS, HEAD_DIM)

    def band(t):
        prev = jnp.concatenate([jnp.zeros_like(t[:, :1]), t[:, :-1]], axis=1)
        return jnp.concatenate([prev, t], axis=2)

    k_band, v_band = band(kb), band(vb)
    logits = jnp.einsum('bnqhgd,bnkhd->bnhgqk', qb, k_band).astype(jnp.float32)
    logits = logits * (HEAD_DIM ** -0.5) + band_bias[None, None]
    d = band_geometry()
    in_window = (d >= 0) & (d < WINDOW)
    has_prev = (jnp.arange(nb)[:, None] > 0) | (jnp.arange(2 * BLOCK)[None, :] >= BLOCK)
    mask = in_window[None] & has_prev[:, None, :]
    logits = jnp.where(mask[None, :, None, None], logits, jnp.float32(-1e30))
    sink = jnp.broadcast_to(sinks.astype(jnp.float32).reshape(1, 1, N_KV_HEADS, Q_PER_KV, 1, 1),
                            logits.shape[:-1] + (1,))
    probs = jax.nn.softmax(jnp.concatenate([logits, sink], axis=-1), axis=-1)[..., :-1]
    o = jnp.einsum('bnhgqk,bnkhd->bnqhgd', probs.astype(v.dtype), v_band)
    return o.reshape(b, s, ATTN_W)


def conformer_conv(d_in, conv_w, conv_b, ln_g, ln_b):
    val, gate = jnp.split(d_in, 2, axis=-1)
    glu = val * jax.nn.sigmoid(gate)
    y = lax.conv_general_dilated(glu, conv_w[:, None, :], window_strides=(1,),
                                 padding=[(CONV_K - 1, 0)],
                                 dimension_numbers=('NWC', 'WIO', 'NWC'),
                                 feature_group_count=CONV_W) + conv_b
    return jax.nn.silu(layer_norm(y, ln_g, ln_b))


def setup_inputs(seed: int = 0) -> dict:
    key = jax.random.key(seed)
    ks = jax.random.split(key, 20)
    nrm = lambda k, shape: jax.random.normal(k, shape, dtype=jnp.float32)
    return {
        "x": nrm(ks[0], (BATCH, SEQ, D_MODEL)),
        "w_in": nrm(ks[1], (DEPTH, D_MODEL, D_IN)) * D_MODEL ** -0.5,
        "pool_w": nrm(ks[2], (DEPTH, POOL_GROUPS, POOL_GW, POOL_GW)) * POOL_GW ** -0.5,
        "pool_scale": 1.0 + 0.1 * nrm(ks[3], (DEPTH, POOL_W)),
        "sgu_ln_g": 1.0 + 0.1 * nrm(ks[4], (DEPTH, SGU_W)),
        "sgu_ln_b": 0.1 * nrm(ks[5], (DEPTH, SGU_W)),
        "sgu_w": nrm(ks[6], (DEPTH, SGU_HEADS, CHUNK, CHUNK)) * 0.5 * CHUNK ** -0.5,
        "sgu_b": 1.0 + 0.1 * nrm(ks[7], (DEPTH, SGU_HEADS, CHUNK)),
        "attn_sinks": 0.5 * nrm(ks[8], (DEPTH, N_Q_HEADS)),
        "rel_bias": 0.5 * nrm(ks[9], (NUM_BUCKETS, N_Q_HEADS)),
        "conv_w": nrm(ks[10], (DEPTH, CONV_K, CONV_W)) * CONV_K ** -0.5,
        "conv_b": 0.02 * nrm(ks[11], (DEPTH, CONV_W)),
        "conv_ln_g": 1.0 + 0.1 * nrm(ks[12], (DEPTH, CONV_W)),
        "conv_ln_b": 0.1 * nrm(ks[13], (DEPTH, CONV_W)),
        "w_branch": nrm(ks[14], (DEPTH, N_BRANCH, BRANCH_W, D_MODEL)) * BRANCH_W ** -0.5 * BETA,
        "w_out": nrm(ks[15], (DEPTH, D_MODEL, D_MODEL)) * D_MODEL ** -0.5 * BETA,
        "ln_g": 1.0 + 0.1 * nrm(ks[16], (DEPTH, D_MODEL)),
        "ln_b": 0.1 * nrm(ks[17], (DEPTH, D_MODEL)),
    }


def reference(x, w_in, pool_w, pool_scale, sgu_ln_g, sgu_ln_b, sgu_w, sgu_b, attn_sinks,
              rel_bias, conv_w, conv_b, conv_ln_g, conv_ln_b, w_branch, w_out, ln_g, ln_b):
    b, s, _ = x.shape
    band_bias = relative_band_bias(rel_bias)
    for l in range(DEPTH):
        h = jnp.einsum('bsd,de->bse', x, w_in[l])
        (a_in, a_gate, u, v, b_gate, q, k, vv, c_gate,
         d_in, d_gate, g_logits) = jnp.split(h, SPLIT_POINTS, axis=-1)

        y_a = pool_mixer(a_in, pool_w[l], pool_scale[l]) * jax.nn.silu(a_gate)
        y_b = spatial_gating(u, v, sgu_ln_g[l], sgu_ln_b[l], sgu_w[l], sgu_b[l]) * jax.nn.silu(b_gate)
        y_c = sliding_window_attention(q, k, vv, attn_sinks[l], band_bias) * jax.nn.silu(c_gate)
        y_d = conformer_conv(d_in, conv_w[l], conv_b[l], conv_ln_g[l], conv_ln_b[l]) * jax.nn.silu(d_gate)

        gates = jax.nn.sigmoid(g_logits.reshape(b, s, N_BRANCH, D_MODEL))
        branches = (y_a, y_b, y_c, y_d)
        merged = gates[:, :, 0] * jnp.einsum('bsc,cd->bsd', branches[0], w_branch[l, 0])
        for i in range(1, N_BRANCH):
            merged = merged + gates[:, :, i] * jnp.einsum('bsc,cd->bsd', branches[i], w_branch[l, i])
        out = jnp.einsum('bsd,de->bse', merged, w_out[l])
        x = layer_norm(ALPHA * x + out, ln_g[l], ln_b[l])
    return x
```

```python
import functools
import math

import jax
import jax.numpy as jnp
from jax import lax
from jax.experimental import pallas as pl
from jax.experimental.pallas import tpu as pltpu

F32 = jnp.float32
BF16 = jnp.bfloat16

D_MODEL = 2048
DEPTH = 2
N_BRANCH = 4
BRANCH_W = 1024
POOL_WINDOWS = (2, 4, 8, 16)
POOL_GW = BRANCH_W // len(POOL_WINDOWS)
SGU_HEADS = 8
SGU_HD = BRANCH_W // SGU_HEADS
CHUNK = 128
N_Q_HEADS = 16
N_KV_HEADS = 2
Q_PER_KV = N_Q_HEADS // N_KV_HEADS
HEAD_DIM = 64
KV_W = N_KV_HEADS * HEAD_DIM
WINDOW = 128
BLOCK = 128
NUM_BUCKETS = 32
MAX_DISTANCE = 128
CONV_K = 31
ALPHA = (2 * DEPTH) ** 0.25
LN_EPS = 1e-5
NEG_INF = -1e30

OFF_A = 0
OFF_B = OFF_A + 2 * BRANCH_W
OFF_C = OFF_B + 3 * BRANCH_W
OFF_D = OFF_C + 2 * BRANCH_W + 2 * KV_W
OFF_G = OFF_D + 3 * BRANCH_W
D_IN = OFF_G + N_BRANCH * D_MODEL

LANES = 128
SUBLANES = 8
SEQ_TILE = 512
POOL_HALO = 16
CONV_HALO = 32
ROW_CHUNK = 64
CONV_ROWS = 32
CONV_COLS = 256
MERGE_COLS = 512
PAIRS = Q_PER_KV // 2
PQ = PAIRS * BLOCK
VMEM_LIMIT_BYTES = 56 * 1024 * 1024


def _sigmoid(x):
    return 1.0 / (1.0 + jnp.exp(-x))


def _silu(x):
    return x * _sigmoid(x)


def _layer_norm_rows(v, g, b):
    mu = jnp.mean(v, axis=-1, keepdims=True)
    d = v - mu
    var = jnp.mean(d * d, axis=-1, keepdims=True)
    return d * lax.rsqrt(var + LN_EPS) * g + b


def _dot(a, b):
    return jnp.dot(a, b, preferred_element_type=F32)


def _t5_bucket(n):
    max_exact = NUM_BUCKETS // 2
    nf = jnp.maximum(n, 1).astype(F32)
    large = max_exact + (jnp.log(nf / max_exact) / math.log(MAX_DISTANCE / max_exact)
                         * (NUM_BUCKETS - max_exact)).astype(jnp.int32)
    large = jnp.minimum(large, NUM_BUCKETS - 1)
    return jnp.where(n < max_exact, n, large)


def _band_bucket_ids():
    i = jnp.arange(BLOCK)[:, None]
    j = jnp.arange(2 * BLOCK)[None, :]
    return _t5_bucket(jnp.clip(i + BLOCK - j, 0, WINDOW - 1)).astype(jnp.int32)


def _bias_kernel(bucket_ref, table_ref, o_ref):
    h = pl.program_id(0)
    bucket = bucket_ref[...]
    acc = jnp.zeros(bucket.shape, F32)
    for b in range(NUM_BUCKETS):
        acc = jnp.where(bucket == b, table_ref[b, h], acc)
    o_ref[0] = acc


def _band_bias(rel_bias):
    return pl.pallas_call(
        _bias_kernel,
        grid=(N_Q_HEADS,),
        in_specs=[pl.BlockSpec((BLOCK, 2 * BLOCK), lambda h: (0, 0)),
                  pl.BlockSpec(memory_space=pltpu.SMEM)],
        out_specs=pl.BlockSpec((1, BLOCK, 2 * BLOCK), lambda h: (h, 0, 0)),
        out_shape=jax.ShapeDtypeStruct((N_Q_HEADS, BLOCK, 2 * BLOCK), F32),
        name="band_bias",
    )(_band_bucket_ids(), rel_bias.astype(F32))


def _pool_kernel(x_ref, w_ref, pw_ref, sc_ref, o_ref, abuf, gate_buf, mix_buf):
    i = pl.program_id(0)
    t_rows = x_ref.shape[0]

    @pl.when(i == 0)
    def _():
        abuf[0:POOL_HALO, :] = jnp.zeros((POOL_HALO, BRANCH_W), F32)

    xb = x_ref[...]
    abuf[POOL_HALO:POOL_HALO + t_rows, :] = _dot(xb, w_ref[:, 0:BRANCH_W])
    gate_buf[...] = _dot(xb, w_ref[:, BRANCH_W:2 * BRANCH_W])

    for g, win in enumerate(POOL_WINDOWS):
        c0 = g * POOL_GW
        for r0 in range(0, t_rows, ROW_CHUNK):
            base = POOL_HALO + r0
            cur = abuf[base:base + ROW_CHUNK, c0:c0 + POOL_GW]
            acc = cur
            for s in range(1, win):
                acc = acc + abuf[base - s:base - s + ROW_CHUNK, c0:c0 + POOL_GW]
            t = i * t_rows + r0 + lax.broadcasted_iota(jnp.int32, (ROW_CHUNK, POOL_GW), 0)
            cnt = jnp.minimum(t + 1, win).astype(F32)
            mix_buf[r0:r0 + ROW_CHUNK, c0:c0 + POOL_GW] = (acc / cnt - cur).astype(BF16)

    for g in range(len(POOL_WINDOWS)):
        c0 = g * POOL_GW
        y = _dot(mix_buf[:, c0:c0 + POOL_GW], pw_ref[g])
        y = y * sc_ref[:, c0:c0 + POOL_GW] * _silu(gate_buf[:, c0:c0 + POOL_GW])
        o_ref[:, c0:c0 + POOL_GW] = y.astype(BF16)

    abuf[0:POOL_HALO, :] = abuf[t_rows:t_rows + POOL_HALO, :]


def _sgu_kernel(x_ref, w_ref, lng_ref, lnb_ref, sw_ref, sb_ref, o_ref,
                u_buf, v_buf, gate_buf, vn_buf):
    t_rows = x_ref.shape[0]
    xb = x_ref[...]
    u_buf[...] = _dot(xb, w_ref[:, 0:BRANCH_W])
    v_buf[...] = _dot(xb, w_ref[:, BRANCH_W:2 * BRANCH_W])
    gate_buf[...] = _dot(xb, w_ref[:, 2 * BRANCH_W:3 * BRANCH_W])

    g = lng_ref[...]
    b = lnb_ref[...]
    for r0 in range(0, t_rows, ROW_CHUNK):
        vn = _layer_norm_rows(v_buf[r0:r0 + ROW_CHUNK, :], g, b)
        vn_buf[r0:r0 + ROW_CHUNK, :] = vn.astype(BF16)

    row = lax.broadcasted_iota(jnp.int32, (CHUNK, CHUNK), 0)
    col = lax.broadcasted_iota(jnp.int32, (CHUNK, CHUNK), 1)
    causal = row >= col
    for h in range(SGU_HEADS):
        c0 = h * SGU_HD
        w_h = jnp.where(causal, sw_ref[h], 0.0).astype(BF16)
        bias_h = sb_ref[:, c0:c0 + SGU_HD]
        for r0 in range(0, t_rows, CHUNK):
            sp = _dot(w_h, vn_buf[r0:r0 + CHUNK, c0:c0 + SGU_HD]) + bias_h
            y = u_buf[r0:r0 + CHUNK, c0:c0 + SGU_HD] * sp * _silu(gate_buf[r0:r0 + CHUNK, c0:c0 + SGU_HD])
            o_ref[r0:r0 + CHUNK, c0:c0 + SGU_HD] = y.astype(BF16)


def _attn_kernel(x_ref, w_ref, bias_ref, sink_ref, o_ref,
                 q_buf, gate_buf, k_tab, v_tab, code_buf, l_buf, p_buf):
    i = pl.program_id(0)
    t_rows = x_ref.shape[0]
    n_blocks = t_rows // BLOCK

    @pl.when(i == 0)
    def _():
        zero = jnp.zeros((BLOCK, LANES), BF16)
        for j in range(N_KV_HEADS):
            for half in range(2):
                k_tab[j, half, 0:BLOCK, :] = zero
                v_tab[j, half, 0:BLOCK, :] = zero
        row = lax.broadcasted_iota(jnp.int32, (PQ, 2 * 2 * BLOCK), 0)
        col = lax.broadcasted_iota(jnp.int32, (PQ, 2 * 2 * BLOCK), 1)
        key = col & (2 * BLOCK - 1)
        u = key - (row & (BLOCK - 1)) - 1
        in_window = (u >= 0) & (u < WINDOW)
        code_buf[...] = jnp.where(in_window, jnp.where(key >= BLOCK, 2, 1), 0)

    xb = x_ref[...]
    q_buf[...] = _dot(xb, w_ref[:, 0:BRANCH_W]).astype(BF16)
    gate_buf[...] = _dot(xb, w_ref[:, BRANCH_W + 2 * KV_W:2 * BRANCH_W + 2 * KV_W])

    lane = lax.broadcasted_iota(jnp.int32, (t_rows, LANES), 1)
    low = lane < HEAD_DIM
    kv = _dot(xb, w_ref[:, BRANCH_W:BRANCH_W + 2 * KV_W])
    for tab, c0 in ((k_tab, 0), (v_tab, KV_W)):
        val = kv[:, c0:c0 + KV_W]
        swapped = pltpu.roll(val, HEAD_DIM, axis=1)
        zero = jnp.zeros_like(val)
        tab[0, 0, BLOCK:BLOCK + t_rows, :] = jnp.where(low, val, zero).astype(BF16)
        tab[0, 1, BLOCK:BLOCK + t_rows, :] = jnp.where(low, zero, swapped).astype(BF16)
        tab[1, 0, BLOCK:BLOCK + t_rows, :] = jnp.where(low, swapped, zero).astype(BF16)
        tab[1, 1, BLOCK:BLOCK + t_rows, :] = jnp.where(low, zero, val).astype(BF16)

    scale = HEAD_DIM ** -0.5
    for n in range(n_blocks):
        r0 = n * BLOCK
        first_block = jnp.logical_and(i == 0, n == 0)
        threshold = jnp.where(first_block, 2, 1)
        for j in range(N_KV_HEADS):
            q_cols = j * PAIRS * LANES
            qs = jnp.concatenate(
                [q_buf[r0:r0 + BLOCK, q_cols + p * LANES:q_cols + (p + 1) * LANES] for p in range(PAIRS)],
                axis=0)
            k2 = jnp.concatenate([k_tab[j, 0, r0:r0 + 2 * BLOCK, :], k_tab[j, 1, r0:r0 + 2 * BLOCK, :]], axis=0)
            v2 = jnp.concatenate([v_tab[j, 0, r0:r0 + 2 * BLOCK, :], v_tab[j, 1, r0:r0 + 2 * BLOCK, :]], axis=0)
            l_buf[...] = lax.dot_general(qs, k2, (((1,), (1,)), ((), ())), preferred_element_type=F32)
            for p in range(PAIRS):
                rows = slice(p * BLOCK, (p + 1) * BLOCK)
                for par in range(2):
                    cols = slice(par * 2 * BLOCK, (par + 1) * 2 * BLOCK)
                    s = l_buf[rows, cols] * scale + bias_ref[j, rows, cols]
                    s = jnp.where(code_buf[rows, cols] >= threshold, s, NEG_INF)
                    sink = sink_ref[j, par, rows, :]
                    m = jnp.maximum(jnp.max(s, axis=1, keepdims=True), sink)
                    e = jnp.exp(s - m)
                    den = jnp.sum(e, axis=1, keepdims=True) + jnp.exp(sink - m)
                    p_buf[rows, cols] = (e * (1.0 / den)).astype(BF16)
            o = _dot(p_buf[...], v2)
            for p in range(PAIRS):
                cols = slice(q_cols + p * LANES, q_cols + (p + 1) * LANES)
                y = o[p * BLOCK:(p + 1) * BLOCK, :] * _silu(gate_buf[r0:r0 + BLOCK, cols])
                o_ref[r0:r0 + BLOCK, cols] = y.astype(BF16)

    for j in range(N_KV_HEADS):
        for half in range(2):
            k_tab[j, half, 0:BLOCK, :] = k_tab[j, half, t_rows:t_rows + BLOCK, :]
            v_tab[j, half, 0:BLOCK, :] = v_tab[j, half, t_rows:t_rows + BLOCK, :]


def _conv_kernel(x_ref, w_ref, cw_ref, cb_ref, lng_ref, lnb_ref, o_ref, glu_buf, gate_buf, y_buf):
    i = pl.program_id(0)
    t_rows = x_ref.shape[0]

    @pl.when(i == 0)
    def _():
        glu_buf[0:CONV_HALO, :] = jnp.zeros((CONV_HALO, BRANCH_W), F32)

    xb = x_ref[...]
    val = _dot(xb, w_ref[:, 0:BRANCH_W])
    glu_gate = _dot(xb, w_ref[:, BRANCH_W:2 * BRANCH_W])
    glu_buf[CONV_HALO:CONV_HALO + t_rows, :] = val * _sigmoid(glu_gate)
    gate_buf[...] = _dot(xb, w_ref[:, 2 * BRANCH_W:3 * BRANCH_W])

    first = CONV_HALO - (CONV_K - 1)
    for c0 in range(0, BRANCH_W, CONV_COLS):
        bias = cb_ref[:, c0:c0 + CONV_COLS]
        for r0 in range(0, t_rows, CONV_ROWS):
            acc = jnp.broadcast_to(bias, (CONV_ROWS, CONV_COLS))
            for k in range(CONV_K):
                wk = pltpu.repeat(cw_ref[k, :, c0:c0 + CONV_COLS], CONV_ROWS // SUBLANES, axis=0)
                acc = acc + wk * glu_buf[first + r0 + k:first + r0 + k + CONV_ROWS, c0:c0 + CONV_COLS]
            y_buf[r0:r0 + CONV_ROWS, c0:c0 + CONV_COLS] = acc

    g = lng_ref[...]
    b = lnb_ref[...]
    for r0 in range(0, t_rows, ROW_CHUNK):
        y = _silu(_layer_norm_rows(y_buf[r0:r0 + ROW_CHUNK, :], g, b))
        o_ref[r0:r0 + ROW_CHUNK, :] = (y * _silu(gate_buf[r0:r0 + ROW_CHUNK, :])).astype(BF16)

    glu_buf[0:CONV_HALO, :] = glu_buf[t_rows:t_rows + CONV_HALO, :]


def _merge_kernel(x_ref, ya_ref, yb_ref, yc_ref, yd_ref, wg_ref, wb_ref, o_ref):
    xb = x_ref[...]
    acc = None
    for br, y_ref in enumerate((ya_ref, yb_ref, yc_ref, yd_ref)):
        gate = _sigmoid(_dot(xb, wg_ref[0, :, br * MERGE_COLS:(br + 1) * MERGE_COLS]))
        term = gate * _dot(y_ref[...], wb_ref[0, br])
        acc = term if acc is None else acc + term
    o_ref[...] = acc.astype(BF16)


def _out_kernel(m_ref, w_ref, x_ref, g_ref, b_ref, o_ref, ob_ref, z_buf):
    t_rows = x_ref.shape[0]
    z_buf[...] = _dot(m_ref[...], w_ref[...])
    g = g_ref[...]
    b = b_ref[...]
    for r0 in range(0, t_rows, ROW_CHUNK // 2):
        rows = slice(r0, r0 + ROW_CHUNK // 2)
        y = _layer_norm_rows(ALPHA * x_ref[rows, :] + z_buf[rows, :], g, b)
        o_ref[rows, :] = y
        ob_ref[rows, :] = y.astype(BF16)


def _resident(shape):
    zeros = (0,) * len(shape)
    return pl.BlockSpec(shape, lambda *_: zeros, pipeline_mode=pl.Buffered(1))


def _params():
    return pltpu.CompilerParams(dimension_semantics=("arbitrary",), vmem_limit_bytes=VMEM_LIMIT_BYTES)


def _row_tile(width):
    return pl.BlockSpec((SEQ_TILE, width), lambda i: (i, 0))


def _branch_call(kernel_fn, name, seq, x_bf, operands, scratch):
    specs = [_row_tile(D_MODEL)] + [_resident(op.shape) for op in operands]
    return pl.pallas_call(
        kernel_fn,
        grid=(seq // SEQ_TILE,),
        in_specs=specs,
        out_specs=_row_tile(BRANCH_W),
        out_shape=jax.ShapeDtypeStruct((seq, BRANCH_W), BF16),
        scratch_shapes=scratch,
        compiler_params=_params(),
        name=name,
    )(x_bf, *operands)


def _layer(x, x_bf, band, w_in, pool_w, pool_scale, sgu_ln_g, sgu_ln_b, sgu_w, sgu_b, sinks,
           conv_w, conv_b, conv_ln_g, conv_ln_b, w_branch, w_out, ln_g, ln_b):
    seq = x.shape[0]
    t = SEQ_TILE
    w16 = w_in.astype(BF16)
    row = lambda v: v.reshape(1, -1).astype(F32)

    y_a = _branch_call(
        _pool_kernel, "branch_pool", seq, x_bf,
        [w16[:, OFF_A:OFF_B], pool_w.astype(BF16), row(pool_scale)],
        [pltpu.VMEM((t + POOL_HALO, BRANCH_W), F32), pltpu.VMEM((t, BRANCH_W), F32),
         pltpu.VMEM((t, BRANCH_W), BF16)])

    sgu_bias = jnp.repeat(jnp.transpose(sgu_b).astype(F32), SGU_HD, axis=1)
    y_b = _branch_call(
        _sgu_kernel, "branch_sgu", seq, x_bf,
        [w16[:, OFF_B:OFF_C], row(sgu_ln_g), row(sgu_ln_b), sgu_w.astype(F32), sgu_bias],
        [pltpu.VMEM((t, BRANCH_W), F32), pltpu.VMEM((t, BRANCH_W), F32),
         pltpu.VMEM((t, BRANCH_W), F32), pltpu.VMEM((t, BRANCH_W), BF16)])

    sink_rows = jnp.transpose(sinks.astype(F32).reshape(N_KV_HEADS, PAIRS, 2), (0, 2, 1))
    sink_rows = jnp.repeat(sink_rows, BLOCK, axis=2)[..., None]
    y_c = _branch_call(
        _attn_kernel, "branch_attn", seq, x_bf,
        [w16[:, OFF_C:OFF_D], band, sink_rows],
        [pltpu.VMEM((t, BRANCH_W), BF16), pltpu.VMEM((t, BRANCH_W), F32),
         pltpu.VMEM((N_KV_HEADS, 2, t + BLOCK, LANES), BF16),
         pltpu.VMEM((N_KV_HEADS, 2, t + BLOCK, LANES), BF16),
         pltpu.VMEM((PQ, 4 * BLOCK), jnp.int32), pltpu.VMEM((PQ, 4 * BLOCK), F32),
         pltpu.VMEM((PQ, 4 * BLOCK), BF16)])

    conv_w8 = jnp.broadcast_to(conv_w.astype(F32)[:, None, :], (CONV_K, SUBLANES, BRANCH_W))
    y_d = _branch_call(
        _conv_kernel, "branch_conv", seq, x_bf,
        [w16[:, OFF_D:OFF_G], conv_w8, row(conv_b), row(conv_ln_g), row(conv_ln_b)],
        [pltpu.VMEM((t + CONV_HALO, BRANCH_W), F32), pltpu.VMEM((t, BRANCH_W), F32),
         pltpu.VMEM((t, BRANCH_W), F32)])

    n_cb = D_MODEL // MERGE_COLS
    wg = w16[:, OFF_G:].reshape(D_MODEL, N_BRANCH, n_cb, MERGE_COLS)
    wg = jnp.transpose(wg, (2, 0, 1, 3)).reshape(n_cb, D_MODEL, N_BRANCH * MERGE_COLS)
    wb = w_branch.astype(BF16).reshape(N_BRANCH, BRANCH_W, n_cb, MERGE_COLS)
    wb = jnp.transpose(wb, (2, 0, 1, 3))
    y_spec = pl.BlockSpec((t, BRANCH_W), lambda c, i: (i, 0))
    merged = pl.pallas_call(
        _merge_kernel,
        grid=(n_cb, seq // t),
        in_specs=[pl.BlockSpec((t, D_MODEL), lambda c, i: (i, 0)), y_spec, y_spec, y_spec, y_spec,
                  pl.BlockSpec((1, D_MODEL, N_BRANCH * MERGE_COLS), lambda c, i: (c, 0, 0)),
                  pl.BlockSpec((1, N_BRANCH, BRANCH_W, MERGE_COLS), lambda c, i: (c, 0, 0, 0))],
        out_specs=pl.BlockSpec((t, MERGE_COLS), lambda c, i: (i, c)),
        out_shape=jax.ShapeDtypeStruct((seq, D_MODEL), BF16),
        compiler_params=pltpu.CompilerParams(dimension_semantics=("arbitrary", "arbitrary"),
                                             vmem_limit_bytes=VMEM_LIMIT_BYTES),
        name="merge",
    )(x_bf, y_a, y_b, y_c, y_d, wg, wb)

    x_next, x_next_bf = pl.pallas_call(
        _out_kernel,
        grid=(seq // t,),
        in_specs=[_row_tile(D_MODEL), _resident((D_MODEL, D_MODEL)), _row_tile(D_MODEL),
                  _resident((1, D_MODEL)), _resident((1, D_MODEL))],
        out_specs=[_row_tile(D_MODEL), _row_tile(D_MODEL)],
        out_shape=[jax.ShapeDtypeStruct((seq, D_MODEL), F32), jax.ShapeDtypeStruct((seq, D_MODEL), BF16)],
        scratch_shapes=[pltpu.VMEM((t, D_MODEL), F32)],
        compiler_params=_params(),
        name="out_norm",
    )(merged, w_out.astype(BF16), x, row(ln_g), row(ln_b))
    return x_next, x_next_bf


def kernel(x, w_in, pool_w, pool_scale, sgu_ln_g, sgu_ln_b, sgu_w, sgu_b, attn_sinks, rel_bias, conv_w,
           conv_b, conv_ln_g, conv_ln_b, w_branch, w_out, ln_g, ln_b):
    batch, seq, d_model = x.shape
    assert d_model == D_MODEL and w_in.shape == (DEPTH, D_MODEL, D_IN)
    assert seq % SEQ_TILE == 0 and SEQ_TILE % BLOCK == 0

    band = _band_bias(rel_bias).reshape(N_KV_HEADS, PAIRS, 2, BLOCK, 2 * BLOCK)
    band = jnp.transpose(band, (0, 1, 3, 2, 4)).reshape(N_KV_HEADS, PQ, 4 * BLOCK)

    outs = []
    for bi in range(batch):
        xc = x[bi].astype(F32)
        xc_bf = xc.astype(BF16)
        for l in range(DEPTH):
            xc, xc_bf = _layer(xc, xc_bf, band, w_in[l], pool_w[l], pool_scale[l], sgu_ln_g[l], sgu_ln_b[l],
                               sgu_w[l], sgu_b[l], attn_sinks[l], conv_w[l], conv_b[l], conv_ln_g[l],
                               conv_ln_b[l], w_branch[l], w_out[l], ln_g[l], ln_b[l])
        outs.append(xc)
    return jnp.stack(outs).astype(x.dtype)
```

```python
import functools
import math

import jax
import jax.numpy as jnp
from jax import lax
from jax.experimental import pallas as pl
from jax.experimental.pallas import tpu as pltpu

F32 = jnp.float32
BF16 = jnp.bfloat16

D_MODEL = 2048
DEPTH = 2
N_BRANCH = 4
BRANCH_W = 1024
POOL_WINDOWS = (2, 4, 8, 16)
POOL_GW = BRANCH_W // len(POOL_WINDOWS)
SGU_HEADS = 8
SGU_HD = BRANCH_W // SGU_HEADS
CHUNK = 128
N_Q_HEADS = 16
N_KV_HEADS = 2
Q_PER_KV = N_Q_HEADS // N_KV_HEADS
HEAD_DIM = 64
KV_W = N_KV_HEADS * HEAD_DIM
WINDOW = 128
BLOCK = 128
NUM_BUCKETS = 32
MAX_DISTANCE = 128
CONV_K = 31
ALPHA = (2 * DEPTH) ** 0.25
LN_EPS = 1e-5
NEG_INF = -1e30

OFF_A = 0
OFF_B = OFF_A + 2 * BRANCH_W
OFF_C = OFF_B + 3 * BRANCH_W
OFF_D = OFF_C + 2 * BRANCH_W + 2 * KV_W
OFF_G = OFF_D + 3 * BRANCH_W
D_IN = OFF_G + N_BRANCH * D_MODEL

LANES = 128
SUBLANES = 8
SEQ_TILE = 512
POOL_HALO = 16
CONV_HALO = 32
ROW_CHUNK = 64
CONV_ROWS = 32
CONV_COLS = 256
MERGE_COLS = 512
PAIRS = Q_PER_KV // 2
PQ = PAIRS * BLOCK
ATTN_GROUP_BLOCKS = 2
VMEM_LIMIT_BYTES = 56 * 1024 * 1024


def _sigmoid(x):
    return 0.5 * jnp.tanh(0.5 * x) + 0.5


def _silu(x):
    return x * _sigmoid(x)


def _layer_norm_rows(v, g, b):
    mu = jnp.mean(v, axis=-1, keepdims=True)
    d = v - mu
    var = jnp.mean(d * d, axis=-1, keepdims=True)
    return d * lax.rsqrt(var + LN_EPS) * g + b


def _dot(a, b):
    return jnp.dot(a, b, preferred_element_type=F32)


def _t5_bucket(n):
    max_exact = NUM_BUCKETS // 2
    nf = jnp.maximum(n, 1).astype(F32)
    large = max_exact + (jnp.log(nf / max_exact) / math.log(MAX_DISTANCE / max_exact)
                         * (NUM_BUCKETS - max_exact)).astype(jnp.int32)
    large = jnp.minimum(large, NUM_BUCKETS - 1)
    return jnp.where(n < max_exact, n, large)


def _band_bucket_ids():
    i = jnp.arange(BLOCK)[:, None]
    j = jnp.arange(2 * BLOCK)[None, :]
    return _t5_bucket(jnp.clip(i + BLOCK - j, 0, WINDOW - 1)).astype(jnp.int32)


def _bias_kernel(bucket_ref, table_ref, o_ref):
    h = pl.program_id(0)
    bucket = bucket_ref[...]
    acc = jnp.zeros(bucket.shape, F32)
    for b in range(NUM_BUCKETS):
        acc = jnp.where(bucket == b, table_ref[b, h], acc)
    o_ref[0] = acc


def _band_bias(rel_bias):
    return pl.pallas_call(
        _bias_kernel,
        grid=(N_Q_HEADS,),
        in_specs=[pl.BlockSpec((BLOCK, 2 * BLOCK), lambda h: (0, 0)),
                  pl.BlockSpec(memory_space=pltpu.SMEM)],
        out_specs=pl.BlockSpec((1, BLOCK, 2 * BLOCK), lambda h: (h, 0, 0)),
        out_shape=jax.ShapeDtypeStruct((N_Q_HEADS, BLOCK, 2 * BLOCK), F32),
        name="band_bias",
    )(_band_bucket_ids(), rel_bias.astype(F32))


def _pool_kernel(x_ref, w_ref, pw_ref, sc_ref, o_ref, abuf, gate_buf, mix_buf):
    i = pl.program_id(0)
    t_rows = x_ref.shape[0]

    @pl.when(i == 0)
    def _():
        abuf[0:POOL_HALO, :] = jnp.zeros((POOL_HALO, BRANCH_W), F32)

    xb = x_ref[...]
    abuf[POOL_HALO:POOL_HALO + t_rows, :] = _dot(xb, w_ref[:, 0:BRANCH_W])
    gate_buf[...] = _dot(xb, w_ref[:, BRANCH_W:2 * BRANCH_W])

    for g, win in enumerate(POOL_WINDOWS):
        c0 = g * POOL_GW
        for r0 in range(0, t_rows, ROW_CHUNK):
            base = POOL_HALO + r0
            cur = abuf[base:base + ROW_CHUNK, c0:c0 + POOL_GW]
            acc = cur
            for s in range(1, win):
                acc = acc + abuf[base - s:base - s + ROW_CHUNK, c0:c0 + POOL_GW]
            t = i * t_rows + r0 + lax.broadcasted_iota(jnp.int32, (ROW_CHUNK, POOL_GW), 0)
            cnt = jnp.minimum(t + 1, win).astype(F32)
            mix_buf[r0:r0 + ROW_CHUNK, c0:c0 + POOL_GW] = (acc / cnt - cur).astype(BF16)

    for g in range(len(POOL_WINDOWS)):
        c0 = g * POOL_GW
        y = _dot(mix_buf[:, c0:c0 + POOL_GW], pw_ref[g])
        y = y * sc_ref[:, c0:c0 + POOL_GW] * _silu(gate_buf[:, c0:c0 + POOL_GW])
        o_ref[:, c0:c0 + POOL_GW] = y.astype(BF16)

    abuf[0:POOL_HALO, :] = abuf[t_rows:t_rows + POOL_HALO, :]


def _sgu_kernel(x_ref, w_ref, lng_ref, lnb_ref, sw_ref, sb_ref, o_ref,
                u_buf, v_buf, gate_buf, vn_buf):
    t_rows = x_ref.shape[0]
    xb = x_ref[...]
    u_buf[...] = _dot(xb, w_ref[:, 0:BRANCH_W])
    v_buf[...] = _dot(xb, w_ref[:, BRANCH_W:2 * BRANCH_W])
    gate_buf[...] = _dot(xb, w_ref[:, 2 * BRANCH_W:3 * BRANCH_W])

    g = lng_ref[...]
    b = lnb_ref[...]
    for r0 in range(0, t_rows, ROW_CHUNK):
        vn = _layer_norm_rows(v_buf[r0:r0 + ROW_CHUNK, :], g, b)
        vn_buf[r0:r0 + ROW_CHUNK, :] = vn.astype(BF16)

    row = lax.broadcasted_iota(jnp.int32, (CHUNK, CHUNK), 0)
    col = lax.broadcasted_iota(jnp.int32, (CHUNK, CHUNK), 1)
    causal = row >= col
    for h in range(SGU_HEADS):
        c0 = h * SGU_HD
        w_h = jnp.where(causal, sw_ref[h], 0.0).astype(BF16)
        bias_h = sb_ref[:, c0:c0 + SGU_HD]
        for r0 in range(0, t_rows, CHUNK):
            sp = _dot(w_h, vn_buf[r0:r0 + CHUNK, c0:c0 + SGU_HD]) + bias_h
            y = u_buf[r0:r0 + CHUNK, c0:c0 + SGU_HD] * sp * _silu(gate_buf[r0:r0 + CHUNK, c0:c0 + SGU_HD])
            o_ref[r0:r0 + CHUNK, c0:c0 + SGU_HD] = y.astype(BF16)


def _attn_kernel(x_ref, w_ref, bias_ref, sink_ref, sinkc_ref, o_ref,
                 q_buf, gate_buf, k_tab, v_tab, code_buf, ones_buf, s_buf, p_buf, m_buf, mc_buf):
    i = pl.program_id(0)
    t_rows = x_ref.shape[0]
    n_blocks = t_rows // BLOCK

    @pl.when(i == 0)
    def _():
        zero = jnp.zeros((BLOCK, LANES), BF16)
        for j in range(N_KV_HEADS):
            for half in range(2):
                k_tab[j, half, 0:BLOCK, :] = zero
                v_tab[j, half, 0:BLOCK, :] = zero
        row = lax.broadcasted_iota(jnp.int32, (PQ, 2 * 2 * BLOCK), 0)
        col = lax.broadcasted_iota(jnp.int32, (PQ, 2 * 2 * BLOCK), 1)
        key = col & (2 * BLOCK - 1)
        u = key - (row & (BLOCK - 1)) - 1
        in_window = (u >= 0) & (u < WINDOW)
        code_buf[...] = jnp.where(in_window, jnp.where(key >= BLOCK, 2, 1), 0)
        krow = lax.broadcasted_iota(jnp.int32, (4 * BLOCK, LANES), 0)
        klane = lax.broadcasted_iota(jnp.int32, (4 * BLOCK, LANES), 1)
        ones_buf[...] = jnp.where((krow >= 2 * BLOCK) == (klane >= HEAD_DIM), 1.0, 0.0).astype(BF16)

    xb = x_ref[...]
    q_buf[...] = (_dot(xb, w_ref[:, 0:BRANCH_W]) * (HEAD_DIM ** -0.5)).astype(BF16)
    gate_buf[...] = _dot(xb, w_ref[:, BRANCH_W + 2 * KV_W:2 * BRANCH_W + 2 * KV_W])

    lane = lax.broadcasted_iota(jnp.int32, (t_rows, LANES), 1)
    low = lane < HEAD_DIM
    kv = _dot(xb, w_ref[:, BRANCH_W:BRANCH_W + 2 * KV_W])
    for tab, c0 in ((k_tab, 0), (v_tab, KV_W)):
        val = kv[:, c0:c0 + KV_W]
        swapped = pltpu.roll(val, HEAD_DIM, axis=1)
        zero = jnp.zeros_like(val)
        tab[0, 0, BLOCK:BLOCK + t_rows, :] = jnp.where(low, val, zero).astype(BF16)
        tab[0, 1, BLOCK:BLOCK + t_rows, :] = jnp.where(low, zero, swapped).astype(BF16)
        tab[1, 0, BLOCK:BLOCK + t_rows, :] = jnp.where(low, swapped, zero).astype(BF16)
        tab[1, 1, BLOCK:BLOCK + t_rows, :] = jnp.where(low, zero, val).astype(BF16)

    low128 = lax.broadcasted_iota(jnp.int32, (BLOCK, LANES), 1) < HEAD_DIM
    for g0 in range(0, n_blocks, ATTN_GROUP_BLOCKS):
        units = [(n, j) for n in range(g0, g0 + ATTN_GROUP_BLOCKS) for j in range(N_KV_HEADS)]

        for it, (n, j) in enumerate(units):
            r0 = n * BLOCK
            q_cols = j * PAIRS * LANES
            qs = jnp.concatenate(
                [q_buf[r0:r0 + BLOCK, q_cols + p * LANES:q_cols + (p + 1) * LANES] for p in range(PAIRS)], axis=0)
            k2 = jnp.concatenate([k_tab[j, 0, r0:r0 + 2 * BLOCK, :], k_tab[j, 1, r0:r0 + 2 * BLOCK, :]], axis=0)
            logits = lax.dot_general(qs, k2, (((1,), (1,)), ((), ())), preferred_element_type=F32)
            threshold = jnp.where(jnp.logical_and(i == 0, n == 0), 2, 1)
            for p in range(PAIRS):
                rows = slice(p * BLOCK, (p + 1) * BLOCK)
                row_max = []
                for par in range(2):
                    slabs = []
                    for half in range(2):
                        cols = slice((2 * par + half) * LANES, (2 * par + half + 1) * LANES)
                        s = logits[rows, cols] + bias_ref[j, rows, cols]
                        s = jnp.where(code_buf[rows, cols] >= threshold, s, NEG_INF)
                        s_buf[it, rows, cols] = s
                        slabs.append(s)
                    m = jnp.max(jnp.maximum(slabs[0], slabs[1]), axis=1, keepdims=True)
                    m = jnp.maximum(jnp.broadcast_to(m, (BLOCK, LANES)), sink_ref[j, par, rows, :])
                    m_buf[it, par, rows, :] = m
                    row_max.append(m)
                mc_buf[it, rows, :] = jnp.where(low128, row_max[0], row_max[1])

        for it, (n, j) in enumerate(units):
            for p in range(PAIRS):
                rows = slice(p * BLOCK, (p + 1) * BLOCK)
                for par in range(2):
                    m = m_buf[it, par, rows, :]
                    for half in range(2):
                        cols = slice((2 * par + half) * LANES, (2 * par + half + 1) * LANES)
                        p_buf[it, rows, cols] = jnp.exp(s_buf[it, rows, cols] - m).astype(BF16)

        for it, (n, j) in enumerate(units):
            r0 = n * BLOCK
            q_cols = j * PAIRS * LANES
            v2 = jnp.concatenate([v_tab[j, 0, r0:r0 + 2 * BLOCK, :], v_tab[j, 1, r0:r0 + 2 * BLOCK, :]], axis=0)
            o = _dot(p_buf[it], jnp.concatenate([v2, ones_buf[...]], axis=1))
            for p in range(PAIRS):
                rows = slice(p * BLOCK, (p + 1) * BLOCK)
                cols = slice(q_cols + p * LANES, q_cols + (p + 1) * LANES)
                den = o[rows, LANES:2 * LANES] + jnp.exp(sinkc_ref[j, rows, :] - mc_buf[it, rows, :])
                y = o[rows, 0:LANES] * (1.0 / den) * _silu(gate_buf[r0:r0 + BLOCK, cols])
                o_ref[r0:r0 + BLOCK, cols] = y.astype(BF16)

    for j in range(N_KV_HEADS):
        for half in range(2):
            k_tab[j, half, 0:BLOCK, :] = k_tab[j, half, t_rows:t_rows + BLOCK, :]
            v_tab[j, half, 0:BLOCK, :] = v_tab[j, half, t_rows:t_rows + BLOCK, :]


def _conv_kernel(x_ref, w_ref, cw_ref, cb_ref, lng_ref, lnb_ref, o_ref, glu_buf, gate_buf, y_buf):
    i = pl.program_id(0)
    t_rows = x_ref.shape[0]

    @pl.when(i == 0)
    def _():
        glu_buf[0:CONV_HALO, :] = jnp.zeros((CONV_HALO, BRANCH_W), F32)

    xb = x_ref[...]
    val = _dot(xb, w_ref[:, 0:BRANCH_W])
    glu_gate = _dot(xb, w_ref[:, BRANCH_W:2 * BRANCH_W])
    glu_buf[CONV_HALO:CONV_HALO + t_rows, :] = val * _sigmoid(glu_gate)
    gate_buf[...] = _dot(xb, w_ref[:, 2 * BRANCH_W:3 * BRANCH_W])

    first = CONV_HALO - (CONV_K - 1)
    for c0 in range(0, BRANCH_W, CONV_COLS):
        bias = cb_ref[:, c0:c0 + CONV_COLS]
        for r0 in range(0, t_rows, CONV_ROWS):
            acc = jnp.broadcast_to(bias, (CONV_ROWS, CONV_COLS))
            for k in range(CONV_K):
                wk = pltpu.repeat(cw_ref[k, :, c0:c0 + CONV_COLS], CONV_ROWS // SUBLANES, axis=0)
                acc = acc + wk * glu_buf[first + r0 + k:first + r0 + k + CONV_ROWS, c0:c0 + CONV_COLS]
            y_buf[r0:r0 + CONV_ROWS, c0:c0 + CONV_COLS] = acc

    g = lng_ref[...]
    b = lnb_ref[...]
    for r0 in range(0, t_rows, ROW_CHUNK):
        y = _silu(_layer_norm_rows(y_buf[r0:r0 + ROW_CHUNK, :], g, b))
        o_ref[r0:r0 + ROW_CHUNK, :] = (y * _silu(gate_buf[r0:r0 + ROW_CHUNK, :])).astype(BF16)

    glu_buf[0:CONV_HALO, :] = glu_buf[t_rows:t_rows + CONV_HALO, :]


def _merge_kernel(x_ref, ya_ref, yb_ref, yc_ref, yd_ref, wga_ref, wgb_ref, wgc_ref, wgd_ref, wb_ref, o_ref):
    xb = x_ref[...]
    acc = None
    branches = ((ya_ref, wga_ref), (yb_ref, wgb_ref), (yc_ref, wgc_ref), (yd_ref, wgd_ref))
    for br, (y_ref, wg_ref) in enumerate(branches):
        gate = _sigmoid(_dot(xb, wg_ref[...]))
        term = gate * _dot(y_ref[...], wb_ref[br])
        acc = term if acc is None else acc + term
    o_ref[...] = acc.astype(BF16)


def _out_kernel(m_ref, w_ref, x_ref, g_ref, b_ref, o_ref, ob_ref, z_buf):
    t_rows = x_ref.shape[0]
    z_buf[...] = _dot(m_ref[...], w_ref[...])
    g = g_ref[...]
    b = b_ref[...]
    for r0 in range(0, t_rows, ROW_CHUNK // 2):
        rows = slice(r0, r0 + ROW_CHUNK // 2)
        y = _layer_norm_rows(ALPHA * x_ref[rows, :] + z_buf[rows, :], g, b)
        o_ref[rows, :] = y
        ob_ref[rows, :] = y.astype(BF16)


def _resident(shape):
    zeros = (0,) * len(shape)
    return pl.BlockSpec(shape, lambda *_: zeros, pipeline_mode=pl.Buffered(1))


def _params():
    return pltpu.CompilerParams(dimension_semantics=("arbitrary",), vmem_limit_bytes=VMEM_LIMIT_BYTES)


def _row_tile(width):
    return pl.BlockSpec((SEQ_TILE, width), lambda i: (i, 0))


def _branch_call(kernel_fn, name, seq, x_bf, operands, scratch):
    specs = [_row_tile(D_MODEL)] + [_resident(op.shape) for op in operands]
    return pl.pallas_call(
        kernel_fn,
        grid=(seq // SEQ_TILE,),
        in_specs=specs,
        out_specs=_row_tile(BRANCH_W),
        out_shape=jax.ShapeDtypeStruct((seq, BRANCH_W), BF16),
        scratch_shapes=scratch,
        compiler_params=_params(),
        name=name,
    )(x_bf, *operands)


def _layer(x, x_bf, band, w_in, pool_w, pool_scale, sgu_ln_g, sgu_ln_b, sgu_w, sgu_b, sinks,
           conv_w, conv_b, conv_ln_g, conv_ln_b, w_branch, w_out, ln_g, ln_b):
    seq = x.shape[0]
    t = SEQ_TILE
    w16 = lambda a, b: w_in[:, a:b].astype(BF16)
    row = lambda v: v.reshape(1, -1).astype(F32)

    y_a = _branch_call(
        _pool_kernel, "branch_pool", seq, x_bf,
        [w16(OFF_A, OFF_B), pool_w.astype(BF16), row(pool_scale)],
        [pltpu.VMEM((t + POOL_HALO, BRANCH_W), F32), pltpu.VMEM((t, BRANCH_W), F32),
         pltpu.VMEM((t, BRANCH_W), BF16)])

    sgu_bias = jnp.repeat(jnp.transpose(sgu_b).astype(F32), SGU_HD, axis=1)
    y_b = _branch_call(
        _sgu_kernel, "branch_sgu", seq, x_bf,
        [w16(OFF_B, OFF_C), row(sgu_ln_g), row(sgu_ln_b), sgu_w.astype(F32), sgu_bias],
        [pltpu.VMEM((t, BRANCH_W), F32), pltpu.VMEM((t, BRANCH_W), F32),
         pltpu.VMEM((t, BRANCH_W), F32), pltpu.VMEM((t, BRANCH_W), BF16)])

    sink_hp = sinks.astype(F32).reshape(N_KV_HEADS, PAIRS, 2)
    sink_rows = jnp.repeat(jnp.transpose(sink_hp, (0, 2, 1)), BLOCK, axis=2)
    sink_rows = jnp.broadcast_to(sink_rows[..., None], (N_KV_HEADS, 2, PQ, LANES))
    sink_lanes = jnp.repeat(jnp.repeat(sink_hp, BLOCK, axis=1), HEAD_DIM, axis=2)
    n_units = N_KV_HEADS * ATTN_GROUP_BLOCKS
    y_c = _branch_call(
        _attn_kernel, "branch_attn", seq, x_bf,
        [w16(OFF_C, OFF_D), band, sink_rows, sink_lanes],
        [pltpu.VMEM((t, BRANCH_W), BF16), pltpu.VMEM((t, BRANCH_W), F32),
         pltpu.VMEM((N_KV_HEADS, 2, t + BLOCK, LANES), BF16),
         pltpu.VMEM((N_KV_HEADS, 2, t + BLOCK, LANES), BF16),
         pltpu.VMEM((PQ, 4 * BLOCK), jnp.int32),
         pltpu.VMEM((4 * BLOCK, LANES), BF16),
         pltpu.VMEM((n_units, PQ, 4 * BLOCK), F32),
         pltpu.VMEM((n_units, PQ, 4 * BLOCK), BF16),
         pltpu.VMEM((n_units, 2, PQ, LANES), F32),
         pltpu.VMEM((n_units, PQ, LANES), F32)])

    conv_w8 = jnp.broadcast_to(conv_w.astype(F32)[:, None, :], (CONV_K, SUBLANES, BRANCH_W))
    y_d = _branch_call(
        _conv_kernel, "branch_conv", seq, x_bf,
        [w16(OFF_D, OFF_G), conv_w8, row(conv_b), row(conv_ln_g), row(conv_ln_b)],
        [pltpu.VMEM((t + CONV_HALO, BRANCH_W), F32), pltpu.VMEM((t, BRANCH_W), F32),
         pltpu.VMEM((t, BRANCH_W), F32)])

    n_cb = D_MODEL // MERGE_COLS
    wg = w16(OFF_G, D_IN)
    wb = w_branch.astype(BF16)
    y_spec = pl.BlockSpec((t, BRANCH_W), lambda c, i: (i, 0))
    wg_specs = [pl.BlockSpec((D_MODEL, MERGE_COLS), functools.partial(lambda br, c, i: (0, br * n_cb + c), br))
                for br in range(N_BRANCH)]
    merged = pl.pallas_call(
        _merge_kernel,
        grid=(n_cb, seq // t),
        in_specs=[pl.BlockSpec((t, D_MODEL), lambda c, i: (i, 0)), y_spec, y_spec, y_spec, y_spec,
                  *wg_specs,
                  pl.BlockSpec((N_BRANCH, BRANCH_W, MERGE_COLS), lambda c, i: (0, 0, c))],
        out_specs=pl.BlockSpec((t, MERGE_COLS), lambda c, i: (i, c)),
        out_shape=jax.ShapeDtypeStruct((seq, D_MODEL), BF16),
        compiler_params=pltpu.CompilerParams(dimension_semantics=("arbitrary", "arbitrary"),
                                             vmem_limit_bytes=VMEM_LIMIT_BYTES),
        name="merge",
    )(x_bf, y_a, y_b, y_c, y_d, wg, wg, wg, wg, wb)

    x_next, x_next_bf = pl.pallas_call(
        _out_kernel,
        grid=(seq // t,),
        in_specs=[_row_tile(D_MODEL), _resident((D_MODEL, D_MODEL)), _row_tile(D_MODEL),
                  _resident((1, D_MODEL)), _resident((1, D_MODEL))],
        out_specs=[_row_tile(D_MODEL), _row_tile(D_MODEL)],
        out_shape=[jax.ShapeDtypeStruct((seq, D_MODEL), F32), jax.ShapeDtypeStruct((seq, D_MODEL), BF16)],
        scratch_shapes=[pltpu.VMEM((t, D_MODEL), F32)],
        compiler_params=_params(),
        name="out_norm",
    )(merged, w_out.astype(BF16), x, row(ln_g), row(ln_b))
    return x_next, x_next_bf


def kernel(x, w_in, pool_w, pool_scale, sgu_ln_g, sgu_ln_b, sgu_w, sgu_b, attn_sinks, rel_bias, conv_w,
           conv_b, conv_ln_g, conv_ln_b, w_branch, w_out, ln_g, ln_b):
    batch, seq, d_model = x.shape
    assert d_model == D_MODEL and w_in.shape == (DEPTH, D_MODEL, D_IN)
    assert seq % SEQ_TILE == 0 and SEQ_TILE % BLOCK == 0

    band = _band_bias(rel_bias).reshape(N_KV_HEADS, PAIRS, 2, BLOCK, 2 * BLOCK)
    band = jnp.transpose(band, (0, 1, 3, 2, 4)).reshape(N_KV_HEADS, PQ, 4 * BLOCK)

    outs = []
    for bi in range(batch):
        xc = x[bi].astype(F32)
        xc_bf = xc.astype(BF16)
        for l in range(DEPTH):
            xc, xc_bf = _layer(xc, xc_bf, band, w_in[l], pool_w[l], pool_scale[l], sgu_ln_g[l], sgu_ln_b[l],
                               sgu_w[l], sgu_b[l], attn_sinks[l], conv_w[l], conv_b[l], conv_ln_g[l],
                               conv_ln_b[l], w_branch[l], w_out[l], ln_g[l], ln_b[l])
        outs.append(xc.astype(x.dtype)[None])
    return outs[0] if batch == 1 else jnp.concatenate(outs, axis=0)
```

```python
import functools
import math

import jax
import jax.numpy as jnp
from jax import lax
from jax.experimental import pallas as pl
from jax.experimental.pallas import tpu as pltpu

F32 = jnp.float32
BF16 = jnp.bfloat16

D_MODEL = 2048
DEPTH = 2
N_BRANCH = 4
BRANCH_W = 1024
POOL_WINDOWS = (2, 4, 8, 16)
POOL_GW = BRANCH_W // len(POOL_WINDOWS)
SGU_HEADS = 8
SGU_HD = BRANCH_W // SGU_HEADS
CHUNK = 128
N_Q_HEADS = 16
N_KV_HEADS = 2
Q_PER_KV = N_Q_HEADS // N_KV_HEADS
HEAD_DIM = 64
KV_W = N_KV_HEADS * HEAD_DIM
WINDOW = 128
BLOCK = 128
NUM_BUCKETS = 32
MAX_DISTANCE = 128
CONV_K = 31
ALPHA = (2 * DEPTH) ** 0.25
LN_EPS = 1e-5
NEG_INF = -1e30

OFF_A = 0
OFF_B = OFF_A + 2 * BRANCH_W
OFF_C = OFF_B + 3 * BRANCH_W
OFF_D = OFF_C + 2 * BRANCH_W + 2 * KV_W
OFF_G = OFF_D + 3 * BRANCH_W
D_IN = OFF_G + N_BRANCH * D_MODEL

LANES = 128
SUBLANES = 8
SEQ_TILE = 512
POOL_HALO = 16
CONV_HALO = 32
ROW_CHUNK = 64
CONV_ROWS = 32
PROJ_COLS = 256
MERGE_COLS = 512
PAIRS = Q_PER_KV // 2
PQ = PAIRS * BLOCK
ATTN_GROUP_BLOCKS = 2
VMEM_LIMIT_BYTES = 56 * 1024 * 1024


def _sigmoid(x):
    return 0.5 * jnp.tanh(0.5 * x) + 0.5


def _silu(x):
    return x * _sigmoid(x)


def _layer_norm_rows(v, g, b):
    mu = jnp.mean(v, axis=-1, keepdims=True)
    d = v - mu
    var = jnp.mean(d * d, axis=-1, keepdims=True)
    return d * lax.rsqrt(var + LN_EPS) * g + b


def _dot(a, b):
    return jnp.dot(a, b, preferred_element_type=F32)


def _t5_bucket(n):
    max_exact = NUM_BUCKETS // 2
    nf = jnp.maximum(n, 1).astype(F32)
    large = max_exact + (jnp.log(nf / max_exact) / math.log(MAX_DISTANCE / max_exact)
                         * (NUM_BUCKETS - max_exact)).astype(jnp.int32)
    large = jnp.minimum(large, NUM_BUCKETS - 1)
    return jnp.where(n < max_exact, n, large)


def _band_bucket_ids():
    i = jnp.arange(BLOCK)[:, None]
    j = jnp.arange(2 * BLOCK)[None, :]
    return _t5_bucket(jnp.clip(i + BLOCK - j, 0, WINDOW - 1)).astype(jnp.int32)


def _bias_kernel(bucket_ref, table_ref, o_ref):
    h = pl.program_id(0)
    bucket = bucket_ref[...]
    acc = jnp.zeros(bucket.shape, F32)
    for b in range(NUM_BUCKETS):
        acc = jnp.where(bucket == b, table_ref[b, h], acc)
    o_ref[0] = acc


def _band_bias(rel_bias):
    return pl.pallas_call(
        _bias_kernel,
        grid=(N_Q_HEADS,),
        in_specs=[pl.BlockSpec((BLOCK, 2 * BLOCK), lambda h: (0, 0)),
                  pl.BlockSpec(memory_space=pltpu.SMEM)],
        out_specs=pl.BlockSpec((1, BLOCK, 2 * BLOCK), lambda h: (h, 0, 0)),
        out_shape=jax.ShapeDtypeStruct((N_Q_HEADS, BLOCK, 2 * BLOCK), F32),
        name="band_bias",
    )(_band_bucket_ids(), rel_bias.astype(F32))


def _pool_kernel(x_ref, w_ref, pw_ref, sc_ref, o_ref, abuf, gate_buf, mix_buf):
    i = pl.program_id(0)
    t_rows = x_ref.shape[0]

    @pl.when(i == 0)
    def _():
        abuf[0:POOL_HALO, :] = jnp.zeros((POOL_HALO, BRANCH_W), F32)

    xb = x_ref[...]
    abuf[POOL_HALO:POOL_HALO + t_rows, :] = _dot(xb, w_ref[:, 0:BRANCH_W])
    gate_buf[...] = _dot(xb, w_ref[:, BRANCH_W:2 * BRANCH_W])

    for g, win in enumerate(POOL_WINDOWS):
        c0 = g * POOL_GW
        for r0 in range(0, t_rows, ROW_CHUNK):
            base = POOL_HALO + r0
            cur = abuf[base:base + ROW_CHUNK, c0:c0 + POOL_GW]
            acc = cur
            for s in range(1, win):
                acc = acc + abuf[base - s:base - s + ROW_CHUNK, c0:c0 + POOL_GW]
            t = i * t_rows + r0 + lax.broadcasted_iota(jnp.int32, (ROW_CHUNK, POOL_GW), 0)
            cnt = jnp.minimum(t + 1, win).astype(F32)
            mix_buf[r0:r0 + ROW_CHUNK, c0:c0 + POOL_GW] = (acc / cnt - cur).astype(BF16)

    for g in range(len(POOL_WINDOWS)):
        c0 = g * POOL_GW
        y = _dot(mix_buf[:, c0:c0 + POOL_GW], pw_ref[g])
        y = y * sc_ref[:, c0:c0 + POOL_GW] * _silu(gate_buf[:, c0:c0 + POOL_GW])
        o_ref[:, c0:c0 + POOL_GW] = y.astype(BF16)

    abuf[0:POOL_HALO, :] = abuf[t_rows:t_rows + POOL_HALO, :]


def _sgu_kernel(x_ref, w_ref, lng_ref, lnb_ref, sw_ref, sb_ref, o_ref,
                u_buf, v_buf, gate_buf, vn_buf):
    t_rows = x_ref.shape[0]
    xb = x_ref[...]
    u_buf[...] = _dot(xb, w_ref[:, 0:BRANCH_W])
    v_buf[...] = _dot(xb, w_ref[:, BRANCH_W:2 * BRANCH_W])
    gate_buf[...] = _dot(xb, w_ref[:, 2 * BRANCH_W:3 * BRANCH_W])

    g = lng_ref[...]
    b = lnb_ref[...]
    for r0 in range(0, t_rows, ROW_CHUNK):
        vn = _layer_norm_rows(v_buf[r0:r0 + ROW_CHUNK, :], g, b)
        vn_buf[r0:r0 + ROW_CHUNK, :] = vn.astype(BF16)

    row = lax.broadcasted_iota(jnp.int32, (CHUNK, CHUNK), 0)
    col = lax.broadcasted_iota(jnp.int32, (CHUNK, CHUNK), 1)
    causal = row >= col
    for h in range(SGU_HEADS):
        c0 = h * SGU_HD
        w_h = jnp.where(causal, sw_ref[h], 0.0).astype(BF16)
        bias_h = sb_ref[:, c0:c0 + SGU_HD]
        for r0 in range(0, t_rows, CHUNK):
            sp = _dot(w_h, vn_buf[r0:r0 + CHUNK, c0:c0 + SGU_HD]) + bias_h
            y = u_buf[r0:r0 + CHUNK, c0:c0 + SGU_HD] * sp * _silu(gate_buf[r0:r0 + CHUNK, c0:c0 + SGU_HD])
            o_ref[r0:r0 + CHUNK, c0:c0 + SGU_HD] = y.astype(BF16)


def _attn_kernel(x_ref, w_ref, bias_ref, sink_ref, sinkc_ref, o_ref,
                 q_buf, gate_buf, k_tab, v_tab, code_buf, ones_buf, s_buf, p_buf, m_buf, mc_buf):
    i = pl.program_id(0)
    t_rows = x_ref.shape[0]
    n_blocks = t_rows // BLOCK

    @pl.when(i == 0)
    def _():
        zero = jnp.zeros((BLOCK, LANES), BF16)
        for j in range(N_KV_HEADS):
            for half in range(2):
                k_tab[j, half, 0:BLOCK, :] = zero
                v_tab[j, half, 0:BLOCK, :] = zero
        row = lax.broadcasted_iota(jnp.int32, (PQ, 2 * 2 * BLOCK), 0)
        col = lax.broadcasted_iota(jnp.int32, (PQ, 2 * 2 * BLOCK), 1)
        key = col & (2 * BLOCK - 1)
        u = key - (row & (BLOCK - 1)) - 1
        in_window = (u >= 0) & (u < WINDOW)
        code_buf[...] = jnp.where(in_window, jnp.where(key >= BLOCK, 2, 1), 0)
        krow = lax.broadcasted_iota(jnp.int32, (4 * BLOCK, LANES), 0)
        klane = lax.broadcasted_iota(jnp.int32, (4 * BLOCK, LANES), 1)
        ones_buf[...] = jnp.where((krow >= 2 * BLOCK) == (klane >= HEAD_DIM), 1.0, 0.0).astype(BF16)

    xb = x_ref[...]
    q_buf[...] = (_dot(xb, w_ref[:, 0:BRANCH_W]) * (HEAD_DIM ** -0.5)).astype(BF16)
    gate_buf[...] = _dot(xb, w_ref[:, BRANCH_W + 2 * KV_W:2 * BRANCH_W + 2 * KV_W])

    lane = lax.broadcasted_iota(jnp.int32, (t_rows, LANES), 1)
    low = lane < HEAD_DIM
    kv = _dot(xb, w_ref[:, BRANCH_W:BRANCH_W + 2 * KV_W])
    for tab, c0 in ((k_tab, 0), (v_tab, KV_W)):
        val = kv[:, c0:c0 + KV_W]
        swapped = pltpu.roll(val, HEAD_DIM, axis=1)
        zero = jnp.zeros_like(val)
        tab[0, 0, BLOCK:BLOCK + t_rows, :] = jnp.where(low, val, zero).astype(BF16)
        tab[0, 1, BLOCK:BLOCK + t_rows, :] = jnp.where(low, zero, swapped).astype(BF16)
        tab[1, 0, BLOCK:BLOCK + t_rows, :] = jnp.where(low, swapped, zero).astype(BF16)
        tab[1, 1, BLOCK:BLOCK + t_rows, :] = jnp.where(low, zero, val).astype(BF16)

    low128 = lax.broadcasted_iota(jnp.int32, (BLOCK, LANES), 1) < HEAD_DIM
    for g0 in range(0, n_blocks, ATTN_GROUP_BLOCKS):
        units = [(n, j) for n in range(g0, g0 + ATTN_GROUP_BLOCKS) for j in range(N_KV_HEADS)]

        for it, (n, j) in enumerate(units):
            r0 = n * BLOCK
            q_cols = j * PAIRS * LANES
            qs = jnp.concatenate(
                [q_buf[r0:r0 + BLOCK, q_cols + p * LANES:q_cols + (p + 1) * LANES] for p in range(PAIRS)], axis=0)
            k2 = jnp.concatenate([k_tab[j, 0, r0:r0 + 2 * BLOCK, :], k_tab[j, 1, r0:r0 + 2 * BLOCK, :]], axis=0)
            logits = lax.dot_general(qs, k2, (((1,), (1,)), ((), ())), preferred_element_type=F32)
            threshold = jnp.where(jnp.logical_and(i == 0, n == 0), 2, 1)
            for p in range(PAIRS):
                rows = slice(p * BLOCK, (p + 1) * BLOCK)
                row_max = []
                for par in range(2):
                    slabs = []
                    for half in range(2):
                        cols = slice((2 * par + half) * LANES, (2 * par + half + 1) * LANES)
                        s = logits[rows, cols] + bias_ref[j, rows, cols]
                        s = jnp.where(code_buf[rows, cols] >= threshold, s, NEG_INF)
                        s_buf[it, rows, cols] = s
                        slabs.append(s)
                    m = jnp.max(jnp.maximum(slabs[0], slabs[1]), axis=1, keepdims=True)
                    m = jnp.maximum(jnp.broadcast_to(m, (BLOCK, LANES)), sink_ref[j, par, rows, :])
                    m_buf[it, par, rows, :] = m
                    row_max.append(m)
                mc_buf[it, rows, :] = jnp.where(low128, row_max[0], row_max[1])

        for it, (n, j) in enumerate(units):
            for p in range(PAIRS):
                rows = slice(p * BLOCK, (p + 1) * BLOCK)
                for par in range(2):
                    m = m_buf[it, par, rows, :]
                    for half in range(2):
                        cols = slice((2 * par + half) * LANES, (2 * par + half + 1) * LANES)
                        p_buf[it, rows, cols] = jnp.exp(s_buf[it, rows, cols] - m).astype(BF16)

        for it, (n, j) in enumerate(units):
            r0 = n * BLOCK
            q_cols = j * PAIRS * LANES
            v2 = jnp.concatenate([v_tab[j, 0, r0:r0 + 2 * BLOCK, :], v_tab[j, 1, r0:r0 + 2 * BLOCK, :]], axis=0)
            o = _dot(p_buf[it], jnp.concatenate([v2, ones_buf[...]], axis=1))
            for p in range(PAIRS):
                rows = slice(p * BLOCK, (p + 1) * BLOCK)
                cols = slice(q_cols + p * LANES, q_cols + (p + 1) * LANES)
                den = o[rows, LANES:2 * LANES] + jnp.exp(sinkc_ref[j, rows, :] - mc_buf[it, rows, :])
                y = o[rows, 0:LANES] * (1.0 / den) * _silu(gate_buf[r0:r0 + BLOCK, cols])
                o_ref[r0:r0 + BLOCK, cols] = y.astype(BF16)

    for j in range(N_KV_HEADS):
        for half in range(2):
            k_tab[j, half, 0:BLOCK, :] = k_tab[j, half, t_rows:t_rows + BLOCK, :]
            v_tab[j, half, 0:BLOCK, :] = v_tab[j, half, t_rows:t_rows + BLOCK, :]


CONV_W_CHUNKS = 3 * BRANCH_W // PROJ_COLS
CONV_COLUMNS = BRANCH_W // LANES
CONV_BLOCKS = BRANCH_W // PROJ_COLS


def _conv_kernel(x_ref, wchunk_ref, cw_ref, cb_ref, lng_ref, lnb_ref, o_ref,
                 w_buf, xs_buf, glu_buf, gate_buf, shift_buf, y_buf, sig_buf):
    s = pl.program_id(0)
    t_rows = x_ref.shape[0]

    for j in range(CONV_W_CHUNKS):
        @pl.when(s == j)
        def _(j=j):
            w_buf[j] = wchunk_ref[...].astype(BF16)

    @pl.when(s >= CONV_W_CHUNKS)
    def _():
        @pl.when(s == CONV_W_CHUNKS)
        def _():
            for c in range(CONV_COLUMNS):
                glu_buf[c, 0:CONV_HALO, :] = jnp.zeros((CONV_HALO, LANES), F32)

        xs_buf[...] = x_ref[...]

        def project(n):
            sig_buf[...] = _sigmoid(_dot(xs_buf[...], w_buf[CONV_BLOCKS + n]))
            val = _dot(xs_buf[...], w_buf[n])
            glu_buf[2 * n, CONV_HALO:CONV_HALO + t_rows, :] = val[:, 0:LANES] * sig_buf[:, 0:LANES]
            glu_buf[2 * n + 1, CONV_HALO:CONV_HALO + t_rows, :] = val[:, LANES:2 * LANES] * sig_buf[:, LANES:2 * LANES]

        first = CONV_HALO - (CONV_K - 1)
        span = t_rows + CONV_HALO - SUBLANES

        def conv(c):
            for r in range(1, SUBLANES):
                shift_buf[r, 0:span, :] = glu_buf[c, r:r + span, :]
            bias = jnp.broadcast_to(cb_ref[c], (CONV_ROWS, LANES))
            for r0 in range(0, t_rows, CONV_ROWS):
                acc = bias
                for k in range(CONV_K):
                    a8 = (first + k) // SUBLANES * SUBLANES
                    r = (first + k) % SUBLANES
                    if r == 0:
                        src = glu_buf[c, r0 + a8:r0 + a8 + CONV_ROWS, :]
                    else:
                        src = shift_buf[r, r0 + a8:r0 + a8 + CONV_ROWS, :]
                    tap = jnp.concatenate([cw_ref[c, k]] * (CONV_ROWS // SUBLANES), axis=0)
                    acc = acc + tap * src
                y_buf[c, r0:r0 + CONV_ROWS, :] = acc

        project(0)

        def block_step(n, carry):
            project(n + 1)
            conv(2 * n)
            conv(2 * n + 1)
            return carry

        lax.fori_loop(0, CONV_BLOCKS - 1, block_step, 0)

        conv(2 * CONV_BLOCKS - 2)
        conv(2 * CONV_BLOCKS - 1)
        for g in range(CONV_BLOCKS):
            gate = _dot(xs_buf[...], w_buf[2 * CONV_BLOCKS + g])
            gate_buf[2 * g] = gate[:, 0:LANES]
            gate_buf[2 * g + 1] = gate[:, LANES:2 * LANES]

        for r0 in range(0, t_rows, ROW_CHUNK):
            rows = slice(r0, r0 + ROW_CHUNK)
            ys = [y_buf[c, rows, :] for c in range(CONV_COLUMNS)]
            total = ys[0]
            for y in ys[1:]:
                total = total + y
            mu = jnp.sum(total, axis=1, keepdims=True) * (1.0 / BRANCH_W)
            ds = [y - mu for y in ys]
            sq = ds[0] * ds[0]
            for d in ds[1:]:
                sq = sq + d * d
            rstd = lax.rsqrt(jnp.sum(sq, axis=1, keepdims=True) * (1.0 / BRANCH_W) + LN_EPS)
            for c in range(CONV_COLUMNS):
                y = _silu(ds[c] * rstd * lng_ref[c] + lnb_ref[c]) * _silu(gate_buf[c, rows, :])
                o_ref[rows, c * LANES:(c + 1) * LANES] = y.astype(BF16)

        for c in range(CONV_COLUMNS):
            glu_buf[c, 0:CONV_HALO, :] = glu_buf[c, t_rows:t_rows + CONV_HALO, :]


def _merge_kernel(x_ref, ya_ref, yb_ref, yc_ref, yd_ref, wga_ref, wgb_ref, wgc_ref, wgd_ref, wb_ref, o_ref):
    xb = x_ref[...]
    acc = None
    branches = ((ya_ref, wga_ref), (yb_ref, wgb_ref), (yc_ref, wgc_ref), (yd_ref, wgd_ref))
    for br, (y_ref, wg_ref) in enumerate(branches):
        gate = _sigmoid(_dot(xb, wg_ref[...]))
        term = gate * _dot(y_ref[...], wb_ref[br])
        acc = term if acc is None else acc + term
    o_ref[...] = acc.astype(BF16)


def _out_kernel(m_ref, w_ref, x_ref, g_ref, b_ref, o_ref, ob_ref, z_buf):
    t_rows = x_ref.shape[0]
    z_buf[...] = _dot(m_ref[...], w_ref[...])
    g = g_ref[...]
    b = b_ref[...]
    for r0 in range(0, t_rows, ROW_CHUNK // 2):
        rows = slice(r0, r0 + ROW_CHUNK // 2)
        y = _layer_norm_rows(ALPHA * x_ref[rows, :] + z_buf[rows, :], g, b)
        o_ref[rows, :] = y
        ob_ref[rows, :] = y.astype(BF16)


def _resident(shape):
    zeros = (0,) * len(shape)
    return pl.BlockSpec(shape, lambda *_: zeros, pipeline_mode=pl.Buffered(1))


def _params():
    return pltpu.CompilerParams(dimension_semantics=("arbitrary",), vmem_limit_bytes=VMEM_LIMIT_BYTES)


def _row_tile(width):
    return pl.BlockSpec((SEQ_TILE, width), lambda i: (i, 0))


def _branch_call(kernel_fn, name, seq, x_bf, operands, scratch):
    return pl.pallas_call(
        kernel_fn,
        grid=(seq // SEQ_TILE,),
        in_specs=[_row_tile(D_MODEL)] + [_resident(op.shape) for op in operands],
        out_specs=_row_tile(BRANCH_W),
        out_shape=jax.ShapeDtypeStruct((seq, BRANCH_W), BF16),
        scratch_shapes=scratch,
        compiler_params=_params(),
        name=name,
    )(x_bf, *operands)


def _conv_call(seq, x_bf, w_in_all, layer, conv_w, conv_b, ln_g, ln_b):
    t = SEQ_TILE
    lead = CONV_W_CHUNKS
    tile_map = lambda s: (jnp.maximum(s - lead, 0), 0)
    chunk_map = lambda s: (layer, 0, OFF_D // PROJ_COLS + jnp.minimum(s, lead - 1))
    per_col = lambda v: v.astype(F32).reshape(CONV_COLUMNS, 1, LANES)
    taps = jnp.transpose(conv_w.astype(F32).reshape(CONV_K, CONV_COLUMNS, LANES), (1, 0, 2))
    taps = jnp.broadcast_to(taps[:, :, None, :], (CONV_COLUMNS, CONV_K, SUBLANES, LANES))
    small = [taps, per_col(conv_b), per_col(ln_g), per_col(ln_b)]
    return pl.pallas_call(
        _conv_kernel,
        grid=(lead + seq // t,),
        in_specs=[pl.BlockSpec((t, D_MODEL), tile_map),
                  pl.BlockSpec((None, D_MODEL, PROJ_COLS), chunk_map)] + [_resident(op.shape) for op in small],
        out_specs=pl.BlockSpec((t, BRANCH_W), tile_map),
        out_shape=jax.ShapeDtypeStruct((seq, BRANCH_W), BF16),
        scratch_shapes=[pltpu.VMEM((CONV_W_CHUNKS, D_MODEL, PROJ_COLS), BF16),
                        pltpu.VMEM((t, D_MODEL), BF16),
                        pltpu.VMEM((CONV_COLUMNS, t + CONV_HALO, LANES), F32),
                        pltpu.VMEM((CONV_COLUMNS, t, LANES), F32),
                        pltpu.VMEM((SUBLANES, t + CONV_HALO - SUBLANES, LANES), F32),
                        pltpu.VMEM((CONV_COLUMNS, t, LANES), F32),
                        pltpu.VMEM((t, PROJ_COLS), F32)],
        compiler_params=_params(),
        name="branch_conv",
    )(x_bf, w_in_all, *small)


def _layer(x, x_bf, band, w_in_all, layer, pool_w, pool_scale, sgu_ln_g, sgu_ln_b, sgu_w, sgu_b, sinks,
           conv_w, conv_b, conv_ln_g, conv_ln_b, w_branch, w_out, ln_g, ln_b):
    seq = x.shape[0]
    t = SEQ_TILE
    w16 = lambda a, b: w_in_all[layer, :, a:b].astype(BF16)
    row = lambda v: v.reshape(1, -1).astype(F32)

    y_a = _branch_call(
        _pool_kernel, "branch_pool", seq, x_bf,
        [w16(OFF_A, OFF_B), pool_w.astype(BF16), row(pool_scale)],
        [pltpu.VMEM((t + POOL_HALO, BRANCH_W), F32), pltpu.VMEM((t, BRANCH_W), F32),
         pltpu.VMEM((t, BRANCH_W), BF16)])

    sgu_bias = jnp.repeat(jnp.transpose(sgu_b).astype(F32), SGU_HD, axis=1)
    y_b = _branch_call(
        _sgu_kernel, "branch_sgu", seq, x_bf,
        [w16(OFF_B, OFF_C), row(sgu_ln_g), row(sgu_ln_b), sgu_w.astype(F32), sgu_bias],
        [pltpu.VMEM((t, BRANCH_W), F32), pltpu.VMEM((t, BRANCH_W), F32),
         pltpu.VMEM((t, BRANCH_W), F32), pltpu.VMEM((t, BRANCH_W), BF16)])

    sink_hp = sinks.astype(F32).reshape(N_KV_HEADS, PAIRS, 2)
    sink_rows = jnp.repeat(jnp.transpose(sink_hp, (0, 2, 1)), BLOCK, axis=2)
    sink_rows = jnp.broadcast_to(sink_rows[..., None], (N_KV_HEADS, 2, PQ, LANES))
    sink_lanes = jnp.repeat(jnp.repeat(sink_hp, BLOCK, axis=1), HEAD_DIM, axis=2)
    n_units = N_KV_HEADS * ATTN_GROUP_BLOCKS
    y_c = _branch_call(
        _attn_kernel, "branch_attn", seq, x_bf,
        [w16(OFF_C, OFF_D), band, sink_rows, sink_lanes],
        [pltpu.VMEM((t, BRANCH_W), BF16), pltpu.VMEM((t, BRANCH_W), F32),
         pltpu.VMEM((N_KV_HEADS, 2, t + BLOCK, LANES), BF16),
         pltpu.VMEM((N_KV_HEADS, 2, t + BLOCK, LANES), BF16),
         pltpu.VMEM((PQ, 4 * BLOCK), jnp.int32),
         pltpu.VMEM((4 * BLOCK, LANES), BF16),
         pltpu.VMEM((n_units, PQ, 4 * BLOCK), F32),
         pltpu.VMEM((n_units, PQ, 4 * BLOCK), BF16),
         pltpu.VMEM((n_units, 2, PQ, LANES), F32),
         pltpu.VMEM((n_units, PQ, LANES), F32)])

    y_d = _conv_call(seq, x_bf, w_in_all, layer, conv_w, conv_b, conv_ln_g, conv_ln_b)

    n_cb = D_MODEL // MERGE_COLS
    wg = w16(OFF_G, D_IN)
    wb = w_branch.astype(BF16)
    y_spec = pl.BlockSpec((t, BRANCH_W), lambda c, i: (i, 0))
    wg_specs = [pl.BlockSpec((D_MODEL, MERGE_COLS), functools.partial(lambda br, c, i: (0, br * n_cb + c), br))
                for br in range(N_BRANCH)]
    merged = pl.pallas_call(
        _merge_kernel,
        grid=(n_cb, seq // t),
        in_specs=[pl.BlockSpec((t, D_MODEL), lambda c, i: (i, 0)), y_spec, y_spec, y_spec, y_spec,
                  *wg_specs,
                  pl.BlockSpec((N_BRANCH, BRANCH_W, MERGE_COLS), lambda c, i: (0, 0, c))],
        out_specs=pl.BlockSpec((t, MERGE_COLS), lambda c, i: (i, c)),
        out_shape=jax.ShapeDtypeStruct((seq, D_MODEL), BF16),
        compiler_params=pltpu.CompilerParams(dimension_semantics=("arbitrary", "arbitrary"),
                                             vmem_limit_bytes=VMEM_LIMIT_BYTES),
        name="merge",
    )(x_bf, y_a, y_b, y_c, y_d, wg, wg, wg, wg, wb)

    x_next, x_next_bf = pl.pallas_call(
        _out_kernel,
        grid=(seq // t,),
        in_specs=[_row_tile(D_MODEL), _resident((D_MODEL, D_MODEL)), _row_tile(D_MODEL),
                  _resident((1, D_MODEL)), _resident((1, D_MODEL))],
        out_specs=[_row_tile(D_MODEL), _row_tile(D_MODEL)],
        out_shape=[jax.ShapeDtypeStruct((seq, D_MODEL), F32), jax.ShapeDtypeStruct((seq, D_MODEL), BF16)],
        scratch_shapes=[pltpu.VMEM((t, D_MODEL), F32)],
        compiler_params=_params(),
        name="out_norm",
    )(merged, w_out.astype(BF16), x, row(ln_g), row(ln_b))
    return x_next, x_next_bf


def kernel(x, w_in, pool_w, pool_scale, sgu_ln_g, sgu_ln_b, sgu_w, sgu_b, attn_sinks, rel_bias, conv_w,
           conv_b, conv_ln_g, conv_ln_b, w_branch, w_out, ln_g, ln_b):
    batch, seq, d_model = x.shape
    assert d_model == D_MODEL and w_in.shape == (DEPTH, D_MODEL, D_IN)
    assert seq % SEQ_TILE == 0 and SEQ_TILE % (BLOCK * ATTN_GROUP_BLOCKS) == 0

    band = _band_bias(rel_bias).reshape(N_KV_HEADS, PAIRS, 2, BLOCK, 2 * BLOCK)
    band = jnp.transpose(band, (0, 1, 3, 2, 4)).reshape(N_KV_HEADS, PQ, 4 * BLOCK)

    w_in = w_in.astype(F32)
    outs = []
    for bi in range(batch):
        xc = x[bi].astype(F32)
        xc_bf = xc.astype(BF16)
        for l in range(DEPTH):
            xc, xc_bf = _layer(xc, xc_bf, band, w_in, l, pool_w[l], pool_scale[l], sgu_ln_g[l], sgu_ln_b[l],
                               sgu_w[l], sgu_b[l], attn_sinks[l], conv_w[l], conv_b[l], conv_ln_g[l],
                               conv_ln_b[l], w_branch[l], w_out[l], ln_g[l], ln_b[l])
        outs.append(xc.astype(x.dtype)[None])
    return outs[0] if batch == 1 else jnp.concatenate(outs, axis=0)
```

```python
import functools
import math

import jax
import jax.numpy as jnp
from jax import lax
from jax.experimental import pallas as pl
from jax.experimental.pallas import tpu as pltpu

F32 = jnp.float32
BF16 = jnp.bfloat16

D_MODEL = 2048
DEPTH = 2
N_BRANCH = 4
BRANCH_W = 1024
POOL_WINDOWS = (2, 4, 8, 16)
POOL_GW = BRANCH_W // len(POOL_WINDOWS)
SGU_HEADS = 8
SGU_HD = BRANCH_W // SGU_HEADS
CHUNK = 128
N_Q_HEADS = 16
N_KV_HEADS = 2
Q_PER_KV = N_Q_HEADS // N_KV_HEADS
HEAD_DIM = 64
KV_W = N_KV_HEADS * HEAD_DIM
WINDOW = 128
BLOCK = 128
NUM_BUCKETS = 32
MAX_DISTANCE = 128
CONV_K = 31
ALPHA = (2 * DEPTH) ** 0.25
LN_EPS = 1e-5
NEG_INF = -1e30

OFF_A = 0
OFF_B = OFF_A + 2 * BRANCH_W
OFF_C = OFF_B + 3 * BRANCH_W
OFF_D = OFF_C + 2 * BRANCH_W + 2 * KV_W
OFF_G = OFF_D + 3 * BRANCH_W
D_IN = OFF_G + N_BRANCH * D_MODEL

LANES = 128
SUBLANES = 8
SEQ_TILE = 512
POOL_HALO = 16
CONV_HALO = 32
ROW_CHUNK = 64
CONV_ROWS = 32
PROJ_COLS = 256
MERGE_COLS = 512
PAIRS = Q_PER_KV // 2
PQ = PAIRS * BLOCK
ATTN_GROUP_BLOCKS = 2
VMEM_LIMIT_BYTES = 56 * 1024 * 1024


def _sigmoid(x):
    return 0.5 * jnp.tanh(0.5 * x) + 0.5


def _silu(x):
    return x * _sigmoid(x)


def _layer_norm_rows(v, g, b):
    mu = jnp.mean(v, axis=-1, keepdims=True)
    d = v - mu
    var = jnp.mean(d * d, axis=-1, keepdims=True)
    return d * lax.rsqrt(var + LN_EPS) * g + b


def _dot(a, b):
    return jnp.dot(a, b, preferred_element_type=F32)


def _t5_bucket(n):
    max_exact = NUM_BUCKETS // 2
    nf = jnp.maximum(n, 1).astype(F32)
    large = max_exact + (jnp.log(nf / max_exact) / math.log(MAX_DISTANCE / max_exact)
                         * (NUM_BUCKETS - max_exact)).astype(jnp.int32)
    large = jnp.minimum(large, NUM_BUCKETS - 1)
    return jnp.where(n < max_exact, n, large)


def _band_bucket_ids():
    i = jnp.arange(BLOCK)[:, None]
    j = jnp.arange(2 * BLOCK)[None, :]
    return _t5_bucket(jnp.clip(i + BLOCK - j, 0, WINDOW - 1)).astype(jnp.int32)


def _bias_kernel(bucket_ref, table_ref, o_ref):
    h = pl.program_id(0)
    bucket = bucket_ref[...]
    acc = jnp.zeros(bucket.shape, F32)
    for b in range(NUM_BUCKETS):
        acc = jnp.where(bucket == b, table_ref[b, h], acc)
    o_ref[0] = acc


def _band_bias(rel_bias):
    return pl.pallas_call(
        _bias_kernel,
        grid=(N_Q_HEADS,),
        in_specs=[pl.BlockSpec((BLOCK, 2 * BLOCK), lambda h: (0, 0)),
                  pl.BlockSpec(memory_space=pltpu.SMEM)],
        out_specs=pl.BlockSpec((1, BLOCK, 2 * BLOCK), lambda h: (h, 0, 0)),
        out_shape=jax.ShapeDtypeStruct((N_Q_HEADS, BLOCK, 2 * BLOCK), F32),
        name="band_bias",
    )(_band_bucket_ids(), rel_bias.astype(F32))


W_IN_SPLITS = ((OFF_A, OFF_B), (OFF_B, OFF_C), (OFF_C, OFF_D), (OFF_G, D_IN))
CONV_W_CHUNKS = (OFF_G - OFF_D) // PROJ_COLS
CAST_ROWS = 128


def _cast_w_in_kernel(w_ref, oa_ref, ob_ref, oc_ref, og_ref, od_ref):
    for (a, b), o_ref in zip(W_IN_SPLITS, (oa_ref, ob_ref, oc_ref, og_ref)):
        o_ref[...] = w_ref[:, a:b].astype(BF16)
    for j in range(CONV_W_CHUNKS):
        od_ref[j] = w_ref[:, OFF_D + j * PROJ_COLS:OFF_D + (j + 1) * PROJ_COLS].astype(BF16)


def _cast_w_in(w_in):
    widths = [b - a for a, b in W_IN_SPLITS]
    out_shape = [jax.ShapeDtypeStruct((DEPTH, D_MODEL, w), BF16) for w in widths]
    out_shape.append(jax.ShapeDtypeStruct((DEPTH, CONV_W_CHUNKS, D_MODEL, PROJ_COLS), BF16))
    out_specs = [pl.BlockSpec((None, CAST_ROWS, w), lambda l, r: (l, r, 0)) for w in widths]
    out_specs.append(pl.BlockSpec((None, CONV_W_CHUNKS, CAST_ROWS, PROJ_COLS), lambda l, r: (l, 0, r, 0)))
    return pl.pallas_call(
        _cast_w_in_kernel,
        grid=(DEPTH, D_MODEL // CAST_ROWS),
        in_specs=[pl.BlockSpec((None, CAST_ROWS, D_IN), lambda l, r: (l, r, 0))],
        out_specs=out_specs,
        out_shape=out_shape,
        compiler_params=pltpu.CompilerParams(dimension_semantics=("arbitrary", "arbitrary"),
                                             vmem_limit_bytes=VMEM_LIMIT_BYTES),
        name="cast_w_in",
    )(w_in)


def _pool_kernel(x_ref, w_ref, pw_ref, sc_ref, o_ref, abuf, gate_buf, mix_buf):
    i = pl.program_id(0)
    t_rows = x_ref.shape[0]

    @pl.when(i == 0)
    def _():
        abuf[0:POOL_HALO, :] = jnp.zeros((POOL_HALO, BRANCH_W), F32)

    xb = x_ref[...]
    abuf[POOL_HALO:POOL_HALO + t_rows, :] = _dot(xb, w_ref[:, 0:BRANCH_W])
    gate_buf[...] = _dot(xb, w_ref[:, BRANCH_W:2 * BRANCH_W])

    for g, win in enumerate(POOL_WINDOWS):
        c0 = g * POOL_GW
        for r0 in range(0, t_rows, ROW_CHUNK):
            base = POOL_HALO + r0
            cur = abuf[base:base + ROW_CHUNK, c0:c0 + POOL_GW]
            acc = cur
            for s in range(1, win):
                acc = acc + abuf[base - s:base - s + ROW_CHUNK, c0:c0 + POOL_GW]
            t = i * t_rows + r0 + lax.broadcasted_iota(jnp.int32, (ROW_CHUNK, POOL_GW), 0)
            cnt = jnp.minimum(t + 1, win).astype(F32)
            mix_buf[r0:r0 + ROW_CHUNK, c0:c0 + POOL_GW] = (acc / cnt - cur).astype(BF16)

    for g in range(len(POOL_WINDOWS)):
        c0 = g * POOL_GW
        y = _dot(mix_buf[:, c0:c0 + POOL_GW], pw_ref[g])
        y = y * sc_ref[:, c0:c0 + POOL_GW] * _silu(gate_buf[:, c0:c0 + POOL_GW])
        o_ref[:, c0:c0 + POOL_GW] = y.astype(BF16)

    abuf[0:POOL_HALO, :] = abuf[t_rows:t_rows + POOL_HALO, :]


def _sgu_kernel(x_ref, w_ref, lng_ref, lnb_ref, sw_ref, sb_ref, o_ref,
                u_buf, v_buf, gate_buf, vn_buf):
    t_rows = x_ref.shape[0]
    xb = x_ref[...]
    u_buf[...] = _dot(xb, w_ref[:, 0:BRANCH_W])
    v_buf[...] = _dot(xb, w_ref[:, BRANCH_W:2 * BRANCH_W])
    gate_buf[...] = _dot(xb, w_ref[:, 2 * BRANCH_W:3 * BRANCH_W])

    g = lng_ref[...]
    b = lnb_ref[...]
    for r0 in range(0, t_rows, ROW_CHUNK):
        vn = _layer_norm_rows(v_buf[r0:r0 + ROW_CHUNK, :], g, b)
        vn_buf[r0:r0 + ROW_CHUNK, :] = vn.astype(BF16)

    row = lax.broadcasted_iota(jnp.int32, (CHUNK, CHUNK), 0)
    col = lax.broadcasted_iota(jnp.int32, (CHUNK, CHUNK), 1)
    causal = row >= col
    for h in range(SGU_HEADS):
        c0 = h * SGU_HD
        w_h = jnp.where(causal, sw_ref[h], 0.0).astype(BF16)
        bias_h = sb_ref[:, c0:c0 + SGU_HD]
        for r0 in range(0, t_rows, CHUNK):
            sp = _dot(w_h, vn_buf[r0:r0 + CHUNK, c0:c0 + SGU_HD]) + bias_h
            y = u_buf[r0:r0 + CHUNK, c0:c0 + SGU_HD] * sp * _silu(gate_buf[r0:r0 + CHUNK, c0:c0 + SGU_HD])
            o_ref[r0:r0 + CHUNK, c0:c0 + SGU_HD] = y.astype(BF16)


def _attn_kernel(x_ref, w_ref, bias_ref, sink_ref, sinkc_ref, o_ref,
                 q_buf, gate_buf, k_tab, v_tab, code_buf, ones_buf, s_buf, p_buf, m_buf, mc_buf):
    i = pl.program_id(0)
    t_rows = x_ref.shape[0]
    n_blocks = t_rows // BLOCK

    @pl.when(i == 0)
    def _():
        zero = jnp.zeros((BLOCK, LANES), BF16)
        for j in range(N_KV_HEADS):
            for half in range(2):
                k_tab[j, half, 0:BLOCK, :] = zero
                v_tab[j, half, 0:BLOCK, :] = zero
        row = lax.broadcasted_iota(jnp.int32, (PQ, 2 * 2 * BLOCK), 0)
        col = lax.broadcasted_iota(jnp.int32, (PQ, 2 * 2 * BLOCK), 1)
        key = col & (2 * BLOCK - 1)
        u = key - (row & (BLOCK - 1)) - 1
        in_window = (u >= 0) & (u < WINDOW)
        code_buf[...] = jnp.where(in_window, jnp.where(key >= BLOCK, 2, 1), 0)
        krow = lax.broadcasted_iota(jnp.int32, (4 * BLOCK, LANES), 0)
        klane = lax.broadcasted_iota(jnp.int32, (4 * BLOCK, LANES), 1)
        ones_buf[...] = jnp.where((krow >= 2 * BLOCK) == (klane >= HEAD_DIM), 1.0, 0.0).astype(BF16)

    xb = x_ref[...]
    q_buf[...] = (_dot(xb, w_ref[:, 0:BRANCH_W]) * (HEAD_DIM ** -0.5)).astype(BF16)
    gate_buf[...] = _dot(xb, w_ref[:, BRANCH_W + 2 * KV_W:2 * BRANCH_W + 2 * KV_W])

    lane = lax.broadcasted_iota(jnp.int32, (t_rows, LANES), 1)
    low = lane < HEAD_DIM
    kv = _dot(xb, w_ref[:, BRANCH_W:BRANCH_W + 2 * KV_W])
    for tab, c0 in ((k_tab, 0), (v_tab, KV_W)):
        val = kv[:, c0:c0 + KV_W]
        swapped = pltpu.roll(val, HEAD_DIM, axis=1)
        zero = jnp.zeros_like(val)
        tab[0, 0, BLOCK:BLOCK + t_rows, :] = jnp.where(low, val, zero).astype(BF16)
        tab[0, 1, BLOCK:BLOCK + t_rows, :] = jnp.where(low, zero, swapped).astype(BF16)
        tab[1, 0, BLOCK:BLOCK + t_rows, :] = jnp.where(low, swapped, zero).astype(BF16)
        tab[1, 1, BLOCK:BLOCK + t_rows, :] = jnp.where(low, zero, val).astype(BF16)

    low128 = lax.broadcasted_iota(jnp.int32, (BLOCK, LANES), 1) < HEAD_DIM
    for g0 in range(0, n_blocks, ATTN_GROUP_BLOCKS):
        units = [(n, j) for n in range(g0, g0 + ATTN_GROUP_BLOCKS) for j in range(N_KV_HEADS)]

        for it, (n, j) in enumerate(units):
            r0 = n * BLOCK
            q_cols = j * PAIRS * LANES
            qs = jnp.concatenate(
                [q_buf[r0:r0 + BLOCK, q_cols + p * LANES:q_cols + (p + 1) * LANES] for p in range(PAIRS)], axis=0)
            k2 = jnp.concatenate([k_tab[j, 0, r0:r0 + 2 * BLOCK, :], k_tab[j, 1, r0:r0 + 2 * BLOCK, :]], axis=0)
            logits = lax.dot_general(qs, k2, (((1,), (1,)), ((), ())), preferred_element_type=F32)
            threshold = jnp.where(jnp.logical_and(i == 0, n == 0), 2, 1)
            for p in range(PAIRS):
                rows = slice(p * BLOCK, (p + 1) * BLOCK)
                row_max = []
                for par in range(2):
                    slabs = []
                    for half in range(2):
                        cols = slice((2 * par + half) * LANES, (2 * par + half + 1) * LANES)
                        s = logits[rows, cols] + bias_ref[j, rows, cols]
                        s = jnp.where(code_buf[rows, cols] >= threshold, s, NEG_INF)
                        s_buf[it, rows, cols] = s
                        slabs.append(s)
                    m = jnp.max(jnp.maximum(slabs[0], slabs[1]), axis=1, keepdims=True)
                    m = jnp.maximum(jnp.broadcast_to(m, (BLOCK, LANES)), sink_ref[j, par, rows, :])
                    m_buf[it, par, rows, :] = m
                    row_max.append(m)
                mc_buf[it, rows, :] = jnp.where(low128, row_max[0], row_max[1])

        for it, (n, j) in enumerate(units):
            for p in range(PAIRS):
                rows = slice(p * BLOCK, (p + 1) * BLOCK)
                for par in range(2):
                    m = m_buf[it, par, rows, :]
                    for half in range(2):
                        cols = slice((2 * par + half) * LANES, (2 * par + half + 1) * LANES)
                        p_buf[it, rows, cols] = jnp.exp(s_buf[it, rows, cols] - m).astype(BF16)

        for it, (n, j) in enumerate(units):
            r0 = n * BLOCK
            q_cols = j * PAIRS * LANES
            v2 = jnp.concatenate([v_tab[j, 0, r0:r0 + 2 * BLOCK, :], v_tab[j, 1, r0:r0 + 2 * BLOCK, :]], axis=0)
            o = _dot(p_buf[it], jnp.concatenate([v2, ones_buf[...]], axis=1))
            for p in range(PAIRS):
                rows = slice(p * BLOCK, (p + 1) * BLOCK)
                cols = slice(q_cols + p * LANES, q_cols + (p + 1) * LANES)
                den = o[rows, LANES:2 * LANES] + jnp.exp(sinkc_ref[j, rows, :] - mc_buf[it, rows, :])
                y = o[rows, 0:LANES] * (1.0 / den) * _silu(gate_buf[r0:r0 + BLOCK, cols])
                o_ref[r0:r0 + BLOCK, cols] = y.astype(BF16)

    for j in range(N_KV_HEADS):
        for half in range(2):
            k_tab[j, half, 0:BLOCK, :] = k_tab[j, half, t_rows:t_rows + BLOCK, :]
            v_tab[j, half, 0:BLOCK, :] = v_tab[j, half, t_rows:t_rows + BLOCK, :]


CONV_COLUMNS = BRANCH_W // LANES
CONV_BLOCKS = BRANCH_W // PROJ_COLS


def _conv_kernel(x_ref, w_ref, cw_ref, cb_ref, lng_ref, lnb_ref, o_ref,
                 xs_buf, glu_buf, gate_buf, shift_buf, y_buf):
    i = pl.program_id(0)
    t_rows = x_ref.shape[0]

    @pl.when(i == 0)
    def _():
        for c in range(CONV_COLUMNS):
            glu_buf[c, 0:CONV_HALO, :] = jnp.zeros((CONV_HALO, LANES), F32)

    xs_buf[...] = x_ref[...]

    def project(n, carry):
        glu = _dot(xs_buf[...], w_ref[n]) * _sigmoid(_dot(xs_buf[...], w_ref[CONV_BLOCKS + n]))
        glu_buf[2 * n, CONV_HALO:CONV_HALO + t_rows, :] = glu[:, 0:LANES]
        glu_buf[2 * n + 1, CONV_HALO:CONV_HALO + t_rows, :] = glu[:, LANES:2 * LANES]
        gate = _dot(xs_buf[...], w_ref[2 * CONV_BLOCKS + n])
        gate_buf[2 * n] = gate[:, 0:LANES]
        gate_buf[2 * n + 1] = gate[:, LANES:2 * LANES]
        return carry

    lax.fori_loop(0, CONV_BLOCKS, project, 0)

    first = CONV_HALO - (CONV_K - 1)
    span = t_rows + CONV_HALO - SUBLANES

    def conv(c, carry):
        for r in range(1, SUBLANES):
            shift_buf[r, 0:span, :] = glu_buf[c, r:r + span, :]
        bias = jnp.broadcast_to(cb_ref[c], (CONV_ROWS, LANES))
        for r0 in range(0, t_rows, CONV_ROWS):
            acc = bias
            for k in range(CONV_K):
                a8 = (first + k) // SUBLANES * SUBLANES
                r = (first + k) % SUBLANES
                if r == 0:
                    src = glu_buf[c, r0 + a8:r0 + a8 + CONV_ROWS, :]
                else:
                    src = shift_buf[r, r0 + a8:r0 + a8 + CONV_ROWS, :]
                tap = jnp.concatenate([cw_ref[c, k]] * (CONV_ROWS // SUBLANES), axis=0)
                acc = acc + tap * src
            y_buf[c, r0:r0 + CONV_ROWS, :] = acc
        return carry

    lax.fori_loop(0, CONV_COLUMNS, conv, 0)

    for r0 in range(0, t_rows, ROW_CHUNK):
        rows = slice(r0, r0 + ROW_CHUNK)
        ys = [y_buf[c, rows, :] for c in range(CONV_COLUMNS)]
        total = ys[0]
        for y in ys[1:]:
            total = total + y
        mu = jnp.sum(total, axis=1, keepdims=True) * (1.0 / BRANCH_W)
        ds = [y - mu for y in ys]
        sq = ds[0] * ds[0]
        for d in ds[1:]:
            sq = sq + d * d
        rstd = lax.rsqrt(jnp.sum(sq, axis=1, keepdims=True) * (1.0 / BRANCH_W) + LN_EPS)
        for c in range(CONV_COLUMNS):
            y = _silu(ds[c] * rstd * lng_ref[c] + lnb_ref[c]) * _silu(gate_buf[c, rows, :])
            o_ref[rows, c * LANES:(c + 1) * LANES] = y.astype(BF16)

    for c in range(CONV_COLUMNS):
        glu_buf[c, 0:CONV_HALO, :] = glu_buf[c, t_rows:t_rows + CONV_HALO, :]


def _merge_kernel(x_ref, ya_ref, yb_ref, yc_ref, yd_ref, wga_ref, wgb_ref, wgc_ref, wgd_ref, wb_ref, o_ref):
    xb = x_ref[...]
    acc = None
    branches = ((ya_ref, wga_ref), (yb_ref, wgb_ref), (yc_ref, wgc_ref), (yd_ref, wgd_ref))
    for br, (y_ref, wg_ref) in enumerate(branches):
        gate = _sigmoid(_dot(xb, wg_ref[...]))
        term = gate * _dot(y_ref[...], wb_ref[br])
        acc = term if acc is None else acc + term
    o_ref[...] = acc.astype(BF16)


def _out_kernel(m_ref, w_ref, x_ref, g_ref, b_ref, o_ref, ob_ref, z_buf):
    t_rows = x_ref.shape[0]
    z_buf[...] = _dot(m_ref[...], w_ref[...])
    g = g_ref[...]
    b = b_ref[...]
    for r0 in range(0, t_rows, ROW_CHUNK // 2):
        rows = slice(r0, r0 + ROW_CHUNK // 2)
        y = _layer_norm_rows(ALPHA * x_ref[rows, :] + z_buf[rows, :], g, b)
        o_ref[rows, :] = y
        ob_ref[rows, :] = y.astype(BF16)


def _resident(shape):
    zeros = (0,) * len(shape)
    return pl.BlockSpec(shape, lambda *_: zeros, pipeline_mode=pl.Buffered(1))


def _params():
    return pltpu.CompilerParams(dimension_semantics=("arbitrary",), vmem_limit_bytes=VMEM_LIMIT_BYTES)


def _row_tile(width):
    return pl.BlockSpec((SEQ_TILE, width), lambda i: (i, 0))


def _layer_weight(w_all, layer):
    idx = (layer,) + (0,) * (w_all.ndim - 1)
    return pl.BlockSpec((None,) + w_all.shape[1:], lambda *_: idx, pipeline_mode=pl.Buffered(1))


def _branch_call(kernel_fn, name, seq, x_bf, w_all, layer, operands, scratch):
    return pl.pallas_call(
        kernel_fn,
        grid=(seq // SEQ_TILE,),
        in_specs=[_row_tile(D_MODEL), _layer_weight(w_all, layer)] + [_resident(op.shape) for op in operands],
        out_specs=_row_tile(BRANCH_W),
        out_shape=jax.ShapeDtypeStruct((seq, BRANCH_W), BF16),
        scratch_shapes=scratch,
        compiler_params=_params(),
        name=name,
    )(x_bf, w_all, *operands)


def _layer(x, x_bf, band, w_parts, layer, pool_w, pool_scale, sgu_ln_g, sgu_ln_b, sgu_w, sgu_b, sinks,
           conv_w, conv_b, conv_ln_g, conv_ln_b, w_branch, w_out, ln_g, ln_b):
    seq = x.shape[0]
    t = SEQ_TILE
    w_pool, w_sgu, w_attn, w_gates, w_conv = w_parts
    row = lambda v: v.reshape(1, -1).astype(F32)

    y_a = _branch_call(
        _pool_kernel, "branch_pool", seq, x_bf, w_pool, layer,
        [pool_w.astype(BF16), row(pool_scale)],
        [pltpu.VMEM((t + POOL_HALO, BRANCH_W), F32), pltpu.VMEM((t, BRANCH_W), F32),
         pltpu.VMEM((t, BRANCH_W), BF16)])

    sgu_bias = jnp.repeat(jnp.transpose(sgu_b).astype(F32), SGU_HD, axis=1)
    y_b = _branch_call(
        _sgu_kernel, "branch_sgu", seq, x_bf, w_sgu, layer,
        [row(sgu_ln_g), row(sgu_ln_b), sgu_w.astype(F32), sgu_bias],
        [pltpu.VMEM((t, BRANCH_W), F32), pltpu.VMEM((t, BRANCH_W), F32),
         pltpu.VMEM((t, BRANCH_W), F32), pltpu.VMEM((t, BRANCH_W), BF16)])

    sink_hp = sinks.astype(F32).reshape(N_KV_HEADS, PAIRS, 2)
    sink_rows = jnp.repeat(jnp.transpose(sink_hp, (0, 2, 1)), BLOCK, axis=2)
    sink_rows = jnp.broadcast_to(sink_rows[..., None], (N_KV_HEADS, 2, PQ, LANES))
    sink_lanes = jnp.repeat(jnp.repeat(sink_hp, BLOCK, axis=1), HEAD_DIM, axis=2)
    n_units = N_KV_HEADS * ATTN_GROUP_BLOCKS
    y_c = _branch_call(
        _attn_kernel, "branch_attn", seq, x_bf, w_attn, layer,
        [band, sink_rows, sink_lanes],
        [pltpu.VMEM((t, BRANCH_W), BF16), pltpu.VMEM((t, BRANCH_W), F32),
         pltpu.VMEM((N_KV_HEADS, 2, t + BLOCK, LANES), BF16),
         pltpu.VMEM((N_KV_HEADS, 2, t + BLOCK, LANES), BF16),
         pltpu.VMEM((PQ, 4 * BLOCK), jnp.int32),
         pltpu.VMEM((4 * BLOCK, LANES), BF16),
         pltpu.VMEM((n_units, PQ, 4 * BLOCK), F32),
         pltpu.VMEM((n_units, PQ, 4 * BLOCK), BF16),
         pltpu.VMEM((n_units, 2, PQ, LANES), F32),
         pltpu.VMEM((n_units, PQ, LANES), F32)])

    per_col = lambda v: v.astype(F32).reshape(CONV_COLUMNS, 1, LANES)
    taps = jnp.transpose(conv_w.astype(F32).reshape(CONV_K, CONV_COLUMNS, LANES), (1, 0, 2))
    taps = jnp.broadcast_to(taps[:, :, None, :], (CONV_COLUMNS, CONV_K, SUBLANES, LANES))
    y_d = _branch_call(
        _conv_kernel, "branch_conv", seq, x_bf, w_conv, layer,
        [taps, per_col(conv_b), per_col(conv_ln_g), per_col(conv_ln_b)],
        [pltpu.VMEM((t, D_MODEL), BF16),
         pltpu.VMEM((CONV_COLUMNS, t + CONV_HALO, LANES), F32),
         pltpu.VMEM((CONV_COLUMNS, t, LANES), F32),
         pltpu.VMEM((SUBLANES, t + CONV_HALO - SUBLANES, LANES), F32),
         pltpu.VMEM((CONV_COLUMNS, t, LANES), F32)])

    n_cb = D_MODEL // MERGE_COLS
    wg = w_gates
    wb = w_branch.astype(BF16)
    y_spec = pl.BlockSpec((t, BRANCH_W), lambda c, i: (i, 0))
    wg_specs = [pl.BlockSpec((None, D_MODEL, MERGE_COLS),
                             functools.partial(lambda br, c, i: (layer, 0, br * n_cb + c), br))
                for br in range(N_BRANCH)]
    merged = pl.pallas_call(
        _merge_kernel,
        grid=(n_cb, seq // t),
        in_specs=[pl.BlockSpec((t, D_MODEL), lambda c, i: (i, 0)), y_spec, y_spec, y_spec, y_spec,
                  *wg_specs,
                  pl.BlockSpec((N_BRANCH, BRANCH_W, MERGE_COLS), lambda c, i: (0, 0, c))],
        out_specs=pl.BlockSpec((t, MERGE_COLS), lambda c, i: (i, c)),
        out_shape=jax.ShapeDtypeStruct((seq, D_MODEL), BF16),
        compiler_params=pltpu.CompilerParams(dimension_semantics=("arbitrary", "arbitrary"),
                                             vmem_limit_bytes=VMEM_LIMIT_BYTES),
        name="merge",
    )(x_bf, y_a, y_b, y_c, y_d, wg, wg, wg, wg, wb)

    x_next, x_next_bf = pl.pallas_call(
        _out_kernel,
        grid=(seq // t,),
        in_specs=[_row_tile(D_MODEL), _resident((D_MODEL, D_MODEL)), _row_tile(D_MODEL),
                  _resident((1, D_MODEL)), _resident((1, D_MODEL))],
        out_specs=[_row_tile(D_MODEL), _row_tile(D_MODEL)],
        out_shape=[jax.ShapeDtypeStruct((seq, D_MODEL), F32), jax.ShapeDtypeStruct((seq, D_MODEL), BF16)],
        scratch_shapes=[pltpu.VMEM((t, D_MODEL), F32)],
        compiler_params=_params(),
        name="out_norm",
    )(merged, w_out.astype(BF16), x, row(ln_g), row(ln_b))
    return x_next, x_next_bf


def kernel(x, w_in, pool_w, pool_scale, sgu_ln_g, sgu_ln_b, sgu_w, sgu_b, attn_sinks, rel_bias, conv_w,
           conv_b, conv_ln_g, conv_ln_b, w_branch, w_out, ln_g, ln_b):
    batch, seq, d_model = x.shape
    assert d_model == D_MODEL and w_in.shape == (DEPTH, D_MODEL, D_IN)
    assert seq % SEQ_TILE == 0 and SEQ_TILE % (BLOCK * ATTN_GROUP_BLOCKS) == 0

    band = _band_bias(rel_bias).reshape(N_KV_HEADS, PAIRS, 2, BLOCK, 2 * BLOCK)
    band = jnp.transpose(band, (0, 1, 3, 2, 4)).reshape(N_KV_HEADS, PQ, 4 * BLOCK)

    w_pool, w_sgu, w_attn, w_gates, w_conv = _cast_w_in(w_in.astype(F32))
    w_parts = (w_pool, w_sgu, w_attn, w_gates, w_conv)
    outs = []
    for bi in range(batch):
        xc = x[bi].astype(F32)
        xc_bf = xc.astype(BF16)
        for l in range(DEPTH):
            xc, xc_bf = _layer(xc, xc_bf, band, w_parts, l, pool_w[l], pool_scale[l], sgu_ln_g[l], sgu_ln_b[l],
                               sgu_w[l], sgu_b[l], attn_sinks[l], conv_w[l], conv_b[l], conv_ln_g[l],
                               conv_ln_b[l], w_branch[l], w_out[l], ln_g[l], ln_b[l])
        outs.append(xc.astype(x.dtype)[None])
    return outs[0] if batch == 1 else jnp.concatenate(outs, axis=0)
```

```python
import functools
import math

import jax
import jax.numpy as jnp
from jax import lax
from jax.experimental import pallas as pl
from jax.experimental.pallas import tpu as pltpu

F32 = jnp.float32
BF16 = jnp.bfloat16

D_MODEL = 2048
DEPTH = 2
N_BRANCH = 4
BRANCH_W = 1024
POOL_WINDOWS = (2, 4, 8, 16)
POOL_GW = BRANCH_W // len(POOL_WINDOWS)
SGU_HEADS = 8
SGU_HD = BRANCH_W // SGU_HEADS
CHUNK = 128
N_Q_HEADS = 16
N_KV_HEADS = 2
Q_PER_KV = N_Q_HEADS // N_KV_HEADS
HEAD_DIM = 64
KV_W = N_KV_HEADS * HEAD_DIM
WINDOW = 128
BLOCK = 128
NUM_BUCKETS = 32
MAX_DISTANCE = 128
CONV_K = 31
ALPHA = (2 * DEPTH) ** 0.25
LN_EPS = 1e-5
NEG_INF = -1e30

OFF_A = 0
OFF_B = OFF_A + 2 * BRANCH_W
OFF_C = OFF_B + 3 * BRANCH_W
OFF_D = OFF_C + 2 * BRANCH_W + 2 * KV_W
OFF_G = OFF_D + 3 * BRANCH_W
D_IN = OFF_G + N_BRANCH * D_MODEL

LANES = 128
SUBLANES = 8
SEQ_TILE = 512
POOL_HALO = 16
CONV_HALO = 32
ROW_CHUNK = 64
CONV_ROWS = 32
PROJ_COLS = 256
MERGE_COLS = 512
PAIRS = Q_PER_KV // 2
PQ = PAIRS * BLOCK
ATTN_GROUP_BLOCKS = 2
MERGE_TILE = 1024
V7X_VMEM_BYTES = 64 * 1024 * 1024


def _sigmoid(x):
    return 0.5 * jnp.tanh(0.5 * x) + 0.5


def _silu(x):
    return x * _sigmoid(x)


def _layer_norm_rows(v, g, b):
    mu = jnp.mean(v, axis=-1, keepdims=True)
    d = v - mu
    var = jnp.mean(d * d, axis=-1, keepdims=True)
    return d * lax.rsqrt(var + LN_EPS) * g + b


def _dot(a, b):
    return jnp.dot(a, b, preferred_element_type=F32)


def _t5_bucket(n):
    max_exact = NUM_BUCKETS // 2
    nf = jnp.maximum(n, 1).astype(F32)
    large = max_exact + (jnp.log(nf / max_exact) / math.log(MAX_DISTANCE / max_exact)
                         * (NUM_BUCKETS - max_exact)).astype(jnp.int32)
    large = jnp.minimum(large, NUM_BUCKETS - 1)
    return jnp.where(n < max_exact, n, large)


def _band_bucket_ids():
    i = jnp.arange(BLOCK)[:, None]
    j = jnp.arange(2 * BLOCK)[None, :]
    return _t5_bucket(jnp.clip(i + BLOCK - j, 0, WINDOW - 1)).astype(jnp.int32)


def _bias_kernel(bucket_ref, table_ref, o_ref):
    h = pl.program_id(0)
    bucket = bucket_ref[...]
    acc = jnp.zeros(bucket.shape, F32)
    for b in range(NUM_BUCKETS):
        acc = jnp.where(bucket == b, table_ref[b, h], acc)
    o_ref[0] = acc


def _band_bias(rel_bias):
    return pl.pallas_call(
        _bias_kernel,
        grid=(N_Q_HEADS,),
        in_specs=[pl.BlockSpec((BLOCK, 2 * BLOCK), lambda h: (0, 0)),
                  pl.BlockSpec(memory_space=pltpu.SMEM)],
        out_specs=pl.BlockSpec((1, BLOCK, 2 * BLOCK), lambda h: (h, 0, 0)),
        out_shape=jax.ShapeDtypeStruct((N_Q_HEADS, BLOCK, 2 * BLOCK), F32),
        name="band_bias",
    )(_band_bucket_ids(), rel_bias.astype(F32))


W_IN_SPLITS = ((OFF_A, OFF_B), (OFF_B, OFF_C), (OFF_C, OFF_D), (OFF_G, D_IN))
CONV_W_CHUNKS = (OFF_G - OFF_D) // PROJ_COLS
CAST_ROWS = 128


def _cast_w_in_kernel(w_ref, oa_ref, ob_ref, oc_ref, og_ref, od_ref):
    for (a, b), o_ref in zip(W_IN_SPLITS, (oa_ref, ob_ref, oc_ref, og_ref)):
        o_ref[...] = w_ref[:, a:b].astype(BF16)
    for j in range(CONV_W_CHUNKS):
        od_ref[j] = w_ref[:, OFF_D + j * PROJ_COLS:OFF_D + (j + 1) * PROJ_COLS].astype(BF16)


def _cast_w_in(w_in):
    widths = [b - a for a, b in W_IN_SPLITS]
    outputs = [(jax.ShapeDtypeStruct((DEPTH, D_MODEL, w), BF16),
                pl.BlockSpec((None, CAST_ROWS, w), lambda l, r: (l, r, 0))) for w in widths]
    outputs.append((jax.ShapeDtypeStruct((DEPTH, CONV_W_CHUNKS, D_MODEL, PROJ_COLS), BF16),
                    pl.BlockSpec((None, CONV_W_CHUNKS, CAST_ROWS, PROJ_COLS), lambda l, r: (l, 0, r, 0))))
    return _call(_cast_w_in_kernel, "cast_w_in", (DEPTH, D_MODEL // CAST_ROWS),
                 [(w_in, pl.BlockSpec((None, CAST_ROWS, D_IN), lambda l, r: (l, r, 0)))],
                 outputs, [], value_bytes=0)


def _pool_kernel(x_ref, w_ref, pw_ref, sc_ref, o_ref, abuf, gate_buf, mix_buf):
    i = pl.program_id(0)
    t_rows = x_ref.shape[0]

    @pl.when(i == 0)
    def _():
        abuf[0:POOL_HALO, :] = jnp.zeros((POOL_HALO, BRANCH_W), F32)

    xb = x_ref[...].astype(BF16)
    abuf[POOL_HALO:POOL_HALO + t_rows, :] = _dot(xb, w_ref[:, 0:BRANCH_W])
    gate_buf[...] = _dot(xb, w_ref[:, BRANCH_W:2 * BRANCH_W])

    for g, win in enumerate(POOL_WINDOWS):
        c0 = g * POOL_GW
        for r0 in range(0, t_rows, ROW_CHUNK):
            base = POOL_HALO + r0
            cur = abuf[base:base + ROW_CHUNK, c0:c0 + POOL_GW]
            acc = cur
            for s in range(1, win):
                acc = acc + abuf[base - s:base - s + ROW_CHUNK, c0:c0 + POOL_GW]
            t = i * t_rows + r0 + lax.broadcasted_iota(jnp.int32, (ROW_CHUNK, POOL_GW), 0)
            cnt = jnp.minimum(t + 1, win).astype(F32)
            mix_buf[r0:r0 + ROW_CHUNK, c0:c0 + POOL_GW] = (acc / cnt - cur).astype(BF16)

    for g in range(len(POOL_WINDOWS)):
        c0 = g * POOL_GW
        y = _dot(mix_buf[:, c0:c0 + POOL_GW], pw_ref[g])
        y = y * sc_ref[:, c0:c0 + POOL_GW] * _silu(gate_buf[:, c0:c0 + POOL_GW])
        o_ref[:, c0:c0 + POOL_GW] = y.astype(BF16)

    abuf[0:POOL_HALO, :] = abuf[t_rows:t_rows + POOL_HALO, :]


def _sgu_kernel(x_ref, w_ref, lng_ref, lnb_ref, sw_ref, sb_ref, o_ref,
                u_buf, v_buf, gate_buf, vn_buf):
    t_rows = x_ref.shape[0]
    xb = x_ref[...].astype(BF16)
    u_buf[...] = _dot(xb, w_ref[:, 0:BRANCH_W])
    v_buf[...] = _dot(xb, w_ref[:, BRANCH_W:2 * BRANCH_W])
    gate_buf[...] = _dot(xb, w_ref[:, 2 * BRANCH_W:3 * BRANCH_W])

    g = lng_ref[...]
    b = lnb_ref[...]
    for r0 in range(0, t_rows, ROW_CHUNK):
        vn = _layer_norm_rows(v_buf[r0:r0 + ROW_CHUNK, :], g, b)
        vn_buf[r0:r0 + ROW_CHUNK, :] = vn.astype(BF16)

    row = lax.broadcasted_iota(jnp.int32, (CHUNK, CHUNK), 0)
    col = lax.broadcasted_iota(jnp.int32, (CHUNK, CHUNK), 1)
    causal = row >= col
    for h in range(SGU_HEADS):
        c0 = h * SGU_HD
        w_h = jnp.where(causal, sw_ref[h], 0.0).astype(BF16)
        bias_h = sb_ref[:, c0:c0 + SGU_HD]
        for r0 in range(0, t_rows, CHUNK):
            sp = _dot(w_h, vn_buf[r0:r0 + CHUNK, c0:c0 + SGU_HD]) + bias_h
            y = u_buf[r0:r0 + CHUNK, c0:c0 + SGU_HD] * sp * _silu(gate_buf[r0:r0 + CHUNK, c0:c0 + SGU_HD])
            o_ref[r0:r0 + CHUNK, c0:c0 + SGU_HD] = y.astype(BF16)


def _attn_kernel(x_ref, w_ref, bias_ref, sink_ref, sinkc_ref, o_ref,
                 q_buf, gate_buf, k_tab, v_tab, code_buf, ones_buf, s_buf, p_buf, m_buf, mc_buf):
    i = pl.program_id(0)
    t_rows = x_ref.shape[0]
    n_blocks = t_rows // BLOCK

    @pl.when(i == 0)
    def _():
        zero = jnp.zeros((BLOCK, LANES), BF16)
        for j in range(N_KV_HEADS):
            for half in range(2):
                k_tab[j, half, 0:BLOCK, :] = zero
                v_tab[j, half, 0:BLOCK, :] = zero
        row = lax.broadcasted_iota(jnp.int32, (PQ, 2 * 2 * BLOCK), 0)
        col = lax.broadcasted_iota(jnp.int32, (PQ, 2 * 2 * BLOCK), 1)
        key = col & (2 * BLOCK - 1)
        u = key - (row & (BLOCK - 1)) - 1
        in_window = (u >= 0) & (u < WINDOW)
        code_buf[...] = jnp.where(in_window, jnp.where(key >= BLOCK, 2, 1), 0)
        krow = lax.broadcasted_iota(jnp.int32, (4 * BLOCK, LANES), 0)
        klane = lax.broadcasted_iota(jnp.int32, (4 * BLOCK, LANES), 1)
        ones_buf[...] = jnp.where((krow >= 2 * BLOCK) == (klane >= HEAD_DIM), 1.0, 0.0).astype(BF16)

    xb = x_ref[...].astype(BF16)
    q_buf[...] = (_dot(xb, w_ref[:, 0:BRANCH_W]) * (HEAD_DIM ** -0.5)).astype(BF16)
    gate_buf[...] = _dot(xb, w_ref[:, BRANCH_W + 2 * KV_W:2 * BRANCH_W + 2 * KV_W])

    lane = lax.broadcasted_iota(jnp.int32, (t_rows, LANES), 1)
    low = lane < HEAD_DIM
    kv = _dot(xb, w_ref[:, BRANCH_W:BRANCH_W + 2 * KV_W])
    for tab, c0 in ((k_tab, 0), (v_tab, KV_W)):
        val = kv[:, c0:c0 + KV_W]
        swapped = pltpu.roll(val, HEAD_DIM, axis=1)
        zero = jnp.zeros_like(val)
        tab[0, 0, BLOCK:BLOCK + t_rows, :] = jnp.where(low, val, zero).astype(BF16)
        tab[0, 1, BLOCK:BLOCK + t_rows, :] = jnp.where(low, zero, swapped).astype(BF16)
        tab[1, 0, BLOCK:BLOCK + t_rows, :] = jnp.where(low, swapped, zero).astype(BF16)
        tab[1, 1, BLOCK:BLOCK + t_rows, :] = jnp.where(low, zero, val).astype(BF16)

    low128 = lax.broadcasted_iota(jnp.int32, (BLOCK, LANES), 1) < HEAD_DIM
    for g0 in range(0, n_blocks, ATTN_GROUP_BLOCKS):
        units = [(n, j) for n in range(g0, g0 + ATTN_GROUP_BLOCKS) for j in range(N_KV_HEADS)]

        for it, (n, j) in enumerate(units):
            r0 = n * BLOCK
            q_cols = j * PAIRS * LANES
            qs = jnp.concatenate(
                [q_buf[r0:r0 + BLOCK, q_cols + p * LANES:q_cols + (p + 1) * LANES] for p in range(PAIRS)], axis=0)
            k2 = jnp.concatenate([k_tab[j, 0, r0:r0 + 2 * BLOCK, :], k_tab[j, 1, r0:r0 + 2 * BLOCK, :]], axis=0)
            logits = lax.dot_general(qs, k2, (((1,), (1,)), ((), ())), preferred_element_type=F32)
            threshold = jnp.where(jnp.logical_and(i == 0, n == 0), 2, 1)
            for p in range(PAIRS):
                rows = slice(p * BLOCK, (p + 1) * BLOCK)
                row_max = []
                for par in range(2):
                    slabs = []
                    for half in range(2):
                        cols = slice((2 * par + half) * LANES, (2 * par + half + 1) * LANES)
                        s = logits[rows, cols] + bias_ref[j, rows, cols]
                        s = jnp.where(code_buf[rows, cols] >= threshold, s, NEG_INF)
                        s_buf[it, rows, cols] = s
                        slabs.append(s)
                    m = jnp.max(jnp.maximum(slabs[0], slabs[1]), axis=1, keepdims=True)
                    m = jnp.maximum(jnp.broadcast_to(m, (BLOCK, LANES)), sink_ref[j, par, rows, :])
                    m_buf[it, par, rows, :] = m
                    row_max.append(m)
                mc_buf[it, rows, :] = jnp.where(low128, row_max[0], row_max[1])

        for it, (n, j) in enumerate(units):
            for p in range(PAIRS):
                rows = slice(p * BLOCK, (p + 1) * BLOCK)
                for par in range(2):
                    m = m_buf[it, par, rows, :]
                    for half in range(2):
                        cols = slice((2 * par + half) * LANES, (2 * par + half + 1) * LANES)
                        p_buf[it, rows, cols] = jnp.exp(s_buf[it, rows, cols] - m).astype(BF16)

        for it, (n, j) in enumerate(units):
            r0 = n * BLOCK
            q_cols = j * PAIRS * LANES
            v2 = jnp.concatenate([v_tab[j, 0, r0:r0 + 2 * BLOCK, :], v_tab[j, 1, r0:r0 + 2 * BLOCK, :]], axis=0)
            o = _dot(p_buf[it], jnp.concatenate([v2, ones_buf[...]], axis=1))
            for p in range(PAIRS):
                rows = slice(p * BLOCK, (p + 1) * BLOCK)
                cols = slice(q_cols + p * LANES, q_cols + (p + 1) * LANES)
                den = o[rows, LANES:2 * LANES] + jnp.exp(sinkc_ref[j, rows, :] - mc_buf[it, rows, :])
                y = o[rows, 0:LANES] * (1.0 / den) * _silu(gate_buf[r0:r0 + BLOCK, cols])
                o_ref[r0:r0 + BLOCK, cols] = y.astype(BF16)

    for j in range(N_KV_HEADS):
        for half in range(2):
            k_tab[j, half, 0:BLOCK, :] = k_tab[j, half, t_rows:t_rows + BLOCK, :]
            v_tab[j, half, 0:BLOCK, :] = v_tab[j, half, t_rows:t_rows + BLOCK, :]


CONV_COLUMNS = BRANCH_W // LANES
CONV_BLOCKS = BRANCH_W // PROJ_COLS


def _conv_kernel(x_ref, w_ref, cw_ref, cb_ref, lng_ref, lnb_ref, o_ref,
                 xs_buf, glu_buf, gate_buf, shift_buf, y_buf):
    i = pl.program_id(0)
    t_rows = x_ref.shape[0]

    @pl.when(i == 0)
    def _():
        for c in range(CONV_COLUMNS):
            glu_buf[c, 0:CONV_HALO, :] = jnp.zeros((CONV_HALO, LANES), F32)

    xs_buf[...] = x_ref[...].astype(BF16)

    def project(n, carry):
        glu = _dot(xs_buf[...], w_ref[n]) * _sigmoid(_dot(xs_buf[...], w_ref[CONV_BLOCKS + n]))
        glu_buf[2 * n, CONV_HALO:CONV_HALO + t_rows, :] = glu[:, 0:LANES]
        glu_buf[2 * n + 1, CONV_HALO:CONV_HALO + t_rows, :] = glu[:, LANES:2 * LANES]
        gate = _dot(xs_buf[...], w_ref[2 * CONV_BLOCKS + n])
        gate_buf[2 * n] = gate[:, 0:LANES]
        gate_buf[2 * n + 1] = gate[:, LANES:2 * LANES]
        return carry

    lax.fori_loop(0, CONV_BLOCKS, project, 0)

    first = CONV_HALO - (CONV_K - 1)
    span = t_rows + CONV_HALO - SUBLANES

    def conv(c, carry):
        for r in range(1, SUBLANES):
            shift_buf[r, 0:span, :] = glu_buf[c, r:r + span, :]
        bias = jnp.broadcast_to(cb_ref[c], (CONV_ROWS, LANES))
        for r0 in range(0, t_rows, CONV_ROWS):
            acc = bias
            for k in range(CONV_K):
                a8 = (first + k) // SUBLANES * SUBLANES
                r = (first + k) % SUBLANES
                if r == 0:
                    src = glu_buf[c, r0 + a8:r0 + a8 + CONV_ROWS, :]
                else:
                    src = shift_buf[r, r0 + a8:r0 + a8 + CONV_ROWS, :]
                tap = jnp.concatenate([cw_ref[c, k]] * (CONV_ROWS // SUBLANES), axis=0)
                acc = acc + tap * src
            y_buf[c, r0:r0 + CONV_ROWS, :] = acc
        return carry

    lax.fori_loop(0, CONV_COLUMNS, conv, 0)

    for r0 in range(0, t_rows, ROW_CHUNK):
        rows = slice(r0, r0 + ROW_CHUNK)
        ys = [y_buf[c, rows, :] for c in range(CONV_COLUMNS)]
        total = ys[0]
        for y in ys[1:]:
            total = total + y
        mu = jnp.sum(total, axis=1, keepdims=True) * (1.0 / BRANCH_W)
        ds = [y - mu for y in ys]
        sq = ds[0] * ds[0]
        for d in ds[1:]:
            sq = sq + d * d
        rstd = lax.rsqrt(jnp.sum(sq, axis=1, keepdims=True) * (1.0 / BRANCH_W) + LN_EPS)
        for c in range(CONV_COLUMNS):
            y = _silu(ds[c] * rstd * lng_ref[c] + lnb_ref[c]) * _silu(gate_buf[c, rows, :])
            o_ref[rows, c * LANES:(c + 1) * LANES] = y.astype(BF16)

    for c in range(CONV_COLUMNS):
        glu_buf[c, 0:CONV_HALO, :] = glu_buf[c, t_rows:t_rows + CONV_HALO, :]


def _merge_kernel(x_ref, ya_ref, yb_ref, yc_ref, yd_ref, wga_ref, wgb_ref, wgc_ref, wgd_ref, wb_ref, o_ref):
    xb = x_ref[...].astype(BF16)
    acc = None
    branches = ((ya_ref, wga_ref), (yb_ref, wgb_ref), (yc_ref, wgc_ref), (yd_ref, wgd_ref))
    for br, (y_ref, wg_ref) in enumerate(branches):
        gate = _sigmoid(_dot(xb, wg_ref[...]))
        term = gate * _dot(y_ref[...], wb_ref[br])
        acc = term if acc is None else acc + term
    o_ref[...] = acc.astype(BF16)


def _out_kernel(m_ref, w_ref, x_ref, g_ref, b_ref, o_ref, ob_ref, z_buf):
    t_rows = x_ref.shape[0]
    z_buf[...] = _dot(m_ref[...], w_ref[...])
    g = g_ref[...]
    b = b_ref[...]
    for r0 in range(0, t_rows, ROW_CHUNK // 2):
        rows = slice(r0, r0 + ROW_CHUNK // 2)
        y = _layer_norm_rows(ALPHA * x_ref[rows, :] + z_buf[rows, :], g, b)
        o_ref[rows, :] = y
        ob_ref[rows, :] = y.astype(BF16)


def _resident(shape):
    zeros = (0,) * len(shape)
    return pl.BlockSpec(shape, lambda *_: zeros, pipeline_mode=pl.Buffered(1))


def _nbytes(shape, dtype):
    return math.prod(d for d in shape if d is not None) * jnp.dtype(dtype).itemsize


def _call(kernel_fn, name, grid, inputs, outputs, scratch, value_bytes):
    arrays, in_specs = zip(*inputs)
    out_shape, out_specs = zip(*outputs)
    windows = [(spec, a.dtype) for a, spec in inputs] + [(spec, o.dtype) for o, spec in outputs]
    need = value_bytes + sum(_nbytes(v.shape, v.dtype) for v in scratch)
    for spec, dtype in windows:
        buffers = spec.pipeline_mode.buffer_count if spec.pipeline_mode is not None else 2
        need += buffers * _nbytes(spec.block_shape, dtype)
    assert need <= V7X_VMEM_BYTES, (name, need)
    return pl.pallas_call(
        kernel_fn,
        grid=grid,
        in_specs=list(in_specs),
        out_specs=list(out_specs) if len(out_specs) > 1 else out_specs[0],
        out_shape=list(out_shape) if len(out_shape) > 1 else out_shape[0],
        scratch_shapes=scratch,
        compiler_params=pltpu.CompilerParams(dimension_semantics=("arbitrary",) * len(grid),
                                             vmem_limit_bytes=need),
        name=name,
    )(*arrays)


def _row_tile(width, rows=SEQ_TILE):
    return pl.BlockSpec((rows, width), lambda i: (i, 0))


def _layer_weight(w_all, layer):
    idx = (layer,) + (0,) * (w_all.ndim - 1)
    return pl.BlockSpec((None,) + w_all.shape[1:], lambda *_: idx, pipeline_mode=pl.Buffered(1))


def _branch_call(kernel_fn, name, seq, x_in, w_all, layer, operands, scratch):
    value_bytes = 2 * _nbytes((SEQ_TILE, BRANCH_W), F32)
    inputs = [(x_in, _row_tile(D_MODEL)), (w_all, _layer_weight(w_all, layer))]
    inputs += [(op, _resident(op.shape)) for op in operands]
    outputs = [(jax.ShapeDtypeStruct((seq, BRANCH_W), BF16), _row_tile(BRANCH_W))]
    return _call(kernel_fn, name, (seq // SEQ_TILE,), inputs, outputs, scratch, value_bytes)


def _layer(x, x_in, band, w_parts, layer, pool_w, pool_scale, sgu_ln_g, sgu_ln_b, sgu_w, sgu_b, sinks,
           conv_w, conv_b, conv_ln_g, conv_ln_b, w_branch16, w_out16, ln_g, ln_b):
    seq = x.shape[0]
    t = SEQ_TILE
    w_pool, w_sgu, w_attn, w_gates, w_conv = w_parts
    row = lambda v: v.reshape(1, -1).astype(F32)

    y_a = _branch_call(
        _pool_kernel, "branch_pool", seq, x_in,w_pool, layer,
        [pool_w.astype(BF16), row(pool_scale)],
        [pltpu.VMEM((t + POOL_HALO, BRANCH_W), F32), pltpu.VMEM((t, BRANCH_W), F32),
         pltpu.VMEM((t, BRANCH_W), BF16)])

    sgu_bias = jnp.repeat(jnp.transpose(sgu_b).astype(F32), SGU_HD, axis=1)
    y_b = _branch_call(
        _sgu_kernel, "branch_sgu", seq, x_in,w_sgu, layer,
        [row(sgu_ln_g), row(sgu_ln_b), sgu_w.astype(F32), sgu_bias],
        [pltpu.VMEM((t, BRANCH_W), F32), pltpu.VMEM((t, BRANCH_W), F32),
         pltpu.VMEM((t, BRANCH_W), F32), pltpu.VMEM((t, BRANCH_W), BF16)])

    sink_hp = sinks.astype(F32).reshape(N_KV_HEADS, PAIRS, 2)
    sink_rows = jnp.repeat(jnp.transpose(sink_hp, (0, 2, 1)), BLOCK, axis=2)
    sink_rows = jnp.broadcast_to(sink_rows[..., None], (N_KV_HEADS, 2, PQ, LANES))
    sink_lanes = jnp.repeat(jnp.repeat(sink_hp, BLOCK, axis=1), HEAD_DIM, axis=2)
    n_units = N_KV_HEADS * ATTN_GROUP_BLOCKS
    y_c = _branch_call(
        _attn_kernel, "branch_attn", seq, x_in,w_attn, layer,
        [band, sink_rows, sink_lanes],
        [pltpu.VMEM((t, BRANCH_W), BF16), pltpu.VMEM((t, BRANCH_W), F32),
         pltpu.VMEM((N_KV_HEADS, 2, t + BLOCK, LANES), BF16),
         pltpu.VMEM((N_KV_HEADS, 2, t + BLOCK, LANES), BF16),
         pltpu.VMEM((PQ, 4 * BLOCK), jnp.int32),
         pltpu.VMEM((4 * BLOCK, LANES), BF16),
         pltpu.VMEM((n_units, PQ, 4 * BLOCK), F32),
         pltpu.VMEM((n_units, PQ, 4 * BLOCK), BF16),
         pltpu.VMEM((n_units, 2, PQ, LANES), F32),
         pltpu.VMEM((n_units, PQ, LANES), F32)])

    per_col = lambda v: v.astype(F32).reshape(CONV_COLUMNS, 1, LANES)
    taps = jnp.transpose(conv_w.astype(F32).reshape(CONV_K, CONV_COLUMNS, LANES), (1, 0, 2))
    taps = jnp.broadcast_to(taps[:, :, None, :], (CONV_COLUMNS, CONV_K, SUBLANES, LANES))
    y_d = _branch_call(
        _conv_kernel, "branch_conv", seq, x_in,w_conv, layer,
        [taps, per_col(conv_b), per_col(conv_ln_g), per_col(conv_ln_b)],
        [pltpu.VMEM((t, D_MODEL), BF16),
         pltpu.VMEM((CONV_COLUMNS, t + CONV_HALO, LANES), F32),
         pltpu.VMEM((CONV_COLUMNS, t, LANES), F32),
         pltpu.VMEM((SUBLANES, t + CONV_HALO - SUBLANES, LANES), F32),
         pltpu.VMEM((CONV_COLUMNS, t, LANES), F32)])

    n_cb = D_MODEL // MERGE_COLS
    tm = MERGE_TILE
    once = pl.Buffered(1)
    x_spec = pl.BlockSpec((tm, D_MODEL), lambda c, i: (i, 0))
    y_spec = pl.BlockSpec((tm, BRANCH_W), lambda c, i: (i, 0))
    wg_specs = [pl.BlockSpec((None, D_MODEL, MERGE_COLS),
                             functools.partial(lambda br, c, i: (layer, 0, br * n_cb + c), br), pipeline_mode=once)
                for br in range(N_BRANCH)]
    wb_spec = pl.BlockSpec((None, N_BRANCH, BRANCH_W, MERGE_COLS), lambda c, i: (layer, 0, 0, c),
                           pipeline_mode=once)
    merged = _call(
        _merge_kernel, "merge", (n_cb, seq // tm),
        [(x_in, x_spec)] + [(y, y_spec) for y in (y_a, y_b, y_c, y_d)]
        + [(w_gates, spec) for spec in wg_specs] + [(w_branch16, wb_spec)],
        [(jax.ShapeDtypeStruct((seq, D_MODEL), BF16), pl.BlockSpec((tm, MERGE_COLS), lambda c, i: (i, c)))],
        [], value_bytes=4 * _nbytes((tm, MERGE_COLS), F32))

    x_next, x_next_bf = _call(
        _out_kernel, "out_norm", (seq // t,),
        [(merged, _row_tile(D_MODEL)), (w_out16, _layer_weight(w_out16, layer)), (x, _row_tile(D_MODEL)),
         (row(ln_g), _resident((1, D_MODEL))), (row(ln_b), _resident((1, D_MODEL)))],
        [(jax.ShapeDtypeStruct((seq, D_MODEL), F32), _row_tile(D_MODEL)),
         (jax.ShapeDtypeStruct((seq, D_MODEL), BF16), _row_tile(D_MODEL))],
        [pltpu.VMEM((t, D_MODEL), F32)], value_bytes=_nbytes((t, D_MODEL), F32))
    return x_next, x_next_bf


def kernel(x, w_in, pool_w, pool_scale, sgu_ln_g, sgu_ln_b, sgu_w, sgu_b, attn_sinks, rel_bias, conv_w,
           conv_b, conv_ln_g, conv_ln_b, w_branch, w_out, ln_g, ln_b):
    batch, seq, d_model = x.shape
    assert d_model == D_MODEL and w_in.shape == (DEPTH, D_MODEL, D_IN)
    assert seq % SEQ_TILE == 0 and SEQ_TILE % (BLOCK * ATTN_GROUP_BLOCKS) == 0

    band = _band_bias(rel_bias).reshape(N_KV_HEADS, PAIRS, 2, BLOCK, 2 * BLOCK)
    band = jnp.transpose(band, (0, 1, 3, 2, 4)).reshape(N_KV_HEADS, PQ, 4 * BLOCK)

    w_parts = _cast_w_in(w_in.astype(F32))
    w_branch16 = w_branch.astype(BF16)
    w_out16 = w_out.astype(BF16)
    outs = []
    for bi in range(batch):
        xc = x[bi].astype(F32)
        x_in = xc
        for l in range(DEPTH):
            xc, x_in = _layer(xc, x_in, band, w_parts, l, pool_w[l], pool_scale[l], sgu_ln_g[l], sgu_ln_b[l],
                              sgu_w[l], sgu_b[l], attn_sinks[l], conv_w[l], conv_b[l], conv_ln_g[l],
                              conv_ln_b[l], w_branch16, w_out16, ln_g[l], ln_b[l])
        outs.append(xc.astype(x.dtype)[None])
    return outs[0] if batch == 1 else jnp.concatenate(outs, axis=0)
```

```python
import functools
import math

import jax
import jax.numpy as jnp
from jax import lax
from jax.experimental import pallas as pl
from jax.experimental.pallas import tpu as pltpu

F32 = jnp.float32
BF16 = jnp.bfloat16

D_MODEL = 2048
DEPTH = 2
N_BRANCH = 4
BRANCH_W = 1024
POOL_WINDOWS = (2, 4, 8, 16)
POOL_GW = BRANCH_W // len(POOL_WINDOWS)
SGU_HEADS = 8
SGU_HD = BRANCH_W // SGU_HEADS
CHUNK = 128
N_Q_HEADS = 16
N_KV_HEADS = 2
Q_PER_KV = N_Q_HEADS // N_KV_HEADS
HEAD_DIM = 64
KV_W = N_KV_HEADS * HEAD_DIM
WINDOW = 128
BLOCK = 128
NUM_BUCKETS = 32
MAX_DISTANCE = 128
CONV_K = 31
ALPHA = (2 * DEPTH) ** 0.25
LN_EPS = 1e-5
NEG_INF = -1e30

OFF_A = 0
OFF_B = OFF_A + 2 * BRANCH_W
OFF_C = OFF_B + 3 * BRANCH_W
OFF_D = OFF_C + 2 * BRANCH_W + 2 * KV_W
OFF_G = OFF_D + 3 * BRANCH_W
D_IN = OFF_G + N_BRANCH * D_MODEL

LANES = 128
SUBLANES = 8
SEQ_TILE = 512
POOL_HALO = 16
CONV_HALO = 32
ROW_CHUNK = 64
CONV_ROWS = 32
STREAM_ROWS = 128
PROJ_COLS = 256
MERGE_COLS = 512
PAIRS = Q_PER_KV // 2
PQ = PAIRS * BLOCK
ATTN_GROUP_BLOCKS = 2
MERGE_TILE = 1024
V7X_VMEM_BYTES = 64 * 1024 * 1024


def _sigmoid(x):
    return 0.5 * jnp.tanh(0.5 * x) + 0.5


def _silu(x):
    return x * _sigmoid(x)


def _layer_norm_rows(v, g, b):
    mu = jnp.mean(v, axis=-1, keepdims=True)
    d = v - mu
    var = jnp.mean(d * d, axis=-1, keepdims=True)
    return d * lax.rsqrt(var + LN_EPS) * g + b


def _dot(a, b):
    return jnp.dot(a, b, preferred_element_type=F32)


def _t5_bucket(n):
    max_exact = NUM_BUCKETS // 2
    nf = jnp.maximum(n, 1).astype(F32)
    large = max_exact + (jnp.log(nf / max_exact) / math.log(MAX_DISTANCE / max_exact)
                         * (NUM_BUCKETS - max_exact)).astype(jnp.int32)
    large = jnp.minimum(large, NUM_BUCKETS - 1)
    return jnp.where(n < max_exact, n, large)


def _band_bucket_ids():
    i = jnp.arange(BLOCK)[:, None]
    j = jnp.arange(2 * BLOCK)[None, :]
    return _t5_bucket(jnp.clip(i + BLOCK - j, 0, WINDOW - 1)).astype(jnp.int32)


def _bias_kernel(bucket_ref, table_ref, o_ref):
    h = pl.program_id(0)
    bucket = bucket_ref[...]
    acc = jnp.zeros(bucket.shape, F32)
    for b in range(NUM_BUCKETS):
        acc = jnp.where(bucket == b, table_ref[b, h], acc)
    o_ref[0] = acc


def _band_bias(rel_bias):
    return pl.pallas_call(
        _bias_kernel,
        grid=(N_Q_HEADS,),
        in_specs=[pl.BlockSpec((BLOCK, 2 * BLOCK), lambda h: (0, 0)),
                  pl.BlockSpec(memory_space=pltpu.SMEM)],
        out_specs=pl.BlockSpec((1, BLOCK, 2 * BLOCK), lambda h: (h, 0, 0)),
        out_shape=jax.ShapeDtypeStruct((N_Q_HEADS, BLOCK, 2 * BLOCK), F32),
        name="band_bias",
    )(_band_bucket_ids(), rel_bias.astype(F32))


W_IN_SPLITS = ((OFF_A, OFF_B), (OFF_B, OFF_C), (OFF_C, OFF_D), (OFF_G, D_IN))
CONV_W_CHUNKS = (OFF_G - OFF_D) // PROJ_COLS
CAST_ROWS = 128


def _cast_w_in_kernel(w_ref, oa_ref, ob_ref, oc_ref, og_ref, od_ref):
    for (a, b), o_ref in zip(W_IN_SPLITS, (oa_ref, ob_ref, oc_ref, og_ref)):
        o_ref[...] = w_ref[:, a:b].astype(BF16)
    for j in range(CONV_W_CHUNKS):
        od_ref[j] = w_ref[:, OFF_D + j * PROJ_COLS:OFF_D + (j + 1) * PROJ_COLS].astype(BF16)


def _cast_w_in(w_in):
    widths = [b - a for a, b in W_IN_SPLITS]
    outputs = [(jax.ShapeDtypeStruct((DEPTH, D_MODEL, w), BF16),
                pl.BlockSpec((None, CAST_ROWS, w), lambda l, r: (l, r, 0))) for w in widths]
    outputs.append((jax.ShapeDtypeStruct((DEPTH, CONV_W_CHUNKS, D_MODEL, PROJ_COLS), BF16),
                    pl.BlockSpec((None, CONV_W_CHUNKS, CAST_ROWS, PROJ_COLS), lambda l, r: (l, 0, r, 0))))
    return _call(_cast_w_in_kernel, "cast_w_in", (DEPTH, D_MODEL // CAST_ROWS),
                 [(w_in, pl.BlockSpec((None, CAST_ROWS, D_IN), lambda l, r: (l, r, 0)))],
                 outputs, [], value_bytes=0)


def _pool_kernel(x_ref, w_ref, pw_ref, sc_ref, o_ref, abuf, gate_buf, mix_buf):
    i = pl.program_id(0)
    t_rows = x_ref.shape[0]

    @pl.when(i == 0)
    def _():
        abuf[0:POOL_HALO, :] = jnp.zeros((POOL_HALO, BRANCH_W), F32)

    xb = x_ref[...].astype(BF16)
    abuf[POOL_HALO:POOL_HALO + t_rows, :] = _dot(xb, w_ref[:, 0:BRANCH_W])
    gate_buf[...] = _dot(xb, w_ref[:, BRANCH_W:2 * BRANCH_W])

    for g, win in enumerate(POOL_WINDOWS):
        c0 = g * POOL_GW
        for r0 in range(0, t_rows, ROW_CHUNK):
            base = POOL_HALO + r0
            cur = abuf[base:base + ROW_CHUNK, c0:c0 + POOL_GW]
            acc = cur
            for s in range(1, win):
                acc = acc + abuf[base - s:base - s + ROW_CHUNK, c0:c0 + POOL_GW]
            t = i * t_rows + r0 + lax.broadcasted_iota(jnp.int32, (ROW_CHUNK, POOL_GW), 0)
            cnt = jnp.minimum(t + 1, win).astype(F32)
            mix_buf[r0:r0 + ROW_CHUNK, c0:c0 + POOL_GW] = (acc / cnt - cur).astype(BF16)

    for g in range(len(POOL_WINDOWS)):
        c0 = g * POOL_GW
        y = _dot(mix_buf[:, c0:c0 + POOL_GW], pw_ref[g])
        y = y * sc_ref[:, c0:c0 + POOL_GW] * _silu(gate_buf[:, c0:c0 + POOL_GW])
        o_ref[:, c0:c0 + POOL_GW] = y.astype(BF16)

    abuf[0:POOL_HALO, :] = abuf[t_rows:t_rows + POOL_HALO, :]


def _sgu_kernel(x_ref, w_ref, lng_ref, lnb_ref, sw_ref, sb_ref, o_ref,
                u_buf, v_buf, gate_buf, vn_buf):
    t_rows = x_ref.shape[0]
    xb = x_ref[...].astype(BF16)
    u_buf[...] = _dot(xb, w_ref[:, 0:BRANCH_W])
    v_buf[...] = _dot(xb, w_ref[:, BRANCH_W:2 * BRANCH_W])
    gate_buf[...] = _dot(xb, w_ref[:, 2 * BRANCH_W:3 * BRANCH_W])

    g = lng_ref[...]
    b = lnb_ref[...]
    for r0 in range(0, t_rows, ROW_CHUNK):
        vn = _layer_norm_rows(v_buf[r0:r0 + ROW_CHUNK, :], g, b)
        vn_buf[r0:r0 + ROW_CHUNK, :] = vn.astype(BF16)

    row = lax.broadcasted_iota(jnp.int32, (CHUNK, CHUNK), 0)
    col = lax.broadcasted_iota(jnp.int32, (CHUNK, CHUNK), 1)
    causal = row >= col
    for h in range(SGU_HEADS):
        c0 = h * SGU_HD
        w_h = jnp.where(causal, sw_ref[h], 0.0).astype(BF16)
        bias_h = sb_ref[:, c0:c0 + SGU_HD]
        for r0 in range(0, t_rows, CHUNK):
            sp = _dot(w_h, vn_buf[r0:r0 + CHUNK, c0:c0 + SGU_HD]) + bias_h
            y = u_buf[r0:r0 + CHUNK, c0:c0 + SGU_HD] * sp * _silu(gate_buf[r0:r0 + CHUNK, c0:c0 + SGU_HD])
            o_ref[r0:r0 + CHUNK, c0:c0 + SGU_HD] = y.astype(BF16)


def _attn_kernel(x_ref, w_ref, bias_ref, sink_ref, sinkc_ref, o_ref,
                 q_buf, gate_buf, k_tab, v_tab, code_buf, ones_buf, s_buf, p_buf, m_buf, mc_buf):
    i = pl.program_id(0)
    t_rows = x_ref.shape[0]
    n_blocks = t_rows // BLOCK

    @pl.when(i == 0)
    def _():
        zero = jnp.zeros((BLOCK, LANES), BF16)
        for j in range(N_KV_HEADS):
            for half in range(2):
                k_tab[j, half, 0:BLOCK, :] = zero
                v_tab[j, half, 0:BLOCK, :] = zero
        row = lax.broadcasted_iota(jnp.int32, (PQ, 2 * 2 * BLOCK), 0)
        col = lax.broadcasted_iota(jnp.int32, (PQ, 2 * 2 * BLOCK), 1)
        key = col & (2 * BLOCK - 1)
        u = key - (row & (BLOCK - 1)) - 1
        in_window = (u >= 0) & (u < WINDOW)
        code_buf[...] = jnp.where(in_window, jnp.where(key >= BLOCK, 2, 1), 0)
        krow = lax.broadcasted_iota(jnp.int32, (4 * BLOCK, LANES), 0)
        klane = lax.broadcasted_iota(jnp.int32, (4 * BLOCK, LANES), 1)
        ones_buf[...] = jnp.where((krow >= 2 * BLOCK) == (klane >= HEAD_DIM), 1.0, 0.0).astype(BF16)

    xb = x_ref[...].astype(BF16)
    q_buf[...] = (_dot(xb, w_ref[:, 0:BRANCH_W]) * (HEAD_DIM ** -0.5)).astype(BF16)
    gate_buf[...] = _dot(xb, w_ref[:, BRANCH_W + 2 * KV_W:2 * BRANCH_W + 2 * KV_W])

    lane = lax.broadcasted_iota(jnp.int32, (t_rows, LANES), 1)
    low = lane < HEAD_DIM
    kv = _dot(xb, w_ref[:, BRANCH_W:BRANCH_W + 2 * KV_W])
    for tab, c0 in ((k_tab, 0), (v_tab, KV_W)):
        val = kv[:, c0:c0 + KV_W]
        swapped = pltpu.roll(val, HEAD_DIM, axis=1)
        zero = jnp.zeros_like(val)
        tab[0, 0, BLOCK:BLOCK + t_rows, :] = jnp.where(low, val, zero).astype(BF16)
        tab[0, 1, BLOCK:BLOCK + t_rows, :] = jnp.where(low, zero, swapped).astype(BF16)
        tab[1, 0, BLOCK:BLOCK + t_rows, :] = jnp.where(low, swapped, zero).astype(BF16)
        tab[1, 1, BLOCK:BLOCK + t_rows, :] = jnp.where(low, zero, val).astype(BF16)

    low128 = lax.broadcasted_iota(jnp.int32, (BLOCK, LANES), 1) < HEAD_DIM
    for g0 in range(0, n_blocks, ATTN_GROUP_BLOCKS):
        units = [(n, j) for n in range(g0, g0 + ATTN_GROUP_BLOCKS) for j in range(N_KV_HEADS)]

        for it, (n, j) in enumerate(units):
            r0 = n * BLOCK
            q_cols = j * PAIRS * LANES
            qs = jnp.concatenate(
                [q_buf[r0:r0 + BLOCK, q_cols + p * LANES:q_cols + (p + 1) * LANES] for p in range(PAIRS)], axis=0)
            k2 = jnp.concatenate([k_tab[j, 0, r0:r0 + 2 * BLOCK, :], k_tab[j, 1, r0:r0 + 2 * BLOCK, :]], axis=0)
            logits = lax.dot_general(qs, k2, (((1,), (1,)), ((), ())), preferred_element_type=F32)
            threshold = jnp.where(jnp.logical_and(i == 0, n == 0), 2, 1)
            for p in range(PAIRS):
                rows = slice(p * BLOCK, (p + 1) * BLOCK)
                row_max = []
                for par in range(2):
                    slabs = []
                    for half in range(2):
                        cols = slice((2 * par + half) * LANES, (2 * par + half + 1) * LANES)
                        s = logits[rows, cols] + bias_ref[j, rows, cols]
                        s = jnp.where(code_buf[rows, cols] >= threshold, s, NEG_INF)
                        s_buf[it, rows, cols] = s
                        slabs.append(s)
                    m = jnp.max(jnp.maximum(slabs[0], slabs[1]), axis=1, keepdims=True)
                    m = jnp.maximum(jnp.broadcast_to(m, (BLOCK, LANES)), sink_ref[j, par, rows, :])
                    m_buf[it, par, rows, :] = m
                    row_max.append(m)
                mc_buf[it, rows, :] = jnp.where(low128, row_max[0], row_max[1])

        for it, (n, j) in enumerate(units):
            for p in range(PAIRS):
                rows = slice(p * BLOCK, (p + 1) * BLOCK)
                for par in range(2):
                    m = m_buf[it, par, rows, :]
                    for half in range(2):
                        cols = slice((2 * par + half) * LANES, (2 * par + half + 1) * LANES)
                        p_buf[it, rows, cols] = jnp.exp(s_buf[it, rows, cols] - m).astype(BF16)

        for it, (n, j) in enumerate(units):
            r0 = n * BLOCK
            q_cols = j * PAIRS * LANES
            v2 = jnp.concatenate([v_tab[j, 0, r0:r0 + 2 * BLOCK, :], v_tab[j, 1, r0:r0 + 2 * BLOCK, :]], axis=0)
            o = _dot(p_buf[it], jnp.concatenate([v2, ones_buf[...]], axis=1))
            for p in range(PAIRS):
                rows = slice(p * BLOCK, (p + 1) * BLOCK)
                cols = slice(q_cols + p * LANES, q_cols + (p + 1) * LANES)
                den = o[rows, LANES:2 * LANES] + jnp.exp(sinkc_ref[j, rows, :] - mc_buf[it, rows, :])
                y = o[rows, 0:LANES] * (1.0 / den) * _silu(gate_buf[r0:r0 + BLOCK, cols])
                o_ref[r0:r0 + BLOCK, cols] = y.astype(BF16)

    for j in range(N_KV_HEADS):
        for half in range(2):
            k_tab[j, half, 0:BLOCK, :] = k_tab[j, half, t_rows:t_rows + BLOCK, :]
            v_tab[j, half, 0:BLOCK, :] = v_tab[j, half, t_rows:t_rows + BLOCK, :]


CONV_COLUMNS = BRANCH_W // LANES
CONV_BLOCKS = BRANCH_W // PROJ_COLS


def _conv_stream_kernel(x_ref, w_ref, cw_ref, cb_ref, lng_ref, lnb_ref, zero_ref, o_ref,
                        xs_buf, glu_buf, gate_buf, shift_buf, y_buf):
    i = pl.program_id(0)
    t_rows = x_ref.shape[0]

    @pl.when(i == 0)
    def _():
        for c in range(CONV_COLUMNS):
            glu_buf[c, 0:CONV_HALO, :] = jnp.zeros((CONV_HALO, LANES), F32)

    xs_buf[...] = x_ref[...].astype(BF16)
    first = CONV_HALO - (CONV_K - 1)
    span = t_rows + CONV_HALO - SUBLANES

    def glu_block(n):
        gate_val = _dot(xs_buf[...], w_ref[CONV_BLOCKS + n])
        glu = _dot(xs_buf[...], w_ref[n]) * _sigmoid(gate_val)
        glu_buf[2 * n, CONV_HALO:CONV_HALO + t_rows, :] = glu[:, 0:LANES]
        glu_buf[2 * n + 1, CONV_HALO:CONV_HALO + t_rows, :] = glu[:, LANES:2 * LANES]
        return gate_val[t_rows - SUBLANES:t_rows, 0:LANES]

    def gate_block(n):
        gate = _dot(xs_buf[...], w_ref[2 * CONV_BLOCKS + n])
        gate_buf[2 * n] = gate[:, 0:LANES]
        gate_buf[2 * n + 1] = gate[:, LANES:2 * LANES]
        return gate[t_rows - SUBLANES:t_rows, 0:LANES]

    def conv_block(n, token, second_trigger):
        for half in range(2):
            c = 2 * n + half
            if half == 1:
                token = token + second_trigger()
            for r in range(1, SUBLANES):
                shift_buf[half, r, 0:span, :] = glu_buf[c, r:r + span, :]
            bias = jnp.broadcast_to(cb_ref[c], (STREAM_ROWS, LANES))
            for r0 in range(0, t_rows, STREAM_ROWS):
                nothing = pltpu.bitcast(pltpu.bitcast(token, jnp.int32) & zero_ref[...], F32)
                acc = bias
                for k in range(CONV_K):
                    a8 = (first + k) // SUBLANES * SUBLANES
                    r = (first + k) % SUBLANES
                    if r == 0:
                        src = glu_buf[c, r0 + a8:r0 + a8 + STREAM_ROWS, :]
                    else:
                        src = shift_buf[half, r, r0 + a8:r0 + a8 + STREAM_ROWS, :]
                    tap = jnp.concatenate([cw_ref[c, k] + nothing] * (STREAM_ROWS // SUBLANES), axis=0)
                    acc = acc + tap * src
                y_buf[c, r0:r0 + STREAM_ROWS, :] = acc
                token = acc[0:SUBLANES, :]

    for n in range(CONV_BLOCKS):
        conv_block(n, glu_block(n), lambda n=n: gate_block(n))

    for r0 in range(0, t_rows, ROW_CHUNK):
        rows = slice(r0, r0 + ROW_CHUNK)
        ys = [y_buf[c, rows, :] for c in range(CONV_COLUMNS)]
        total = ys[0]
        for y in ys[1:]:
            total = total + y
        mu = jnp.sum(total, axis=1, keepdims=True) * (1.0 / BRANCH_W)
        ds = [y - mu for y in ys]
        sq = ds[0] * ds[0]
        for d in ds[1:]:
            sq = sq + d * d
        rstd = lax.rsqrt(jnp.sum(sq, axis=1, keepdims=True) * (1.0 / BRANCH_W) + LN_EPS)
        for c in range(CONV_COLUMNS):
            y = _silu(ds[c] * rstd * lng_ref[c] + lnb_ref[c]) * _silu(gate_buf[c, rows, :])
            o_ref[rows, c * LANES:(c + 1) * LANES] = y.astype(BF16)

    for c in range(CONV_COLUMNS):
        glu_buf[c, 0:CONV_HALO, :] = glu_buf[c, t_rows:t_rows + CONV_HALO, :]


def _conv_kernel(x_ref, w_ref, cw_ref, cb_ref, lng_ref, lnb_ref, o_ref,
                 xs_buf, glu_buf, gate_buf, shift_buf, y_buf):
    i = pl.program_id(0)
    t_rows = x_ref.shape[0]

    @pl.when(i == 0)
    def _():
        for c in range(CONV_COLUMNS):
            glu_buf[c, 0:CONV_HALO, :] = jnp.zeros((CONV_HALO, LANES), F32)

    xs_buf[...] = x_ref[...].astype(BF16)

    def project(n, carry):
        glu = _dot(xs_buf[...], w_ref[n]) * _sigmoid(_dot(xs_buf[...], w_ref[CONV_BLOCKS + n]))
        glu_buf[2 * n, CONV_HALO:CONV_HALO + t_rows, :] = glu[:, 0:LANES]
        glu_buf[2 * n + 1, CONV_HALO:CONV_HALO + t_rows, :] = glu[:, LANES:2 * LANES]
        gate = _dot(xs_buf[...], w_ref[2 * CONV_BLOCKS + n])
        gate_buf[2 * n] = gate[:, 0:LANES]
        gate_buf[2 * n + 1] = gate[:, LANES:2 * LANES]
        return carry

    lax.fori_loop(0, CONV_BLOCKS, project, 0)

    first = CONV_HALO - (CONV_K - 1)
    span = t_rows + CONV_HALO - SUBLANES

    def conv(c, carry):
        for r in range(1, SUBLANES):
            shift_buf[r, 0:span, :] = glu_buf[c, r:r + span, :]
        bias = jnp.broadcast_to(cb_ref[c], (CONV_ROWS, LANES))
        for r0 in range(0, t_rows, CONV_ROWS):
            acc = bias
            for k in range(CONV_K):
                a8 = (first + k) // SUBLANES * SUBLANES
                r = (first + k) % SUBLANES
                if r == 0:
                    src = glu_buf[c, r0 + a8:r0 + a8 + CONV_ROWS, :]
                else:
                    src = shift_buf[r, r0 + a8:r0 + a8 + CONV_ROWS, :]
                tap = jnp.concatenate([cw_ref[c, k]] * (CONV_ROWS // SUBLANES), axis=0)
                acc = acc + tap * src
            y_buf[c, r0:r0 + CONV_ROWS, :] = acc
        return carry

    lax.fori_loop(0, CONV_COLUMNS, conv, 0)

    for r0 in range(0, t_rows, ROW_CHUNK):
        rows = slice(r0, r0 + ROW_CHUNK)
        ys = [y_buf[c, rows, :] for c in range(CONV_COLUMNS)]
        total = ys[0]
        for y in ys[1:]:
            total = total + y
        mu = jnp.sum(total, axis=1, keepdims=True) * (1.0 / BRANCH_W)
        ds = [y - mu for y in ys]
        sq = ds[0] * ds[0]
        for d in ds[1:]:
            sq = sq + d * d
        rstd = lax.rsqrt(jnp.sum(sq, axis=1, keepdims=True) * (1.0 / BRANCH_W) + LN_EPS)
        for c in range(CONV_COLUMNS):
            y = _silu(ds[c] * rstd * lng_ref[c] + lnb_ref[c]) * _silu(gate_buf[c, rows, :])
            o_ref[rows, c * LANES:(c + 1) * LANES] = y.astype(BF16)

    for c in range(CONV_COLUMNS):
        glu_buf[c, 0:CONV_HALO, :] = glu_buf[c, t_rows:t_rows + CONV_HALO, :]


def _merge_kernel(x_ref, ya_ref, yb_ref, yc_ref, yd_ref, wga_ref, wgb_ref, wgc_ref, wgd_ref, wb_ref, o_ref):
    xb = x_ref[...].astype(BF16)
    acc = None
    branches = ((ya_ref, wga_ref), (yb_ref, wgb_ref), (yc_ref, wgc_ref), (yd_ref, wgd_ref))
    for br, (y_ref, wg_ref) in enumerate(branches):
        gate = _sigmoid(_dot(xb, wg_ref[...]))
        term = gate * _dot(y_ref[...], wb_ref[br])
        acc = term if acc is None else acc + term
    o_ref[...] = acc.astype(BF16)


def _out_kernel(m_ref, w_ref, x_ref, g_ref, b_ref, o_ref, ob_ref, z_buf):
    t_rows = x_ref.shape[0]
    z_buf[...] = _dot(m_ref[...], w_ref[...])
    g = g_ref[...]
    b = b_ref[...]
    for r0 in range(0, t_rows, ROW_CHUNK // 2):
        rows = slice(r0, r0 + ROW_CHUNK // 2)
        y = _layer_norm_rows(ALPHA * x_ref[rows, :] + z_buf[rows, :], g, b)
        o_ref[rows, :] = y
        ob_ref[rows, :] = y.astype(BF16)


def _resident(shape):
    zeros = (0,) * len(shape)
    return pl.BlockSpec(shape, lambda *_: zeros, pipeline_mode=pl.Buffered(1))


def _nbytes(shape, dtype):
    return math.prod(d for d in shape if d is not None) * jnp.dtype(dtype).itemsize


def _call(kernel_fn, name, grid, inputs, outputs, scratch, value_bytes):
    arrays, in_specs = zip(*inputs)
    out_shape, out_specs = zip(*outputs)
    windows = [(spec, a.dtype) for a, spec in inputs] + [(spec, o.dtype) for o, spec in outputs]
    need = value_bytes + sum(_nbytes(v.shape, v.dtype) for v in scratch)
    for spec, dtype in windows:
        buffers = spec.pipeline_mode.buffer_count if spec.pipeline_mode is not None else 2
        need += buffers * _nbytes(spec.block_shape, dtype)
    assert need <= V7X_VMEM_BYTES, (name, need)
    return pl.pallas_call(
        kernel_fn,
        grid=grid,
        in_specs=list(in_specs),
        out_specs=list(out_specs) if len(out_specs) > 1 else out_specs[0],
        out_shape=list(out_shape) if len(out_shape) > 1 else out_shape[0],
        scratch_shapes=scratch,
        compiler_params=pltpu.CompilerParams(dimension_semantics=("arbitrary",) * len(grid),
                                             vmem_limit_bytes=need),
        name=name,
    )(*arrays)


def _row_tile(width, rows=SEQ_TILE):
    return pl.BlockSpec((rows, width), lambda i: (i, 0))


def _layer_weight(w_all, layer):
    idx = (layer,) + (0,) * (w_all.ndim - 1)
    return pl.BlockSpec((None,) + w_all.shape[1:], lambda *_: idx, pipeline_mode=pl.Buffered(1))


def _branch_call(kernel_fn, name, seq, x_in, w_all, layer, operands, scratch, value_tiles=2):
    value_bytes = value_tiles * _nbytes((SEQ_TILE, BRANCH_W), F32)
    inputs = [(x_in, _row_tile(D_MODEL)), (w_all, _layer_weight(w_all, layer))]
    inputs += [(op, _resident(op.shape)) for op in operands]
    outputs = [(jax.ShapeDtypeStruct((seq, BRANCH_W), BF16), _row_tile(BRANCH_W))]
    return _call(kernel_fn, name, (seq // SEQ_TILE,), inputs, outputs, scratch, value_bytes)


def _layer(x, x_in, band, w_parts, layer, pool_w, pool_scale, sgu_ln_g, sgu_ln_b, sgu_w, sgu_b, sinks,
           conv_w, conv_b, conv_ln_g, conv_ln_b, w_branch16, w_out16, ln_g, ln_b):
    seq = x.shape[0]
    t = SEQ_TILE
    w_pool, w_sgu, w_attn, w_gates, w_conv = w_parts
    row = lambda v: v.reshape(1, -1).astype(F32)

    y_a = _branch_call(
        _pool_kernel, "branch_pool", seq, x_in,w_pool, layer,
        [pool_w.astype(BF16), row(pool_scale)],
        [pltpu.VMEM((t + POOL_HALO, BRANCH_W), F32), pltpu.VMEM((t, BRANCH_W), F32),
         pltpu.VMEM((t, BRANCH_W), BF16)])

    sgu_bias = jnp.repeat(jnp.transpose(sgu_b).astype(F32), SGU_HD, axis=1)
    y_b = _branch_call(
        _sgu_kernel, "branch_sgu", seq, x_in,w_sgu, layer,
        [row(sgu_ln_g), row(sgu_ln_b), sgu_w.astype(F32), sgu_bias],
        [pltpu.VMEM((t, BRANCH_W), F32), pltpu.VMEM((t, BRANCH_W), F32),
         pltpu.VMEM((t, BRANCH_W), F32), pltpu.VMEM((t, BRANCH_W), BF16)])

    sink_hp = sinks.astype(F32).reshape(N_KV_HEADS, PAIRS, 2)
    sink_rows = jnp.repeat(jnp.transpose(sink_hp, (0, 2, 1)), BLOCK, axis=2)
    sink_rows = jnp.broadcast_to(sink_rows[..., None], (N_KV_HEADS, 2, PQ, LANES))
    sink_lanes = jnp.repeat(jnp.repeat(sink_hp, BLOCK, axis=1), HEAD_DIM, axis=2)
    n_units = N_KV_HEADS * ATTN_GROUP_BLOCKS
    y_c = _branch_call(
        _attn_kernel, "branch_attn", seq, x_in,w_attn, layer,
        [band, sink_rows, sink_lanes],
        [pltpu.VMEM((t, BRANCH_W), BF16), pltpu.VMEM((t, BRANCH_W), F32),
         pltpu.VMEM((N_KV_HEADS, 2, t + BLOCK, LANES), BF16),
         pltpu.VMEM((N_KV_HEADS, 2, t + BLOCK, LANES), BF16),
         pltpu.VMEM((PQ, 4 * BLOCK), jnp.int32),
         pltpu.VMEM((4 * BLOCK, LANES), BF16),
         pltpu.VMEM((n_units, PQ, 4 * BLOCK), F32),
         pltpu.VMEM((n_units, PQ, 4 * BLOCK), BF16),
         pltpu.VMEM((n_units, 2, PQ, LANES), F32),
         pltpu.VMEM((n_units, PQ, LANES), F32)])

    per_col = lambda v: v.astype(F32).reshape(CONV_COLUMNS, 1, LANES)
    taps = jnp.transpose(conv_w.astype(F32).reshape(CONV_K, CONV_COLUMNS, LANES), (1, 0, 2))
    taps = jnp.broadcast_to(taps[:, :, None, :], (CONV_COLUMNS, CONV_K, SUBLANES, LANES))
    y_d = _branch_call(
        _conv_stream_kernel, "branch_conv", seq, x_in, w_conv, layer,
        [taps, per_col(conv_b), per_col(conv_ln_g), per_col(conv_ln_b), jnp.zeros((SUBLANES, LANES), jnp.int32)],
        [pltpu.VMEM((t, D_MODEL), BF16),
         pltpu.VMEM((CONV_COLUMNS, t + CONV_HALO, LANES), F32),
         pltpu.VMEM((CONV_COLUMNS, t, LANES), F32),
         pltpu.VMEM((2, SUBLANES, t + CONV_HALO - SUBLANES, LANES), F32),
         pltpu.VMEM((CONV_COLUMNS, t, LANES), F32)], value_tiles=4)

    n_cb = D_MODEL // MERGE_COLS
    tm = MERGE_TILE
    once = pl.Buffered(1)
    x_spec = pl.BlockSpec((tm, D_MODEL), lambda c, i: (i, 0))
    y_spec = pl.BlockSpec((tm, BRANCH_W), lambda c, i: (i, 0))
    wg_specs = [pl.BlockSpec((None, D_MODEL, MERGE_COLS),
                             functools.partial(lambda br, c, i: (layer, 0, br * n_cb + c), br), pipeline_mode=once)
                for br in range(N_BRANCH)]
    wb_spec = pl.BlockSpec((None, N_BRANCH, BRANCH_W, MERGE_COLS), lambda c, i: (layer, 0, 0, c),
                           pipeline_mode=once)
    merged = _call(
        _merge_kernel, "merge", (n_cb, seq // tm),
        [(x_in, x_spec)] + [(y, y_spec) for y in (y_a, y_b, y_c, y_d)]
        + [(w_gates, spec) for spec in wg_specs] + [(w_branch16, wb_spec)],
        [(jax.ShapeDtypeStruct((seq, D_MODEL), BF16), pl.BlockSpec((tm, MERGE_COLS), lambda c, i: (i, c)))],
        [], value_bytes=4 * _nbytes((tm, MERGE_COLS), F32))

    x_next, x_next_bf = _call(
        _out_kernel, "out_norm", (seq // t,),
        [(merged, _row_tile(D_MODEL)), (w_out16, _layer_weight(w_out16, layer)), (x, _row_tile(D_MODEL)),
         (row(ln_g), _resident((1, D_MODEL))), (row(ln_b), _resident((1, D_MODEL)))],
        [(jax.ShapeDtypeStruct((seq, D_MODEL), F32), _row_tile(D_MODEL)),
         (jax.ShapeDtypeStruct((seq, D_MODEL), BF16), _row_tile(D_MODEL))],
        [pltpu.VMEM((t, D_MODEL), F32)], value_bytes=_nbytes((t, D_MODEL), F32))
    return x_next, x_next_bf


def kernel(x, w_in, pool_w, pool_scale, sgu_ln_g, sgu_ln_b, sgu_w, sgu_b, attn_sinks, rel_bias, conv_w,
           conv_b, conv_ln_g, conv_ln_b, w_branch, w_out, ln_g, ln_b):
    batch, seq, d_model = x.shape
    assert d_model == D_MODEL and w_in.shape == (DEPTH, D_MODEL, D_IN)
    assert seq % SEQ_TILE == 0 and SEQ_TILE % (BLOCK * ATTN_GROUP_BLOCKS) == 0

    band = _band_bias(rel_bias).reshape(N_KV_HEADS, PAIRS, 2, BLOCK, 2 * BLOCK)
    band = jnp.transpose(band, (0, 1, 3, 2, 4)).reshape(N_KV_HEADS, PQ, 4 * BLOCK)

    w_parts = _cast_w_in(w_in.astype(F32))
    w_branch16 = w_branch.astype(BF16)
    w_out16 = w_out.astype(BF16)
    outs = []
    for bi in range(batch):
        xc = x[bi].astype(F32)
        x_in = xc
        for l in range(DEPTH):
            xc, x_in = _layer(xc, x_in, band, w_parts, l, pool_w[l], pool_scale[l], sgu_ln_g[l], sgu_ln_b[l],
                              sgu_w[l], sgu_b[l], attn_sinks[l], conv_w[l], conv_b[l], conv_ln_g[l],
                              conv_ln_b[l], w_branch16, w_out16, ln_g[l], ln_b[l])
        outs.append(xc.astype(x.dtype)[None])
    return outs[0] if batch == 1 else jnp.concatenate(outs, axis=0)
```

```python
import functools
import math

import jax
import jax.numpy as jnp
from jax import lax
from jax.experimental import pallas as pl
from jax.experimental.pallas import tpu as pltpu

F32 = jnp.float32
BF16 = jnp.bfloat16

D_MODEL = 2048
DEPTH = 2
N_BRANCH = 4
BRANCH_W = 1024
POOL_WINDOWS = (2, 4, 8, 16)
POOL_GW = BRANCH_W // len(POOL_WINDOWS)
SGU_HEADS = 8
SGU_HD = BRANCH_W // SGU_HEADS
CHUNK = 128
N_Q_HEADS = 16
N_KV_HEADS = 2
Q_PER_KV = N_Q_HEADS // N_KV_HEADS
HEAD_DIM = 64
KV_W = N_KV_HEADS * HEAD_DIM
WINDOW = 128
BLOCK = 128
NUM_BUCKETS = 32
MAX_DISTANCE = 128
CONV_K = 31
ALPHA = (2 * DEPTH) ** 0.25
LN_EPS = 1e-5
NEG_INF = -1e30

OFF_A = 0
OFF_B = OFF_A + 2 * BRANCH_W
OFF_C = OFF_B + 3 * BRANCH_W
OFF_D = OFF_C + 2 * BRANCH_W + 2 * KV_W
OFF_G = OFF_D + 3 * BRANCH_W
D_IN = OFF_G + N_BRANCH * D_MODEL

LANES = 128
SUBLANES = 8
SEQ_TILE = 512
POOL_HALO = 16
CONV_HALO = 32
ROW_CHUNK = 64
CONV_ROWS = 32
STREAM_ROWS = 128
PROJ_COLS = 256
MERGE_COLS = 512
PAIRS = Q_PER_KV // 2
PQ = PAIRS * BLOCK
ATTN_GROUP_BLOCKS = 2
MERGE_TILE = 1024
V7X_VMEM_BYTES = 64 * 1024 * 1024


def _sigmoid(x):
    return 0.5 * jnp.tanh(0.5 * x) + 0.5


def _silu(x):
    return x * _sigmoid(x)


def _layer_norm_rows(v, g, b):
    mu = jnp.mean(v, axis=-1, keepdims=True)
    d = v - mu
    var = jnp.mean(d * d, axis=-1, keepdims=True)
    return d * lax.rsqrt(var + LN_EPS) * g + b


def _dot(a, b):
    return jnp.dot(a, b, preferred_element_type=F32)


def _t5_bucket(n):
    max_exact = NUM_BUCKETS // 2
    nf = jnp.maximum(n, 1).astype(F32)
    large = max_exact + (jnp.log(nf / max_exact) / math.log(MAX_DISTANCE / max_exact)
                         * (NUM_BUCKETS - max_exact)).astype(jnp.int32)
    large = jnp.minimum(large, NUM_BUCKETS - 1)
    return jnp.where(n < max_exact, n, large)


def _band_bucket_ids():
    i = jnp.arange(BLOCK)[:, None]
    j = jnp.arange(2 * BLOCK)[None, :]
    return _t5_bucket(jnp.clip(i + BLOCK - j, 0, WINDOW - 1)).astype(jnp.int32)


def _bias_kernel(bucket_ref, table_ref, o_ref):
    h = pl.program_id(0)
    bucket = bucket_ref[...]
    acc = jnp.zeros(bucket.shape, F32)
    for b in range(NUM_BUCKETS):
        acc = jnp.where(bucket == b, table_ref[b, h], acc)
    o_ref[0] = acc


def _band_bias(rel_bias):
    return pl.pallas_call(
        _bias_kernel,
        grid=(N_Q_HEADS,),
        in_specs=[pl.BlockSpec((BLOCK, 2 * BLOCK), lambda h: (0, 0)),
                  pl.BlockSpec(memory_space=pltpu.SMEM)],
        out_specs=pl.BlockSpec((1, BLOCK, 2 * BLOCK), lambda h: (h, 0, 0)),
        out_shape=jax.ShapeDtypeStruct((N_Q_HEADS, BLOCK, 2 * BLOCK), F32),
        name="band_bias",
    )(_band_bucket_ids(), rel_bias.astype(F32))


W_IN_SPLITS = ((OFF_A, OFF_B), (OFF_B, OFF_C), (OFF_C, OFF_D), (OFF_G, D_IN))
CONV_W_CHUNKS = (OFF_G - OFF_D) // PROJ_COLS
CAST_ROWS = 128


def _cast_w_in_kernel(w_ref, oa_ref, ob_ref, oc_ref, og_ref, od_ref):
    for (a, b), o_ref in zip(W_IN_SPLITS, (oa_ref, ob_ref, oc_ref, og_ref)):
        o_ref[...] = w_ref[:, a:b].astype(BF16)
    for j in range(CONV_W_CHUNKS):
        od_ref[j] = w_ref[:, OFF_D + j * PROJ_COLS:OFF_D + (j + 1) * PROJ_COLS].astype(BF16)


def _cast_w_in(w_in):
    widths = [b - a for a, b in W_IN_SPLITS]
    outputs = [(jax.ShapeDtypeStruct((DEPTH, D_MODEL, w), BF16),
                pl.BlockSpec((None, CAST_ROWS, w), lambda l, r: (l, r, 0))) for w in widths]
    outputs.append((jax.ShapeDtypeStruct((DEPTH, CONV_W_CHUNKS, D_MODEL, PROJ_COLS), BF16),
                    pl.BlockSpec((None, CONV_W_CHUNKS, CAST_ROWS, PROJ_COLS), lambda l, r: (l, 0, r, 0))))
    return _call(_cast_w_in_kernel, "cast_w_in", (DEPTH, D_MODEL // CAST_ROWS),
                 [(w_in, pl.BlockSpec((None, CAST_ROWS, D_IN), lambda l, r: (l, r, 0)))],
                 outputs, [], value_bytes=0)


def _pool_kernel(x_ref, w_ref, pw_ref, sc_ref, o_ref, abuf, gate_buf, mix_buf):
    i = pl.program_id(0)
    t_rows = x_ref.shape[0]

    @pl.when(i == 0)
    def _():
        abuf[0:POOL_HALO, :] = jnp.zeros((POOL_HALO, BRANCH_W), F32)

    xb = x_ref[...].astype(BF16)
    abuf[POOL_HALO:POOL_HALO + t_rows, :] = _dot(xb, w_ref[:, 0:BRANCH_W])
    gate_buf[...] = _dot(xb, w_ref[:, BRANCH_W:2 * BRANCH_W])

    for g, win in enumerate(POOL_WINDOWS):
        c0 = g * POOL_GW
        for r0 in range(0, t_rows, ROW_CHUNK):
            base = POOL_HALO + r0
            cur = abuf[base:base + ROW_CHUNK, c0:c0 + POOL_GW]
            acc = cur
            for s in range(1, win):
                acc = acc + abuf[base - s:base - s + ROW_CHUNK, c0:c0 + POOL_GW]
            t = i * t_rows + r0 + lax.broadcasted_iota(jnp.int32, (ROW_CHUNK, POOL_GW), 0)
            cnt = jnp.minimum(t + 1, win).astype(F32)
            mix_buf[r0:r0 + ROW_CHUNK, c0:c0 + POOL_GW] = (acc / cnt - cur).astype(BF16)

    for g in range(len(POOL_WINDOWS)):
        c0 = g * POOL_GW
        y = _dot(mix_buf[:, c0:c0 + POOL_GW], pw_ref[g])
        y = y * sc_ref[:, c0:c0 + POOL_GW] * _silu(gate_buf[:, c0:c0 + POOL_GW])
        o_ref[:, c0:c0 + POOL_GW] = y.astype(BF16)

    abuf[0:POOL_HALO, :] = abuf[t_rows:t_rows + POOL_HALO, :]


def _sgu_kernel(x_ref, w_ref, lng_ref, lnb_ref, sw_ref, sb_ref, o_ref,
                u_buf, v_buf, gate_buf, vn_buf):
    t_rows = x_ref.shape[0]
    xb = x_ref[...].astype(BF16)
    u_buf[...] = _dot(xb, w_ref[:, 0:BRANCH_W])
    v_buf[...] = _dot(xb, w_ref[:, BRANCH_W:2 * BRANCH_W])
    gate_buf[...] = _dot(xb, w_ref[:, 2 * BRANCH_W:3 * BRANCH_W])

    g = lng_ref[...]
    b = lnb_ref[...]
    for r0 in range(0, t_rows, ROW_CHUNK):
        vn = _layer_norm_rows(v_buf[r0:r0 + ROW_CHUNK, :], g, b)
        vn_buf[r0:r0 + ROW_CHUNK, :] = vn.astype(BF16)

    row = lax.broadcasted_iota(jnp.int32, (CHUNK, CHUNK), 0)
    col = lax.broadcasted_iota(jnp.int32, (CHUNK, CHUNK), 1)
    causal = row >= col
    for h in range(SGU_HEADS):
        c0 = h * SGU_HD
        w_h = jnp.where(causal, sw_ref[h], 0.0).astype(BF16)
        bias_h = sb_ref[:, c0:c0 + SGU_HD]
        for r0 in range(0, t_rows, CHUNK):
            sp = _dot(w_h, vn_buf[r0:r0 + CHUNK, c0:c0 + SGU_HD]) + bias_h
            y = u_buf[r0:r0 + CHUNK, c0:c0 + SGU_HD] * sp * _silu(gate_buf[r0:r0 + CHUNK, c0:c0 + SGU_HD])
            o_ref[r0:r0 + CHUNK, c0:c0 + SGU_HD] = y.astype(BF16)


def _attn_kernel(x_ref, w_ref, bias_ref, sink_ref, sinkc_ref, zero_ref, o_ref,
                 xs_buf, q_buf, gate_buf, k_tab, v_tab, code_buf, ones_buf, s_buf, p_buf, m_buf, mc_buf):
    i = pl.program_id(0)
    t_rows = x_ref.shape[0]
    n_blocks = t_rows // BLOCK

    @pl.when(i == 0)
    def _():
        zero = jnp.zeros((BLOCK, LANES), BF16)
        for j in range(N_KV_HEADS):
            for half in range(2):
                k_tab[j, half, 0:BLOCK, :] = zero
                v_tab[j, half, 0:BLOCK, :] = zero
        row = lax.broadcasted_iota(jnp.int32, (PQ, 2 * 2 * BLOCK), 0)
        col = lax.broadcasted_iota(jnp.int32, (PQ, 2 * 2 * BLOCK), 1)
        key = col & (2 * BLOCK - 1)
        u = key - (row & (BLOCK - 1)) - 1
        in_window = (u >= 0) & (u < WINDOW)
        code_buf[...] = jnp.where(in_window, jnp.where(key >= BLOCK, 2, 1), 0)
        krow = lax.broadcasted_iota(jnp.int32, (4 * BLOCK, LANES), 0)
        klane = lax.broadcasted_iota(jnp.int32, (4 * BLOCK, LANES), 1)
        ones_buf[...] = jnp.where((krow >= 2 * BLOCK) == (klane >= HEAD_DIM), 1.0, 0.0).astype(BF16)

    xs_buf[...] = x_ref[...].astype(BF16)
    g_len = ATTN_GROUP_BLOCKS * BLOCK
    low = lax.broadcasted_iota(jnp.int32, (g_len, LANES), 1) < HEAD_DIM
    gate_col0 = BRANCH_W + 2 * KV_W

    def project_group(g0, after):
        g_rows = slice(g0 * BLOCK, g0 * BLOCK + g_len)
        xg = xs_buf[g_rows, :]
        if after is not None:
            nothing = pltpu.bitcast(after, jnp.int32) & zero_ref[...]
            nothing = jnp.concatenate([nothing] * (g_len // (2 * SUBLANES)), axis=0)
            nothing = jnp.concatenate([nothing] * (D_MODEL // LANES), axis=1)
            xg = pltpu.bitcast(pltpu.bitcast(xg, jnp.int32) | nothing, BF16)
        for piece in range(BRANCH_W // PROJ_COLS):
            c0 = piece * PROJ_COLS
            q = _dot(xg, w_ref[:, c0:c0 + PROJ_COLS]) * (HEAD_DIM ** -0.5)
            q_buf[g_rows, c0:c0 + PROJ_COLS] = q.astype(BF16)
        kv = _dot(xg, w_ref[:, BRANCH_W:BRANCH_W + 2 * KV_W])
        t_rows_g = slice(BLOCK + g0 * BLOCK, BLOCK + g0 * BLOCK + g_len)
        for tab, c0 in ((k_tab, 0), (v_tab, KV_W)):
            val = kv[:, c0:c0 + KV_W]
            swapped = pltpu.roll(val, HEAD_DIM, axis=1)
            zero = jnp.zeros_like(val)
            tab[0, 0, t_rows_g, :] = jnp.where(low, val, zero).astype(BF16)
            tab[0, 1, t_rows_g, :] = jnp.where(low, zero, swapped).astype(BF16)
            tab[1, 0, t_rows_g, :] = jnp.where(low, swapped, zero).astype(BF16)
            tab[1, 1, t_rows_g, :] = jnp.where(low, zero, val).astype(BF16)
        for piece in range(BRANCH_W // PROJ_COLS):
            c0 = piece * PROJ_COLS
            gate = _dot(xg, w_ref[:, gate_col0 + c0:gate_col0 + c0 + PROJ_COLS])
            gate_buf[g_rows, c0:c0 + PROJ_COLS] = gate

    low128 = lax.broadcasted_iota(jnp.int32, (BLOCK, LANES), 1) < HEAD_DIM
    project_group(0, None)
    for g0 in range(0, n_blocks, ATTN_GROUP_BLOCKS):
        units = [(n, j) for n in range(g0, g0 + ATTN_GROUP_BLOCKS) for j in range(N_KV_HEADS)]

        for it, (n, j) in enumerate(units):
            r0 = n * BLOCK
            q_cols = j * PAIRS * LANES
            qs = jnp.concatenate(
                [q_buf[r0:r0 + BLOCK, q_cols + p * LANES:q_cols + (p + 1) * LANES] for p in range(PAIRS)], axis=0)
            k2 = jnp.concatenate([k_tab[j, 0, r0:r0 + 2 * BLOCK, :], k_tab[j, 1, r0:r0 + 2 * BLOCK, :]], axis=0)
            logits = lax.dot_general(qs, k2, (((1,), (1,)), ((), ())), preferred_element_type=F32)
            threshold = jnp.where(jnp.logical_and(i == 0, n == 0), 2, 1)
            for p in range(PAIRS):
                rows = slice(p * BLOCK, (p + 1) * BLOCK)
                row_max = []
                for par in range(2):
                    slabs = []
                    for half in range(2):
                        cols = slice((2 * par + half) * LANES, (2 * par + half + 1) * LANES)
                        s = logits[rows, cols] + bias_ref[j, rows, cols]
                        s = jnp.where(code_buf[rows, cols] >= threshold, s, NEG_INF)
                        s_buf[it, rows, cols] = s
                        slabs.append(s)
                    m = jnp.max(jnp.maximum(slabs[0], slabs[1]), axis=1, keepdims=True)
                    m = jnp.maximum(jnp.broadcast_to(m, (BLOCK, LANES)), sink_ref[j, par, rows, :])
                    m_buf[it, par, rows, :] = m
                    row_max.append(m)
                mc_buf[it, rows, :] = jnp.where(low128, row_max[0], row_max[1])

        if g0 + ATTN_GROUP_BLOCKS < n_blocks:
            project_group(g0 + ATTN_GROUP_BLOCKS, logits[0:SUBLANES, 0:LANES])

        for it, (n, j) in enumerate(units):
            for p in range(PAIRS):
                rows = slice(p * BLOCK, (p + 1) * BLOCK)
                for par in range(2):
                    m = m_buf[it, par, rows, :]
                    for half in range(2):
                        cols = slice((2 * par + half) * LANES, (2 * par + half + 1) * LANES)
                        p_buf[it, rows, cols] = jnp.exp(s_buf[it, rows, cols] - m).astype(BF16)

        for it, (n, j) in enumerate(units):
            r0 = n * BLOCK
            q_cols = j * PAIRS * LANES
            v2 = jnp.concatenate([v_tab[j, 0, r0:r0 + 2 * BLOCK, :], v_tab[j, 1, r0:r0 + 2 * BLOCK, :]], axis=0)
            o = _dot(p_buf[it], jnp.concatenate([v2, ones_buf[...]], axis=1))
            for p in range(PAIRS):
                rows = slice(p * BLOCK, (p + 1) * BLOCK)
                cols = slice(q_cols + p * LANES, q_cols + (p + 1) * LANES)
                den = o[rows, LANES:2 * LANES] + jnp.exp(sinkc_ref[j, rows, :] - mc_buf[it, rows, :])
                y = o[rows, 0:LANES] * (1.0 / den) * _silu(gate_buf[r0:r0 + BLOCK, cols])
                o_ref[r0:r0 + BLOCK, cols] = y.astype(BF16)

    for j in range(N_KV_HEADS):
        for half in range(2):
            k_tab[j, half, 0:BLOCK, :] = k_tab[j, half, t_rows:t_rows + BLOCK, :]
            v_tab[j, half, 0:BLOCK, :] = v_tab[j, half, t_rows:t_rows + BLOCK, :]


CONV_COLUMNS = BRANCH_W // LANES
CONV_BLOCKS = BRANCH_W // PROJ_COLS


def _conv_stream_kernel(x_ref, w_ref, cw_ref, cb_ref, lng_ref, lnb_ref, zero_ref, o_ref,
                        xs_buf, glu_buf, gate_buf, shift_buf, y_buf):
    i = pl.program_id(0)
    t_rows = x_ref.shape[0]

    @pl.when(i == 0)
    def _():
        for c in range(CONV_COLUMNS):
            glu_buf[c, 0:CONV_HALO, :] = jnp.zeros((CONV_HALO, LANES), F32)

    xs_buf[...] = x_ref[...].astype(BF16)
    first = CONV_HALO - (CONV_K - 1)
    span = t_rows + CONV_HALO - SUBLANES

    def glu_block(n):
        gate_val = _dot(xs_buf[...], w_ref[CONV_BLOCKS + n])
        glu = _dot(xs_buf[...], w_ref[n]) * _sigmoid(gate_val)
        glu_buf[2 * n, CONV_HALO:CONV_HALO + t_rows, :] = glu[:, 0:LANES]
        glu_buf[2 * n + 1, CONV_HALO:CONV_HALO + t_rows, :] = glu[:, LANES:2 * LANES]
        return gate_val[t_rows - SUBLANES:t_rows, 0:LANES]

    def gate_block(n):
        gate = _dot(xs_buf[...], w_ref[2 * CONV_BLOCKS + n])
        gate_buf[2 * n] = gate[:, 0:LANES]
        gate_buf[2 * n + 1] = gate[:, LANES:2 * LANES]
        return gate[t_rows - SUBLANES:t_rows, 0:LANES]

    def conv_block(n, token, second_trigger):
        for half in range(2):
            c = 2 * n + half
            if half == 1:
                token = token + second_trigger()
            for r in range(1, SUBLANES):
                shift_buf[half, r, 0:span, :] = glu_buf[c, r:r + span, :]
            bias = jnp.broadcast_to(cb_ref[c], (STREAM_ROWS, LANES))
            for r0 in range(0, t_rows, STREAM_ROWS):
                nothing = pltpu.bitcast(pltpu.bitcast(token, jnp.int32) & zero_ref[...], F32)
                acc = bias
                for k in range(CONV_K):
                    a8 = (first + k) // SUBLANES * SUBLANES
                    r = (first + k) % SUBLANES
                    if r == 0:
                        src = glu_buf[c, r0 + a8:r0 + a8 + STREAM_ROWS, :]
                    else:
                        src = shift_buf[half, r, r0 + a8:r0 + a8 + STREAM_ROWS, :]
                    tap = jnp.concatenate([cw_ref[c, k] + nothing] * (STREAM_ROWS // SUBLANES), axis=0)
                    acc = acc + tap * src
                y_buf[c, r0:r0 + STREAM_ROWS, :] = acc
                token = acc[0:SUBLANES, :]

    for n in range(CONV_BLOCKS):
        conv_block(n, glu_block(n), lambda n=n: gate_block(n))

    for r0 in range(0, t_rows, ROW_CHUNK):
        rows = slice(r0, r0 + ROW_CHUNK)
        ys = [y_buf[c, rows, :] for c in range(CONV_COLUMNS)]
        total = ys[0]
        for y in ys[1:]:
            total = total + y
        mu = jnp.sum(total, axis=1, keepdims=True) * (1.0 / BRANCH_W)
        ds = [y - mu for y in ys]
        sq = ds[0] * ds[0]
        for d in ds[1:]:
            sq = sq + d * d
        rstd = lax.rsqrt(jnp.sum(sq, axis=1, keepdims=True) * (1.0 / BRANCH_W) + LN_EPS)
        for c in range(CONV_COLUMNS):
            y = _silu(ds[c] * rstd * lng_ref[c] + lnb_ref[c]) * _silu(gate_buf[c, rows, :])
            o_ref[rows, c * LANES:(c + 1) * LANES] = y.astype(BF16)

    for c in range(CONV_COLUMNS):
        glu_buf[c, 0:CONV_HALO, :] = glu_buf[c, t_rows:t_rows + CONV_HALO, :]


def _conv_kernel(x_ref, w_ref, cw_ref, cb_ref, lng_ref, lnb_ref, o_ref,
                 xs_buf, glu_buf, gate_buf, shift_buf, y_buf):
    i = pl.program_id(0)
    t_rows = x_ref.shape[0]

    @pl.when(i == 0)
    def _():
        for c in range(CONV_COLUMNS):
            glu_buf[c, 0:CONV_HALO, :] = jnp.zeros((CONV_HALO, LANES), F32)

    xs_buf[...] = x_ref[...].astype(BF16)

    def project(n, carry):
        glu = _dot(xs_buf[...], w_ref[n]) * _sigmoid(_dot(xs_buf[...], w_ref[CONV_BLOCKS + n]))
        glu_buf[2 * n, CONV_HALO:CONV_HALO + t_rows, :] = glu[:, 0:LANES]
        glu_buf[2 * n + 1, CONV_HALO:CONV_HALO + t_rows, :] = glu[:, LANES:2 * LANES]
        gate = _dot(xs_buf[...], w_ref[2 * CONV_BLOCKS + n])
        gate_buf[2 * n] = gate[:, 0:LANES]
        gate_buf[2 * n + 1] = gate[:, LANES:2 * LANES]
        return carry

    lax.fori_loop(0, CONV_BLOCKS, project, 0)

    first = CONV_HALO - (CONV_K - 1)
    span = t_rows + CONV_HALO - SUBLANES

    def conv(c, carry):
        for r in range(1, SUBLANES):
            shift_buf[r, 0:span, :] = glu_buf[c, r:r + span, :]
        bias = jnp.broadcast_to(cb_ref[c], (CONV_ROWS, LANES))
        for r0 in range(0, t_rows, CONV_ROWS):
            acc = bias
            for k in range(CONV_K):
                a8 = (first + k) // SUBLANES * SUBLANES
                r = (first + k) % SUBLANES
                if r == 0:
                    src = glu_buf[c, r0 + a8:r0 + a8 + CONV_ROWS, :]
                else:
                    src = shift_buf[r, r0 + a8:r0 + a8 + CONV_ROWS, :]
                tap = jnp.concatenate([cw_ref[c, k]] * (CONV_ROWS // SUBLANES), axis=0)
                acc = acc + tap * src
            y_buf[c, r0:r0 + CONV_ROWS, :] = acc
        return carry

    lax.fori_loop(0, CONV_COLUMNS, conv, 0)

    for r0 in range(0, t_rows, ROW_CHUNK):
        rows = slice(r0, r0 + ROW_CHUNK)
        ys = [y_buf[c, rows, :] for c in range(CONV_COLUMNS)]
        total = ys[0]
        for y in ys[1:]:
            total = total + y
        mu = jnp.sum(total, axis=1, keepdims=True) * (1.0 / BRANCH_W)
        ds = [y - mu for y in ys]
        sq = ds[0] * ds[0]
        for d in ds[1:]:
            sq = sq + d * d
        rstd = lax.rsqrt(jnp.sum(sq, axis=1, keepdims=True) * (1.0 / BRANCH_W) + LN_EPS)
        for c in range(CONV_COLUMNS):
            y = _silu(ds[c] * rstd * lng_ref[c] + lnb_ref[c]) * _silu(gate_buf[c, rows, :])
            o_ref[rows, c * LANES:(c + 1) * LANES] = y.astype(BF16)

    for c in range(CONV_COLUMNS):
        glu_buf[c, 0:CONV_HALO, :] = glu_buf[c, t_rows:t_rows + CONV_HALO, :]


def _merge_kernel(x_ref, ya_ref, yb_ref, yc_ref, yd_ref, wga_ref, wgb_ref, wgc_ref, wgd_ref, wb_ref, o_ref):
    xb = x_ref[...].astype(BF16)
    acc = None
    branches = ((ya_ref, wga_ref), (yb_ref, wgb_ref), (yc_ref, wgc_ref), (yd_ref, wgd_ref))
    for br, (y_ref, wg_ref) in enumerate(branches):
        gate = _sigmoid(_dot(xb, wg_ref[...]))
        term = gate * _dot(y_ref[...], wb_ref[br])
        acc = term if acc is None else acc + term
    o_ref[...] = acc.astype(BF16)


def _out_kernel(m_ref, w_ref, x_ref, g_ref, b_ref, zero_ref, o_ref, ob_ref, z_buf):
    t_rows = x_ref.shape[0]
    half_rows = t_rows // 2
    chunk = ROW_CHUNK // 2
    tokens = []
    for h in range(2):
        rows = slice(h * half_rows, (h + 1) * half_rows)
        z = _dot(m_ref[rows, :], w_ref[...])
        z_buf[rows, :] = z
        tokens.append(z[half_rows - SUBLANES:half_rows, 0:LANES])
    g = g_ref[...]
    b = b_ref[...]
    for h in range(2):
        nothing = pltpu.bitcast(pltpu.bitcast(tokens[h], jnp.int32) & zero_ref[...], F32)
        nothing = jnp.concatenate([nothing] * (chunk // SUBLANES), axis=0)
        nothing = jnp.concatenate([nothing] * (D_MODEL // LANES), axis=1)
        for r0 in range(h * half_rows, (h + 1) * half_rows, chunk):
            rows = slice(r0, r0 + chunk)
            y = _layer_norm_rows(ALPHA * x_ref[rows, :] + (z_buf[rows, :] + nothing), g, b)
            o_ref[rows, :] = y
            ob_ref[rows, :] = y.astype(BF16)


def _resident(shape):
    zeros = (0,) * len(shape)
    return pl.BlockSpec(shape, lambda *_: zeros, pipeline_mode=pl.Buffered(1))


def _nbytes(shape, dtype):
    return math.prod(d for d in shape if d is not None) * jnp.dtype(dtype).itemsize


def _call(kernel_fn, name, grid, inputs, outputs, scratch, value_bytes):
    arrays, in_specs = zip(*inputs)
    out_shape, out_specs = zip(*outputs)
    windows = [(spec, a.dtype) for a, spec in inputs] + [(spec, o.dtype) for o, spec in outputs]
    need = value_bytes + sum(_nbytes(v.shape, v.dtype) for v in scratch)
    for spec, dtype in windows:
        buffers = spec.pipeline_mode.buffer_count if spec.pipeline_mode is not None else 2
        need += buffers * _nbytes(spec.block_shape, dtype)
    assert need <= V7X_VMEM_BYTES, (name, need)
    return pl.pallas_call(
        kernel_fn,
        grid=grid,
        in_specs=list(in_specs),
        out_specs=list(out_specs) if len(out_specs) > 1 else out_specs[0],
        out_shape=list(out_shape) if len(out_shape) > 1 else out_shape[0],
        scratch_shapes=scratch,
        compiler_params=pltpu.CompilerParams(dimension_semantics=("arbitrary",) * len(grid),
                                             vmem_limit_bytes=need),
        name=name,
    )(*arrays)


def _row_tile(width, rows=SEQ_TILE):
    return pl.BlockSpec((rows, width), lambda i: (i, 0))


def _layer_weight(w_all, layer):
    idx = (layer,) + (0,) * (w_all.ndim - 1)
    return pl.BlockSpec((None,) + w_all.shape[1:], lambda *_: idx, pipeline_mode=pl.Buffered(1))


def _branch_call(kernel_fn, name, seq, x_in, w_all, layer, operands, scratch, value_tiles=2):
    value_bytes = value_tiles * _nbytes((SEQ_TILE, BRANCH_W), F32)
    inputs = [(x_in, _row_tile(D_MODEL)), (w_all, _layer_weight(w_all, layer))]
    inputs += [(op, _resident(op.shape)) for op in operands]
    outputs = [(jax.ShapeDtypeStruct((seq, BRANCH_W), BF16), _row_tile(BRANCH_W))]
    return _call(kernel_fn, name, (seq // SEQ_TILE,), inputs, outputs, scratch, value_bytes)


def _layer(x, x_in, band, w_parts, layer, pool_w, pool_scale, sgu_ln_g, sgu_ln_b, sgu_w, sgu_b, sinks,
           conv_w, conv_b, conv_ln_g, conv_ln_b, w_branch16, w_out16, ln_g, ln_b):
    seq = x.shape[0]
    t = SEQ_TILE
    w_pool, w_sgu, w_attn, w_gates, w_conv = w_parts
    row = lambda v: v.reshape(1, -1).astype(F32)

    y_a = _branch_call(
        _pool_kernel, "branch_pool", seq, x_in,w_pool, layer,
        [pool_w.astype(BF16), row(pool_scale)],
        [pltpu.VMEM((t + POOL_HALO, BRANCH_W), F32), pltpu.VMEM((t, BRANCH_W), F32),
         pltpu.VMEM((t, BRANCH_W), BF16)])

    sgu_bias = jnp.repeat(jnp.transpose(sgu_b).astype(F32), SGU_HD, axis=1)
    y_b = _branch_call(
        _sgu_kernel, "branch_sgu", seq, x_in,w_sgu, layer,
        [row(sgu_ln_g), row(sgu_ln_b), sgu_w.astype(F32), sgu_bias],
        [pltpu.VMEM((t, BRANCH_W), F32), pltpu.VMEM((t, BRANCH_W), F32),
         pltpu.VMEM((t, BRANCH_W), F32), pltpu.VMEM((t, BRANCH_W), BF16)])

    sink_hp = sinks.astype(F32).reshape(N_KV_HEADS, PAIRS, 2)
    sink_rows = jnp.repeat(jnp.transpose(sink_hp, (0, 2, 1)), BLOCK, axis=2)
    sink_rows = jnp.broadcast_to(sink_rows[..., None], (N_KV_HEADS, 2, PQ, LANES))
    sink_lanes = jnp.repeat(jnp.repeat(sink_hp, BLOCK, axis=1), HEAD_DIM, axis=2)
    n_units = N_KV_HEADS * ATTN_GROUP_BLOCKS
    y_c = _branch_call(
        _attn_kernel, "branch_attn", seq, x_in,w_attn, layer,
        [band, sink_rows, sink_lanes, jnp.zeros((SUBLANES, LANES), jnp.int32)],
        [pltpu.VMEM((t, D_MODEL), BF16), pltpu.VMEM((t, BRANCH_W), BF16), pltpu.VMEM((t, BRANCH_W), F32),
         pltpu.VMEM((N_KV_HEADS, 2, t + BLOCK, LANES), BF16),
         pltpu.VMEM((N_KV_HEADS, 2, t + BLOCK, LANES), BF16),
         pltpu.VMEM((PQ, 4 * BLOCK), jnp.int32),
         pltpu.VMEM((4 * BLOCK, LANES), BF16),
         pltpu.VMEM((n_units, PQ, 4 * BLOCK), F32),
         pltpu.VMEM((n_units, PQ, 4 * BLOCK), BF16),
         pltpu.VMEM((n_units, 2, PQ, LANES), F32),
         pltpu.VMEM((n_units, PQ, LANES), F32)])

    per_col = lambda v: v.astype(F32).reshape(CONV_COLUMNS, 1, LANES)
    taps = jnp.transpose(conv_w.astype(F32).reshape(CONV_K, CONV_COLUMNS, LANES), (1, 0, 2))
    taps = jnp.broadcast_to(taps[:, :, None, :], (CONV_COLUMNS, CONV_K, SUBLANES, LANES))
    y_d = _branch_call(
        _conv_stream_kernel, "branch_conv", seq, x_in, w_conv, layer,
        [taps, per_col(conv_b), per_col(conv_ln_g), per_col(conv_ln_b), jnp.zeros((SUBLANES, LANES), jnp.int32)],
        [pltpu.VMEM((t, D_MODEL), BF16),
         pltpu.VMEM((CONV_COLUMNS, t + CONV_HALO, LANES), F32),
         pltpu.VMEM((CONV_COLUMNS, t, LANES), F32),
         pltpu.VMEM((2, SUBLANES, t + CONV_HALO - SUBLANES, LANES), F32),
         pltpu.VMEM((CONV_COLUMNS, t, LANES), F32)], value_tiles=4)

    n_cb = D_MODEL // MERGE_COLS
    tm = MERGE_TILE
    once = pl.Buffered(1)
    x_spec = pl.BlockSpec((tm, D_MODEL), lambda c, i: (i, 0))
    y_spec = pl.BlockSpec((tm, BRANCH_W), lambda c, i: (i, 0))
    wg_specs = [pl.BlockSpec((None, D_MODEL, MERGE_COLS),
                             functools.partial(lambda br, c, i: (layer, 0, br * n_cb + c), br), pipeline_mode=once)
                for br in range(N_BRANCH)]
    wb_spec = pl.BlockSpec((None, N_BRANCH, BRANCH_W, MERGE_COLS), lambda c, i: (layer, 0, 0, c),
                           pipeline_mode=once)
    merged = _call(
        _merge_kernel, "merge", (n_cb, seq // tm),
        [(x_in, x_spec)] + [(y, y_spec) for y in (y_a, y_b, y_c, y_d)]
        + [(w_gates, spec) for spec in wg_specs] + [(w_branch16, wb_spec)],
        [(jax.ShapeDtypeStruct((seq, D_MODEL), BF16), pl.BlockSpec((tm, MERGE_COLS), lambda c, i: (i, c)))],
        [], value_bytes=4 * _nbytes((tm, MERGE_COLS), F32))

    x_next, x_next_bf = _call(
        _out_kernel, "out_norm", (seq // t,),
        [(merged, _row_tile(D_MODEL)), (w_out16, _layer_weight(w_out16, layer)), (x, _row_tile(D_MODEL)),
         (row(ln_g), _resident((1, D_MODEL))), (row(ln_b), _resident((1, D_MODEL))),
         (jnp.zeros((SUBLANES, LANES), jnp.int32), _resident((SUBLANES, LANES)))],
        [(jax.ShapeDtypeStruct((seq, D_MODEL), F32), _row_tile(D_MODEL)),
         (jax.ShapeDtypeStruct((seq, D_MODEL), BF16), _row_tile(D_MODEL))],
        [pltpu.VMEM((t, D_MODEL), F32)], value_bytes=_nbytes((t, D_MODEL), F32))
    return x_next, x_next_bf


def kernel(x, w_in, pool_w, pool_scale, sgu_ln_g, sgu_ln_b, sgu_w, sgu_b, attn_sinks, rel_bias, conv_w,
           conv_b, conv_ln_g, conv_ln_b, w_branch, w_out, ln_g, ln_b):
    batch, seq, d_model = x.shape
    assert d_model == D_MODEL and w_in.shape == (DEPTH, D_MODEL, D_IN)
    assert seq % SEQ_TILE == 0 and SEQ_TILE % (BLOCK * ATTN_GROUP_BLOCKS) == 0

    band = _band_bias(rel_bias).reshape(N_KV_HEADS, PAIRS, 2, BLOCK, 2 * BLOCK)
    band = jnp.transpose(band, (0, 1, 3, 2, 4)).reshape(N_KV_HEADS, PQ, 4 * BLOCK)

    w_parts = _cast_w_in(w_in.astype(F32))
    w_branch16 = w_branch.astype(BF16)
    w_out16 = w_out.astype(BF16)
    outs = []
    for bi in range(batch):
        xc = x[bi].astype(F32)
        x_in = xc
        for l in range(DEPTH):
            xc, x_in = _layer(xc, x_in, band, w_parts, l, pool_w[l], pool_scale[l], sgu_ln_g[l], sgu_ln_b[l],
                              sgu_w[l], sgu_b[l], attn_sinks[l], conv_w[l], conv_b[l], conv_ln_g[l],
                              conv_ln_b[l], w_branch16, w_out16, ln_g[l], ln_b[l])
        outs.append(xc.astype(x.dtype)[None])
    return outs[0] if batch == 1 else jnp.concatenate(outs, axis=0)
```

```python
import functools
import math

import jax
import jax.numpy as jnp
from jax import lax
from jax.experimental import pallas as pl
from jax.experimental.pallas import tpu as pltpu

F32 = jnp.float32
BF16 = jnp.bfloat16

D_MODEL = 2048
DEPTH = 2
N_BRANCH = 4
BRANCH_W = 1024
POOL_WINDOWS = (2, 4, 8, 16)
POOL_GW = BRANCH_W // len(POOL_WINDOWS)
SGU_HEADS = 8
SGU_HD = BRANCH_W // SGU_HEADS
CHUNK = 128
N_Q_HEADS = 16
N_KV_HEADS = 2
Q_PER_KV = N_Q_HEADS // N_KV_HEADS
HEAD_DIM = 64
KV_W = N_KV_HEADS * HEAD_DIM
WINDOW = 128
BLOCK = 128
NUM_BUCKETS = 32
MAX_DISTANCE = 128
CONV_K = 31
ALPHA = (2 * DEPTH) ** 0.25
LN_EPS = 1e-5
NEG_INF = -1e30

OFF_A = 0
OFF_B = OFF_A + 2 * BRANCH_W
OFF_C = OFF_B + 3 * BRANCH_W
OFF_D = OFF_C + 2 * BRANCH_W + 2 * KV_W
OFF_G = OFF_D + 3 * BRANCH_W
D_IN = OFF_G + N_BRANCH * D_MODEL

LANES = 128
SUBLANES = 8
SEQ_TILE = 512
POOL_HALO = 16
CONV_HALO = 32
ROW_CHUNK = 64
STREAM_ROWS = 128
PROJ_COLS = 256
MERGE_COLS = 512
PAIRS = Q_PER_KV // 2
PQ = PAIRS * BLOCK
ATTN_GROUP_BLOCKS = 2
MERGE_TILE = 1024
V7X_VMEM_BYTES = 64 * 1024 * 1024


def _sigmoid(x):
    return 0.5 * jnp.tanh(0.5 * x) + 0.5


def _silu(x):
    return x * _sigmoid(x)


def _layer_norm_rows(v, g, b):
    mu = jnp.mean(v, axis=-1, keepdims=True)
    d = v - mu
    var = jnp.mean(d * d, axis=-1, keepdims=True)
    return d * lax.rsqrt(var + LN_EPS) * g + b


def _dot(a, b):
    return jnp.dot(a, b, preferred_element_type=F32)


def _t5_bucket(n):
    max_exact = NUM_BUCKETS // 2
    nf = jnp.maximum(n, 1).astype(F32)
    large = max_exact + (jnp.log(nf / max_exact) / math.log(MAX_DISTANCE / max_exact)
                         * (NUM_BUCKETS - max_exact)).astype(jnp.int32)
    large = jnp.minimum(large, NUM_BUCKETS - 1)
    return jnp.where(n < max_exact, n, large)


def _band_bucket_ids():
    i = jnp.arange(BLOCK)[:, None]
    j = jnp.arange(2 * BLOCK)[None, :]
    return _t5_bucket(jnp.clip(i + BLOCK - j, 0, WINDOW - 1)).astype(jnp.int32)


def _bias_kernel(bucket_ref, table_ref, o_ref):
    h = pl.program_id(0)
    bucket = bucket_ref[...]
    acc = jnp.zeros(bucket.shape, F32)
    for b in range(NUM_BUCKETS):
        acc = jnp.where(bucket == b, table_ref[b, h], acc)
    o_ref[0] = acc


def _band_bias(rel_bias):
    return pl.pallas_call(
        _bias_kernel,
        grid=(N_Q_HEADS,),
        in_specs=[pl.BlockSpec((BLOCK, 2 * BLOCK), lambda h: (0, 0)),
                  pl.BlockSpec(memory_space=pltpu.SMEM)],
        out_specs=pl.BlockSpec((1, BLOCK, 2 * BLOCK), lambda h: (h, 0, 0)),
        out_shape=jax.ShapeDtypeStruct((N_Q_HEADS, BLOCK, 2 * BLOCK), F32),
        name="band_bias",
    )(_band_bucket_ids(), rel_bias.astype(F32))


W_IN_SPLITS = ((OFF_A, OFF_B), (OFF_B, OFF_C), (OFF_C, OFF_D), (OFF_G, D_IN))
CONV_W_CHUNKS = (OFF_G - OFF_D) // PROJ_COLS
CAST_ROWS = 128


def _cast_w_in_kernel(w_ref, oa_ref, ob_ref, oc_ref, og_ref, od_ref):
    for (a, b), o_ref in zip(W_IN_SPLITS, (oa_ref, ob_ref, oc_ref, og_ref)):
        o_ref[...] = w_ref[:, a:b].astype(BF16)
    for j in range(CONV_W_CHUNKS):
        od_ref[j] = w_ref[:, OFF_D + j * PROJ_COLS:OFF_D + (j + 1) * PROJ_COLS].astype(BF16)


def _cast_w_in(w_in):
    widths = [b - a for a, b in W_IN_SPLITS]
    outputs = [(jax.ShapeDtypeStruct((DEPTH, D_MODEL, w), BF16),
                pl.BlockSpec((None, CAST_ROWS, w), lambda l, r: (l, r, 0))) for w in widths]
    outputs.append((jax.ShapeDtypeStruct((DEPTH, CONV_W_CHUNKS, D_MODEL, PROJ_COLS), BF16),
                    pl.BlockSpec((None, CONV_W_CHUNKS, CAST_ROWS, PROJ_COLS), lambda l, r: (l, 0, r, 0))))
    return _call(_cast_w_in_kernel, "cast_w_in", (DEPTH, D_MODEL // CAST_ROWS),
                 [(w_in, pl.BlockSpec((None, CAST_ROWS, D_IN), lambda l, r: (l, r, 0)))],
                 outputs, [], value_bytes=0)


def _pool_kernel(x_ref, w_ref, pw_ref, sc_ref, o_ref, abuf, gate_buf, mix_buf):
    i = pl.program_id(0)
    t_rows = x_ref.shape[0]

    @pl.when(i == 0)
    def _():
        abuf[0:POOL_HALO, :] = jnp.zeros((POOL_HALO, BRANCH_W), F32)

    xb = x_ref[...].astype(BF16)
    abuf[POOL_HALO:POOL_HALO + t_rows, :] = _dot(xb, w_ref[:, 0:BRANCH_W])
    gate_buf[...] = _dot(xb, w_ref[:, BRANCH_W:2 * BRANCH_W])

    for g, win in enumerate(POOL_WINDOWS):
        c0 = g * POOL_GW
        for r0 in range(0, t_rows, ROW_CHUNK):
            base = POOL_HALO + r0
            cur = abuf[base:base + ROW_CHUNK, c0:c0 + POOL_GW]
            acc = cur
            for s in range(1, win):
                acc = acc + abuf[base - s:base - s + ROW_CHUNK, c0:c0 + POOL_GW]
            t = i * t_rows + r0 + lax.broadcasted_iota(jnp.int32, (ROW_CHUNK, POOL_GW), 0)
            cnt = jnp.minimum(t + 1, win).astype(F32)
            mix_buf[r0:r0 + ROW_CHUNK, c0:c0 + POOL_GW] = (acc / cnt - cur).astype(BF16)

    for g in range(len(POOL_WINDOWS)):
        c0 = g * POOL_GW
        y = _dot(mix_buf[:, c0:c0 + POOL_GW], pw_ref[g])
        y = y * sc_ref[:, c0:c0 + POOL_GW] * _silu(gate_buf[:, c0:c0 + POOL_GW])
        o_ref[:, c0:c0 + POOL_GW] = y.astype(BF16)

    abuf[0:POOL_HALO, :] = abuf[t_rows:t_rows + POOL_HALO, :]


def _sgu_kernel(x_ref, w_ref, lng_ref, lnb_ref, sw_ref, sb_ref, o_ref,
                u_buf, v_buf, gate_buf, vn_buf):
    t_rows = x_ref.shape[0]
    xb = x_ref[...].astype(BF16)
    u_buf[...] = _dot(xb, w_ref[:, 0:BRANCH_W])
    v_buf[...] = _dot(xb, w_ref[:, BRANCH_W:2 * BRANCH_W])
    gate_buf[...] = _dot(xb, w_ref[:, 2 * BRANCH_W:3 * BRANCH_W])

    g = lng_ref[...]
    b = lnb_ref[...]
    for r0 in range(0, t_rows, ROW_CHUNK):
        vn = _layer_norm_rows(v_buf[r0:r0 + ROW_CHUNK, :], g, b)
        vn_buf[r0:r0 + ROW_CHUNK, :] = vn.astype(BF16)

    row = lax.broadcasted_iota(jnp.int32, (CHUNK, CHUNK), 0)
    col = lax.broadcasted_iota(jnp.int32, (CHUNK, CHUNK), 1)
    causal = row >= col
    for h in range(SGU_HEADS):
        c0 = h * SGU_HD
        w_h = jnp.where(causal, sw_ref[h], 0.0).astype(BF16)
        bias_h = sb_ref[:, c0:c0 + SGU_HD]
        for r0 in range(0, t_rows, CHUNK):
            sp = _dot(w_h, vn_buf[r0:r0 + CHUNK, c0:c0 + SGU_HD]) + bias_h
            y = u_buf[r0:r0 + CHUNK, c0:c0 + SGU_HD] * sp * _silu(gate_buf[r0:r0 + CHUNK, c0:c0 + SGU_HD])
            o_ref[r0:r0 + CHUNK, c0:c0 + SGU_HD] = y.astype(BF16)


def _attn_kernel(x_ref, w_ref, bias_ref, sink_ref, sinkc_ref, zero_ref, o_ref,
                 xs_buf, q_buf, gate_buf, k_tab, v_tab, code_buf, ones_buf, s_buf, p_buf, m_buf, mc_buf):
    i = pl.program_id(0)
    t_rows = x_ref.shape[0]
    n_blocks = t_rows // BLOCK

    @pl.when(i == 0)
    def _():
        zero = jnp.zeros((BLOCK, LANES), BF16)
        for j in range(N_KV_HEADS):
            for half in range(2):
                k_tab[j, half, 0:BLOCK, :] = zero
                v_tab[j, half, 0:BLOCK, :] = zero
        row = lax.broadcasted_iota(jnp.int32, (PQ, 2 * 2 * BLOCK), 0)
        col = lax.broadcasted_iota(jnp.int32, (PQ, 2 * 2 * BLOCK), 1)
        key = col & (2 * BLOCK - 1)
        u = key - (row & (BLOCK - 1)) - 1
        in_window = (u >= 0) & (u < WINDOW)
        code_buf[...] = jnp.where(in_window, jnp.where(key >= BLOCK, 2, 1), 0)
        krow = lax.broadcasted_iota(jnp.int32, (4 * BLOCK, LANES), 0)
        klane = lax.broadcasted_iota(jnp.int32, (4 * BLOCK, LANES), 1)
        ones_buf[...] = jnp.where((krow >= 2 * BLOCK) == (klane >= HEAD_DIM), 1.0, 0.0).astype(BF16)

    g_len = ATTN_GROUP_BLOCKS * BLOCK
    xs_buf[...] = x_ref[...].astype(BF16)
    low = lax.broadcasted_iota(jnp.int32, (g_len, LANES), 1) < HEAD_DIM
    gate_col0 = BRANCH_W + 2 * KV_W

    def group_operand(g0, after):
        xg = xs_buf[g0 * BLOCK:g0 * BLOCK + g_len, :]
        if after is not None:
            nothing = pltpu.bitcast(after, jnp.int32) & zero_ref[...]
            nothing = jnp.concatenate([nothing] * (g_len // (2 * SUBLANES)), axis=0)
            nothing = jnp.concatenate([nothing] * (D_MODEL // LANES), axis=1)
            xg = pltpu.bitcast(pltpu.bitcast(xg, jnp.int32) | nothing, BF16)
        return xg

    def project_gate(g0, after):
        g_rows = slice(g0 * BLOCK, g0 * BLOCK + g_len)
        xg = group_operand(g0, after)
        for piece in range(BRANCH_W // PROJ_COLS):
            c0 = piece * PROJ_COLS
            gate_buf[g_rows, c0:c0 + PROJ_COLS] = _dot(xg, w_ref[:, gate_col0 + c0:gate_col0 + c0 + PROJ_COLS])

    def project_qkv(g0, after):
        g_rows = slice(g0 * BLOCK, g0 * BLOCK + g_len)
        xg = group_operand(g0, after)
        for piece in range(BRANCH_W // PROJ_COLS):
            c0 = piece * PROJ_COLS
            q = _dot(xg, w_ref[:, c0:c0 + PROJ_COLS]) * (HEAD_DIM ** -0.5)
            q_buf[g_rows, c0:c0 + PROJ_COLS] = q.astype(BF16)
        kv = _dot(xg, w_ref[:, BRANCH_W:BRANCH_W + 2 * KV_W])
        t_rows_g = slice(BLOCK + g0 * BLOCK, BLOCK + g0 * BLOCK + g_len)
        for tab, c0 in ((k_tab, 0), (v_tab, KV_W)):
            val = kv[:, c0:c0 + KV_W]
            swapped = pltpu.roll(val, HEAD_DIM, axis=1)
            zero = jnp.zeros_like(val)
            tab[0, 0, t_rows_g, :] = jnp.where(low, val, zero).astype(BF16)
            tab[0, 1, t_rows_g, :] = jnp.where(low, zero, swapped).astype(BF16)
            tab[1, 0, t_rows_g, :] = jnp.where(low, swapped, zero).astype(BF16)
            tab[1, 1, t_rows_g, :] = jnp.where(low, zero, val).astype(BF16)

    low128 = lax.broadcasted_iota(jnp.int32, (BLOCK, LANES), 1) < HEAD_DIM

    project_qkv(0, None)
    for g0 in range(0, n_blocks, ATTN_GROUP_BLOCKS):
        units = [(n, j) for n in range(g0, g0 + ATTN_GROUP_BLOCKS) for j in range(N_KV_HEADS)]

        for it, (n, j) in enumerate(units):
            r0 = n * BLOCK
            q_cols = j * PAIRS * LANES
            qs = jnp.concatenate(
                [q_buf[r0:r0 + BLOCK, q_cols + p * LANES:q_cols + (p + 1) * LANES] for p in range(PAIRS)], axis=0)
            k2 = jnp.concatenate([k_tab[j, 0, r0:r0 + 2 * BLOCK, :], k_tab[j, 1, r0:r0 + 2 * BLOCK, :]], axis=0)
            logits = lax.dot_general(qs, k2, (((1,), (1,)), ((), ())), preferred_element_type=F32)
            threshold = jnp.where(jnp.logical_and(i == 0, n == 0), 2, 1)
            for p in range(PAIRS):
                rows = slice(p * BLOCK, (p + 1) * BLOCK)
                row_max = []
                for par in range(2):
                    slabs = []
                    for half in range(2):
                        cols = slice((2 * par + half) * LANES, (2 * par + half + 1) * LANES)
                        s = logits[rows, cols] + bias_ref[j, rows, cols]
                        s = jnp.where(code_buf[rows, cols] >= threshold, s, NEG_INF)
                        s_buf[it, rows, cols] = s
                        slabs.append(s)
                    m = jnp.max(jnp.maximum(slabs[0], slabs[1]), axis=1, keepdims=True)
                    m = jnp.maximum(jnp.broadcast_to(m, (BLOCK, LANES)), sink_ref[j, par, rows, :])
                    m_buf[it, par, rows, :] = m
                    row_max.append(m)
                mc_buf[it, rows, :] = jnp.where(low128, row_max[0], row_max[1])

        after = logits[0:SUBLANES, 0:LANES]
        if g0 + ATTN_GROUP_BLOCKS < n_blocks:
            project_qkv(g0 + ATTN_GROUP_BLOCKS, after)
        project_gate(g0, after)

        for it, (n, j) in enumerate(units):
            for p in range(PAIRS):
                rows = slice(p * BLOCK, (p + 1) * BLOCK)
                for par in range(2):
                    m = m_buf[it, par, rows, :]
                    for half in range(2):
                        cols = slice((2 * par + half) * LANES, (2 * par + half + 1) * LANES)
                        p_buf[it, rows, cols] = jnp.exp(s_buf[it, rows, cols] - m).astype(BF16)

        for it, (n, j) in enumerate(units):
            r0 = n * BLOCK
            q_cols = j * PAIRS * LANES
            v2 = jnp.concatenate([v_tab[j, 0, r0:r0 + 2 * BLOCK, :], v_tab[j, 1, r0:r0 + 2 * BLOCK, :]], axis=0)
            o = _dot(p_buf[it], jnp.concatenate([v2, ones_buf[...]], axis=1))
            for p in range(PAIRS):
                rows = slice(p * BLOCK, (p + 1) * BLOCK)
                cols = slice(q_cols + p * LANES, q_cols + (p + 1) * LANES)
                den = o[rows, LANES:2 * LANES] + jnp.exp(sinkc_ref[j, rows, :] - mc_buf[it, rows, :])
                y = o[rows, 0:LANES] * (1.0 / den) * _silu(gate_buf[r0:r0 + BLOCK, cols])
                o_ref[r0:r0 + BLOCK, cols] = y.astype(BF16)

    for tab in (k_tab, v_tab):
        for j in range(N_KV_HEADS):
            for half in range(2):
                tab[j, half, 0:BLOCK, :] = tab[j, half, t_rows:t_rows + BLOCK, :]


CONV_COLUMNS = BRANCH_W // LANES
CONV_BLOCKS = BRANCH_W // PROJ_COLS


def _conv_stream_kernel(x_ref, w_ref, cw_ref, cb_ref, lng_ref, lnb_ref, zero_ref, o_ref,
                        xs_buf, glu_buf, gate_buf, shift_buf, y_buf):
    i = pl.program_id(0)
    t_rows = x_ref.shape[0]

    @pl.when(i == 0)
    def _():
        for c in range(CONV_COLUMNS):
            glu_buf[c, 0:CONV_HALO, :] = jnp.zeros((CONV_HALO, LANES), F32)

    xs_buf[...] = x_ref[...].astype(BF16)
    first = CONV_HALO - (CONV_K - 1)
    span = t_rows + CONV_HALO - SUBLANES

    def glu_block(n):
        gate_val = _dot(xs_buf[...], w_ref[CONV_BLOCKS + n])
        glu = _dot(xs_buf[...], w_ref[n]) * _sigmoid(gate_val)
        glu_buf[2 * n, CONV_HALO:CONV_HALO + t_rows, :] = glu[:, 0:LANES]
        glu_buf[2 * n + 1, CONV_HALO:CONV_HALO + t_rows, :] = glu[:, LANES:2 * LANES]
        return gate_val[t_rows - SUBLANES:t_rows, 0:LANES]

    def gate_block(n):
        gate = _dot(xs_buf[...], w_ref[2 * CONV_BLOCKS + n])
        gate_buf[2 * n] = gate[:, 0:LANES]
        gate_buf[2 * n + 1] = gate[:, LANES:2 * LANES]
        return gate[t_rows - SUBLANES:t_rows, 0:LANES]

    def conv_block(n, token, second_trigger):
        for half in range(2):
            c = 2 * n + half
            if half == 1:
                token = token + second_trigger()
            for r in range(1, SUBLANES):
                shift_buf[half, r, 0:span, :] = glu_buf[c, r:r + span, :]
            bias = jnp.broadcast_to(cb_ref[c], (STREAM_ROWS, LANES))
            for r0 in range(0, t_rows, STREAM_ROWS):
                nothing = pltpu.bitcast(pltpu.bitcast(token, jnp.int32) & zero_ref[...], F32)
                acc = bias
                for k in range(CONV_K):
                    a8 = (first + k) // SUBLANES * SUBLANES
                    r = (first + k) % SUBLANES
                    if r == 0:
                        src = glu_buf[c, r0 + a8:r0 + a8 + STREAM_ROWS, :]
                    else:
                        src = shift_buf[half, r, r0 + a8:r0 + a8 + STREAM_ROWS, :]
                    tap = jnp.concatenate([cw_ref[c, k] + nothing] * (STREAM_ROWS // SUBLANES), axis=0)
                    acc = acc + tap * src
                y_buf[c, r0:r0 + STREAM_ROWS, :] = acc
                token = acc[0:SUBLANES, :]

    for n in range(CONV_BLOCKS):
        conv_block(n, glu_block(n), lambda n=n: gate_block(n))

    for r0 in range(0, t_rows, ROW_CHUNK):
        rows = slice(r0, r0 + ROW_CHUNK)
        ys = [y_buf[c, rows, :] for c in range(CONV_COLUMNS)]
        total = ys[0]
        for y in ys[1:]:
            total = total + y
        mu = jnp.sum(total, axis=1, keepdims=True) * (1.0 / BRANCH_W)
        ds = [y - mu for y in ys]
        sq = ds[0] * ds[0]
        for d in ds[1:]:
            sq = sq + d * d
        rstd = lax.rsqrt(jnp.sum(sq, axis=1, keepdims=True) * (1.0 / BRANCH_W) + LN_EPS)
        for c in range(CONV_COLUMNS):
            y = _silu(ds[c] * rstd * lng_ref[c] + lnb_ref[c]) * _silu(gate_buf[c, rows, :])
            o_ref[rows, c * LANES:(c + 1) * LANES] = y.astype(BF16)

    for c in range(CONV_COLUMNS):
        glu_buf[c, 0:CONV_HALO, :] = glu_buf[c, t_rows:t_rows + CONV_HALO, :]


def _merge_kernel(x_ref, ya_ref, yb_ref, yc_ref, yd_ref, wga_ref, wgb_ref, wgc_ref, wgd_ref, wb_ref, o_ref):
    xb = x_ref[...].astype(BF16)
    acc = None
    branches = ((ya_ref, wga_ref), (yb_ref, wgb_ref), (yc_ref, wgc_ref), (yd_ref, wgd_ref))
    for br, (y_ref, wg_ref) in enumerate(branches):
        gate = _sigmoid(_dot(xb, wg_ref[...]))
        term = gate * _dot(y_ref[...], wb_ref[br])
        acc = term if acc is None else acc + term
    o_ref[...] = acc.astype(BF16)


def _out_kernel(m_ref, w_ref, x_ref, g_ref, b_ref, zero_ref, o_ref, ob_ref, z_buf):
    t_rows = x_ref.shape[0]
    half_rows = t_rows // 2
    chunk = ROW_CHUNK // 2
    tokens = []
    for h in range(2):
        rows = slice(h * half_rows, (h + 1) * half_rows)
        z = _dot(m_ref[rows, :], w_ref[...])
        z_buf[rows, :] = z
        tokens.append(z[half_rows - SUBLANES:half_rows, 0:LANES])
    g = g_ref[...]
    b = b_ref[...]
    for h in range(2):
        nothing = pltpu.bitcast(pltpu.bitcast(tokens[h], jnp.int32) & zero_ref[...], F32)
        nothing = jnp.concatenate([nothing] * (chunk // SUBLANES), axis=0)
        nothing = jnp.concatenate([nothing] * (D_MODEL // LANES), axis=1)
        for r0 in range(h * half_rows, (h + 1) * half_rows, chunk):
            rows = slice(r0, r0 + chunk)
            y = _layer_norm_rows(ALPHA * x_ref[rows, :] + (z_buf[rows, :] + nothing), g, b)
            o_ref[rows, :] = y
            ob_ref[rows, :] = y.astype(BF16)


def _resident(shape):
    zeros = (0,) * len(shape)
    return pl.BlockSpec(shape, lambda *_: zeros, pipeline_mode=pl.Buffered(1))


def _nbytes(shape, dtype):
    return math.prod(d for d in shape if d is not None) * jnp.dtype(dtype).itemsize


def _call(kernel_fn, name, grid, inputs, outputs, scratch, value_bytes):
    arrays, in_specs = zip(*inputs)
    out_shape, out_specs = zip(*outputs)
    windows = [(spec, a.dtype) for a, spec in inputs] + [(spec, o.dtype) for o, spec in outputs]
    need = value_bytes + sum(_nbytes(v.shape, v.dtype) for v in scratch)
    for spec, dtype in windows:
        buffers = spec.pipeline_mode.buffer_count if spec.pipeline_mode is not None else 2
        need += buffers * _nbytes(spec.block_shape, dtype)
    assert need <= V7X_VMEM_BYTES, (name, need)
    return pl.pallas_call(
        kernel_fn,
        grid=grid,
        in_specs=list(in_specs),
        out_specs=list(out_specs) if len(out_specs) > 1 else out_specs[0],
        out_shape=list(out_shape) if len(out_shape) > 1 else out_shape[0],
        scratch_shapes=scratch,
        compiler_params=pltpu.CompilerParams(dimension_semantics=("arbitrary",) * len(grid),
                                             vmem_limit_bytes=need),
        name=name,
    )(*arrays)


def _row_tile(width, rows=SEQ_TILE):
    return pl.BlockSpec((rows, width), lambda i: (i, 0))


def _layer_weight(w_all, layer):
    idx = (layer,) + (0,) * (w_all.ndim - 1)
    return pl.BlockSpec((None,) + w_all.shape[1:], lambda *_: idx, pipeline_mode=pl.Buffered(1))


def _branch_call(kernel_fn, name, seq, x_in, w_all, layer, operands, scratch, value_tiles=2):
    value_bytes = value_tiles * _nbytes((SEQ_TILE, BRANCH_W), F32)
    inputs = [(x_in, _row_tile(D_MODEL)), (w_all, _layer_weight(w_all, layer))]
    inputs += [(op, _resident(op.shape)) for op in operands]
    outputs = [(jax.ShapeDtypeStruct((seq, BRANCH_W), BF16), _row_tile(BRANCH_W))]
    return _call(kernel_fn, name, (seq // SEQ_TILE,), inputs, outputs, scratch, value_bytes)


def _layer(x, x_in, band, w_parts, layer, pool_w, pool_scale, sgu_ln_g, sgu_ln_b, sgu_w, sgu_b, sinks,
           conv_w, conv_b, conv_ln_g, conv_ln_b, w_branch16, w_out16, ln_g, ln_b):
    seq = x.shape[0]
    t = SEQ_TILE
    w_pool, w_sgu, w_attn, w_gates, w_conv = w_parts
    row = lambda v: v.reshape(1, -1).astype(F32)

    y_a = _branch_call(
        _pool_kernel, "branch_pool", seq, x_in, w_pool, layer,
        [pool_w.astype(BF16), row(pool_scale)],
        [pltpu.VMEM((t + POOL_HALO, BRANCH_W), F32), pltpu.VMEM((t, BRANCH_W), F32),
         pltpu.VMEM((t, BRANCH_W), BF16)])

    sgu_bias = jnp.repeat(jnp.transpose(sgu_b).astype(F32), SGU_HD, axis=1)
    y_b = _branch_call(
        _sgu_kernel, "branch_sgu", seq, x_in, w_sgu, layer,
        [row(sgu_ln_g), row(sgu_ln_b), sgu_w.astype(F32), sgu_bias],
        [pltpu.VMEM((t, BRANCH_W), F32), pltpu.VMEM((t, BRANCH_W), F32),
         pltpu.VMEM((t, BRANCH_W), F32), pltpu.VMEM((t, BRANCH_W), BF16)])

    sink_hp = sinks.astype(F32).reshape(N_KV_HEADS, PAIRS, 2)
    sink_rows = jnp.repeat(jnp.transpose(sink_hp, (0, 2, 1)), BLOCK, axis=2)
    sink_rows = jnp.broadcast_to(sink_rows[..., None], (N_KV_HEADS, 2, PQ, LANES))
    sink_lanes = jnp.repeat(jnp.repeat(sink_hp, BLOCK, axis=1), HEAD_DIM, axis=2)
    n_units = N_KV_HEADS * ATTN_GROUP_BLOCKS
    y_c = _branch_call(
        _attn_kernel, "branch_attn", seq, x_in, w_attn, layer,
        [band, sink_rows, sink_lanes, jnp.zeros((SUBLANES, LANES), jnp.int32)],
        [pltpu.VMEM((t, D_MODEL), BF16), pltpu.VMEM((t, BRANCH_W), BF16), pltpu.VMEM((t, BRANCH_W), F32),
         pltpu.VMEM((N_KV_HEADS, 2, t + BLOCK, LANES), BF16),
         pltpu.VMEM((N_KV_HEADS, 2, t + BLOCK, LANES), BF16),
         pltpu.VMEM((PQ, 4 * BLOCK), jnp.int32),
         pltpu.VMEM((4 * BLOCK, LANES), BF16),
         pltpu.VMEM((n_units, PQ, 4 * BLOCK), F32),
         pltpu.VMEM((n_units, PQ, 4 * BLOCK), BF16),
         pltpu.VMEM((n_units, 2, PQ, LANES), F32),
         pltpu.VMEM((n_units, PQ, LANES), F32)])

    per_col = lambda v: v.astype(F32).reshape(CONV_COLUMNS, 1, LANES)
    taps = jnp.transpose(conv_w.astype(F32).reshape(CONV_K, CONV_COLUMNS, LANES), (1, 0, 2))
    taps = jnp.broadcast_to(taps[:, :, None, :], (CONV_COLUMNS, CONV_K, SUBLANES, LANES))
    y_d = _branch_call(
        _conv_stream_kernel, "branch_conv", seq, x_in, w_conv, layer,
        [taps, per_col(conv_b), per_col(conv_ln_g), per_col(conv_ln_b), jnp.zeros((SUBLANES, LANES), jnp.int32)],
        [pltpu.VMEM((t, D_MODEL), BF16),
         pltpu.VMEM((CONV_COLUMNS, t + CONV_HALO, LANES), F32),
         pltpu.VMEM((CONV_COLUMNS, t, LANES), F32),
         pltpu.VMEM((2, SUBLANES, t + CONV_HALO - SUBLANES, LANES), F32),
         pltpu.VMEM((CONV_COLUMNS, t, LANES), F32)], value_tiles=4)

    n_cb = D_MODEL // MERGE_COLS
    tm = MERGE_TILE
    once = pl.Buffered(1)
    x_spec = pl.BlockSpec((tm, D_MODEL), lambda c, i: (i, 0))
    y_spec = pl.BlockSpec((tm, BRANCH_W), lambda c, i: (i, 0))
    wg_specs = [pl.BlockSpec((None, D_MODEL, MERGE_COLS),
                             functools.partial(lambda br, c, i: (layer, 0, br * n_cb + c), br), pipeline_mode=once)
                for br in range(N_BRANCH)]
    wb_spec = pl.BlockSpec((None, N_BRANCH, BRANCH_W, MERGE_COLS), lambda c, i: (layer, 0, 0, c),
                           pipeline_mode=once)
    merged = _call(
        _merge_kernel, "merge", (n_cb, seq // tm),
        [(x_in, x_spec)] + [(y, y_spec) for y in (y_a, y_b, y_c, y_d)]
        + [(w_gates, spec) for spec in wg_specs] + [(w_branch16, wb_spec)],
        [(jax.ShapeDtypeStruct((seq, D_MODEL), BF16), pl.BlockSpec((tm, MERGE_COLS), lambda c, i: (i, c)))],
        [], value_bytes=4 * _nbytes((tm, MERGE_COLS), F32))

    x_next, x_next_bf = _call(
        _out_kernel, "out_norm", (seq // t,),
        [(merged, _row_tile(D_MODEL)), (w_out16, _layer_weight(w_out16, layer)), (x, _row_tile(D_MODEL)),
         (row(ln_g), _resident((1, D_MODEL))), (row(ln_b), _resident((1, D_MODEL))),
         (jnp.zeros((SUBLANES, LANES), jnp.int32), _resident((SUBLANES, LANES)))],
        [(jax.ShapeDtypeStruct((seq, D_MODEL), F32), _row_tile(D_MODEL)),
         (jax.ShapeDtypeStruct((seq, D_MODEL), BF16), _row_tile(D_MODEL))],
        [pltpu.VMEM((t, D_MODEL), F32)], value_bytes=_nbytes((t, D_MODEL), F32))
    return x_next, x_next_bf


def kernel(x, w_in, pool_w, pool_scale, sgu_ln_g, sgu_ln_b, sgu_w, sgu_b, attn_sinks, rel_bias, conv_w,
           conv_b, conv_ln_g, conv_ln_b, w_branch, w_out, ln_g, ln_b):
    batch, seq, d_model = x.shape
    assert d_model == D_MODEL and w_in.shape == (DEPTH, D_MODEL, D_IN)
    assert seq % SEQ_TILE == 0 and SEQ_TILE % (BLOCK * ATTN_GROUP_BLOCKS) == 0

    band = _band_bias(rel_bias).reshape(N_KV_HEADS, PAIRS, 2, BLOCK, 2 * BLOCK)
    band = jnp.transpose(band, (0, 1, 3, 2, 4)).reshape(N_KV_HEADS, PQ, 4 * BLOCK)

    w_parts = _cast_w_in(w_in.astype(F32))
    w_branch16 = w_branch.astype(BF16)
    w_out16 = w_out.astype(BF16)
    outs = []
    for bi in range(batch):
        xc = x[bi].astype(F32)
        x_in = xc
        for l in range(DEPTH):
            xc, x_in = _layer(xc, x_in, band, w_parts, l, pool_w[l], pool_scale[l], sgu_ln_g[l], sgu_ln_b[l],
                              sgu_w[l], sgu_b[l], attn_sinks[l], conv_w[l], conv_b[l], conv_ln_g[l],
                              conv_ln_b[l], w_branch16, w_out16, ln_g[l], ln_b[l])
        outs.append(xc.astype(x.dtype)[None])
    return outs[0] if batch == 1 else jnp.concatenate(outs, axis=0)
```

```python
import functools
import math

import jax
import jax.numpy as jnp
from jax import lax
from jax.experimental import pallas as pl
from jax.experimental.pallas import tpu as pltpu

F32 = jnp.float32
BF16 = jnp.bfloat16

D_MODEL = 2048
DEPTH = 2
N_BRANCH = 4
BRANCH_W = 1024
POOL_WINDOWS = (2, 4, 8, 16)
POOL_GW = BRANCH_W // len(POOL_WINDOWS)
SGU_HEADS = 8
SGU_HD = BRANCH_W // SGU_HEADS
CHUNK = 128
N_Q_HEADS = 16
N_KV_HEADS = 2
Q_PER_KV = N_Q_HEADS // N_KV_HEADS
HEAD_DIM = 64
KV_W = N_KV_HEADS * HEAD_DIM
WINDOW = 128
BLOCK = 128
NUM_BUCKETS = 32
MAX_DISTANCE = 128
CONV_K = 31
ALPHA = (2 * DEPTH) ** 0.25
LN_EPS = 1e-5
NEG_INF = -1e30

OFF_A = 0
OFF_B = OFF_A + 2 * BRANCH_W
OFF_C = OFF_B + 3 * BRANCH_W
OFF_D = OFF_C + 2 * BRANCH_W + 2 * KV_W
OFF_G = OFF_D + 3 * BRANCH_W
D_IN = OFF_G + N_BRANCH * D_MODEL

LANES = 128
SUBLANES = 8
SEQ_TILE = 512
POOL_HALO = 16
CONV_HALO = 32
ROW_CHUNK = 64
STREAM_ROWS = 128
PROJ_COLS = 256
MERGE_COLS = 512
PAIRS = Q_PER_KV // 2
PQ = PAIRS * BLOCK
ATTN_GROUP_BLOCKS = 2
MERGE_TILE = 512
V7X_VMEM_BYTES = 64 * 1024 * 1024


def _sigmoid(x):
    return 0.5 * jnp.tanh(0.5 * x) + 0.5


def _silu(x):
    return x * _sigmoid(x)


def _layer_norm_rows(v, g, b):
    mu = jnp.mean(v, axis=-1, keepdims=True)
    d = v - mu
    var = jnp.mean(d * d, axis=-1, keepdims=True)
    return d * lax.rsqrt(var + LN_EPS) * g + b


def _dot(a, b):
    return jnp.dot(a, b, preferred_element_type=F32)


def _t5_bucket(n):
    max_exact = NUM_BUCKETS // 2
    nf = jnp.maximum(n, 1).astype(F32)
    large = max_exact + (jnp.log(nf / max_exact) / math.log(MAX_DISTANCE / max_exact)
                         * (NUM_BUCKETS - max_exact)).astype(jnp.int32)
    large = jnp.minimum(large, NUM_BUCKETS - 1)
    return jnp.where(n < max_exact, n, large)


def _band_bucket_ids():
    i = jnp.arange(BLOCK)[:, None]
    j = jnp.arange(2 * BLOCK)[None, :]
    return _t5_bucket(jnp.clip(i + BLOCK - j, 0, WINDOW - 1)).astype(jnp.int32)


def _bias_kernel(bucket_ref, table_ref, o_ref):
    h = pl.program_id(0)
    bucket = bucket_ref[...]
    acc = jnp.zeros(bucket.shape, F32)
    for b in range(NUM_BUCKETS):
        acc = jnp.where(bucket == b, table_ref[b, h], acc)
    o_ref[0] = acc


def _band_bias(rel_bias):
    return pl.pallas_call(
        _bias_kernel,
        grid=(N_Q_HEADS,),
        in_specs=[pl.BlockSpec((BLOCK, 2 * BLOCK), lambda h: (0, 0)),
                  pl.BlockSpec(memory_space=pltpu.SMEM)],
        out_specs=pl.BlockSpec((1, BLOCK, 2 * BLOCK), lambda h: (h, 0, 0)),
        out_shape=jax.ShapeDtypeStruct((N_Q_HEADS, BLOCK, 2 * BLOCK), F32),
        name="band_bias",
    )(_band_bucket_ids(), rel_bias.astype(F32))


W_IN_SPLITS = ((OFF_A, OFF_B), (OFF_B, OFF_C), (OFF_C, OFF_D), (OFF_G, D_IN))
CONV_W_CHUNKS = (OFF_G - OFF_D) // PROJ_COLS
CAST_ROWS = 128


def _cast_w_in_kernel(w_ref, oa_ref, ob_ref, oc_ref, og_ref, od_ref):
    for (a, b), o_ref in zip(W_IN_SPLITS, (oa_ref, ob_ref, oc_ref, og_ref)):
        o_ref[...] = w_ref[:, a:b].astype(BF16)
    for j in range(CONV_W_CHUNKS):
        od_ref[j] = w_ref[:, OFF_D + j * PROJ_COLS:OFF_D + (j + 1) * PROJ_COLS].astype(BF16)


def _cast_w_in(w_in):
    widths = [b - a for a, b in W_IN_SPLITS]
    outputs = [(jax.ShapeDtypeStruct((DEPTH, D_MODEL, w), BF16),
                pl.BlockSpec((None, CAST_ROWS, w), lambda l, r: (l, r, 0))) for w in widths]
    outputs.append((jax.ShapeDtypeStruct((DEPTH, CONV_W_CHUNKS, D_MODEL, PROJ_COLS), BF16),
                    pl.BlockSpec((None, CONV_W_CHUNKS, CAST_ROWS, PROJ_COLS), lambda l, r: (l, 0, r, 0))))
    return _call(_cast_w_in_kernel, "cast_w_in", (DEPTH, D_MODEL // CAST_ROWS),
                 [(w_in, pl.BlockSpec((None, CAST_ROWS, D_IN), lambda l, r: (l, r, 0)))],
                 outputs, [], value_bytes=0)


def _pool_kernel(x_ref, w_ref, pw_ref, sc_ref, o_ref, abuf, gate_buf, mix_buf):
    i = pl.program_id(0)
    t_rows = x_ref.shape[0]

    @pl.when(i == 0)
    def _():
        abuf[0:POOL_HALO, :] = jnp.zeros((POOL_HALO, BRANCH_W), F32)

    xb = x_ref[...].astype(BF16)
    abuf[POOL_HALO:POOL_HALO + t_rows, :] = _dot(xb, w_ref[:, 0:BRANCH_W])
    gate_buf[...] = _dot(xb, w_ref[:, BRANCH_W:2 * BRANCH_W])

    for g, win in enumerate(POOL_WINDOWS):
        c0 = g * POOL_GW
        for r0 in range(0, t_rows, ROW_CHUNK):
            base = POOL_HALO + r0
            cur = abuf[base:base + ROW_CHUNK, c0:c0 + POOL_GW]
            acc = cur
            for s in range(1, win):
                acc = acc + abuf[base - s:base - s + ROW_CHUNK, c0:c0 + POOL_GW]
            t = i * t_rows + r0 + lax.broadcasted_iota(jnp.int32, (ROW_CHUNK, POOL_GW), 0)
            cnt = jnp.minimum(t + 1, win).astype(F32)
            mix_buf[r0:r0 + ROW_CHUNK, c0:c0 + POOL_GW] = (acc / cnt - cur).astype(BF16)

    for g in range(len(POOL_WINDOWS)):
        c0 = g * POOL_GW
        y = _dot(mix_buf[:, c0:c0 + POOL_GW], pw_ref[g])
        y = y * sc_ref[:, c0:c0 + POOL_GW] * _silu(gate_buf[:, c0:c0 + POOL_GW])
        o_ref[:, c0:c0 + POOL_GW] = y.astype(BF16)

    abuf[0:POOL_HALO, :] = abuf[t_rows:t_rows + POOL_HALO, :]


def _sgu_kernel(x_ref, w_ref, lng_ref, lnb_ref, sw_ref, sb_ref, o_ref,
                u_buf, v_buf, gate_buf, vn_buf):
    t_rows = x_ref.shape[0]
    xb = x_ref[...].astype(BF16)
    u_buf[...] = _dot(xb, w_ref[:, 0:BRANCH_W])
    v_buf[...] = _dot(xb, w_ref[:, BRANCH_W:2 * BRANCH_W])
    gate_buf[...] = _dot(xb, w_ref[:, 2 * BRANCH_W:3 * BRANCH_W])

    g = lng_ref[...]
    b = lnb_ref[...]
    for r0 in range(0, t_rows, ROW_CHUNK):
        vn = _layer_norm_rows(v_buf[r0:r0 + ROW_CHUNK, :], g, b)
        vn_buf[r0:r0 + ROW_CHUNK, :] = vn.astype(BF16)

    row = lax.broadcasted_iota(jnp.int32, (CHUNK, CHUNK), 0)
    col = lax.broadcasted_iota(jnp.int32, (CHUNK, CHUNK), 1)
    causal = row >= col
    for h in range(SGU_HEADS):
        c0 = h * SGU_HD
        w_h = jnp.where(causal, sw_ref[h], 0.0).astype(BF16)
        bias_h = sb_ref[:, c0:c0 + SGU_HD]
        for r0 in range(0, t_rows, CHUNK):
            sp = _dot(w_h, vn_buf[r0:r0 + CHUNK, c0:c0 + SGU_HD]) + bias_h
            y = u_buf[r0:r0 + CHUNK, c0:c0 + SGU_HD] * sp * _silu(gate_buf[r0:r0 + CHUNK, c0:c0 + SGU_HD])
            o_ref[r0:r0 + CHUNK, c0:c0 + SGU_HD] = y.astype(BF16)


def _attn_kernel(x_ref, w_ref, bias_ref, sink_ref, sinkc_ref, zero_ref, o_ref,
                 xs_buf, q_buf, gate_buf, k_tab, v_tab, code_buf, ones_buf, s_buf, p_buf, m_buf, mc_buf):
    i = pl.program_id(0)
    t_rows = x_ref.shape[0]
    n_blocks = t_rows // BLOCK

    @pl.when(i == 0)
    def _():
        zero = jnp.zeros((BLOCK, LANES), BF16)
        for j in range(N_KV_HEADS):
            for half in range(2):
                k_tab[j, half, 0:BLOCK, :] = zero
                v_tab[j, half, 0:BLOCK, :] = zero
        row = lax.broadcasted_iota(jnp.int32, (PQ, 2 * 2 * BLOCK), 0)
        col = lax.broadcasted_iota(jnp.int32, (PQ, 2 * 2 * BLOCK), 1)
        key = col & (2 * BLOCK - 1)
        u = key - (row & (BLOCK - 1)) - 1
        in_window = (u >= 0) & (u < WINDOW)
        code_buf[...] = jnp.where(in_window, jnp.where(key >= BLOCK, 2, 1), 0)
        krow = lax.broadcasted_iota(jnp.int32, (4 * BLOCK, LANES), 0)
        klane = lax.broadcasted_iota(jnp.int32, (4 * BLOCK, LANES), 1)
        ones_buf[...] = jnp.where((krow >= 2 * BLOCK) == (klane >= HEAD_DIM), 1.0, 0.0).astype(BF16)

    g_len = ATTN_GROUP_BLOCKS * BLOCK
    xs_buf[...] = x_ref[...].astype(BF16)
    low = lax.broadcasted_iota(jnp.int32, (g_len, LANES), 1) < HEAD_DIM
    gate_col0 = BRANCH_W + 2 * KV_W

    def project_group(g0, after):
        g_rows = slice(g0 * BLOCK, g0 * BLOCK + g_len)
        xg = xs_buf[g_rows, :]
        if after is not None:
            nothing = pltpu.bitcast(after, jnp.int32) & zero_ref[...]
            nothing = jnp.concatenate([nothing] * (g_len // (2 * SUBLANES)), axis=0)
            nothing = jnp.concatenate([nothing] * (D_MODEL // LANES), axis=1)
            xg = pltpu.bitcast(pltpu.bitcast(xg, jnp.int32) | nothing, BF16)
        for piece in range(BRANCH_W // PROJ_COLS):
            c0 = piece * PROJ_COLS
            q = _dot(xg, w_ref[:, c0:c0 + PROJ_COLS]) * (HEAD_DIM ** -0.5)
            q_buf[g_rows, c0:c0 + PROJ_COLS] = q.astype(BF16)
        kv = _dot(xg, w_ref[:, BRANCH_W:BRANCH_W + 2 * KV_W])
        t_rows_g = slice(BLOCK + g0 * BLOCK, BLOCK + g0 * BLOCK + g_len)
        for tab, c0 in ((k_tab, 0), (v_tab, KV_W)):
            val = kv[:, c0:c0 + KV_W]
            swapped = pltpu.roll(val, HEAD_DIM, axis=1)
            zero = jnp.zeros_like(val)
            tab[0, 0, t_rows_g, :] = jnp.where(low, val, zero).astype(BF16)
            tab[0, 1, t_rows_g, :] = jnp.where(low, zero, swapped).astype(BF16)
            tab[1, 0, t_rows_g, :] = jnp.where(low, swapped, zero).astype(BF16)
            tab[1, 1, t_rows_g, :] = jnp.where(low, zero, val).astype(BF16)
        for piece in range(BRANCH_W // PROJ_COLS):
            c0 = piece * PROJ_COLS
            gate_buf[g_rows, c0:c0 + PROJ_COLS] = _dot(xg, w_ref[:, gate_col0 + c0:gate_col0 + c0 + PROJ_COLS])

    low128 = lax.broadcasted_iota(jnp.int32, (BLOCK, LANES), 1) < HEAD_DIM
    project_group(0, None)
    for g0 in range(0, n_blocks, ATTN_GROUP_BLOCKS):
        units = [(n, j) for n in range(g0, g0 + ATTN_GROUP_BLOCKS) for j in range(N_KV_HEADS)]

        for it, (n, j) in enumerate(units):
            r0 = n * BLOCK
            q_cols = j * PAIRS * LANES
            qs = jnp.concatenate(
                [q_buf[r0:r0 + BLOCK, q_cols + p * LANES:q_cols + (p + 1) * LANES] for p in range(PAIRS)], axis=0)
            k2 = jnp.concatenate([k_tab[j, 0, r0:r0 + 2 * BLOCK, :], k_tab[j, 1, r0:r0 + 2 * BLOCK, :]], axis=0)
            logits = lax.dot_general(qs, k2, (((1,), (1,)), ((), ())), preferred_element_type=F32)
            threshold = jnp.where(jnp.logical_and(i == 0, n == 0), 2, 1)
            for p in range(PAIRS):
                rows = slice(p * BLOCK, (p + 1) * BLOCK)
                row_max = []
                for par in range(2):
                    slabs = []
                    for half in range(2):
                        cols = slice((2 * par + half) * LANES, (2 * par + half + 1) * LANES)
                        s = logits[rows, cols] + bias_ref[j, rows, cols]
                        s = jnp.where(code_buf[rows, cols] >= threshold, s, NEG_INF)
                        s_buf[it, rows, cols] = s
                        slabs.append(s)
                    m = jnp.max(jnp.maximum(slabs[0], slabs[1]), axis=1, keepdims=True)
                    m = jnp.maximum(jnp.broadcast_to(m, (BLOCK, LANES)), sink_ref[j, par, rows, :])
                    m_buf[it, par, rows, :] = m
                    row_max.append(m)
                mc_buf[it, rows, :] = jnp.where(low128, row_max[0], row_max[1])

        if g0 + ATTN_GROUP_BLOCKS < n_blocks:
            project_group(g0 + ATTN_GROUP_BLOCKS, logits[0:SUBLANES, 0:LANES])

        for it, (n, j) in enumerate(units):
            for p in range(PAIRS):
                rows = slice(p * BLOCK, (p + 1) * BLOCK)
                for par in range(2):
                    m = m_buf[it, par, rows, :]
                    for half in range(2):
                        cols = slice((2 * par + half) * LANES, (2 * par + half + 1) * LANES)
                        p_buf[it, rows, cols] = jnp.exp(s_buf[it, rows, cols] - m).astype(BF16)

        for it, (n, j) in enumerate(units):
            r0 = n * BLOCK
            q_cols = j * PAIRS * LANES
            v2 = jnp.concatenate([v_tab[j, 0, r0:r0 + 2 * BLOCK, :], v_tab[j, 1, r0:r0 + 2 * BLOCK, :]], axis=0)
            o = _dot(p_buf[it], jnp.concatenate([v2, ones_buf[...]], axis=1))
            for p in range(PAIRS):
                rows = slice(p * BLOCK, (p + 1) * BLOCK)
                cols = slice(q_cols + p * LANES, q_cols + (p + 1) * LANES)
                den = o[rows, LANES:2 * LANES] + jnp.exp(sinkc_ref[j, rows, :] - mc_buf[it, rows, :])
                y = o[rows, 0:LANES] * (1.0 / den) * _silu(gate_buf[r0:r0 + BLOCK, cols])
                o_ref[r0:r0 + BLOCK, cols] = y.astype(BF16)

    for tab in (k_tab, v_tab):
        for j in range(N_KV_HEADS):
            for half in range(2):
                tab[j, half, 0:BLOCK, :] = tab[j, half, t_rows:t_rows + BLOCK, :]


CONV_COLUMNS = BRANCH_W // LANES
CONV_BLOCKS = BRANCH_W // PROJ_COLS


def _conv_stream_kernel(x_ref, w_ref, cw_ref, cb_ref, lng_ref, lnb_ref, zero_ref, o_ref,
                        xs_buf, glu_buf, gate_buf, shift_buf, y_buf):
    i = pl.program_id(0)
    t_rows = x_ref.shape[0]

    @pl.when(i == 0)
    def _():
        for c in range(CONV_COLUMNS):
            glu_buf[c, 0:CONV_HALO, :] = jnp.zeros((CONV_HALO, LANES), F32)

    xs_buf[...] = x_ref[...].astype(BF16)
    first = CONV_HALO - (CONV_K - 1)
    span = t_rows + CONV_HALO - SUBLANES

    def glu_block(n):
        gate_val = _dot(xs_buf[...], w_ref[CONV_BLOCKS + n])
        glu = _dot(xs_buf[...], w_ref[n]) * _sigmoid(gate_val)
        glu_buf[2 * n, CONV_HALO:CONV_HALO + t_rows, :] = glu[:, 0:LANES]
        glu_buf[2 * n + 1, CONV_HALO:CONV_HALO + t_rows, :] = glu[:, LANES:2 * LANES]
        return gate_val[t_rows - SUBLANES:t_rows, 0:LANES]

    def gate_block(n):
        gate = _dot(xs_buf[...], w_ref[2 * CONV_BLOCKS + n])
        gate_buf[2 * n] = gate[:, 0:LANES]
        gate_buf[2 * n + 1] = gate[:, LANES:2 * LANES]
        return gate[t_rows - SUBLANES:t_rows, 0:LANES]

    def conv_block(n, token, second_trigger):
        for half in range(2):
            c = 2 * n + half
            if half == 1:
                token = token + second_trigger()
            for r in range(1, SUBLANES):
                shift_buf[half, r, 0:span, :] = glu_buf[c, r:r + span, :]
            bias = jnp.broadcast_to(cb_ref[c], (STREAM_ROWS, LANES))
            for r0 in range(0, t_rows, STREAM_ROWS):
                nothing = pltpu.bitcast(pltpu.bitcast(token, jnp.int32) & zero_ref[...], F32)
                acc = bias
                for k in range(CONV_K):
                    a8 = (first + k) // SUBLANES * SUBLANES
                    r = (first + k) % SUBLANES
                    if r == 0:
                        src = glu_buf[c, r0 + a8:r0 + a8 + STREAM_ROWS, :]
                    else:
                        src = shift_buf[half, r, r0 + a8:r0 + a8 + STREAM_ROWS, :]
                    tap = jnp.concatenate([cw_ref[c, k] + nothing] * (STREAM_ROWS // SUBLANES), axis=0)
                    acc = acc + tap * src
                y_buf[c, r0:r0 + STREAM_ROWS, :] = acc
                token = acc[0:SUBLANES, :]

    for n in range(CONV_BLOCKS):
        conv_block(n, glu_block(n), lambda n=n: gate_block(n))

    for r0 in range(0, t_rows, ROW_CHUNK):
        rows = slice(r0, r0 + ROW_CHUNK)
        ys = [y_buf[c, rows, :] for c in range(CONV_COLUMNS)]
        total = ys[0]
        for y in ys[1:]:
            total = total + y
        mu = jnp.sum(total, axis=1, keepdims=True) * (1.0 / BRANCH_W)
        ds = [y - mu for y in ys]
        sq = ds[0] * ds[0]
        for d in ds[1:]:
            sq = sq + d * d
        rstd = lax.rsqrt(jnp.sum(sq, axis=1, keepdims=True) * (1.0 / BRANCH_W) + LN_EPS)
        for c in range(CONV_COLUMNS):
            y = _silu(ds[c] * rstd * lng_ref[c] + lnb_ref[c]) * _silu(gate_buf[c, rows, :])
            o_ref[rows, c * LANES:(c + 1) * LANES] = y.astype(BF16)

    for c in range(CONV_COLUMNS):
        glu_buf[c, 0:CONV_HALO, :] = glu_buf[c, t_rows:t_rows + CONV_HALO, :]


def _merge_kernel(x_ref, ya_ref, yb_ref, yc_ref, yd_ref, wga_ref, wgb_ref, wgc_ref, wgd_ref, wb_ref, o_ref):
    xb = x_ref[...].astype(BF16)
    acc = None
    branches = ((ya_ref, wga_ref), (yb_ref, wgb_ref), (yc_ref, wgc_ref), (yd_ref, wgd_ref))
    for br, (y_ref, wg_ref) in enumerate(branches):
        gate = _sigmoid(_dot(xb, wg_ref[...]))
        term = gate * _dot(y_ref[...], wb_ref[br])
        acc = term if acc is None else acc + term
    o_ref[...] = acc.astype(BF16)


def _out_kernel(m_ref, w_ref, x_ref, g_ref, b_ref, zero_ref, o_ref, ob_ref, z_buf):
    t_rows = x_ref.shape[0]
    half_rows = t_rows // 2
    chunk = ROW_CHUNK // 2
    tokens = []
    for h in range(2):
        rows = slice(h * half_rows, (h + 1) * half_rows)
        z = _dot(m_ref[rows, :], w_ref[...])
        z_buf[rows, :] = z
        tokens.append(z[half_rows - SUBLANES:half_rows, 0:LANES])
    g = g_ref[...]
    b = b_ref[...]
    for h in range(2):
        nothing = pltpu.bitcast(pltpu.bitcast(tokens[h], jnp.int32) & zero_ref[...], F32)
        nothing = jnp.concatenate([nothing] * (chunk // SUBLANES), axis=0)
        nothing = jnp.concatenate([nothing] * (D_MODEL // LANES), axis=1)
        for r0 in range(h * half_rows, (h + 1) * half_rows, chunk):
            rows = slice(r0, r0 + chunk)
            y = _layer_norm_rows(ALPHA * x_ref[rows, :] + (z_buf[rows, :] + nothing), g, b)
            o_ref[rows, :] = y
            ob_ref[rows, :] = y.astype(BF16)


def _resident(shape):
    zeros = (0,) * len(shape)
    return pl.BlockSpec(shape, lambda *_: zeros, pipeline_mode=pl.Buffered(1))


def _nbytes(shape, dtype):
    return math.prod(d for d in shape if d is not None) * jnp.dtype(dtype).itemsize


def _call(kernel_fn, name, grid, inputs, outputs, scratch, value_bytes):
    arrays, in_specs = zip(*inputs)
    out_shape, out_specs = zip(*outputs)
    windows = [(spec, a.dtype) for a, spec in inputs] + [(spec, o.dtype) for o, spec in outputs]
    need = value_bytes + sum(_nbytes(v.shape, v.dtype) for v in scratch)
    for spec, dtype in windows:
        buffers = spec.pipeline_mode.buffer_count if spec.pipeline_mode is not None else 2
        need += buffers * _nbytes(spec.block_shape, dtype)
    assert need <= V7X_VMEM_BYTES, (name, need)
    return pl.pallas_call(
        kernel_fn,
        grid=grid,
        in_specs=list(in_specs),
        out_specs=list(out_specs) if len(out_specs) > 1 else out_specs[0],
        out_shape=list(out_shape) if len(out_shape) > 1 else out_shape[0],
        scratch_shapes=scratch,
        compiler_params=pltpu.CompilerParams(dimension_semantics=("arbitrary",) * len(grid),
                                             vmem_limit_bytes=need),
        name=name,
    )(*arrays)


def _row_tile(width, rows=SEQ_TILE):
    return pl.BlockSpec((rows, width), lambda i: (i, 0))


def _layer_weight(w_all, layer):
    idx = (layer,) + (0,) * (w_all.ndim - 1)
    return pl.BlockSpec((None,) + w_all.shape[1:], lambda *_: idx, pipeline_mode=pl.Buffered(1))


def _branch_call(kernel_fn, name, seq, x_in, w_all, layer, operands, scratch, value_tiles=2):
    value_bytes = value_tiles * _nbytes((SEQ_TILE, BRANCH_W), F32)
    inputs = [(x_in, _row_tile(D_MODEL)), (w_all, _layer_weight(w_all, layer))]
    inputs += [(op, _resident(op.shape)) for op in operands]
    outputs = [(jax.ShapeDtypeStruct((seq, BRANCH_W), BF16), _row_tile(BRANCH_W))]
    return _call(kernel_fn, name, (seq // SEQ_TILE,), inputs, outputs, scratch, value_bytes)


def _layer(x, x_in, band, w_parts, layer, pool_w, pool_scale, sgu_ln_g, sgu_ln_b, sgu_w, sgu_b, sinks,
           conv_w, conv_b, conv_ln_g, conv_ln_b, w_branch16, w_out16, ln_g, ln_b):
    seq = x.shape[0]
    t = SEQ_TILE
    w_pool, w_sgu, w_attn, w_gates, w_conv = w_parts
    row = lambda v: v.reshape(1, -1).astype(F32)

    y_a = _branch_call(
        _pool_kernel, "branch_pool", seq, x_in, w_pool, layer,
        [pool_w.astype(BF16), row(pool_scale)],
        [pltpu.VMEM((t + POOL_HALO, BRANCH_W), F32), pltpu.VMEM((t, BRANCH_W), F32),
         pltpu.VMEM((t, BRANCH_W), BF16)])

    sgu_bias = jnp.repeat(jnp.transpose(sgu_b).astype(F32), SGU_HD, axis=1)
    y_b = _branch_call(
        _sgu_kernel, "branch_sgu", seq, x_in, w_sgu, layer,
        [row(sgu_ln_g), row(sgu_ln_b), sgu_w.astype(F32), sgu_bias],
        [pltpu.VMEM((t, BRANCH_W), F32), pltpu.VMEM((t, BRANCH_W), F32),
         pltpu.VMEM((t, BRANCH_W), F32), pltpu.VMEM((t, BRANCH_W), BF16)])

    sink_hp = sinks.astype(F32).reshape(N_KV_HEADS, PAIRS, 2)
    sink_rows = jnp.repeat(jnp.transpose(sink_hp, (0, 2, 1)), BLOCK, axis=2)
    sink_rows = jnp.broadcast_to(sink_rows[..., None], (N_KV_HEADS, 2, PQ, LANES))
    sink_lanes = jnp.repeat(jnp.repeat(sink_hp, BLOCK, axis=1), HEAD_DIM, axis=2)
    n_units = N_KV_HEADS * ATTN_GROUP_BLOCKS
    y_c = _branch_call(
        _attn_kernel, "branch_attn", seq, x_in, w_attn, layer,
        [band, sink_rows, sink_lanes, jnp.zeros((SUBLANES, LANES), jnp.int32)],
        [pltpu.VMEM((t, D_MODEL), BF16), pltpu.VMEM((t, BRANCH_W), BF16), pltpu.VMEM((t, BRANCH_W), F32),
         pltpu.VMEM((N_KV_HEADS, 2, t + BLOCK, LANES), BF16),
         pltpu.VMEM((N_KV_HEADS, 2, t + BLOCK, LANES), BF16),
         pltpu.VMEM((PQ, 4 * BLOCK), jnp.int32),
         pltpu.VMEM((4 * BLOCK, LANES), BF16),
         pltpu.VMEM((n_units, PQ, 4 * BLOCK), F32),
         pltpu.VMEM((n_units, PQ, 4 * BLOCK), BF16),
         pltpu.VMEM((n_units, 2, PQ, LANES), F32),
         pltpu.VMEM((n_units, PQ, LANES), F32)])

    per_col = lambda v: v.astype(F32).reshape(CONV_COLUMNS, 1, LANES)
    taps = jnp.transpose(conv_w.astype(F32).reshape(CONV_K, CONV_COLUMNS, LANES), (1, 0, 2))
    taps = jnp.broadcast_to(taps[:, :, None, :], (CONV_COLUMNS, CONV_K, SUBLANES, LANES))
    y_d = _branch_call(
        _conv_stream_kernel, "branch_conv", seq, x_in, w_conv, layer,
        [taps, per_col(conv_b), per_col(conv_ln_g), per_col(conv_ln_b), jnp.zeros((SUBLANES, LANES), jnp.int32)],
        [pltpu.VMEM((t, D_MODEL), BF16),
         pltpu.VMEM((CONV_COLUMNS, t + CONV_HALO, LANES), F32),
         pltpu.VMEM((CONV_COLUMNS, t, LANES), F32),
         pltpu.VMEM((2, SUBLANES, t + CONV_HALO - SUBLANES, LANES), F32),
         pltpu.VMEM((CONV_COLUMNS, t, LANES), F32)], value_tiles=4)

    n_cb = D_MODEL // MERGE_COLS
    tm = MERGE_TILE
    x_spec = pl.BlockSpec((tm, D_MODEL), lambda c, i: (i, 0))
    y_spec = pl.BlockSpec((tm, BRANCH_W), lambda c, i: (i, 0))
    wg_specs = [pl.BlockSpec((None, D_MODEL, MERGE_COLS),
                             functools.partial(lambda br, c, i: (layer, 0, br * n_cb + c), br))
                for br in range(N_BRANCH)]
    wb_spec = pl.BlockSpec((None, N_BRANCH, BRANCH_W, MERGE_COLS), lambda c, i: (layer, 0, 0, c))
    merged = _call(
        _merge_kernel, "merge", (n_cb, seq // tm),
        [(x_in, x_spec)] + [(y, y_spec) for y in (y_a, y_b, y_c, y_d)]
        + [(w_gates, spec) for spec in wg_specs] + [(w_branch16, wb_spec)],
        [(jax.ShapeDtypeStruct((seq, D_MODEL), BF16), pl.BlockSpec((tm, MERGE_COLS), lambda c, i: (i, c)))],
        [], value_bytes=4 * _nbytes((tm, MERGE_COLS), F32))

    x_next, x_next_bf = _call(
        _out_kernel, "out_norm", (seq // t,),
        [(merged, _row_tile(D_MODEL)), (w_out16, _layer_weight(w_out16, layer)), (x, _row_tile(D_MODEL)),
         (row(ln_g), _resident((1, D_MODEL))), (row(ln_b), _resident((1, D_MODEL))),
         (jnp.zeros((SUBLANES, LANES), jnp.int32), _resident((SUBLANES, LANES)))],
        [(jax.ShapeDtypeStruct((seq, D_MODEL), F32), _row_tile(D_MODEL)),
         (jax.ShapeDtypeStruct((seq, D_MODEL), BF16), _row_tile(D_MODEL))],
        [pltpu.VMEM((t, D_MODEL), F32)], value_bytes=_nbytes((t, D_MODEL), F32))
    return x_next, x_next_bf


def kernel(x, w_in, pool_w, pool_scale, sgu_ln_g, sgu_ln_b, sgu_w, sgu_b, attn_sinks, rel_bias, conv_w,
           conv_b, conv_ln_g, conv_ln_b, w_branch, w_out, ln_g, ln_b):
    batch, seq, d_model = x.shape
    assert d_model == D_MODEL and w_in.shape == (DEPTH, D_MODEL, D_IN)
    assert seq % SEQ_TILE == 0 and SEQ_TILE % (BLOCK * ATTN_GROUP_BLOCKS) == 0

    band = _band_bias(rel_bias).reshape(N_KV_HEADS, PAIRS, 2, BLOCK, 2 * BLOCK)
    band = jnp.transpose(band, (0, 1, 3, 2, 4)).reshape(N_KV_HEADS, PQ, 4 * BLOCK)

    w_parts = _cast_w_in(w_in.astype(F32))
    w_branch16 = w_branch.astype(BF16)
    w_out16 = w_out.astype(BF16)
    outs = []
    for bi in range(batch):
        xc = x[bi].astype(F32)
        x_in = xc
        for l in range(DEPTH):
            xc, x_in = _layer(xc, x_in, band, w_parts, l, pool_w[l], pool_scale[l], sgu_ln_g[l], sgu_ln_b[l],
                              sgu_w[l], sgu_b[l], attn_sinks[l], conv_w[l], conv_b[l], conv_ln_g[l],
                              conv_ln_b[l], w_branch16, w_out16, ln_g[l], ln_b[l])
        outs.append(xc.astype(x.dtype)[None])
    return outs[0] if batch == 1 else jnp.concatenate(outs, axis=0)
```

```python
import functools
import math

import jax
import jax.numpy as jnp
from jax import lax
from jax.experimental import pallas as pl
from jax.experimental.pallas import tpu as pltpu

F32 = jnp.float32
BF16 = jnp.bfloat16

D_MODEL = 2048
DEPTH = 2
N_BRANCH = 4
BRANCH_W = 1024
POOL_WINDOWS = (2, 4, 8, 16)
POOL_GW = BRANCH_W // len(POOL_WINDOWS)
SGU_HEADS = 8
SGU_HD = BRANCH_W // SGU_HEADS
CHUNK = 128
N_Q_HEADS = 16
N_KV_HEADS = 2
Q_PER_KV = N_Q_HEADS // N_KV_HEADS
HEAD_DIM = 64
KV_W = N_KV_HEADS * HEAD_DIM
WINDOW = 128
BLOCK = 128
NUM_BUCKETS = 32
MAX_DISTANCE = 128
CONV_K = 31
ALPHA = (2 * DEPTH) ** 0.25
LN_EPS = 1e-5
NEG_INF = -1e30

OFF_A = 0
OFF_B = OFF_A + 2 * BRANCH_W
OFF_C = OFF_B + 3 * BRANCH_W
OFF_D = OFF_C + 2 * BRANCH_W + 2 * KV_W
OFF_G = OFF_D + 3 * BRANCH_W
D_IN = OFF_G + N_BRANCH * D_MODEL

LANES = 128
SUBLANES = 8
SEQ_TILE = 512
POOL_HALO = 16
CONV_HALO = 32
ROW_CHUNK = 64
STREAM_ROWS = 64
PROJ_COLS = 256
MERGE_COLS = 512
PAIRS = Q_PER_KV // 2
PQ = PAIRS * BLOCK
ATTN_GROUP_BLOCKS = 2
MERGE_TILE = 512
V7X_VMEM_BYTES = 64 * 1024 * 1024


def _sigmoid(x):
    return 0.5 * jnp.tanh(0.5 * x) + 0.5


def _silu(x):
    return x * _sigmoid(x)


def _layer_norm_rows(v, g, b):
    mu = jnp.mean(v, axis=-1, keepdims=True)
    d = v - mu
    var = jnp.mean(d * d, axis=-1, keepdims=True)
    return d * lax.rsqrt(var + LN_EPS) * g + b


def _dot(a, b):
    return jnp.dot(a, b, preferred_element_type=F32)


def _t5_bucket(n):
    max_exact = NUM_BUCKETS // 2
    nf = jnp.maximum(n, 1).astype(F32)
    large = max_exact + (jnp.log(nf / max_exact) / math.log(MAX_DISTANCE / max_exact)
                         * (NUM_BUCKETS - max_exact)).astype(jnp.int32)
    large = jnp.minimum(large, NUM_BUCKETS - 1)
    return jnp.where(n < max_exact, n, large)


def _band_bucket_ids():
    i = jnp.arange(BLOCK)[:, None]
    j = jnp.arange(2 * BLOCK)[None, :]
    return _t5_bucket(jnp.clip(i + BLOCK - j, 0, WINDOW - 1)).astype(jnp.int32)


def _bias_kernel(bucket_ref, table_ref, o_ref):
    h = pl.program_id(0)
    bucket = bucket_ref[...]
    acc = jnp.zeros(bucket.shape, F32)
    for b in range(NUM_BUCKETS):
        acc = jnp.where(bucket == b, table_ref[b, h], acc)
    o_ref[0] = acc


def _band_bias(rel_bias):
    return pl.pallas_call(
        _bias_kernel,
        grid=(N_Q_HEADS,),
        in_specs=[pl.BlockSpec((BLOCK, 2 * BLOCK), lambda h: (0, 0)),
                  pl.BlockSpec(memory_space=pltpu.SMEM)],
        out_specs=pl.BlockSpec((1, BLOCK, 2 * BLOCK), lambda h: (h, 0, 0)),
        out_shape=jax.ShapeDtypeStruct((N_Q_HEADS, BLOCK, 2 * BLOCK), F32),
        name="band_bias",
    )(_band_bucket_ids(), rel_bias.astype(F32))


W_IN_SPLITS = ((OFF_A, OFF_B), (OFF_B, OFF_C), (OFF_C, OFF_D), (OFF_G, D_IN))
CONV_W_CHUNKS = (OFF_G - OFF_D) // PROJ_COLS
CAST_ROWS = 128


def _cast_w_in_kernel(w_ref, oa_ref, ob_ref, oc_ref, og_ref, od_ref):
    for (a, b), o_ref in zip(W_IN_SPLITS, (oa_ref, ob_ref, oc_ref, og_ref)):
        o_ref[...] = w_ref[:, a:b].astype(BF16)
    for j in range(CONV_W_CHUNKS):
        od_ref[j] = w_ref[:, OFF_D + j * PROJ_COLS:OFF_D + (j + 1) * PROJ_COLS].astype(BF16)


def _cast_w_in(w_in):
    widths = [b - a for a, b in W_IN_SPLITS]
    outputs = [(jax.ShapeDtypeStruct((DEPTH, D_MODEL, w), BF16),
                pl.BlockSpec((None, CAST_ROWS, w), lambda l, r: (l, r, 0))) for w in widths]
    outputs.append((jax.ShapeDtypeStruct((DEPTH, CONV_W_CHUNKS, D_MODEL, PROJ_COLS), BF16),
                    pl.BlockSpec((None, CONV_W_CHUNKS, CAST_ROWS, PROJ_COLS), lambda l, r: (l, 0, r, 0))))
    return _call(_cast_w_in_kernel, "cast_w_in", (DEPTH, D_MODEL // CAST_ROWS),
                 [(w_in, pl.BlockSpec((None, CAST_ROWS, D_IN), lambda l, r: (l, r, 0)))],
                 outputs, [], value_bytes=0)


def _pool_kernel(x_ref, w_ref, pw_ref, sc_ref, o_ref, abuf, gate_buf, mix_buf):
    i = pl.program_id(0)
    t_rows = x_ref.shape[0]

    @pl.when(i == 0)
    def _():
        abuf[0:POOL_HALO, :] = jnp.zeros((POOL_HALO, BRANCH_W), F32)

    xb = x_ref[...].astype(BF16)
    abuf[POOL_HALO:POOL_HALO + t_rows, :] = _dot(xb, w_ref[:, 0:BRANCH_W])
    gate_buf[...] = _dot(xb, w_ref[:, BRANCH_W:2 * BRANCH_W])

    for g, win in enumerate(POOL_WINDOWS):
        c0 = g * POOL_GW
        for r0 in range(0, t_rows, ROW_CHUNK):
            base = POOL_HALO + r0
            cur = abuf[base:base + ROW_CHUNK, c0:c0 + POOL_GW]
            acc = cur
            for s in range(1, win):
                acc = acc + abuf[base - s:base - s + ROW_CHUNK, c0:c0 + POOL_GW]
            t = i * t_rows + r0 + lax.broadcasted_iota(jnp.int32, (ROW_CHUNK, POOL_GW), 0)
            cnt = jnp.minimum(t + 1, win).astype(F32)
            mix_buf[r0:r0 + ROW_CHUNK, c0:c0 + POOL_GW] = (acc / cnt - cur).astype(BF16)

    for g in range(len(POOL_WINDOWS)):
        c0 = g * POOL_GW
        y = _dot(mix_buf[:, c0:c0 + POOL_GW], pw_ref[g])
        y = y * sc_ref[:, c0:c0 + POOL_GW] * _silu(gate_buf[:, c0:c0 + POOL_GW])
        o_ref[:, c0:c0 + POOL_GW] = y.astype(BF16)

    abuf[0:POOL_HALO, :] = abuf[t_rows:t_rows + POOL_HALO, :]


def _sgu_kernel(x_ref, w_ref, lng_ref, lnb_ref, sw_ref, sb_ref, o_ref,
                u_buf, v_buf, gate_buf, vn_buf):
    t_rows = x_ref.shape[0]
    xb = x_ref[...].astype(BF16)
    u_buf[...] = _dot(xb, w_ref[:, 0:BRANCH_W])
    v_buf[...] = _dot(xb, w_ref[:, BRANCH_W:2 * BRANCH_W])
    gate_buf[...] = _dot(xb, w_ref[:, 2 * BRANCH_W:3 * BRANCH_W])

    g = lng_ref[...]
    b = lnb_ref[...]
    for r0 in range(0, t_rows, ROW_CHUNK):
        vn = _layer_norm_rows(v_buf[r0:r0 + ROW_CHUNK, :], g, b)
        vn_buf[r0:r0 + ROW_CHUNK, :] = vn.astype(BF16)

    row = lax.broadcasted_iota(jnp.int32, (CHUNK, CHUNK), 0)
    col = lax.broadcasted_iota(jnp.int32, (CHUNK, CHUNK), 1)
    causal = row >= col
    for h in range(SGU_HEADS):
        c0 = h * SGU_HD
        w_h = jnp.where(causal, sw_ref[h], 0.0).astype(BF16)
        bias_h = sb_ref[:, c0:c0 + SGU_HD]
        for r0 in range(0, t_rows, CHUNK):
            sp = _dot(w_h, vn_buf[r0:r0 + CHUNK, c0:c0 + SGU_HD]) + bias_h
            y = u_buf[r0:r0 + CHUNK, c0:c0 + SGU_HD] * sp * _silu(gate_buf[r0:r0 + CHUNK, c0:c0 + SGU_HD])
            o_ref[r0:r0 + CHUNK, c0:c0 + SGU_HD] = y.astype(BF16)


def _attn_kernel(x_ref, w_ref, bias_ref, sink_ref, sinkc_ref, zero_ref, o_ref,
                 xs_buf, q_buf, gate_buf, k_tab, v_tab, code_buf, ones_buf, s_buf, p_buf, m_buf, mc_buf):
    i = pl.program_id(0)
    t_rows = x_ref.shape[0]
    n_blocks = t_rows // BLOCK

    @pl.when(i == 0)
    def _():
        zero = jnp.zeros((BLOCK, LANES), BF16)
        for j in range(N_KV_HEADS):
            for half in range(2):
                k_tab[j, half, 0:BLOCK, :] = zero
                v_tab[j, half, 0:BLOCK, :] = zero
        row = lax.broadcasted_iota(jnp.int32, (PQ, 2 * 2 * BLOCK), 0)
        col = lax.broadcasted_iota(jnp.int32, (PQ, 2 * 2 * BLOCK), 1)
        key = col & (2 * BLOCK - 1)
        u = key - (row & (BLOCK - 1)) - 1
        in_window = (u >= 0) & (u < WINDOW)
        code_buf[...] = jnp.where(in_window, jnp.where(key >= BLOCK, 2, 1), 0)
        krow = lax.broadcasted_iota(jnp.int32, (4 * BLOCK, LANES), 0)
        klane = lax.broadcasted_iota(jnp.int32, (4 * BLOCK, LANES), 1)
        ones_buf[...] = jnp.where((krow >= 2 * BLOCK) == (klane >= HEAD_DIM), 1.0, 0.0).astype(BF16)

    g_len = ATTN_GROUP_BLOCKS * BLOCK
    xs_buf[...] = x_ref[...].astype(BF16)
    low = lax.broadcasted_iota(jnp.int32, (g_len, LANES), 1) < HEAD_DIM
    gate_col0 = BRANCH_W + 2 * KV_W

    def project_group(g0, after):
        g_rows = slice(g0 * BLOCK, g0 * BLOCK + g_len)
        xg = xs_buf[g_rows, :]
        if after is not None:
            nothing = pltpu.bitcast(after, jnp.int32) & zero_ref[...]
            nothing = jnp.concatenate([nothing] * (g_len // (2 * SUBLANES)), axis=0)
            nothing = jnp.concatenate([nothing] * (D_MODEL // LANES), axis=1)
            xg = pltpu.bitcast(pltpu.bitcast(xg, jnp.int32) | nothing, BF16)
        for piece in range(BRANCH_W // PROJ_COLS):
            c0 = piece * PROJ_COLS
            q = _dot(xg, w_ref[:, c0:c0 + PROJ_COLS]) * (HEAD_DIM ** -0.5)
            q_buf[g_rows, c0:c0 + PROJ_COLS] = q.astype(BF16)
        kv = _dot(xg, w_ref[:, BRANCH_W:BRANCH_W + 2 * KV_W])
        t_rows_g = slice(BLOCK + g0 * BLOCK, BLOCK + g0 * BLOCK + g_len)
        for tab, c0 in ((k_tab, 0), (v_tab, KV_W)):
            val = kv[:, c0:c0 + KV_W]
            swapped = pltpu.roll(val, HEAD_DIM, axis=1)
            zero = jnp.zeros_like(val)
            tab[0, 0, t_rows_g, :] = jnp.where(low, val, zero).astype(BF16)
            tab[0, 1, t_rows_g, :] = jnp.where(low, zero, swapped).astype(BF16)
            tab[1, 0, t_rows_g, :] = jnp.where(low, swapped, zero).astype(BF16)
            tab[1, 1, t_rows_g, :] = jnp.where(low, zero, val).astype(BF16)
        for piece in range(BRANCH_W // PROJ_COLS):
            c0 = piece * PROJ_COLS
            gate_buf[g_rows, c0:c0 + PROJ_COLS] = _dot(xg, w_ref[:, gate_col0 + c0:gate_col0 + c0 + PROJ_COLS])

    low128 = lax.broadcasted_iota(jnp.int32, (BLOCK, LANES), 1) < HEAD_DIM
    project_group(0, None)
    for g0 in range(0, n_blocks, ATTN_GROUP_BLOCKS):
        units = [(n, j) for n in range(g0, g0 + ATTN_GROUP_BLOCKS) for j in range(N_KV_HEADS)]

        for it, (n, j) in enumerate(units):
            r0 = n * BLOCK
            q_cols = j * PAIRS * LANES
            qs = jnp.concatenate(
                [q_buf[r0:r0 + BLOCK, q_cols + p * LANES:q_cols + (p + 1) * LANES] for p in range(PAIRS)], axis=0)
            k2 = jnp.concatenate([k_tab[j, 0, r0:r0 + 2 * BLOCK, :], k_tab[j, 1, r0:r0 + 2 * BLOCK, :]], axis=0)
            logits = lax.dot_general(qs, k2, (((1,), (1,)), ((), ())), preferred_element_type=F32)
            threshold = jnp.where(jnp.logical_and(i == 0, n == 0), 2, 1)
            for p in range(PAIRS):
                rows = slice(p * BLOCK, (p + 1) * BLOCK)
                row_max = []
                for par in range(2):
                    slabs = []
                    for half in range(2):
                        cols = slice((2 * par + half) * LANES, (2 * par + half + 1) * LANES)
                        s = logits[rows, cols] + bias_ref[j, rows, cols]
                        s = jnp.where(code_buf[rows, cols] >= threshold, s, NEG_INF)
                        s_buf[it, rows, cols] = s
                        slabs.append(s)
                    m = jnp.max(jnp.maximum(slabs[0], slabs[1]), axis=1, keepdims=True)
                    m = jnp.maximum(jnp.broadcast_to(m, (BLOCK, LANES)), sink_ref[j, par, rows, :])
                    m_buf[it, par, rows, :] = m
                    row_max.append(m)
                mc_buf[it, rows, :] = jnp.where(low128, row_max[0], row_max[1])

        if g0 + ATTN_GROUP_BLOCKS < n_blocks:
            project_group(g0 + ATTN_GROUP_BLOCKS, logits[0:SUBLANES, 0:LANES])

        for it, (n, j) in enumerate(units):
            for p in range(PAIRS):
                rows = slice(p * BLOCK, (p + 1) * BLOCK)
                for par in range(2):
                    m = m_buf[it, par, rows, :]
                    for half in range(2):
                        cols = slice((2 * par + half) * LANES, (2 * par + half + 1) * LANES)
                        p_buf[it, rows, cols] = jnp.exp(s_buf[it, rows, cols] - m).astype(BF16)

        for it, (n, j) in enumerate(units):
            r0 = n * BLOCK
            q_cols = j * PAIRS * LANES
            v2 = jnp.concatenate([v_tab[j, 0, r0:r0 + 2 * BLOCK, :], v_tab[j, 1, r0:r0 + 2 * BLOCK, :]], axis=0)
            o = _dot(p_buf[it], jnp.concatenate([v2, ones_buf[...]], axis=1))
            for p in range(PAIRS):
                rows = slice(p * BLOCK, (p + 1) * BLOCK)
                cols = slice(q_cols + p * LANES, q_cols + (p + 1) * LANES)
                den = o[rows, LANES:2 * LANES] + jnp.exp(sinkc_ref[j, rows, :] - mc_buf[it, rows, :])
                y = o[rows, 0:LANES] * (1.0 / den) * _silu(gate_buf[r0:r0 + BLOCK, cols])
                o_ref[r0:r0 + BLOCK, cols] = y.astype(BF16)

    for tab in (k_tab, v_tab):
        for j in range(N_KV_HEADS):
            for half in range(2):
                tab[j, half, 0:BLOCK, :] = tab[j, half, t_rows:t_rows + BLOCK, :]


CONV_COLUMNS = BRANCH_W // LANES
CONV_BLOCKS = BRANCH_W // PROJ_COLS


def _conv_stream_kernel(x_ref, w_ref, cw_ref, cb_ref, lng_ref, lnb_ref, zero_ref, o_ref,
                        xs_buf, glu_buf, gate_buf, shift_buf, y_buf):
    i = pl.program_id(0)
    t_rows = x_ref.shape[0]

    @pl.when(i == 0)
    def _():
        for c in range(CONV_COLUMNS):
            glu_buf[c, 0:CONV_HALO, :] = jnp.zeros((CONV_HALO, LANES), F32)

    xs_buf[...] = x_ref[...].astype(BF16)
    first = CONV_HALO - (CONV_K - 1)
    span = t_rows + CONV_HALO - SUBLANES

    def glu_block(n):
        gate_val = _dot(xs_buf[...], w_ref[CONV_BLOCKS + n])
        glu = _dot(xs_buf[...], w_ref[n]) * _sigmoid(gate_val)
        glu_buf[2 * n, CONV_HALO:CONV_HALO + t_rows, :] = glu[:, 0:LANES]
        glu_buf[2 * n + 1, CONV_HALO:CONV_HALO + t_rows, :] = glu[:, LANES:2 * LANES]
        return gate_val[t_rows - SUBLANES:t_rows, 0:LANES]

    def gate_block(n):
        gate = _dot(xs_buf[...], w_ref[2 * CONV_BLOCKS + n])
        gate_buf[2 * n] = gate[:, 0:LANES]
        gate_buf[2 * n + 1] = gate[:, LANES:2 * LANES]
        return gate[t_rows - SUBLANES:t_rows, 0:LANES]

    def conv_block(n, token, second_trigger):
        for half in range(2):
            c = 2 * n + half
            if half == 1:
                token = token + second_trigger()
            for r in range(1, SUBLANES):
                shift_buf[half, r, 0:span, :] = glu_buf[c, r:r + span, :]
            bias = jnp.broadcast_to(cb_ref[c], (STREAM_ROWS, LANES))
            for r0 in range(0, t_rows, STREAM_ROWS):
                nothing = pltpu.bitcast(pltpu.bitcast(token, jnp.int32) & zero_ref[...], F32)
                acc = bias
                for k in range(CONV_K):
                    a8 = (first + k) // SUBLANES * SUBLANES
                    r = (first + k) % SUBLANES
                    if r == 0:
                        src = glu_buf[c, r0 + a8:r0 + a8 + STREAM_ROWS, :]
                    else:
                        src = shift_buf[half, r, r0 + a8:r0 + a8 + STREAM_ROWS, :]
                    tap = jnp.concatenate([cw_ref[c, k] + nothing] * (STREAM_ROWS // SUBLANES), axis=0)
                    acc = acc + tap * src
                y_buf[c, r0:r0 + STREAM_ROWS, :] = acc
                token = acc[0:SUBLANES, :]

    for n in range(CONV_BLOCKS):
        conv_block(n, glu_block(n), lambda n=n: gate_block(n))

    for r0 in range(0, t_rows, ROW_CHUNK):
        rows = slice(r0, r0 + ROW_CHUNK)
        ys = [y_buf[c, rows, :] for c in range(CONV_COLUMNS)]
        total = ys[0]
        for y in ys[1:]:
            total = total + y
        mu = jnp.sum(total, axis=1, keepdims=True) * (1.0 / BRANCH_W)
        ds = [y - mu for y in ys]
        sq = ds[0] * ds[0]
        for d in ds[1:]:
            sq = sq + d * d
        rstd = lax.rsqrt(jnp.sum(sq, axis=1, keepdims=True) * (1.0 / BRANCH_W) + LN_EPS)
        for c in range(CONV_COLUMNS):
            y = _silu(ds[c] * rstd * lng_ref[c] + lnb_ref[c]) * _silu(gate_buf[c, rows, :])
            o_ref[rows, c * LANES:(c + 1) * LANES] = y.astype(BF16)

    for c in range(CONV_COLUMNS):
        glu_buf[c, 0:CONV_HALO, :] = glu_buf[c, t_rows:t_rows + CONV_HALO, :]


def _merge_kernel(x_ref, ya_ref, yb_ref, yc_ref, yd_ref, wga_ref, wgb_ref, wgc_ref, wgd_ref, wb_ref, o_ref):
    xb = x_ref[...].astype(BF16)
    acc = None
    branches = ((ya_ref, wga_ref), (yb_ref, wgb_ref), (yc_ref, wgc_ref), (yd_ref, wgd_ref))
    for br, (y_ref, wg_ref) in enumerate(branches):
        gate = _sigmoid(_dot(xb, wg_ref[...]))
        term = gate * _dot(y_ref[...], wb_ref[br])
        acc = term if acc is None else acc + term
    o_ref[...] = acc.astype(BF16)


def _out_kernel(m_ref, w_ref, x_ref, g_ref, b_ref, zero_ref, o_ref, ob_ref, z_buf):
    t_rows = x_ref.shape[0]
    half_rows = t_rows // 2
    chunk = ROW_CHUNK // 2
    tokens = []
    for h in range(2):
        rows = slice(h * half_rows, (h + 1) * half_rows)
        z = _dot(m_ref[rows, :], w_ref[...])
        z_buf[rows, :] = z
        tokens.append(z[half_rows - SUBLANES:half_rows, 0:LANES])
    g = g_ref[...]
    b = b_ref[...]
    for h in range(2):
        nothing = pltpu.bitcast(pltpu.bitcast(tokens[h], jnp.int32) & zero_ref[...], F32)
        nothing = jnp.concatenate([nothing] * (chunk // SUBLANES), axis=0)
        nothing = jnp.concatenate([nothing] * (D_MODEL // LANES), axis=1)
        for r0 in range(h * half_rows, (h + 1) * half_rows, chunk):
            rows = slice(r0, r0 + chunk)
            y = _layer_norm_rows(ALPHA * x_ref[rows, :] + (z_buf[rows, :] + nothing), g, b)
            o_ref[rows, :] = y
            ob_ref[rows, :] = y.astype(BF16)


def _resident(shape):
    zeros = (0,) * len(shape)
    return pl.BlockSpec(shape, lambda *_: zeros, pipeline_mode=pl.Buffered(1))


def _nbytes(shape, dtype):
    return math.prod(d for d in shape if d is not None) * jnp.dtype(dtype).itemsize


def _call(kernel_fn, name, grid, inputs, outputs, scratch, value_bytes):
    arrays, in_specs = zip(*inputs)
    out_shape, out_specs = zip(*outputs)
    windows = [(spec, a.dtype) for a, spec in inputs] + [(spec, o.dtype) for o, spec in outputs]
    need = value_bytes + sum(_nbytes(v.shape, v.dtype) for v in scratch)
    for spec, dtype in windows:
        buffers = spec.pipeline_mode.buffer_count if spec.pipeline_mode is not None else 2
        need += buffers * _nbytes(spec.block_shape, dtype)
    assert need <= V7X_VMEM_BYTES, (name, need)
    return pl.pallas_call(
        kernel_fn,
        grid=grid,
        in_specs=list(in_specs),
        out_specs=list(out_specs) if len(out_specs) > 1 else out_specs[0],
        out_shape=list(out_shape) if len(out_shape) > 1 else out_shape[0],
        scratch_shapes=scratch,
        compiler_params=pltpu.CompilerParams(dimension_semantics=("arbitrary",) * len(grid),
                                             vmem_limit_bytes=need),
        name=name,
    )(*arrays)


def _row_tile(width, rows=SEQ_TILE):
    return pl.BlockSpec((rows, width), lambda i: (i, 0))


def _layer_weight(w_all, layer):
    idx = (layer,) + (0,) * (w_all.ndim - 1)
    return pl.BlockSpec((None,) + w_all.shape[1:], lambda *_: idx, pipeline_mode=pl.Buffered(1))


def _branch_call(kernel_fn, name, seq, x_in, w_all, layer, operands, scratch, value_tiles=2):
    value_bytes = value_tiles * _nbytes((SEQ_TILE, BRANCH_W), F32)
    inputs = [(x_in, _row_tile(D_MODEL)), (w_all, _layer_weight(w_all, layer))]
    inputs += [(op, _resident(op.shape)) for op in operands]
    outputs = [(jax.ShapeDtypeStruct((seq, BRANCH_W), BF16), _row_tile(BRANCH_W))]
    return _call(kernel_fn, name, (seq // SEQ_TILE,), inputs, outputs, scratch, value_bytes)


def _layer(x, x_in, band, w_parts, layer, pool_w, pool_scale, sgu_ln_g, sgu_ln_b, sgu_w, sgu_b, sinks,
           conv_w, conv_b, conv_ln_g, conv_ln_b, w_branch16, w_out16, ln_g, ln_b):
    seq = x.shape[0]
    t = SEQ_TILE
    w_pool, w_sgu, w_attn, w_gates, w_conv = w_parts
    row = lambda v: v.reshape(1, -1).astype(F32)

    y_a = _branch_call(
        _pool_kernel, "branch_pool", seq, x_in, w_pool, layer,
        [pool_w.astype(BF16), row(pool_scale)],
        [pltpu.VMEM((t + POOL_HALO, BRANCH_W), F32), pltpu.VMEM((t, BRANCH_W), F32),
         pltpu.VMEM((t, BRANCH_W), BF16)])

    sgu_bias = jnp.repeat(jnp.transpose(sgu_b).astype(F32), SGU_HD, axis=1)
    y_b = _branch_call(
        _sgu_kernel, "branch_sgu", seq, x_in, w_sgu, layer,
        [row(sgu_ln_g), row(sgu_ln_b), sgu_w.astype(F32), sgu_bias],
        [pltpu.VMEM((t, BRANCH_W), F32), pltpu.VMEM((t, BRANCH_W), F32),
         pltpu.VMEM((t, BRANCH_W), F32), pltpu.VMEM((t, BRANCH_W), BF16)])

    sink_hp = sinks.astype(F32).reshape(N_KV_HEADS, PAIRS, 2)
    sink_rows = jnp.repeat(jnp.transpose(sink_hp, (0, 2, 1)), BLOCK, axis=2)
    sink_rows = jnp.broadcast_to(sink_rows[..., None], (N_KV_HEADS, 2, PQ, LANES))
    sink_lanes = jnp.repeat(jnp.repeat(sink_hp, BLOCK, axis=1), HEAD_DIM, axis=2)
    n_units = N_KV_HEADS * ATTN_GROUP_BLOCKS
    y_c = _branch_call(
        _attn_kernel, "branch_attn", seq, x_in, w_attn, layer,
        [band, sink_rows, sink_lanes, jnp.zeros((SUBLANES, LANES), jnp.int32)],
        [pltpu.VMEM((t, D_MODEL), BF16), pltpu.VMEM((t, BRANCH_W), BF16), pltpu.VMEM((t, BRANCH_W), F32),
         pltpu.VMEM((N_KV_HEADS, 2, t + BLOCK, LANES), BF16),
         pltpu.VMEM((N_KV_HEADS, 2, t + BLOCK, LANES), BF16),
         pltpu.VMEM((PQ, 4 * BLOCK), jnp.int32),
         pltpu.VMEM((4 * BLOCK, LANES), BF16),
         pltpu.VMEM((n_units, PQ, 4 * BLOCK), F32),
         pltpu.VMEM((n_units, PQ, 4 * BLOCK), BF16),
         pltpu.VMEM((n_units, 2, PQ, LANES), F32),
         pltpu.VMEM((n_units, PQ, LANES), F32)])

    per_col = lambda v: v.astype(F32).reshape(CONV_COLUMNS, 1, LANES)
    taps = jnp.transpose(conv_w.astype(F32).reshape(CONV_K, CONV_COLUMNS, LANES), (1, 0, 2))
    taps = jnp.broadcast_to(taps[:, :, None, :], (CONV_COLUMNS, CONV_K, SUBLANES, LANES))
    y_d = _branch_call(
        _conv_stream_kernel, "branch_conv", seq, x_in, w_conv, layer,
        [taps, per_col(conv_b), per_col(conv_ln_g), per_col(conv_ln_b), jnp.zeros((SUBLANES, LANES), jnp.int32)],
        [pltpu.VMEM((t, D_MODEL), BF16),
         pltpu.VMEM((CONV_COLUMNS, t + CONV_HALO, LANES), F32),
         pltpu.VMEM((CONV_COLUMNS, t, LANES), F32),
         pltpu.VMEM((2, SUBLANES, t + CONV_HALO - SUBLANES, LANES), F32),
         pltpu.VMEM((CONV_COLUMNS, t, LANES), F32)], value_tiles=4)

    n_cb = D_MODEL // MERGE_COLS
    tm = MERGE_TILE
    x_spec = pl.BlockSpec((tm, D_MODEL), lambda c, i: (i, 0))
    y_spec = pl.BlockSpec((tm, BRANCH_W), lambda c, i: (i, 0))
    wg_specs = [pl.BlockSpec((None, D_MODEL, MERGE_COLS),
                             functools.partial(lambda br, c, i: (layer, 0, br * n_cb + c), br))
                for br in range(N_BRANCH)]
    wb_spec = pl.BlockSpec((None, N_BRANCH, BRANCH_W, MERGE_COLS), lambda c, i: (layer, 0, 0, c))
    merged = _call(
        _merge_kernel, "merge", (n_cb, seq // tm),
        [(x_in, x_spec)] + [(y, y_spec) for y in (y_a, y_b, y_c, y_d)]
        + [(w_gates, spec) for spec in wg_specs] + [(w_branch16, wb_spec)],
        [(jax.ShapeDtypeStruct((seq, D_MODEL), BF16), pl.BlockSpec((tm, MERGE_COLS), lambda c, i: (i, c)))],
        [], value_bytes=4 * _nbytes((tm, MERGE_COLS), F32))

    x_next, x_next_bf = _call(
        _out_kernel, "out_norm", (seq // t,),
        [(merged, _row_tile(D_MODEL)), (w_out16, _layer_weight(w_out16, layer)), (x, _row_tile(D_MODEL)),
         (row(ln_g), _resident((1, D_MODEL))), (row(ln_b), _resident((1, D_MODEL))),
         (jnp.zeros((SUBLANES, LANES), jnp.int32), _resident((SUBLANES, LANES)))],
        [(jax.ShapeDtypeStruct((seq, D_MODEL), F32), _row_tile(D_MODEL)),
         (jax.ShapeDtypeStruct((seq, D_MODEL), BF16), _row_tile(D_MODEL))],
        [pltpu.VMEM((t, D_MODEL), F32)], value_bytes=_nbytes((t, D_MODEL), F32))
    return x_next, x_next_bf


def kernel(x, w_in, pool_w, pool_scale, sgu_ln_g, sgu_ln_b, sgu_w, sgu_b, attn_sinks, rel_bias, conv_w,
           conv_b, conv_ln_g, conv_ln_b, w_branch, w_out, ln_g, ln_b):
    batch, seq, d_model = x.shape
    assert d_model == D_MODEL and w_in.shape == (DEPTH, D_MODEL, D_IN)
    assert seq % SEQ_TILE == 0 and SEQ_TILE % (BLOCK * ATTN_GROUP_BLOCKS) == 0

    band = _band_bias(rel_bias).reshape(N_KV_HEADS, PAIRS, 2, BLOCK, 2 * BLOCK)
    band = jnp.transpose(band, (0, 1, 3, 2, 4)).reshape(N_KV_HEADS, PQ, 4 * BLOCK)

    w_parts = _cast_w_in(w_in.astype(F32))
    w_branch16 = w_branch.astype(BF16)
    w_out16 = w_out.astype(BF16)
    outs = []
    for bi in range(batch):
        xc = x[bi].astype(F32)
        x_in = xc
        for l in range(DEPTH):
            xc, x_in = _layer(xc, x_in, band, w_parts, l, pool_w[l], pool_scale[l], sgu_ln_g[l], sgu_ln_b[l],
                              sgu_w[l], sgu_b[l], attn_sinks[l], conv_w[l], conv_b[l], conv_ln_g[l],
                              conv_ln_b[l], w_branch16, w_out16, ln_g[l], ln_b[l])
        outs.append(xc.astype(x.dtype)[None])
    return outs[0] if batch == 1 else jnp.concatenate(outs, axis=0)
```

```python
import functools
import math

import jax
import jax.numpy as jnp
from jax import lax
from jax.experimental import pallas as pl
from jax.experimental.pallas import tpu as pltpu

F32 = jnp.float32
BF16 = jnp.bfloat16

D_MODEL = 2048
DEPTH = 2
N_BRANCH = 4
BRANCH_W = 1024
POOL_WINDOWS = (2, 4, 8, 16)
POOL_GW = BRANCH_W // len(POOL_WINDOWS)
SGU_HEADS = 8
SGU_HD = BRANCH_W // SGU_HEADS
CHUNK = 128
N_Q_HEADS = 16
N_KV_HEADS = 2
Q_PER_KV = N_Q_HEADS // N_KV_HEADS
HEAD_DIM = 64
KV_W = N_KV_HEADS * HEAD_DIM
WINDOW = 128
BLOCK = 128
NUM_BUCKETS = 32
MAX_DISTANCE = 128
CONV_K = 31
ALPHA = (2 * DEPTH) ** 0.25
LN_EPS = 1e-5
NEG_INF = -1e30

OFF_A = 0
OFF_B = OFF_A + 2 * BRANCH_W
OFF_C = OFF_B + 3 * BRANCH_W
OFF_D = OFF_C + 2 * BRANCH_W + 2 * KV_W
OFF_G = OFF_D + 3 * BRANCH_W
D_IN = OFF_G + N_BRANCH * D_MODEL

LANES = 128
SUBLANES = 8
SEQ_TILE = 512
POOL_HALO = 16
CONV_HALO = 32
ROW_CHUNK = 64
STREAM_ROWS = 256
PROJ_COLS = 256
MERGE_COLS = 512
PAIRS = Q_PER_KV // 2
PQ = PAIRS * BLOCK
ATTN_GROUP_BLOCKS = 2
MERGE_TILE = 512
V7X_VMEM_BYTES = 64 * 1024 * 1024


def _sigmoid(x):
    return 0.5 * jnp.tanh(0.5 * x) + 0.5


def _silu(x):
    return x * _sigmoid(x)


def _layer_norm_rows(v, g, b):
    mu = jnp.mean(v, axis=-1, keepdims=True)
    d = v - mu
    var = jnp.mean(d * d, axis=-1, keepdims=True)
    return d * lax.rsqrt(var + LN_EPS) * g + b


def _dot(a, b):
    return jnp.dot(a, b, preferred_element_type=F32)


def _t5_bucket(n):
    max_exact = NUM_BUCKETS // 2
    nf = jnp.maximum(n, 1).astype(F32)
    large = max_exact + (jnp.log(nf / max_exact) / math.log(MAX_DISTANCE / max_exact)
                         * (NUM_BUCKETS - max_exact)).astype(jnp.int32)
    large = jnp.minimum(large, NUM_BUCKETS - 1)
    return jnp.where(n < max_exact, n, large)


def _band_bucket_ids():
    i = jnp.arange(BLOCK)[:, None]
    j = jnp.arange(2 * BLOCK)[None, :]
    return _t5_bucket(jnp.clip(i + BLOCK - j, 0, WINDOW - 1)).astype(jnp.int32)


def _bias_kernel(bucket_ref, table_ref, o_ref):
    h = pl.program_id(0)
    bucket = bucket_ref[...]
    acc = jnp.zeros(bucket.shape, F32)
    for b in range(NUM_BUCKETS):
        acc = jnp.where(bucket == b, table_ref[b, h], acc)
    o_ref[0] = acc


def _band_bias(rel_bias):
    return pl.pallas_call(
        _bias_kernel,
        grid=(N_Q_HEADS,),
        in_specs=[pl.BlockSpec((BLOCK, 2 * BLOCK), lambda h: (0, 0)),
                  pl.BlockSpec(memory_space=pltpu.SMEM)],
        out_specs=pl.BlockSpec((1, BLOCK, 2 * BLOCK), lambda h: (h, 0, 0)),
        out_shape=jax.ShapeDtypeStruct((N_Q_HEADS, BLOCK, 2 * BLOCK), F32),
        name="band_bias",
    )(_band_bucket_ids(), rel_bias.astype(F32))


W_IN_SPLITS = ((OFF_A, OFF_B), (OFF_B, OFF_C), (OFF_C, OFF_D), (OFF_G, D_IN))
CONV_W_CHUNKS = (OFF_G - OFF_D) // PROJ_COLS
CAST_ROWS = 128


def _cast_w_in_kernel(w_ref, oa_ref, ob_ref, oc_ref, og_ref, od_ref):
    for (a, b), o_ref in zip(W_IN_SPLITS, (oa_ref, ob_ref, oc_ref, og_ref)):
        o_ref[...] = w_ref[:, a:b].astype(BF16)
    for j in range(CONV_W_CHUNKS):
        od_ref[j] = w_ref[:, OFF_D + j * PROJ_COLS:OFF_D + (j + 1) * PROJ_COLS].astype(BF16)


def _cast_w_in(w_in):
    widths = [b - a for a, b in W_IN_SPLITS]
    outputs = [(jax.ShapeDtypeStruct((DEPTH, D_MODEL, w), BF16),
                pl.BlockSpec((None, CAST_ROWS, w), lambda l, r: (l, r, 0))) for w in widths]
    outputs.append((jax.ShapeDtypeStruct((DEPTH, CONV_W_CHUNKS, D_MODEL, PROJ_COLS), BF16),
                    pl.BlockSpec((None, CONV_W_CHUNKS, CAST_ROWS, PROJ_COLS), lambda l, r: (l, 0, r, 0))))
    return _call(_cast_w_in_kernel, "cast_w_in", (DEPTH, D_MODEL // CAST_ROWS),
                 [(w_in, pl.BlockSpec((None, CAST_ROWS, D_IN), lambda l, r: (l, r, 0)))],
                 outputs, [], value_bytes=0)


def _pool_kernel(x_ref, w_ref, pw_ref, sc_ref, o_ref, abuf, gate_buf, mix_buf):
    i = pl.program_id(0)
    t_rows = x_ref.shape[0]

    @pl.when(i == 0)
    def _():
        abuf[0:POOL_HALO, :] = jnp.zeros((POOL_HALO, BRANCH_W), F32)

    xb = x_ref[...].astype(BF16)
    abuf[POOL_HALO:POOL_HALO + t_rows, :] = _dot(xb, w_ref[:, 0:BRANCH_W])
    gate_buf[...] = _dot(xb, w_ref[:, BRANCH_W:2 * BRANCH_W])

    for g, win in enumerate(POOL_WINDOWS):
        c0 = g * POOL_GW
        for r0 in range(0, t_rows, ROW_CHUNK):
            base = POOL_HALO + r0
            cur = abuf[base:base + ROW_CHUNK, c0:c0 + POOL_GW]
            acc = cur
            for s in range(1, win):
                acc = acc + abuf[base - s:base - s + ROW_CHUNK, c0:c0 + POOL_GW]
            t = i * t_rows + r0 + lax.broadcasted_iota(jnp.int32, (ROW_CHUNK, POOL_GW), 0)
            cnt = jnp.minimum(t + 1, win).astype(F32)
            mix_buf[r0:r0 + ROW_CHUNK, c0:c0 + POOL_GW] = (acc / cnt - cur).astype(BF16)

    for g in range(len(POOL_WINDOWS)):
        c0 = g * POOL_GW
        y = _dot(mix_buf[:, c0:c0 + POOL_GW], pw_ref[g])
        y = y * sc_ref[:, c0:c0 + POOL_GW] * _silu(gate_buf[:, c0:c0 + POOL_GW])
        o_ref[:, c0:c0 + POOL_GW] = y.astype(BF16)

    abuf[0:POOL_HALO, :] = abuf[t_rows:t_rows + POOL_HALO, :]


def _sgu_kernel(x_ref, w_ref, lng_ref, lnb_ref, sw_ref, sb_ref, o_ref,
                u_buf, v_buf, gate_buf, vn_buf):
    t_rows = x_ref.shape[0]
    xb = x_ref[...].astype(BF16)
    u_buf[...] = _dot(xb, w_ref[:, 0:BRANCH_W])
    v_buf[...] = _dot(xb, w_ref[:, BRANCH_W:2 * BRANCH_W])
    gate_buf[...] = _dot(xb, w_ref[:, 2 * BRANCH_W:3 * BRANCH_W])

    g = lng_ref[...]
    b = lnb_ref[...]
    for r0 in range(0, t_rows, ROW_CHUNK):
        vn = _layer_norm_rows(v_buf[r0:r0 + ROW_CHUNK, :], g, b)
        vn_buf[r0:r0 + ROW_CHUNK, :] = vn.astype(BF16)

    row = lax.broadcasted_iota(jnp.int32, (CHUNK, CHUNK), 0)
    col = lax.broadcasted_iota(jnp.int32, (CHUNK, CHUNK), 1)
    causal = row >= col
    for h in range(SGU_HEADS):
        c0 = h * SGU_HD
        w_h = jnp.where(causal, sw_ref[h], 0.0).astype(BF16)
        bias_h = sb_ref[:, c0:c0 + SGU_HD]
        for r0 in range(0, t_rows, CHUNK):
            sp = _dot(w_h, vn_buf[r0:r0 + CHUNK, c0:c0 + SGU_HD]) + bias_h
            y = u_buf[r0:r0 + CHUNK, c0:c0 + SGU_HD] * sp * _silu(gate_buf[r0:r0 + CHUNK, c0:c0 + SGU_HD])
            o_ref[r0:r0 + CHUNK, c0:c0 + SGU_HD] = y.astype(BF16)


def _attn_kernel(x_ref, w_ref, bias_ref, sink_ref, sinkc_ref, zero_ref, o_ref,
                 xs_buf, q_buf, gate_buf, k_tab, v_tab, code_buf, ones_buf, s_buf, p_buf, m_buf, mc_buf):
    i = pl.program_id(0)
    t_rows = x_ref.shape[0]
    n_blocks = t_rows // BLOCK

    @pl.when(i == 0)
    def _():
        zero = jnp.zeros((BLOCK, LANES), BF16)
        for j in range(N_KV_HEADS):
            for half in range(2):
                k_tab[j, half, 0:BLOCK, :] = zero
                v_tab[j, half, 0:BLOCK, :] = zero
        row = lax.broadcasted_iota(jnp.int32, (PQ, 2 * 2 * BLOCK), 0)
        col = lax.broadcasted_iota(jnp.int32, (PQ, 2 * 2 * BLOCK), 1)
        key = col & (2 * BLOCK - 1)
        u = key - (row & (BLOCK - 1)) - 1
        in_window = (u >= 0) & (u < WINDOW)
        code_buf[...] = jnp.where(in_window, jnp.where(key >= BLOCK, 2, 1), 0)
        krow = lax.broadcasted_iota(jnp.int32, (4 * BLOCK, LANES), 0)
        klane = lax.broadcasted_iota(jnp.int32, (4 * BLOCK, LANES), 1)
        ones_buf[...] = jnp.where((krow >= 2 * BLOCK) == (klane >= HEAD_DIM), 1.0, 0.0).astype(BF16)

    g_len = ATTN_GROUP_BLOCKS * BLOCK
    xs_buf[...] = x_ref[...].astype(BF16)
    low = lax.broadcasted_iota(jnp.int32, (g_len, LANES), 1) < HEAD_DIM
    gate_col0 = BRANCH_W + 2 * KV_W

    def project_group(g0, after):
        g_rows = slice(g0 * BLOCK, g0 * BLOCK + g_len)
        xg = xs_buf[g_rows, :]
        if after is not None:
            nothing = pltpu.bitcast(after, jnp.int32) & zero_ref[...]
            nothing = jnp.concatenate([nothing] * (g_len // (2 * SUBLANES)), axis=0)
            nothing = jnp.concatenate([nothing] * (D_MODEL // LANES), axis=1)
            xg = pltpu.bitcast(pltpu.bitcast(xg, jnp.int32) | nothing, BF16)
        for piece in range(BRANCH_W // PROJ_COLS):
            c0 = piece * PROJ_COLS
            q = _dot(xg, w_ref[:, c0:c0 + PROJ_COLS]) * (HEAD_DIM ** -0.5)
            q_buf[g_rows, c0:c0 + PROJ_COLS] = q.astype(BF16)
        kv = _dot(xg, w_ref[:, BRANCH_W:BRANCH_W + 2 * KV_W])
        t_rows_g = slice(BLOCK + g0 * BLOCK, BLOCK + g0 * BLOCK + g_len)
        for tab, c0 in ((k_tab, 0), (v_tab, KV_W)):
            val = kv[:, c0:c0 + KV_W]
            swapped = pltpu.roll(val, HEAD_DIM, axis=1)
            zero = jnp.zeros_like(val)
            tab[0, 0, t_rows_g, :] = jnp.where(low, val, zero).astype(BF16)
            tab[0, 1, t_rows_g, :] = jnp.where(low, zero, swapped).astype(BF16)
            tab[1, 0, t_rows_g, :] = jnp.where(low, swapped, zero).astype(BF16)
            tab[1, 1, t_rows_g, :] = jnp.where(low, zero, val).astype(BF16)
        for piece in range(BRANCH_W // PROJ_COLS):
            c0 = piece * PROJ_COLS
            gate_buf[g_rows, c0:c0 + PROJ_COLS] = _dot(xg, w_ref[:, gate_col0 + c0:gate_col0 + c0 + PROJ_COLS])

    low128 = lax.broadcasted_iota(jnp.int32, (BLOCK, LANES), 1) < HEAD_DIM
    project_group(0, None)
    for g0 in range(0, n_blocks, ATTN_GROUP_BLOCKS):
        units = [(n, j) for n in range(g0, g0 + ATTN_GROUP_BLOCKS) for j in range(N_KV_HEADS)]

        for it, (n, j) in enumerate(units):
            r0 = n * BLOCK
            q_cols = j * PAIRS * LANES
            qs = jnp.concatenate(
                [q_buf[r0:r0 + BLOCK, q_cols + p * LANES:q_cols + (p + 1) * LANES] for p in range(PAIRS)], axis=0)
            k2 = jnp.concatenate([k_tab[j, 0, r0:r0 + 2 * BLOCK, :], k_tab[j, 1, r0:r0 + 2 * BLOCK, :]], axis=0)
            logits = lax.dot_general(qs, k2, (((1,), (1,)), ((), ())), preferred_element_type=F32)
            threshold = jnp.where(jnp.logical_and(i == 0, n == 0), 2, 1)
            for p in range(PAIRS):
                rows = slice(p * BLOCK, (p + 1) * BLOCK)
                row_max = []
                for par in range(2):
                    slabs = []
                    for half in range(2):
                        cols = slice((2 * par + half) * LANES, (2 * par + half + 1) * LANES)
                        s = logits[rows, cols] + bias_ref[j, rows, cols]
                        s = jnp.where(code_buf[rows, cols] >= threshold, s, NEG_INF)
                        s_buf[it, rows, cols] = s
                        slabs.append(s)
                    m = jnp.max(jnp.maximum(slabs[0], slabs[1]), axis=1, keepdims=True)
                    m = jnp.maximum(jnp.broadcast_to(m, (BLOCK, LANES)), sink_ref[j, par, rows, :])
                    m_buf[it, par, rows, :] = m
                    row_max.append(m)
                mc_buf[it, rows, :] = jnp.where(low128, row_max[0], row_max[1])

        if g0 + ATTN_GROUP_BLOCKS < n_blocks:
            project_group(g0 + ATTN_GROUP_BLOCKS, logits[0:SUBLANES, 0:LANES])

        for it, (n, j) in enumerate(units):
            for p in range(PAIRS):
                rows = slice(p * BLOCK, (p + 1) * BLOCK)
                for par in range(2):
                    m = m_buf[it, par, rows, :]
                    for half in range(2):
                        cols = slice((2 * par + half) * LANES, (2 * par + half + 1) * LANES)
                        p_buf[it, rows, cols] = jnp.exp(s_buf[it, rows, cols] - m).astype(BF16)

        for it, (n, j) in enumerate(units):
            r0 = n * BLOCK
            q_cols = j * PAIRS * LANES
            v2 = jnp.concatenate([v_tab[j, 0, r0:r0 + 2 * BLOCK, :], v_tab[j, 1, r0:r0 + 2 * BLOCK, :]], axis=0)
            o = _dot(p_buf[it], jnp.concatenate([v2, ones_buf[...]], axis=1))
            for p in range(PAIRS):
                rows = slice(p * BLOCK, (p + 1) * BLOCK)
                cols = slice(q_cols + p * LANES, q_cols + (p + 1) * LANES)
                den = o[rows, LANES:2 * LANES] + jnp.exp(sinkc_ref[j, rows, :] - mc_buf[it, rows, :])
                y = o[rows, 0:LANES] * (1.0 / den) * _silu(gate_buf[r0:r0 + BLOCK, cols])
                o_ref[r0:r0 + BLOCK, cols] = y.astype(BF16)

    for tab in (k_tab, v_tab):
        for j in range(N_KV_HEADS):
            for half in range(2):
                tab[j, half, 0:BLOCK, :] = tab[j, half, t_rows:t_rows + BLOCK, :]


CONV_COLUMNS = BRANCH_W // LANES
CONV_BLOCKS = BRANCH_W // PROJ_COLS


def _conv_stream_kernel(x_ref, w_ref, cw_ref, cb_ref, lng_ref, lnb_ref, zero_ref, o_ref,
                        xs_buf, glu_buf, gate_buf, shift_buf, y_buf):
    i = pl.program_id(0)
    t_rows = x_ref.shape[0]

    @pl.when(i == 0)
    def _():
        for c in range(CONV_COLUMNS):
            glu_buf[c, 0:CONV_HALO, :] = jnp.zeros((CONV_HALO, LANES), F32)

    xs_buf[...] = x_ref[...].astype(BF16)
    first = CONV_HALO - (CONV_K - 1)
    span = t_rows + CONV_HALO - SUBLANES

    def glu_block(n):
        gate_val = _dot(xs_buf[...], w_ref[CONV_BLOCKS + n])
        glu = _dot(xs_buf[...], w_ref[n]) * _sigmoid(gate_val)
        glu_buf[2 * n, CONV_HALO:CONV_HALO + t_rows, :] = glu[:, 0:LANES]
        glu_buf[2 * n + 1, CONV_HALO:CONV_HALO + t_rows, :] = glu[:, LANES:2 * LANES]
        return gate_val[t_rows - SUBLANES:t_rows, 0:LANES]

    def gate_block(n):
        gate = _dot(xs_buf[...], w_ref[2 * CONV_BLOCKS + n])
        gate_buf[2 * n] = gate[:, 0:LANES]
        gate_buf[2 * n + 1] = gate[:, LANES:2 * LANES]
        return gate[t_rows - SUBLANES:t_rows, 0:LANES]

    def conv_block(n, token, second_trigger):
        for half in range(2):
            c = 2 * n + half
            if half == 1:
                token = token + second_trigger()
            for r in range(1, SUBLANES):
                shift_buf[half, r, 0:span, :] = glu_buf[c, r:r + span, :]
            bias = jnp.broadcast_to(cb_ref[c], (STREAM_ROWS, LANES))
            for r0 in range(0, t_rows, STREAM_ROWS):
                nothing = pltpu.bitcast(pltpu.bitcast(token, jnp.int32) & zero_ref[...], F32)
                acc = bias
                for k in range(CONV_K):
                    a8 = (first + k) // SUBLANES * SUBLANES
                    r = (first + k) % SUBLANES
                    if r == 0:
                        src = glu_buf[c, r0 + a8:r0 + a8 + STREAM_ROWS, :]
                    else:
                        src = shift_buf[half, r, r0 + a8:r0 + a8 + STREAM_ROWS, :]
                    tap = jnp.concatenate([cw_ref[c, k] + nothing] * (STREAM_ROWS // SUBLANES), axis=0)
                    acc = acc + tap * src
                y_buf[c, r0:r0 + STREAM_ROWS, :] = acc
                token = acc[0:SUBLANES, :]

    for n in range(CONV_BLOCKS):
        conv_block(n, glu_block(n), lambda n=n: gate_block(n))

    for r0 in range(0, t_rows, ROW_CHUNK):
        rows = slice(r0, r0 + ROW_CHUNK)
        ys = [y_buf[c, rows, :] for c in range(CONV_COLUMNS)]
        total = ys[0]
        for y in ys[1:]:
            total = total + y
        mu = jnp.sum(total, axis=1, keepdims=True) * (1.0 / BRANCH_W)
        ds = [y - mu for y in ys]
        sq = ds[0] * ds[0]
        for d in ds[1:]:
            sq = sq + d * d
        rstd = lax.rsqrt(jnp.sum(sq, axis=1, keepdims=True) * (1.0 / BRANCH_W) + LN_EPS)
        for c in range(CONV_COLUMNS):
            y = _silu(ds[c] * rstd * lng_ref[c] + lnb_ref[c]) * _silu(gate_buf[c, rows, :])
            o_ref[rows, c * LANES:(c + 1) * LANES] = y.astype(BF16)

    for c in range(CONV_COLUMNS):
        glu_buf[c, 0:CONV_HALO, :] = glu_buf[c, t_rows:t_rows + CONV_HALO, :]


def _merge_kernel(x_ref, ya_ref, yb_ref, yc_ref, yd_ref, wga_ref, wgb_ref, wgc_ref, wgd_ref, wb_ref, o_ref):
    xb = x_ref[...].astype(BF16)
    acc = None
    branches = ((ya_ref, wga_ref), (yb_ref, wgb_ref), (yc_ref, wgc_ref), (yd_ref, wgd_ref))
    for br, (y_ref, wg_ref) in enumerate(branches):
        gate = _sigmoid(_dot(xb, wg_ref[...]))
        term = gate * _dot(y_ref[...], wb_ref[br])
        acc = term if acc is None else acc + term
    o_ref[...] = acc.astype(BF16)


def _out_kernel(m_ref, w_ref, x_ref, g_ref, b_ref, zero_ref, o_ref, ob_ref, z_buf):
    t_rows = x_ref.shape[0]
    half_rows = t_rows // 2
    chunk = ROW_CHUNK // 2
    tokens = []
    for h in range(2):
        rows = slice(h * half_rows, (h + 1) * half_rows)
        z = _dot(m_ref[rows, :], w_ref[...])
        z_buf[rows, :] = z
        tokens.append(z[half_rows - SUBLANES:half_rows, 0:LANES])
    g = g_ref[...]
    b = b_ref[...]
    for h in range(2):
        nothing = pltpu.bitcast(pltpu.bitcast(tokens[h], jnp.int32) & zero_ref[...], F32)
        nothing = jnp.concatenate([nothing] * (chunk // SUBLANES), axis=0)
        nothing = jnp.concatenate([nothing] * (D_MODEL // LANES), axis=1)
        for r0 in range(h * half_rows, (h + 1) * half_rows, chunk):
            rows = slice(r0, r0 + chunk)
            y = _layer_norm_rows(ALPHA * x_ref[rows, :] + (z_buf[rows, :] + nothing), g, b)
            o_ref[rows, :] = y
            ob_ref[rows, :] = y.astype(BF16)


def _resident(shape):
    zeros = (0,) * len(shape)
    return pl.BlockSpec(shape, lambda *_: zeros, pipeline_mode=pl.Buffered(1))


def _nbytes(shape, dtype):
    return math.prod(d for d in shape if d is not None) * jnp.dtype(dtype).itemsize


def _call(kernel_fn, name, grid, inputs, outputs, scratch, value_bytes):
    arrays, in_specs = zip(*inputs)
    out_shape, out_specs = zip(*outputs)
    windows = [(spec, a.dtype) for a, spec in inputs] + [(spec, o.dtype) for o, spec in outputs]
    need = value_bytes + sum(_nbytes(v.shape, v.dtype) for v in scratch)
    for spec, dtype in windows:
        buffers = spec.pipeline_mode.buffer_count if spec.pipeline_mode is not None else 2
        need += buffers * _nbytes(spec.block_shape, dtype)
    assert need <= V7X_VMEM_BYTES, (name, need)
    return pl.pallas_call(
        kernel_fn,
        grid=grid,
        in_specs=list(in_specs),
        out_specs=list(out_specs) if len(out_specs) > 1 else out_specs[0],
        out_shape=list(out_shape) if len(out_shape) > 1 else out_shape[0],
        scratch_shapes=scratch,
        compiler_params=pltpu.CompilerParams(dimension_semantics=("arbitrary",) * len(grid),
                                             vmem_limit_bytes=need),
        name=name,
    )(*arrays)


def _row_tile(width, rows=SEQ_TILE):
    return pl.BlockSpec((rows, width), lambda i: (i, 0))


def _layer_weight(w_all, layer):
    idx = (layer,) + (0,) * (w_all.ndim - 1)
    return pl.BlockSpec((None,) + w_all.shape[1:], lambda *_: idx, pipeline_mode=pl.Buffered(1))


def _branch_call(kernel_fn, name, seq, x_in, w_all, layer, operands, scratch, value_tiles=2):
    value_bytes = value_tiles * _nbytes((SEQ_TILE, BRANCH_W), F32)
    inputs = [(x_in, _row_tile(D_MODEL)), (w_all, _layer_weight(w_all, layer))]
    inputs += [(op, _resident(op.shape)) for op in operands]
    outputs = [(jax.ShapeDtypeStruct((seq, BRANCH_W), BF16), _row_tile(BRANCH_W))]
    return _call(kernel_fn, name, (seq // SEQ_TILE,), inputs, outputs, scratch, value_bytes)


def _layer(x, x_in, band, w_parts, layer, pool_w, pool_scale, sgu_ln_g, sgu_ln_b, sgu_w, sgu_b, sinks,
           conv_w, conv_b, conv_ln_g, conv_ln_b, w_branch16, w_out16, ln_g, ln_b):
    seq = x.shape[0]
    t = SEQ_TILE
    w_pool, w_sgu, w_attn, w_gates, w_conv = w_parts
    row = lambda v: v.reshape(1, -1).astype(F32)

    y_a = _branch_call(
        _pool_kernel, "branch_pool", seq, x_in, w_pool, layer,
        [pool_w.astype(BF16), row(pool_scale)],
        [pltpu.VMEM((t + POOL_HALO, BRANCH_W), F32), pltpu.VMEM((t, BRANCH_W), F32),
         pltpu.VMEM((t, BRANCH_W), BF16)])

    sgu_bias = jnp.repeat(jnp.transpose(sgu_b).astype(F32), SGU_HD, axis=1)
    y_b = _branch_call(
        _sgu_kernel, "branch_sgu", seq, x_in, w_sgu, layer,
        [row(sgu_ln_g), row(sgu_ln_b), sgu_w.astype(F32), sgu_bias],
        [pltpu.VMEM((t, BRANCH_W), F32), pltpu.VMEM((t, BRANCH_W), F32),
         pltpu.VMEM((t, BRANCH_W), F32), pltpu.VMEM((t, BRANCH_W), BF16)])

    sink_hp = sinks.astype(F32).reshape(N_KV_HEADS, PAIRS, 2)
    sink_rows = jnp.repeat(jnp.transpose(sink_hp, (0, 2, 1)), BLOCK, axis=2)
    sink_rows = jnp.broadcast_to(sink_rows[..., None], (N_KV_HEADS, 2, PQ, LANES))
    sink_lanes = jnp.repeat(jnp.repeat(sink_hp, BLOCK, axis=1), HEAD_DIM, axis=2)
    n_units = N_KV_HEADS * ATTN_GROUP_BLOCKS
    y_c = _branch_call(
        _attn_kernel, "branch_attn", seq, x_in, w_attn, layer,
        [band, sink_rows, sink_lanes, jnp.zeros((SUBLANES, LANES), jnp.int32)],
        [pltpu.VMEM((t, D_MODEL), BF16), pltpu.VMEM((t, BRANCH_W), BF16), pltpu.VMEM((t, BRANCH_W), F32),
         pltpu.VMEM((N_KV_HEADS, 2, t + BLOCK, LANES), BF16),
         pltpu.VMEM((N_KV_HEADS, 2, t + BLOCK, LANES), BF16),
         pltpu.VMEM((PQ, 4 * BLOCK), jnp.int32),
         pltpu.VMEM((4 * BLOCK, LANES), BF16),
         pltpu.VMEM((n_units, PQ, 4 * BLOCK), F32),
         pltpu.VMEM((n_units, PQ, 4 * BLOCK), BF16),
         pltpu.VMEM((n_units, 2, PQ, LANES), F32),
         pltpu.VMEM((n_units, PQ, LANES), F32)])

    per_col = lambda v: v.astype(F32).reshape(CONV_COLUMNS, 1, LANES)
    taps = jnp.transpose(conv_w.astype(F32).reshape(CONV_K, CONV_COLUMNS, LANES), (1, 0, 2))
    taps = jnp.broadcast_to(taps[:, :, None, :], (CONV_COLUMNS, CONV_K, SUBLANES, LANES))
    y_d = _branch_call(
        _conv_stream_kernel, "branch_conv", seq, x_in, w_conv, layer,
        [taps, per_col(conv_b), per_col(conv_ln_g), per_col(conv_ln_b), jnp.zeros((SUBLANES, LANES), jnp.int32)],
        [pltpu.VMEM((t, D_MODEL), BF16),
         pltpu.VMEM((CONV_COLUMNS, t + CONV_HALO, LANES), F32),
         pltpu.VMEM((CONV_COLUMNS, t, LANES), F32),
         pltpu.VMEM((2, SUBLANES, t + CONV_HALO - SUBLANES, LANES), F32),
         pltpu.VMEM((CONV_COLUMNS, t, LANES), F32)], value_tiles=4)

    n_cb = D_MODEL // MERGE_COLS
    tm = MERGE_TILE
    x_spec = pl.BlockSpec((tm, D_MODEL), lambda c, i: (i, 0))
    y_spec = pl.BlockSpec((tm, BRANCH_W), lambda c, i: (i, 0))
    wg_specs = [pl.BlockSpec((None, D_MODEL, MERGE_COLS),
                             functools.partial(lambda br, c, i: (layer, 0, br * n_cb + c), br))
                for br in range(N_BRANCH)]
    wb_spec = pl.BlockSpec((None, N_BRANCH, BRANCH_W, MERGE_COLS), lambda c, i: (layer, 0, 0, c))
    merged = _call(
        _merge_kernel, "merge", (n_cb, seq // tm),
        [(x_in, x_spec)] + [(y, y_spec) for y in (y_a, y_b, y_c, y_d)]
        + [(w_gates, spec) for spec in wg_specs] + [(w_branch16, wb_spec)],
        [(jax.ShapeDtypeStruct((seq, D_MODEL), BF16), pl.BlockSpec((tm, MERGE_COLS), lambda c, i: (i, c)))],
        [], value_bytes=4 * _nbytes((tm, MERGE_COLS), F32))

    x_next, x_next_bf = _call(
        _out_kernel, "out_norm", (seq // t,),
        [(merged, _row_tile(D_MODEL)), (w_out16, _layer_weight(w_out16, layer)), (x, _row_tile(D_MODEL)),
         (row(ln_g), _resident((1, D_MODEL))), (row(ln_b), _resident((1, D_MODEL))),
         (jnp.zeros((SUBLANES, LANES), jnp.int32), _resident((SUBLANES, LANES)))],
        [(jax.ShapeDtypeStruct((seq, D_MODEL), F32), _row_tile(D_MODEL)),
         (jax.ShapeDtypeStruct((seq, D_MODEL), BF16), _row_tile(D_MODEL))],
        [pltpu.VMEM((t, D_MODEL), F32)], value_bytes=_nbytes((t, D_MODEL), F32))
    return x_next, x_next_bf


def kernel(x, w_in, pool_w, pool_scale, sgu_ln_g, sgu_ln_b, sgu_w, sgu_b, attn_sinks, rel_bias, conv_w,
           conv_b, conv_ln_g, conv_ln_b, w_branch, w_out, ln_g, ln_b):
    batch, seq, d_model = x.shape
    assert d_model == D_MODEL and w_in.shape == (DEPTH, D_MODEL, D_IN)
    assert seq % SEQ_TILE == 0 and SEQ_TILE % (BLOCK * ATTN_GROUP_BLOCKS) == 0

    band = _band_bias(rel_bias).reshape(N_KV_HEADS, PAIRS, 2, BLOCK, 2 * BLOCK)
    band = jnp.transpose(band, (0, 1, 3, 2, 4)).reshape(N_KV_HEADS, PQ, 4 * BLOCK)

    w_parts = _cast_w_in(w_in.astype(F32))
    w_branch16 = w_branch.astype(BF16)
    w_out16 = w_out.astype(BF16)
    outs = []
    for bi in range(batch):
        xc = x[bi].astype(F32)
        x_in = xc
        for l in range(DEPTH):
            xc, x_in = _layer(xc, x_in, band, w_parts, l, pool_w[l], pool_scale[l], sgu_ln_g[l], sgu_ln_b[l],
                              sgu_w[l], sgu_b[l], attn_sinks[l], conv_w[l], conv_b[l], conv_ln_g[l],
                              conv_ln_b[l], w_branch16, w_out16, ln_g[l], ln_b[l])
        outs.append(xc.astype(x.dtype)[None])
    return outs[0] if batch == 1 else jnp.concatenate(outs, axis=0)
```

```python
import functools
import math

import jax
import jax.numpy as jnp
from jax import lax
from jax.experimental import pallas as pl
from jax.experimental.pallas import tpu as pltpu

F32 = jnp.float32
BF16 = jnp.bfloat16

D_MODEL = 2048
DEPTH = 2
N_BRANCH = 4
BRANCH_W = 1024
POOL_WINDOWS = (2, 4, 8, 16)
POOL_GW = BRANCH_W // len(POOL_WINDOWS)
SGU_HEADS = 8
SGU_HD = BRANCH_W // SGU_HEADS
CHUNK = 128
N_Q_HEADS = 16
N_KV_HEADS = 2
Q_PER_KV = N_Q_HEADS // N_KV_HEADS
HEAD_DIM = 64
KV_W = N_KV_HEADS * HEAD_DIM
WINDOW = 128
BLOCK = 128
NUM_BUCKETS = 32
MAX_DISTANCE = 128
CONV_K = 31
ALPHA = (2 * DEPTH) ** 0.25
LN_EPS = 1e-5
NEG_INF = -1e30

OFF_A = 0
OFF_B = OFF_A + 2 * BRANCH_W
OFF_C = OFF_B + 3 * BRANCH_W
OFF_D = OFF_C + 2 * BRANCH_W + 2 * KV_W
OFF_G = OFF_D + 3 * BRANCH_W
D_IN = OFF_G + N_BRANCH * D_MODEL

LANES = 128
SUBLANES = 8
SEQ_TILE = 512
POOL_HALO = 16
CONV_HALO = 32
ROW_CHUNK = 64
STREAM_ROWS = 512
PROJ_COLS = 256
MERGE_COLS = 512
PAIRS = Q_PER_KV // 2
PQ = PAIRS * BLOCK
ATTN_GROUP_BLOCKS = 2
MERGE_TILE = 512
V7X_VMEM_BYTES = 64 * 1024 * 1024


def _sigmoid(x):
    return 0.5 * jnp.tanh(0.5 * x) + 0.5


def _silu(x):
    return x * _sigmoid(x)


def _layer_norm_rows(v, g, b):
    mu = jnp.mean(v, axis=-1, keepdims=True)
    d = v - mu
    var = jnp.mean(d * d, axis=-1, keepdims=True)
    return d * lax.rsqrt(var + LN_EPS) * g + b


def _dot(a, b):
    return jnp.dot(a, b, preferred_element_type=F32)


def _t5_bucket(n):
    max_exact = NUM_BUCKETS // 2
    nf = jnp.maximum(n, 1).astype(F32)
    large = max_exact + (jnp.log(nf / max_exact) / math.log(MAX_DISTANCE / max_exact)
                         * (NUM_BUCKETS - max_exact)).astype(jnp.int32)
    large = jnp.minimum(large, NUM_BUCKETS - 1)
    return jnp.where(n < max_exact, n, large)


def _band_bucket_ids():
    i = jnp.arange(BLOCK)[:, None]
    j = jnp.arange(2 * BLOCK)[None, :]
    return _t5_bucket(jnp.clip(i + BLOCK - j, 0, WINDOW - 1)).astype(jnp.int32)


def _bias_kernel(bucket_ref, table_ref, o_ref):
    h = pl.program_id(0)
    bucket = bucket_ref[...]
    acc = jnp.zeros(bucket.shape, F32)
    for b in range(NUM_BUCKETS):
        acc = jnp.where(bucket == b, table_ref[b, h], acc)
    o_ref[0] = acc


def _band_bias(rel_bias):
    return pl.pallas_call(
        _bias_kernel,
        grid=(N_Q_HEADS,),
        in_specs=[pl.BlockSpec((BLOCK, 2 * BLOCK), lambda h: (0, 0)),
                  pl.BlockSpec(memory_space=pltpu.SMEM)],
        out_specs=pl.BlockSpec((1, BLOCK, 2 * BLOCK), lambda h: (h, 0, 0)),
        out_shape=jax.ShapeDtypeStruct((N_Q_HEADS, BLOCK, 2 * BLOCK), F32),
        name="band_bias",
    )(_band_bucket_ids(), rel_bias.astype(F32))


W_IN_SPLITS = ((OFF_A, OFF_B), (OFF_B, OFF_C), (OFF_C, OFF_D), (OFF_G, D_IN))
CONV_W_CHUNKS = (OFF_G - OFF_D) // PROJ_COLS
CAST_ROWS = 128


def _cast_w_in_kernel(w_ref, oa_ref, ob_ref, oc_ref, og_ref, od_ref):
    for (a, b), o_ref in zip(W_IN_SPLITS, (oa_ref, ob_ref, oc_ref, og_ref)):
        o_ref[...] = w_ref[:, a:b].astype(BF16)
    for j in range(CONV_W_CHUNKS):
        od_ref[j] = w_ref[:, OFF_D + j * PROJ_COLS:OFF_D + (j + 1) * PROJ_COLS].astype(BF16)


def _cast_w_in(w_in):
    widths = [b - a for a, b in W_IN_SPLITS]
    outputs = [(jax.ShapeDtypeStruct((DEPTH, D_MODEL, w), BF16),
                pl.BlockSpec((None, CAST_ROWS, w), lambda l, r: (l, r, 0))) for w in widths]
    outputs.append((jax.ShapeDtypeStruct((DEPTH, CONV_W_CHUNKS, D_MODEL, PROJ_COLS), BF16),
                    pl.BlockSpec((None, CONV_W_CHUNKS, CAST_ROWS, PROJ_COLS), lambda l, r: (l, 0, r, 0))))
    return _call(_cast_w_in_kernel, "cast_w_in", (DEPTH, D_MODEL // CAST_ROWS),
                 [(w_in, pl.BlockSpec((None, CAST_ROWS, D_IN), lambda l, r: (l, r, 0)))],
                 outputs, [], value_bytes=0)


def _pool_kernel(x_ref, w_ref, pw_ref, sc_ref, o_ref, abuf, gate_buf, mix_buf):
    i = pl.program_id(0)
    t_rows = x_ref.shape[0]

    @pl.when(i == 0)
    def _():
        abuf[0:POOL_HALO, :] = jnp.zeros((POOL_HALO, BRANCH_W), F32)

    xb = x_ref[...].astype(BF16)
    abuf[POOL_HALO:POOL_HALO + t_rows, :] = _dot(xb, w_ref[:, 0:BRANCH_W])
    gate_buf[...] = _dot(xb, w_ref[:, BRANCH_W:2 * BRANCH_W])

    for g, win in enumerate(POOL_WINDOWS):
        c0 = g * POOL_GW
        for r0 in range(0, t_rows, ROW_CHUNK):
            base = POOL_HALO + r0
            cur = abuf[base:base + ROW_CHUNK, c0:c0 + POOL_GW]
            acc = cur
            for s in range(1, win):
                acc = acc + abuf[base - s:base - s + ROW_CHUNK, c0:c0 + POOL_GW]
            t = i * t_rows + r0 + lax.broadcasted_iota(jnp.int32, (ROW_CHUNK, POOL_GW), 0)
            cnt = jnp.minimum(t + 1, win).astype(F32)
            mix_buf[r0:r0 + ROW_CHUNK, c0:c0 + POOL_GW] = (acc / cnt - cur).astype(BF16)

    for g in range(len(POOL_WINDOWS)):
        c0 = g * POOL_GW
        y = _dot(mix_buf[:, c0:c0 + POOL_GW], pw_ref[g])
        y = y * sc_ref[:, c0:c0 + POOL_GW] * _silu(gate_buf[:, c0:c0 + POOL_GW])
        o_ref[:, c0:c0 + POOL_GW] = y.astype(BF16)

    abuf[0:POOL_HALO, :] = abuf[t_rows:t_rows + POOL_HALO, :]


def _sgu_kernel(x_ref, w_ref, lng_ref, lnb_ref, sw_ref, sb_ref, o_ref,
                u_buf, v_buf, gate_buf, vn_buf):
    t_rows = x_ref.shape[0]
    xb = x_ref[...].astype(BF16)
    u_buf[...] = _dot(xb, w_ref[:, 0:BRANCH_W])
    v_buf[...] = _dot(xb, w_ref[:, BRANCH_W:2 * BRANCH_W])
    gate_buf[...] = _dot(xb, w_ref[:, 2 * BRANCH_W:3 * BRANCH_W])

    g = lng_ref[...]
    b = lnb_ref[...]
    for r0 in range(0, t_rows, ROW_CHUNK):
        vn = _layer_norm_rows(v_buf[r0:r0 + ROW_CHUNK, :], g, b)
        vn_buf[r0:r0 + ROW_CHUNK, :] = vn.astype(BF16)

    row = lax.broadcasted_iota(jnp.int32, (CHUNK, CHUNK), 0)
    col = lax.broadcasted_iota(jnp.int32, (CHUNK, CHUNK), 1)
    causal = row >= col
    for h in range(SGU_HEADS):
        c0 = h * SGU_HD
        w_h = jnp.where(causal, sw_ref[h], 0.0).astype(BF16)
        bias_h = sb_ref[:, c0:c0 + SGU_HD]
        for r0 in range(0, t_rows, CHUNK):
            sp = _dot(w_h, vn_buf[r0:r0 + CHUNK, c0:c0 + SGU_HD]) + bias_h
            y = u_buf[r0:r0 + CHUNK, c0:c0 + SGU_HD] * sp * _silu(gate_buf[r0:r0 + CHUNK, c0:c0 + SGU_HD])
            o_ref[r0:r0 + CHUNK, c0:c0 + SGU_HD] = y.astype(BF16)


def _attn_kernel(x_ref, w_ref, bias_ref, sink_ref, sinkc_ref, zero_ref, o_ref,
                 xs_buf, q_buf, gate_buf, k_tab, v_tab, code_buf, ones_buf, s_buf, p_buf, m_buf, mc_buf):
    i = pl.program_id(0)
    t_rows = x_ref.shape[0]
    n_blocks = t_rows // BLOCK

    @pl.when(i == 0)
    def _():
        zero = jnp.zeros((BLOCK, LANES), BF16)
        for j in range(N_KV_HEADS):
            for half in range(2):
                k_tab[j, half, 0:BLOCK, :] = zero
                v_tab[j, half, 0:BLOCK, :] = zero
        row = lax.broadcasted_iota(jnp.int32, (PQ, 2 * 2 * BLOCK), 0)
        col = lax.broadcasted_iota(jnp.int32, (PQ, 2 * 2 * BLOCK), 1)
        key = col & (2 * BLOCK - 1)
        u = key - (row & (BLOCK - 1)) - 1
        in_window = (u >= 0) & (u < WINDOW)
        code_buf[...] = jnp.where(in_window, jnp.where(key >= BLOCK, 2, 1), 0)
        krow = lax.broadcasted_iota(jnp.int32, (4 * BLOCK, LANES), 0)
        klane = lax.broadcasted_iota(jnp.int32, (4 * BLOCK, LANES), 1)
        ones_buf[...] = jnp.where((krow >= 2 * BLOCK) == (klane >= HEAD_DIM), 1.0, 0.0).astype(BF16)

    g_len = ATTN_GROUP_BLOCKS * BLOCK
    xs_buf[...] = x_ref[...].astype(BF16)
    low = lax.broadcasted_iota(jnp.int32, (g_len, LANES), 1) < HEAD_DIM
    gate_col0 = BRANCH_W + 2 * KV_W

    def project_group(g0, after):
        g_rows = slice(g0 * BLOCK, g0 * BLOCK + g_len)
        xg = xs_buf[g_rows, :]
        if after is not None:
            nothing = pltpu.bitcast(after, jnp.int32) & zero_ref[...]
            nothing = jnp.concatenate([nothing] * (g_len // (2 * SUBLANES)), axis=0)
            nothing = jnp.concatenate([nothing] * (D_MODEL // LANES), axis=1)
            xg = pltpu.bitcast(pltpu.bitcast(xg, jnp.int32) | nothing, BF16)
        for piece in range(BRANCH_W // PROJ_COLS):
            c0 = piece * PROJ_COLS
            q = _dot(xg, w_ref[:, c0:c0 + PROJ_COLS]) * (HEAD_DIM ** -0.5)
            q_buf[g_rows, c0:c0 + PROJ_COLS] = q.astype(BF16)
        kv = _dot(xg, w_ref[:, BRANCH_W:BRANCH_W + 2 * KV_W])
        t_rows_g = slice(BLOCK + g0 * BLOCK, BLOCK + g0 * BLOCK + g_len)
        for tab, c0 in ((k_tab, 0), (v_tab, KV_W)):
            val = kv[:, c0:c0 + KV_W]
            swapped = pltpu.roll(val, HEAD_DIM, axis=1)
            zero = jnp.zeros_like(val)
            tab[0, 0, t_rows_g, :] = jnp.where(low, val, zero).astype(BF16)
            tab[0, 1, t_rows_g, :] = jnp.where(low, zero, swapped).astype(BF16)
            tab[1, 0, t_rows_g, :] = jnp.where(low, swapped, zero).astype(BF16)
            tab[1, 1, t_rows_g, :] = jnp.where(low, zero, val).astype(BF16)
        for piece in range(BRANCH_W // PROJ_COLS):
            c0 = piece * PROJ_COLS
            gate_buf[g_rows, c0:c0 + PROJ_COLS] = _dot(xg, w_ref[:, gate_col0 + c0:gate_col0 + c0 + PROJ_COLS])

    low128 = lax.broadcasted_iota(jnp.int32, (BLOCK, LANES), 1) < HEAD_DIM
    project_group(0, None)
    for g0 in range(0, n_blocks, ATTN_GROUP_BLOCKS):
        units = [(n, j) for n in range(g0, g0 + ATTN_GROUP_BLOCKS) for j in range(N_KV_HEADS)]

        for it, (n, j) in enumerate(units):
            r0 = n * BLOCK
            q_cols = j * PAIRS * LANES
            qs = jnp.concatenate(
                [q_buf[r0:r0 + BLOCK, q_cols + p * LANES:q_cols + (p + 1) * LANES] for p in range(PAIRS)], axis=0)
            k2 = jnp.concatenate([k_tab[j, 0, r0:r0 + 2 * BLOCK, :], k_tab[j, 1, r0:r0 + 2 * BLOCK, :]], axis=0)
            logits = lax.dot_general(qs, k2, (((1,), (1,)), ((), ())), preferred_element_type=F32)
            threshold = jnp.where(jnp.logical_and(i == 0, n == 0), 2, 1)
            for p in range(PAIRS):
                rows = slice(p * BLOCK, (p + 1) * BLOCK)
                row_max = []
                for par in range(2):
                    slabs = []
                    for half in range(2):
                        cols = slice((2 * par + half) * LANES, (2 * par + half + 1) * LANES)
                        s = logits[rows, cols] + bias_ref[j, rows, cols]
                        s = jnp.where(code_buf[rows, cols] >= threshold, s, NEG_INF)
                        s_buf[it, rows, cols] = s
                        slabs.append(s)
                    m = jnp.max(jnp.maximum(slabs[0], slabs[1]), axis=1, keepdims=True)
                    m = jnp.maximum(jnp.broadcast_to(m, (BLOCK, LANES)), sink_ref[j, par, rows, :])
                    m_buf[it, par, rows, :] = m
                    row_max.append(m)
                mc_buf[it, rows, :] = jnp.where(low128, row_max[0], row_max[1])

        if g0 + ATTN_GROUP_BLOCKS < n_blocks:
            project_group(g0 + ATTN_GROUP_BLOCKS, logits[0:SUBLANES, 0:LANES])

        for it, (n, j) in enumerate(units):
            for p in range(PAIRS):
                rows = slice(p * BLOCK, (p + 1) * BLOCK)
                for par in range(2):
                    m = m_buf[it, par, rows, :]
                    for half in range(2):
                        cols = slice((2 * par + half) * LANES, (2 * par + half + 1) * LANES)
                        p_buf[it, rows, cols] = jnp.exp(s_buf[it, rows, cols] - m).astype(BF16)

        for it, (n, j) in enumerate(units):
            r0 = n * BLOCK
            q_cols = j * PAIRS * LANES
            v2 = jnp.concatenate([v_tab[j, 0, r0:r0 + 2 * BLOCK, :], v_tab[j, 1, r0:r0 + 2 * BLOCK, :]], axis=0)
            o = _dot(p_buf[it], jnp.concatenate([v2, ones_buf[...]], axis=1))
            for p in range(PAIRS):
                rows = slice(p * BLOCK, (p + 1) * BLOCK)
                cols = slice(q_cols + p * LANES, q_cols + (p + 1) * LANES)
                den = o[rows, LANES:2 * LANES] + jnp.exp(sinkc_ref[j, rows, :] - mc_buf[it, rows, :])
                y = o[rows, 0:LANES] * (1.0 / den) * _silu(gate_buf[r0:r0 + BLOCK, cols])
                o_ref[r0:r0 + BLOCK, cols] = y.astype(BF16)

    for tab in (k_tab, v_tab):
        for j in range(N_KV_HEADS):
            for half in range(2):
                tab[j, half, 0:BLOCK, :] = tab[j, half, t_rows:t_rows + BLOCK, :]


CONV_COLUMNS = BRANCH_W // LANES
CONV_BLOCKS = BRANCH_W // PROJ_COLS


def _conv_stream_kernel(x_ref, w_ref, cw_ref, cb_ref, lng_ref, lnb_ref, zero_ref, o_ref,
                        xs_buf, glu_buf, gate_buf, shift_buf, y_buf):
    i = pl.program_id(0)
    t_rows = x_ref.shape[0]

    @pl.when(i == 0)
    def _():
        for c in range(CONV_COLUMNS):
            glu_buf[c, 0:CONV_HALO, :] = jnp.zeros((CONV_HALO, LANES), F32)

    xs_buf[...] = x_ref[...].astype(BF16)
    first = CONV_HALO - (CONV_K - 1)
    span = t_rows + CONV_HALO - SUBLANES

    def glu_block(n):
        gate_val = _dot(xs_buf[...], w_ref[CONV_BLOCKS + n])
        glu = _dot(xs_buf[...], w_ref[n]) * _sigmoid(gate_val)
        glu_buf[2 * n, CONV_HALO:CONV_HALO + t_rows, :] = glu[:, 0:LANES]
        glu_buf[2 * n + 1, CONV_HALO:CONV_HALO + t_rows, :] = glu[:, LANES:2 * LANES]
        return gate_val[t_rows - SUBLANES:t_rows, 0:LANES]

    def gate_block(n):
        gate = _dot(xs_buf[...], w_ref[2 * CONV_BLOCKS + n])
        gate_buf[2 * n] = gate[:, 0:LANES]
        gate_buf[2 * n + 1] = gate[:, LANES:2 * LANES]
        return gate[t_rows - SUBLANES:t_rows, 0:LANES]

    def conv_block(n, token, second_trigger):
        for half in range(2):
            c = 2 * n + half
            if half == 1:
                token = token + second_trigger()
            for r in range(1, SUBLANES):
                shift_buf[half, r, 0:span, :] = glu_buf[c, r:r + span, :]
            bias = jnp.broadcast_to(cb_ref[c], (STREAM_ROWS, LANES))
            for r0 in range(0, t_rows, STREAM_ROWS):
                nothing = pltpu.bitcast(pltpu.bitcast(token, jnp.int32) & zero_ref[...], F32)
                acc = bias
                for k in range(CONV_K):
                    a8 = (first + k) // SUBLANES * SUBLANES
                    r = (first + k) % SUBLANES
                    if r == 0:
                        src = glu_buf[c, r0 + a8:r0 + a8 + STREAM_ROWS, :]
                    else:
                        src = shift_buf[half, r, r0 + a8:r0 + a8 + STREAM_ROWS, :]
                    tap = jnp.concatenate([cw_ref[c, k] + nothing] * (STREAM_ROWS // SUBLANES), axis=0)
                    acc = acc + tap * src
                y_buf[c, r0:r0 + STREAM_ROWS, :] = acc
                token = acc[0:SUBLANES, :]

    for n in range(CONV_BLOCKS):
        conv_block(n, glu_block(n), lambda n=n: gate_block(n))

    for r0 in range(0, t_rows, ROW_CHUNK):
        rows = slice(r0, r0 + ROW_CHUNK)
        ys = [y_buf[c, rows, :] for c in range(CONV_COLUMNS)]
        total = ys[0]
        for y in ys[1:]:
            total = total + y
        mu = jnp.sum(total, axis=1, keepdims=True) * (1.0 / BRANCH_W)
        ds = [y - mu for y in ys]
        sq = ds[0] * ds[0]
        for d in ds[1:]:
            sq = sq + d * d
        rstd = lax.rsqrt(jnp.sum(sq, axis=1, keepdims=True) * (1.0 / BRANCH_W) + LN_EPS)
        for c in range(CONV_COLUMNS):
            y = _silu(ds[c] * rstd * lng_ref[c] + lnb_ref[c]) * _silu(gate_buf[c, rows, :])
            o_ref[rows, c * LANES:(c + 1) * LANES] = y.astype(BF16)

    for c in range(CONV_COLUMNS):
        glu_buf[c, 0:CONV_HALO, :] = glu_buf[c, t_rows:t_rows + CONV_HALO, :]


def _merge_kernel(x_ref, ya_ref, yb_ref, yc_ref, yd_ref, wga_ref, wgb_ref, wgc_ref, wgd_ref, wb_ref, o_ref):
    xb = x_ref[...].astype(BF16)
    acc = None
    branches = ((ya_ref, wga_ref), (yb_ref, wgb_ref), (yc_ref, wgc_ref), (yd_ref, wgd_ref))
    for br, (y_ref, wg_ref) in enumerate(branches):
        gate = _sigmoid(_dot(xb, wg_ref[...]))
        term = gate * _dot(y_ref[...], wb_ref[br])
        acc = term if acc is None else acc + term
    o_ref[...] = acc.astype(BF16)


def _out_kernel(m_ref, w_ref, x_ref, g_ref, b_ref, zero_ref, o_ref, ob_ref, z_buf):
    t_rows = x_ref.shape[0]
    half_rows = t_rows // 2
    chunk = ROW_CHUNK // 2
    tokens = []
    for h in range(2):
        rows = slice(h * half_rows, (h + 1) * half_rows)
        z = _dot(m_ref[rows, :], w_ref[...])
        z_buf[rows, :] = z
        tokens.append(z[half_rows - SUBLANES:half_rows, 0:LANES])
    g = g_ref[...]
    b = b_ref[...]
    for h in range(2):
        nothing = pltpu.bitcast(pltpu.bitcast(tokens[h], jnp.int32) & zero_ref[...], F32)
        nothing = jnp.concatenate([nothing] * (chunk // SUBLANES), axis=0)
        nothing = jnp.concatenate([nothing] * (D_MODEL // LANES), axis=1)
        for r0 in range(h * half_rows, (h + 1) * half_rows, chunk):
            rows = slice(r0, r0 + chunk)
            y = _layer_norm_rows(ALPHA * x_ref[rows, :] + (z_buf[rows, :] + nothing), g, b)
            o_ref[rows, :] = y
            ob_ref[rows, :] = y.astype(BF16)


def _resident(shape):
    zeros = (0,) * len(shape)
    return pl.BlockSpec(shape, lambda *_: zeros, pipeline_mode=pl.Buffered(1))


def _nbytes(shape, dtype):
    return math.prod(d for d in shape if d is not None) * jnp.dtype(dtype).itemsize


def _call(kernel_fn, name, grid, inputs, outputs, scratch, value_bytes):
    arrays, in_specs = zip(*inputs)
    out_shape, out_specs = zip(*outputs)
    windows = [(spec, a.dtype) for a, spec in inputs] + [(spec, o.dtype) for o, spec in outputs]
    need = value_bytes + sum(_nbytes(v.shape, v.dtype) for v in scratch)
    for spec, dtype in windows:
        buffers = spec.pipeline_mode.buffer_count if spec.pipeline_mode is not None else 2
        need += buffers * _nbytes(spec.block_shape, dtype)
    assert need <= V7X_VMEM_BYTES, (name, need)
    return pl.pallas_call(
        kernel_fn,
        grid=grid,
        in_specs=list(in_specs),
        out_specs=list(out_specs) if len(out_specs) > 1 else out_specs[0],
        out_shape=list(out_shape) if len(out_shape) > 1 else out_shape[0],
        scratch_shapes=scratch,
        compiler_params=pltpu.CompilerParams(dimension_semantics=("arbitrary",) * len(grid),
                                             vmem_limit_bytes=need),
        name=name,
    )(*arrays)


def _row_tile(width, rows=SEQ_TILE):
    return pl.BlockSpec((rows, width), lambda i: (i, 0))


def _layer_weight(w_all, layer):
    idx = (layer,) + (0,) * (w_all.ndim - 1)
    return pl.BlockSpec((None,) + w_all.shape[1:], lambda *_: idx, pipeline_mode=pl.Buffered(1))


def _branch_call(kernel_fn, name, seq, x_in, w_all, layer, operands, scratch, value_tiles=2):
    value_bytes = value_tiles * _nbytes((SEQ_TILE, BRANCH_W), F32)
    inputs = [(x_in, _row_tile(D_MODEL)), (w_all, _layer_weight(w_all, layer))]
    inputs += [(op, _resident(op.shape)) for op in operands]
    outputs = [(jax.ShapeDtypeStruct((seq, BRANCH_W), BF16), _row_tile(BRANCH_W))]
    return _call(kernel_fn, name, (seq // SEQ_TILE,), inputs, outputs, scratch, value_bytes)


def _layer(x, x_in, band, w_parts, layer, pool_w, pool_scale, sgu_ln_g, sgu_ln_b, sgu_w, sgu_b, sinks,
           conv_w, conv_b, conv_ln_g, conv_ln_b, w_branch16, w_out16, ln_g, ln_b):
    seq = x.shape[0]
    t = SEQ_TILE
    w_pool, w_sgu, w_attn, w_gates, w_conv = w_parts
    row = lambda v: v.reshape(1, -1).astype(F32)

    y_a = _branch_call(
        _pool_kernel, "branch_pool", seq, x_in, w_pool, layer,
        [pool_w.astype(BF16), row(pool_scale)],
        [pltpu.VMEM((t + POOL_HALO, BRANCH_W), F32), pltpu.VMEM((t, BRANCH_W), F32),
         pltpu.VMEM((t, BRANCH_W), BF16)])

    sgu_bias = jnp.repeat(jnp.transpose(sgu_b).astype(F32), SGU_HD, axis=1)
    y_b = _branch_call(
        _sgu_kernel, "branch_sgu", seq, x_in, w_sgu, layer,
        [row(sgu_ln_g), row(sgu_ln_b), sgu_w.astype(F32), sgu_bias],
        [pltpu.VMEM((t, BRANCH_W), F32), pltpu.VMEM((t, BRANCH_W), F32),
         pltpu.VMEM((t, BRANCH_W), F32), pltpu.VMEM((t, BRANCH_W), BF16)])

    sink_hp = sinks.astype(F32).reshape(N_KV_HEADS, PAIRS, 2)
    sink_rows = jnp.repeat(jnp.transpose(sink_hp, (0, 2, 1)), BLOCK, axis=2)
    sink_rows = jnp.broadcast_to(sink_rows[..., None], (N_KV_HEADS, 2, PQ, LANES))
    sink_lanes = jnp.repeat(jnp.repeat(sink_hp, BLOCK, axis=1), HEAD_DIM, axis=2)
    n_units = N_KV_HEADS * ATTN_GROUP_BLOCKS
    y_c = _branch_call(
        _attn_kernel, "branch_attn", seq, x_in, w_attn, layer,
        [band, sink_rows, sink_lanes, jnp.zeros((SUBLANES, LANES), jnp.int32)],
        [pltpu.VMEM((t, D_MODEL), BF16), pltpu.VMEM((t, BRANCH_W), BF16), pltpu.VMEM((t, BRANCH_W), F32),
         pltpu.VMEM((N_KV_HEADS, 2, t + BLOCK, LANES), BF16),
         pltpu.VMEM((N_KV_HEADS, 2, t + BLOCK, LANES), BF16),
         pltpu.VMEM((PQ, 4 * BLOCK), jnp.int32),
         pltpu.VMEM((4 * BLOCK, LANES), BF16),
         pltpu.VMEM((n_units, PQ, 4 * BLOCK), F32),
         pltpu.VMEM((n_units, PQ, 4 * BLOCK), BF16),
         pltpu.VMEM((n_units, 2, PQ, LANES), F32),
         pltpu.VMEM((n_units, PQ, LANES), F32)])

    per_col = lambda v: v.astype(F32).reshape(CONV_COLUMNS, 1, LANES)
    taps = jnp.transpose(conv_w.astype(F32).reshape(CONV_K, CONV_COLUMNS, LANES), (1, 0, 2))
    taps = jnp.broadcast_to(taps[:, :, None, :], (CONV_COLUMNS, CONV_K, SUBLANES, LANES))
    y_d = _branch_call(
        _conv_stream_kernel, "branch_conv", seq, x_in, w_conv, layer,
        [taps, per_col(conv_b), per_col(conv_ln_g), per_col(conv_ln_b), jnp.zeros((SUBLANES, LANES), jnp.int32)],
        [pltpu.VMEM((t, D_MODEL), BF16),
         pltpu.VMEM((CONV_COLUMNS, t + CONV_HALO, LANES), F32),
         pltpu.VMEM((CONV_COLUMNS, t, LANES), F32),
         pltpu.VMEM((2, SUBLANES, t + CONV_HALO - SUBLANES, LANES), F32),
         pltpu.VMEM((CONV_COLUMNS, t, LANES), F32)], value_tiles=4)

    n_cb = D_MODEL // MERGE_COLS
    tm = MERGE_TILE
    x_spec = pl.BlockSpec((tm, D_MODEL), lambda c, i: (i, 0))
    y_spec = pl.BlockSpec((tm, BRANCH_W), lambda c, i: (i, 0))
    wg_specs = [pl.BlockSpec((None, D_MODEL, MERGE_COLS),
                             functools.partial(lambda br, c, i: (layer, 0, br * n_cb + c), br))
                for br in range(N_BRANCH)]
    wb_spec = pl.BlockSpec((None, N_BRANCH, BRANCH_W, MERGE_COLS), lambda c, i: (layer, 0, 0, c))
    merged = _call(
        _merge_kernel, "merge", (n_cb, seq // tm),
        [(x_in, x_spec)] + [(y, y_spec) for y in (y_a, y_b, y_c, y_d)]
        + [(w_gates, spec) for spec in wg_specs] + [(w_branch16, wb_spec)],
        [(jax.ShapeDtypeStruct((seq, D_MODEL), BF16), pl.BlockSpec((tm, MERGE_COLS), lambda c, i: (i, c)))],
        [], value_bytes=4 * _nbytes((tm, MERGE_COLS), F32))

    x_next, x_next_bf = _call(
        _out_kernel, "out_norm", (seq // t,),
        [(merged, _row_tile(D_MODEL)), (w_out16, _layer_weight(w_out16, layer)), (x, _row_tile(D_MODEL)),
         (row(ln_g), _resident((1, D_MODEL))), (row(ln_b), _resident((1, D_MODEL))),
         (jnp.zeros((SUBLANES, LANES), jnp.int32), _resident((SUBLANES, LANES)))],
        [(jax.ShapeDtypeStruct((seq, D_MODEL), F32), _row_tile(D_MODEL)),
         (jax.ShapeDtypeStruct((seq, D_MODEL), BF16), _row_tile(D_MODEL))],
        [pltpu.VMEM((t, D_MODEL), F32)], value_bytes=_nbytes((t, D_MODEL), F32))
    return x_next, x_next_bf


def kernel(x, w_in, pool_w, pool_scale, sgu_ln_g, sgu_ln_b, sgu_w, sgu_b, attn_sinks, rel_bias, conv_w,
           conv_b, conv_ln_g, conv_ln_b, w_branch, w_out, ln_g, ln_b):
    batch, seq, d_model = x.shape
    assert d_model == D_MODEL and w_in.shape == (DEPTH, D_MODEL, D_IN)
    assert seq % SEQ_TILE == 0 and SEQ_TILE % (BLOCK * ATTN_GROUP_BLOCKS) == 0

    band = _band_bias(rel_bias).reshape(N_KV_HEADS, PAIRS, 2, BLOCK, 2 * BLOCK)
    band = jnp.transpose(band, (0, 1, 3, 2, 4)).reshape(N_KV_HEADS, PQ, 4 * BLOCK)

    w_parts = _cast_w_in(w_in.astype(F32))
    w_branch16 = w_branch.astype(BF16)
    w_out16 = w_out.astype(BF16)
    outs = []
    for bi in range(batch):
        xc = x[bi].astype(F32)
        x_in = xc
        for l in range(DEPTH):
            xc, x_in = _layer(xc, x_in, band, w_parts, l, pool_w[l], pool_scale[l], sgu_ln_g[l], sgu_ln_b[l],
                              sgu_w[l], sgu_b[l], attn_sinks[l], conv_w[l], conv_b[l], conv_ln_g[l],
                              conv_ln_b[l], w_branch16, w_out16, ln_g[l], ln_b[l])
        outs.append(xc.astype(x.dtype)[None])
    return outs[0] if batch == 1 else jnp.concatenate(outs, axis=0)
```

```python
import functools
import math

import jax
import jax.numpy as jnp
from jax import lax
from jax.experimental import pallas as pl
from jax.experimental.pallas import tpu as pltpu

F32 = jnp.float32
BF16 = jnp.bfloat16

D_MODEL = 2048
DEPTH = 2
N_BRANCH = 4
BRANCH_W = 1024
POOL_WINDOWS = (2, 4, 8, 16)
POOL_GW = BRANCH_W // len(POOL_WINDOWS)
SGU_HEADS = 8
SGU_HD = BRANCH_W // SGU_HEADS
CHUNK = 128
N_Q_HEADS = 16
N_KV_HEADS = 2
Q_PER_KV = N_Q_HEADS // N_KV_HEADS
HEAD_DIM = 64
KV_W = N_KV_HEADS * HEAD_DIM
WINDOW = 128
BLOCK = 128
NUM_BUCKETS = 32
MAX_DISTANCE = 128
CONV_K = 31
ALPHA = (2 * DEPTH) ** 0.25
LN_EPS = 1e-5
NEG_INF = -1e30

OFF_A = 0
OFF_B = OFF_A + 2 * BRANCH_W
OFF_C = OFF_B + 3 * BRANCH_W
OFF_D = OFF_C + 2 * BRANCH_W + 2 * KV_W
OFF_G = OFF_D + 3 * BRANCH_W
D_IN = OFF_G + N_BRANCH * D_MODEL

LANES = 128
SUBLANES = 8
SEQ_TILE = 512
POOL_HALO = 16
CONV_HALO = 32
ROW_CHUNK = 64
STREAM_ROWS = 512
PROJ_COLS = 256
MERGE_COLS = 512
PAIRS = Q_PER_KV // 2
PQ = PAIRS * BLOCK
ATTN_GROUP_BLOCKS = 2
MERGE_TILE = 512
V7X_VMEM_BYTES = 64 * 1024 * 1024


def _sigmoid(x):
    return 0.5 * jnp.tanh(0.5 * x) + 0.5


def _silu(x):
    return x * _sigmoid(x)


def _layer_norm_rows(v, g, b):
    mu = jnp.mean(v, axis=-1, keepdims=True)
    d = v - mu
    var = jnp.mean(d * d, axis=-1, keepdims=True)
    return d * lax.rsqrt(var + LN_EPS) * g + b


def _dot(a, b):
    return jnp.dot(a, b, preferred_element_type=F32)


def _t5_bucket(n):
    max_exact = NUM_BUCKETS // 2
    nf = jnp.maximum(n, 1).astype(F32)
    large = max_exact + (jnp.log(nf / max_exact) / math.log(MAX_DISTANCE / max_exact)
                         * (NUM_BUCKETS - max_exact)).astype(jnp.int32)
    large = jnp.minimum(large, NUM_BUCKETS - 1)
    return jnp.where(n < max_exact, n, large)


def _band_bucket_ids():
    i = jnp.arange(BLOCK)[:, None]
    j = jnp.arange(2 * BLOCK)[None, :]
    return _t5_bucket(jnp.clip(i + BLOCK - j, 0, WINDOW - 1)).astype(jnp.int32)


def _bias_kernel(bucket_ref, table_ref, o_ref):
    h = pl.program_id(0)
    bucket = bucket_ref[...]
    acc = jnp.zeros(bucket.shape, F32)
    for b in range(NUM_BUCKETS):
        acc = jnp.where(bucket == b, table_ref[b, h], acc)
    o_ref[0] = acc


def _band_bias(rel_bias):
    return pl.pallas_call(
        _bias_kernel,
        grid=(N_Q_HEADS,),
        in_specs=[pl.BlockSpec((BLOCK, 2 * BLOCK), lambda h: (0, 0)),
                  pl.BlockSpec(memory_space=pltpu.SMEM)],
        out_specs=pl.BlockSpec((1, BLOCK, 2 * BLOCK), lambda h: (h, 0, 0)),
        out_shape=jax.ShapeDtypeStruct((N_Q_HEADS, BLOCK, 2 * BLOCK), F32),
        name="band_bias",
    )(_band_bucket_ids(), rel_bias.astype(F32))


W_IN_SPLITS = ((OFF_A, OFF_B), (OFF_B, OFF_C), (OFF_C, OFF_D), (OFF_G, D_IN))
CONV_W_CHUNKS = (OFF_G - OFF_D) // PROJ_COLS
CAST_ROWS = 128


def _cast_w_in_kernel(w_ref, oa_ref, ob_ref, oc_ref, og_ref, od_ref):
    for (a, b), o_ref in zip(W_IN_SPLITS, (oa_ref, ob_ref, oc_ref, og_ref)):
        o_ref[...] = w_ref[:, a:b].astype(BF16)
    for j in range(CONV_W_CHUNKS):
        od_ref[j] = w_ref[:, OFF_D + j * PROJ_COLS:OFF_D + (j + 1) * PROJ_COLS].astype(BF16)


def _cast_w_in(w_in):
    widths = [b - a for a, b in W_IN_SPLITS]
    outputs = [(jax.ShapeDtypeStruct((DEPTH, D_MODEL, w), BF16),
                pl.BlockSpec((None, CAST_ROWS, w), lambda l, r: (l, r, 0))) for w in widths]
    outputs.append((jax.ShapeDtypeStruct((DEPTH, CONV_W_CHUNKS, D_MODEL, PROJ_COLS), BF16),
                    pl.BlockSpec((None, CONV_W_CHUNKS, CAST_ROWS, PROJ_COLS), lambda l, r: (l, 0, r, 0))))
    return _call(_cast_w_in_kernel, "cast_w_in", (DEPTH, D_MODEL // CAST_ROWS),
                 [(w_in, pl.BlockSpec((None, CAST_ROWS, D_IN), lambda l, r: (l, r, 0)))],
                 outputs, [], value_bytes=0)


def _pool_kernel(x_ref, w_ref, pw_ref, sc_ref, o_ref, abuf, gate_buf, mix_buf):
    i = pl.program_id(0)
    t_rows = x_ref.shape[0]

    @pl.when(i == 0)
    def _():
        abuf[0:POOL_HALO, :] = jnp.zeros((POOL_HALO, BRANCH_W), F32)

    xb = x_ref[...].astype(BF16)
    abuf[POOL_HALO:POOL_HALO + t_rows, :] = _dot(xb, w_ref[:, 0:BRANCH_W])
    gate_buf[...] = _dot(xb, w_ref[:, BRANCH_W:2 * BRANCH_W])

    for g, win in enumerate(POOL_WINDOWS):
        c0 = g * POOL_GW
        for r0 in range(0, t_rows, ROW_CHUNK):
            base = POOL_HALO + r0
            cur = abuf[base:base + ROW_CHUNK, c0:c0 + POOL_GW]
            acc = cur
            for s in range(1, win):
                acc = acc + abuf[base - s:base - s + ROW_CHUNK, c0:c0 + POOL_GW]
            t = i * t_rows + r0 + lax.broadcasted_iota(jnp.int32, (ROW_CHUNK, POOL_GW), 0)
            cnt = jnp.minimum(t + 1, win).astype(F32)
            mix_buf[r0:r0 + ROW_CHUNK, c0:c0 + POOL_GW] = (acc / cnt - cur).astype(BF16)

    for g in range(len(POOL_WINDOWS)):
        c0 = g * POOL_GW
        y = _dot(mix_buf[:, c0:c0 + POOL_GW], pw_ref[g])
        y = y * sc_ref[:, c0:c0 + POOL_GW] * _silu(gate_buf[:, c0:c0 + POOL_GW])
        o_ref[:, c0:c0 + POOL_GW] = y.astype(BF16)

    abuf[0:POOL_HALO, :] = abuf[t_rows:t_rows + POOL_HALO, :]


def _sgu_kernel(x_ref, w_ref, lng_ref, lnb_ref, sw_ref, sb_ref, o_ref,
                u_buf, v_buf, gate_buf, vn_buf):
    t_rows = x_ref.shape[0]
    xb = x_ref[...].astype(BF16)
    u_buf[...] = _dot(xb, w_ref[:, 0:BRANCH_W])
    v_buf[...] = _dot(xb, w_ref[:, BRANCH_W:2 * BRANCH_W])
    gate_buf[...] = _dot(xb, w_ref[:, 2 * BRANCH_W:3 * BRANCH_W])

    g = lng_ref[...]
    b = lnb_ref[...]
    for r0 in range(0, t_rows, ROW_CHUNK):
        vn = _layer_norm_rows(v_buf[r0:r0 + ROW_CHUNK, :], g, b)
        vn_buf[r0:r0 + ROW_CHUNK, :] = vn.astype(BF16)

    row = lax.broadcasted_iota(jnp.int32, (CHUNK, CHUNK), 0)
    col = lax.broadcasted_iota(jnp.int32, (CHUNK, CHUNK), 1)
    causal = row >= col
    for h in range(SGU_HEADS):
        c0 = h * SGU_HD
        w_h = jnp.where(causal, sw_ref[h], 0.0).astype(BF16)
        bias_h = sb_ref[:, c0:c0 + SGU_HD]
        for r0 in range(0, t_rows, CHUNK):
            sp = _dot(w_h, vn_buf[r0:r0 + CHUNK, c0:c0 + SGU_HD]) + bias_h
            y = u_buf[r0:r0 + CHUNK, c0:c0 + SGU_HD] * sp * _silu(gate_buf[r0:r0 + CHUNK, c0:c0 + SGU_HD])
            o_ref[r0:r0 + CHUNK, c0:c0 + SGU_HD] = y.astype(BF16)


def _attn_kernel(x_ref, w_ref, bias_ref, sink_ref, sinkc_ref, zero_ref, o_ref,
                 xs_buf, q_buf, gate_buf, k_tab, v_tab, code_buf, ones_buf, s_buf, p_buf, m_buf, mc_buf):
    i = pl.program_id(0)
    t_rows = x_ref.shape[0]
    n_blocks = t_rows // BLOCK

    @pl.when(i == 0)
    def _():
        zero = jnp.zeros((BLOCK, LANES), BF16)
        for j in range(N_KV_HEADS):
            for half in range(2):
                k_tab[j, half, 0:BLOCK, :] = zero
                v_tab[j, half, 0:BLOCK, :] = zero
        row = lax.broadcasted_iota(jnp.int32, (PQ, 2 * 2 * BLOCK), 0)
        col = lax.broadcasted_iota(jnp.int32, (PQ, 2 * 2 * BLOCK), 1)
        key = col & (2 * BLOCK - 1)
        u = key - (row & (BLOCK - 1)) - 1
        in_window = (u >= 0) & (u < WINDOW)
        code_buf[...] = jnp.where(in_window, jnp.where(key >= BLOCK, 2, 1), 0)
        krow = lax.broadcasted_iota(jnp.int32, (4 * BLOCK, LANES), 0)
        klane = lax.broadcasted_iota(jnp.int32, (4 * BLOCK, LANES), 1)
        ones_buf[...] = jnp.where((krow >= 2 * BLOCK) == (klane >= HEAD_DIM), 1.0, 0.0).astype(BF16)

    g_len = ATTN_GROUP_BLOCKS * BLOCK
    xs_buf[...] = x_ref[...].astype(BF16)
    low = lax.broadcasted_iota(jnp.int32, (g_len, LANES), 1) < HEAD_DIM
    gate_col0 = BRANCH_W + 2 * KV_W

    def project_group(g0, after):
        g_rows = slice(g0 * BLOCK, g0 * BLOCK + g_len)
        xg = xs_buf[g_rows, :]
        if after is not None:
            nothing = pltpu.bitcast(after, jnp.int32) & zero_ref[...]
            nothing = jnp.concatenate([nothing] * (g_len // (2 * SUBLANES)), axis=0)
            nothing = jnp.concatenate([nothing] * (D_MODEL // LANES), axis=1)
            xg = pltpu.bitcast(pltpu.bitcast(xg, jnp.int32) | nothing, BF16)
        for piece in range(BRANCH_W // PROJ_COLS):
            c0 = piece * PROJ_COLS
            q = _dot(xg, w_ref[:, c0:c0 + PROJ_COLS]) * (HEAD_DIM ** -0.5)
            q_buf[g_rows, c0:c0 + PROJ_COLS] = q.astype(BF16)
        kv = _dot(xg, w_ref[:, BRANCH_W:BRANCH_W + 2 * KV_W])
        t_rows_g = slice(BLOCK + g0 * BLOCK, BLOCK + g0 * BLOCK + g_len)
        for tab, c0 in ((k_tab, 0), (v_tab, KV_W)):
            val = kv[:, c0:c0 + KV_W]
            swapped = pltpu.roll(val, HEAD_DIM, axis=1)
            zero = jnp.zeros_like(val)
            tab[0, 0, t_rows_g, :] = jnp.where(low, val, zero).astype(BF16)
            tab[0, 1, t_rows_g, :] = jnp.where(low, zero, swapped).astype(BF16)
            tab[1, 0, t_rows_g, :] = jnp.where(low, swapped, zero).astype(BF16)
            tab[1, 1, t_rows_g, :] = jnp.where(low, zero, val).astype(BF16)
        for piece in range(BRANCH_W // PROJ_COLS):
            c0 = piece * PROJ_COLS
            gate_buf[g_rows, c0:c0 + PROJ_COLS] = _dot(xg, w_ref[:, gate_col0 + c0:gate_col0 + c0 + PROJ_COLS])

    low128 = lax.broadcasted_iota(jnp.int32, (BLOCK, LANES), 1) < HEAD_DIM
    project_group(0, None)
    for g0 in range(0, n_blocks, ATTN_GROUP_BLOCKS):
        units = [(n, j) for n in range(g0, g0 + ATTN_GROUP_BLOCKS) for j in range(N_KV_HEADS)]

        for it, (n, j) in enumerate(units):
            r0 = n * BLOCK
            q_cols = j * PAIRS * LANES
            qs = jnp.concatenate(
                [q_buf[r0:r0 + BLOCK, q_cols + p * LANES:q_cols + (p + 1) * LANES] for p in range(PAIRS)], axis=0)
            k2 = jnp.concatenate([k_tab[j, 0, r0:r0 + 2 * BLOCK, :], k_tab[j, 1, r0:r0 + 2 * BLOCK, :]], axis=0)
            logits = lax.dot_general(qs, k2, (((1,), (1,)), ((), ())), preferred_element_type=F32)
            threshold = jnp.where(jnp.logical_and(i == 0, n == 0), 2, 1)
            for p in range(PAIRS):
                rows = slice(p * BLOCK, (p + 1) * BLOCK)
                row_max = []
                for par in range(2):
                    slabs = []
                    for half in range(2):
                        cols = slice((2 * par + half) * LANES, (2 * par + half + 1) * LANES)
                        s = logits[rows, cols] + bias_ref[j, rows, cols]
                        s = jnp.where(code_buf[rows, cols] >= threshold, s, NEG_INF)
                        s_buf[it, rows, cols] = s
                        slabs.append(s)
                    m = jnp.max(jnp.maximum(slabs[0], slabs[1]), axis=1, keepdims=True)
                    m = jnp.maximum(jnp.broadcast_to(m, (BLOCK, LANES)), sink_ref[j, par, rows, :])
                    m_buf[it, par, rows, :] = m
                    row_max.append(m)
                mc_buf[it, rows, :] = jnp.where(low128, row_max[0], row_max[1])

        if g0 + ATTN_GROUP_BLOCKS < n_blocks:
            project_group(g0 + ATTN_GROUP_BLOCKS, logits[0:SUBLANES, 0:LANES])

        for it, (n, j) in enumerate(units):
            for p in range(PAIRS):
                rows = slice(p * BLOCK, (p + 1) * BLOCK)
                for par in range(2):
                    m = m_buf[it, par, rows, :]
                    for half in range(2):
                        cols = slice((2 * par + half) * LANES, (2 * par + half + 1) * LANES)
                        p_buf[it, rows, cols] = jnp.exp(s_buf[it, rows, cols] - m).astype(BF16)

        for it, (n, j) in enumerate(units):
            r0 = n * BLOCK
            q_cols = j * PAIRS * LANES
            v2 = jnp.concatenate([v_tab[j, 0, r0:r0 + 2 * BLOCK, :], v_tab[j, 1, r0:r0 + 2 * BLOCK, :]], axis=0)
            o = _dot(p_buf[it], jnp.concatenate([v2, ones_buf[...]], axis=1))
            for p in range(PAIRS):
                rows = slice(p * BLOCK, (p + 1) * BLOCK)
                cols = slice(q_cols + p * LANES, q_cols + (p + 1) * LANES)
                den = o[rows, LANES:2 * LANES] + jnp.exp(sinkc_ref[j, rows, :] - mc_buf[it, rows, :])
                y = o[rows, 0:LANES] * (1.0 / den) * _silu(gate_buf[r0:r0 + BLOCK, cols])
                o_ref[r0:r0 + BLOCK, cols] = y.astype(BF16)

    for tab in (k_tab, v_tab):
        for j in range(N_KV_HEADS):
            for half in range(2):
                tab[j, half, 0:BLOCK, :] = tab[j, half, t_rows:t_rows + BLOCK, :]


CONV_COLUMNS = BRANCH_W // LANES
CONV_BLOCKS = BRANCH_W // PROJ_COLS


def _conv_stream_kernel(x_ref, w_ref, cw_ref, cb_ref, lng_ref, lnb_ref, zero_ref, o_ref,
                        xs_buf, glu_buf, gate_buf, shift_buf, y_buf):
    i = pl.program_id(0)
    t_rows = x_ref.shape[0]

    @pl.when(i == 0)
    def _():
        for c in range(CONV_COLUMNS):
            glu_buf[c, 0:CONV_HALO, :] = jnp.zeros((CONV_HALO, LANES), F32)

    xs_buf[...] = x_ref[...].astype(BF16)
    first = CONV_HALO - (CONV_K - 1)
    span = t_rows + CONV_HALO - SUBLANES

    def glu_block(n):
        gate_val = _dot(xs_buf[...], w_ref[CONV_BLOCKS + n])
        glu = _dot(xs_buf[...], w_ref[n]) * _sigmoid(gate_val)
        glu_buf[2 * n, CONV_HALO:CONV_HALO + t_rows, :] = glu[:, 0:LANES]
        glu_buf[2 * n + 1, CONV_HALO:CONV_HALO + t_rows, :] = glu[:, LANES:2 * LANES]
        return gate_val[t_rows - SUBLANES:t_rows, 0:LANES]

    def gate_block(n):
        gate = _dot(xs_buf[...], w_ref[2 * CONV_BLOCKS + n])
        gate_buf[2 * n] = gate[:, 0:LANES]
        gate_buf[2 * n + 1] = gate[:, LANES:2 * LANES]
        return gate[t_rows - SUBLANES:t_rows, 0:LANES]

    def conv_block(n, token, second_trigger):
        for half in range(2):
            c = 2 * n + half
            if half == 1:
                token = token + second_trigger()
            for r in range(1, SUBLANES):
                shift_buf[half, r, 0:span, :] = glu_buf[c, r:r + span, :]
            bias = jnp.broadcast_to(cb_ref[c], (STREAM_ROWS, LANES))
            for r0 in range(0, t_rows, STREAM_ROWS):
                nothing = pltpu.bitcast(pltpu.bitcast(token, jnp.int32) & zero_ref[...], F32)
                acc = bias
                for k in range(CONV_K):
                    a8 = (first + k) // SUBLANES * SUBLANES
                    r = (first + k) % SUBLANES
                    if r == 0:
                        src = glu_buf[c, r0 + a8:r0 + a8 + STREAM_ROWS, :]
                    else:
                        src = shift_buf[half, r, r0 + a8:r0 + a8 + STREAM_ROWS, :]
                    tap = jnp.concatenate([cw_ref[c, k] + nothing] * (STREAM_ROWS // SUBLANES), axis=0)
                    acc = acc + tap * src
                y_buf[c, r0:r0 + STREAM_ROWS, :] = acc

    for n in range(CONV_BLOCKS):
        conv_block(n, glu_block(n), lambda n=n: gate_block(n))

    for r0 in range(0, t_rows, ROW_CHUNK):
        rows = slice(r0, r0 + ROW_CHUNK)
        ys = [y_buf[c, rows, :] for c in range(CONV_COLUMNS)]
        total = ys[0]
        for y in ys[1:]:
            total = total + y
        mu = jnp.sum(total, axis=1, keepdims=True) * (1.0 / BRANCH_W)
        ds = [y - mu for y in ys]
        sq = ds[0] * ds[0]
        for d in ds[1:]:
            sq = sq + d * d
        rstd = lax.rsqrt(jnp.sum(sq, axis=1, keepdims=True) * (1.0 / BRANCH_W) + LN_EPS)
        for c in range(CONV_COLUMNS):
            y = _silu(ds[c] * rstd * lng_ref[c] + lnb_ref[c]) * _silu(gate_buf[c, rows, :])
            o_ref[rows, c * LANES:(c + 1) * LANES] = y.astype(BF16)

    for c in range(CONV_COLUMNS):
        glu_buf[c, 0:CONV_HALO, :] = glu_buf[c, t_rows:t_rows + CONV_HALO, :]


def _merge_kernel(x_ref, ya_ref, yb_ref, yc_ref, yd_ref, wga_ref, wgb_ref, wgc_ref, wgd_ref, wb_ref, o_ref):
    xb = x_ref[...].astype(BF16)
    acc = None
    branches = ((ya_ref, wga_ref), (yb_ref, wgb_ref), (yc_ref, wgc_ref), (yd_ref, wgd_ref))
    for br, (y_ref, wg_ref) in enumerate(branches):
        gate = _sigmoid(_dot(xb, wg_ref[...]))
        term = gate * _dot(y_ref[...], wb_ref[br])
        acc = term if acc is None else acc + term
    o_ref[...] = acc.astype(BF16)


def _out_kernel(m_ref, w_ref, x_ref, g_ref, b_ref, zero_ref, o_ref, ob_ref, z_buf):
    t_rows = x_ref.shape[0]
    half_rows = t_rows // 2
    chunk = ROW_CHUNK // 2
    tokens = []
    for h in range(2):
        rows = slice(h * half_rows, (h + 1) * half_rows)
        z = _dot(m_ref[rows, :], w_ref[...])
        z_buf[rows, :] = z
        tokens.append(z[half_rows - SUBLANES:half_rows, 0:LANES])
    g = g_ref[...]
    b = b_ref[...]
    for h in range(2):
        nothing = pltpu.bitcast(pltpu.bitcast(tokens[h], jnp.int32) & zero_ref[...], F32)
        nothing = jnp.concatenate([nothing] * (chunk // SUBLANES), axis=0)
        nothing = jnp.concatenate([nothing] * (D_MODEL // LANES), axis=1)
        for r0 in range(h * half_rows, (h + 1) * half_rows, chunk):
            rows = slice(r0, r0 + chunk)
            y = _layer_norm_rows(ALPHA * x_ref[rows, :] + (z_buf[rows, :] + nothing), g, b)
            o_ref[rows, :] = y
            ob_ref[rows, :] = y.astype(BF16)


def _resident(shape):
    zeros = (0,) * len(shape)
    return pl.BlockSpec(shape, lambda *_: zeros, pipeline_mode=pl.Buffered(1))


def _nbytes(shape, dtype):
    return math.prod(d for d in shape if d is not None) * jnp.dtype(dtype).itemsize


def _call(kernel_fn, name, grid, inputs, outputs, scratch, value_bytes):
    arrays, in_specs = zip(*inputs)
    out_shape, out_specs = zip(*outputs)
    windows = [(spec, a.dtype) for a, spec in inputs] + [(spec, o.dtype) for o, spec in outputs]
    need = value_bytes + sum(_nbytes(v.shape, v.dtype) for v in scratch)
    for spec, dtype in windows:
        buffers = spec.pipeline_mode.buffer_count if spec.pipeline_mode is not None else 2
        need += buffers * _nbytes(spec.block_shape, dtype)
    assert need <= V7X_VMEM_BYTES, (name, need)
    return pl.pallas_call(
        kernel_fn,
        grid=grid,
        in_specs=list(in_specs),
        out_specs=list(out_specs) if len(out_specs) > 1 else out_specs[0],
        out_shape=list(out_shape) if len(out_shape) > 1 else out_shape[0],
        scratch_shapes=scratch,
        compiler_params=pltpu.CompilerParams(dimension_semantics=("arbitrary",) * len(grid),
                                             vmem_limit_bytes=need),
        name=name,
    )(*arrays)


def _row_tile(width, rows=SEQ_TILE):
    return pl.BlockSpec((rows, width), lambda i: (i, 0))


def _layer_weight(w_all, layer):
    idx = (layer,) + (0,) * (w_all.ndim - 1)
    return pl.BlockSpec((None,) + w_all.shape[1:], lambda *_: idx, pipeline_mode=pl.Buffered(1))


def _branch_call(kernel_fn, name, seq, x_in, w_all, layer, operands, scratch, value_tiles=2):
    value_bytes = value_tiles * _nbytes((SEQ_TILE, BRANCH_W), F32)
    inputs = [(x_in, _row_tile(D_MODEL)), (w_all, _layer_weight(w_all, layer))]
    inputs += [(op, _resident(op.shape)) for op in operands]
    outputs = [(jax.ShapeDtypeStruct((seq, BRANCH_W), BF16), _row_tile(BRANCH_W))]
    return _call(kernel_fn, name, (seq // SEQ_TILE,), inputs, outputs, scratch, value_bytes)


def _layer(x, x_in, band, w_parts, layer, pool_w, pool_scale, sgu_ln_g, sgu_ln_b, sgu_w, sgu_b, sinks,
           conv_w, conv_b, conv_ln_g, conv_ln_b, w_branch16, w_out16, ln_g, ln_b):
    seq = x.shape[0]
    t = SEQ_TILE
    w_pool, w_sgu, w_attn, w_gates, w_conv = w_parts
    row = lambda v: v.reshape(1, -1).astype(F32)

    y_a = _branch_call(
        _pool_kernel, "branch_pool", seq, x_in, w_pool, layer,
        [pool_w.astype(BF16), row(pool_scale)],
        [pltpu.VMEM((t + POOL_HALO, BRANCH_W), F32), pltpu.VMEM((t, BRANCH_W), F32),
         pltpu.VMEM((t, BRANCH_W), BF16)])

    sgu_bias = jnp.repeat(jnp.transpose(sgu_b).astype(F32), SGU_HD, axis=1)
    y_b = _branch_call(
        _sgu_kernel, "branch_sgu", seq, x_in, w_sgu, layer,
        [row(sgu_ln_g), row(sgu_ln_b), sgu_w.astype(F32), sgu_bias],
        [pltpu.VMEM((t, BRANCH_W), F32), pltpu.VMEM((t, BRANCH_W), F32),
         pltpu.VMEM((t, BRANCH_W), F32), pltpu.VMEM((t, BRANCH_W), BF16)])

    sink_hp = sinks.astype(F32).reshape(N_KV_HEADS, PAIRS, 2)
    sink_rows = jnp.repeat(jnp.transpose(sink_hp, (0, 2, 1)), BLOCK, axis=2)
    sink_rows = jnp.broadcast_to(sink_rows[..., None], (N_KV_HEADS, 2, PQ, LANES))
    sink_lanes = jnp.repeat(jnp.repeat(sink_hp, BLOCK, axis=1), HEAD_DIM, axis=2)
    n_units = N_KV_HEADS * ATTN_GROUP_BLOCKS
    y_c = _branch_call(
        _attn_kernel, "branch_attn", seq, x_in, w_attn, layer,
        [band, sink_rows, sink_lanes, jnp.zeros((SUBLANES, LANES), jnp.int32)],
        [pltpu.VMEM((t, D_MODEL), BF16), pltpu.VMEM((t, BRANCH_W), BF16), pltpu.VMEM((t, BRANCH_W), F32),
         pltpu.VMEM((N_KV_HEADS, 2, t + BLOCK, LANES), BF16),
         pltpu.VMEM((N_KV_HEADS, 2, t + BLOCK, LANES), BF16),
         pltpu.VMEM((PQ, 4 * BLOCK), jnp.int32),
         pltpu.VMEM((4 * BLOCK, LANES), BF16),
         pltpu.VMEM((n_units, PQ, 4 * BLOCK), F32),
         pltpu.VMEM((n_units, PQ, 4 * BLOCK), BF16),
         pltpu.VMEM((n_units, 2, PQ, LANES), F32),
         pltpu.VMEM((n_units, PQ, LANES), F32)])

    per_col = lambda v: v.astype(F32).reshape(CONV_COLUMNS, 1, LANES)
    taps = jnp.transpose(conv_w.astype(F32).reshape(CONV_K, CONV_COLUMNS, LANES), (1, 0, 2))
    taps = jnp.broadcast_to(taps[:, :, None, :], (CONV_COLUMNS, CONV_K, SUBLANES, LANES))
    y_d = _branch_call(
        _conv_stream_kernel, "branch_conv", seq, x_in, w_conv, layer,
        [taps, per_col(conv_b), per_col(conv_ln_g), per_col(conv_ln_b), jnp.zeros((SUBLANES, LANES), jnp.int32)],
        [pltpu.VMEM((t, D_MODEL), BF16),
         pltpu.VMEM((CONV_COLUMNS, t + CONV_HALO, LANES), F32),
         pltpu.VMEM((CONV_COLUMNS, t, LANES), F32),
         pltpu.VMEM((2, SUBLANES, t + CONV_HALO - SUBLANES, LANES), F32),
         pltpu.VMEM((CONV_COLUMNS, t, LANES), F32)], value_tiles=4)

    n_cb = D_MODEL // MERGE_COLS
    tm = MERGE_TILE
    x_spec = pl.BlockSpec((tm, D_MODEL), lambda c, i: (i, 0))
    y_spec = pl.BlockSpec((tm, BRANCH_W), lambda c, i: (i, 0))
    wg_specs = [pl.BlockSpec((None, D_MODEL, MERGE_COLS),
                             functools.partial(lambda br, c, i: (layer, 0, br * n_cb + c), br))
                for br in range(N_BRANCH)]
    wb_spec = pl.BlockSpec((None, N_BRANCH, BRANCH_W, MERGE_COLS), lambda c, i: (layer, 0, 0, c))
    merged = _call(
        _merge_kernel, "merge", (n_cb, seq // tm),
        [(x_in, x_spec)] + [(y, y_spec) for y in (y_a, y_b, y_c, y_d)]
        + [(w_gates, spec) for spec in wg_specs] + [(w_branch16, wb_spec)],
        [(jax.ShapeDtypeStruct((seq, D_MODEL), BF16), pl.BlockSpec((tm, MERGE_COLS), lambda c, i: (i, c)))],
        [], value_bytes=4 * _nbytes((tm, MERGE_COLS), F32))

    x_next, x_next_bf = _call(
        _out_kernel, "out_norm", (seq // t,),
        [(merged, _row_tile(D_MODEL)), (w_out16, _layer_weight(w_out16, layer)), (x, _row_tile(D_MODEL)),
         (row(ln_g), _resident((1, D_MODEL))), (row(ln_b), _resident((1, D_MODEL))),
         (jnp.zeros((SUBLANES, LANES), jnp.int32), _resident((SUBLANES, LANES)))],
        [(jax.ShapeDtypeStruct((seq, D_MODEL), F32), _row_tile(D_MODEL)),
         (jax.ShapeDtypeStruct((seq, D_MODEL), BF16), _row_tile(D_MODEL))],
        [pltpu.VMEM((t, D_MODEL), F32)], value_bytes=_nbytes((t, D_MODEL), F32))
    return x_next, x_next_bf


def kernel(x, w_in, pool_w, pool_scale, sgu_ln_g, sgu_ln_b, sgu_w, sgu_b, attn_sinks, rel_bias, conv_w,
           conv_b, conv_ln_g, conv_ln_b, w_branch, w_out, ln_g, ln_b):
    batch, seq, d_model = x.shape
    assert d_model == D_MODEL and w_in.shape == (DEPTH, D_MODEL, D_IN)
    assert seq % SEQ_TILE == 0 and SEQ_TILE % (BLOCK * ATTN_GROUP_BLOCKS) == 0

    band = _band_bias(rel_bias).reshape(N_KV_HEADS, PAIRS, 2, BLOCK, 2 * BLOCK)
    band = jnp.transpose(band, (0, 1, 3, 2, 4)).reshape(N_KV_HEADS, PQ, 4 * BLOCK)

    w_parts = _cast_w_in(w_in.astype(F32))
    w_branch16 = w_branch.astype(BF16)
    w_out16 = w_out.astype(BF16)
    outs = []
    for bi in range(batch):
        xc = x[bi].astype(F32)
        x_in = xc
        for l in range(DEPTH):
            xc, x_in = _layer(xc, x_in, band, w_parts, l, pool_w[l], pool_scale[l], sgu_ln_g[l], sgu_ln_b[l],
                              sgu_w[l], sgu_b[l], attn_sinks[l], conv_w[l], conv_b[l], conv_ln_g[l],
                              conv_ln_b[l], w_branch16, w_out16, ln_g[l], ln_b[l])
        outs.append(xc.astype(x.dtype)[None])
    return outs[0] if batch == 1 else jnp.concatenate(outs, axis=0)
```

```python
import functools
import math

import jax
import jax.numpy as jnp
from jax import lax
from jax.experimental import pallas as pl
from jax.experimental.pallas import tpu as pltpu

F32 = jnp.float32
BF16 = jnp.bfloat16

D_MODEL = 2048
DEPTH = 2
N_BRANCH = 4
BRANCH_W = 1024
POOL_WINDOWS = (2, 4, 8, 16)
POOL_GW = BRANCH_W // len(POOL_WINDOWS)
SGU_HEADS = 8
SGU_HD = BRANCH_W // SGU_HEADS
CHUNK = 128
N_Q_HEADS = 16
N_KV_HEADS = 2
Q_PER_KV = N_Q_HEADS // N_KV_HEADS
HEAD_DIM = 64
KV_W = N_KV_HEADS * HEAD_DIM
WINDOW = 128
BLOCK = 128
NUM_BUCKETS = 32
MAX_DISTANCE = 128
CONV_K = 31
ALPHA = (2 * DEPTH) ** 0.25
LN_EPS = 1e-5
NEG_INF = -1e30

OFF_A = 0
OFF_B = OFF_A + 2 * BRANCH_W
OFF_C = OFF_B + 3 * BRANCH_W
OFF_D = OFF_C + 2 * BRANCH_W + 2 * KV_W
OFF_G = OFF_D + 3 * BRANCH_W
D_IN = OFF_G + N_BRANCH * D_MODEL

LANES = 128
SUBLANES = 8
SEQ_TILE = 512
POOL_HALO = 16
CONV_HALO = 32
ROW_CHUNK = 64
STREAM_ROWS = 512
PROJ_COLS = 256
MERGE_COLS = 512
PAIRS = Q_PER_KV // 2
PQ = PAIRS * BLOCK
ATTN_GROUP_BLOCKS = 2
MERGE_TILE = 512
V7X_VMEM_BYTES = 64 * 1024 * 1024


def _sigmoid(x):
    return 0.5 * jnp.tanh(0.5 * x) + 0.5


def _silu(x):
    return x * _sigmoid(x)


def _layer_norm_rows(v, g, b):
    mu = jnp.mean(v, axis=-1, keepdims=True)
    d = v - mu
    var = jnp.mean(d * d, axis=-1, keepdims=True)
    return d * lax.rsqrt(var + LN_EPS) * g + b


def _dot(a, b):
    return jnp.dot(a, b, preferred_element_type=F32)


def _t5_bucket(n):
    max_exact = NUM_BUCKETS // 2
    nf = jnp.maximum(n, 1).astype(F32)
    large = max_exact + (jnp.log(nf / max_exact) / math.log(MAX_DISTANCE / max_exact)
                         * (NUM_BUCKETS - max_exact)).astype(jnp.int32)
    large = jnp.minimum(large, NUM_BUCKETS - 1)
    return jnp.where(n < max_exact, n, large)


def _band_bucket_ids():
    i = jnp.arange(BLOCK)[:, None]
    j = jnp.arange(2 * BLOCK)[None, :]
    return _t5_bucket(jnp.clip(i + BLOCK - j, 0, WINDOW - 1)).astype(jnp.int32)


def _bias_kernel(bucket_ref, table_ref, o_ref):
    h = pl.program_id(0)
    bucket = bucket_ref[...]
    acc = jnp.zeros(bucket.shape, F32)
    for b in range(NUM_BUCKETS):
        acc = jnp.where(bucket == b, table_ref[b, h], acc)
    o_ref[0] = acc


def _band_bias(rel_bias):
    return pl.pallas_call(
        _bias_kernel,
        grid=(N_Q_HEADS,),
        in_specs=[pl.BlockSpec((BLOCK, 2 * BLOCK), lambda h: (0, 0)),
                  pl.BlockSpec(memory_space=pltpu.SMEM)],
        out_specs=pl.BlockSpec((1, BLOCK, 2 * BLOCK), lambda h: (h, 0, 0)),
        out_shape=jax.ShapeDtypeStruct((N_Q_HEADS, BLOCK, 2 * BLOCK), F32),
        name="band_bias",
    )(_band_bucket_ids(), rel_bias.astype(F32))


W_IN_SPLITS = ((OFF_A, OFF_B), (OFF_B, OFF_C), (OFF_C, OFF_D), (OFF_G, D_IN))
CONV_W_CHUNKS = (OFF_G - OFF_D) // PROJ_COLS
CAST_ROWS = 128


def _cast_w_in_kernel(w_ref, oa_ref, ob_ref, oc_ref, og_ref, od_ref):
    for (a, b), o_ref in zip(W_IN_SPLITS, (oa_ref, ob_ref, oc_ref, og_ref)):
        o_ref[...] = w_ref[:, a:b].astype(BF16)
    for j in range(CONV_W_CHUNKS):
        od_ref[j] = w_ref[:, OFF_D + j * PROJ_COLS:OFF_D + (j + 1) * PROJ_COLS].astype(BF16)


def _cast_w_in(w_in):
    widths = [b - a for a, b in W_IN_SPLITS]
    outputs = [(jax.ShapeDtypeStruct((DEPTH, D_MODEL, w), BF16),
                pl.BlockSpec((None, CAST_ROWS, w), lambda l, r: (l, r, 0))) for w in widths]
    outputs.append((jax.ShapeDtypeStruct((DEPTH, CONV_W_CHUNKS, D_MODEL, PROJ_COLS), BF16),
                    pl.BlockSpec((None, CONV_W_CHUNKS, CAST_ROWS, PROJ_COLS), lambda l, r: (l, 0, r, 0))))
    return _call(_cast_w_in_kernel, "cast_w_in", (DEPTH, D_MODEL // CAST_ROWS),
                 [(w_in, pl.BlockSpec((None, CAST_ROWS, D_IN), lambda l, r: (l, r, 0)))],
                 outputs, [], value_bytes=0)


def _pool_kernel(x_ref, w_ref, pw_ref, sc_ref, o_ref, abuf, gate_buf, mix_buf):
    i = pl.program_id(0)
    t_rows = x_ref.shape[0]

    @pl.when(i == 0)
    def _():
        abuf[0:POOL_HALO, :] = jnp.zeros((POOL_HALO, BRANCH_W), F32)

    xb = x_ref[...].astype(BF16)
    abuf[POOL_HALO:POOL_HALO + t_rows, :] = _dot(xb, w_ref[:, 0:BRANCH_W])
    gate_buf[...] = _dot(xb, w_ref[:, BRANCH_W:2 * BRANCH_W])

    for g, win in enumerate(POOL_WINDOWS):
        c0 = g * POOL_GW
        for r0 in range(0, t_rows, ROW_CHUNK):
            base = POOL_HALO + r0
            cur = abuf[base:base + ROW_CHUNK, c0:c0 + POOL_GW]
            acc = cur
            for s in range(1, win):
                acc = acc + abuf[base - s:base - s + ROW_CHUNK, c0:c0 + POOL_GW]
            t = i * t_rows + r0 + lax.broadcasted_iota(jnp.int32, (ROW_CHUNK, POOL_GW), 0)
            cnt = jnp.minimum(t + 1, win).astype(F32)
            mix_buf[r0:r0 + ROW_CHUNK, c0:c0 + POOL_GW] = (acc / cnt - cur).astype(BF16)

    for g in range(len(POOL_WINDOWS)):
        c0 = g * POOL_GW
        y = _dot(mix_buf[:, c0:c0 + POOL_GW], pw_ref[g])
        y = y * sc_ref[:, c0:c0 + POOL_GW] * _silu(gate_buf[:, c0:c0 + POOL_GW])
        o_ref[:, c0:c0 + POOL_GW] = y.astype(BF16)

    abuf[0:POOL_HALO, :] = abuf[t_rows:t_rows + POOL_HALO, :]


def _sgu_kernel(x_ref, w_ref, lng_ref, lnb_ref, sw_ref, sb_ref, o_ref,
                u_buf, v_buf, gate_buf, vn_buf):
    t_rows = x_ref.shape[0]
    xb = x_ref[...].astype(BF16)
    u_buf[...] = _dot(xb, w_ref[:, 0:BRANCH_W])
    v_buf[...] = _dot(xb, w_ref[:, BRANCH_W:2 * BRANCH_W])
    gate_buf[...] = _dot(xb, w_ref[:, 2 * BRANCH_W:3 * BRANCH_W])

    g = lng_ref[...]
    b = lnb_ref[...]
    for r0 in range(0, t_rows, ROW_CHUNK):
        vn = _layer_norm_rows(v_buf[r0:r0 + ROW_CHUNK, :], g, b)
        vn_buf[r0:r0 + ROW_CHUNK, :] = vn.astype(BF16)

    row = lax.broadcasted_iota(jnp.int32, (CHUNK, CHUNK), 0)
    col = lax.broadcasted_iota(jnp.int32, (CHUNK, CHUNK), 1)
    causal = row >= col
    for h in range(SGU_HEADS):
        c0 = h * SGU_HD
        w_h = jnp.where(causal, sw_ref[h], 0.0).astype(BF16)
        bias_h = sb_ref[:, c0:c0 + SGU_HD]
        for r0 in range(0, t_rows, CHUNK):
            sp = _dot(w_h, vn_buf[r0:r0 + CHUNK, c0:c0 + SGU_HD]) + bias_h
            y = u_buf[r0:r0 + CHUNK, c0:c0 + SGU_HD] * sp * _silu(gate_buf[r0:r0 + CHUNK, c0:c0 + SGU_HD])
            o_ref[r0:r0 + CHUNK, c0:c0 + SGU_HD] = y.astype(BF16)


def _attn_kernel(x_ref, w_ref, bias_ref, sink_ref, sinkc_ref, zero_ref, o_ref,
                 xs_buf, q_buf, gate_buf, k_tab, v_tab, code_buf, ones_buf, s_buf, p_buf, m_buf, mc_buf):
    i = pl.program_id(0)
    t_rows = x_ref.shape[0]
    n_blocks = t_rows // BLOCK

    @pl.when(i == 0)
    def _():
        zero = jnp.zeros((BLOCK, LANES), BF16)
        for j in range(N_KV_HEADS):
            for half in range(2):
                k_tab[j, half, 0:BLOCK, :] = zero
                v_tab[j, half, 0:BLOCK, :] = zero
        row = lax.broadcasted_iota(jnp.int32, (PQ, 2 * 2 * BLOCK), 0)
        col = lax.broadcasted_iota(jnp.int32, (PQ, 2 * 2 * BLOCK), 1)
        key = col & (2 * BLOCK - 1)
        u = key - (row & (BLOCK - 1)) - 1
        in_window = (u >= 0) & (u < WINDOW)
        code_buf[...] = jnp.where(in_window, jnp.where(key >= BLOCK, 2, 1), 0)
        krow = lax.broadcasted_iota(jnp.int32, (4 * BLOCK, LANES), 0)
        klane = lax.broadcasted_iota(jnp.int32, (4 * BLOCK, LANES), 1)
        ones_buf[...] = jnp.where((krow >= 2 * BLOCK) == (klane >= HEAD_DIM), 1.0, 0.0).astype(BF16)

    g_len = ATTN_GROUP_BLOCKS * BLOCK
    xs_buf[...] = x_ref[...].astype(BF16)
    low = lax.broadcasted_iota(jnp.int32, (g_len, LANES), 1) < HEAD_DIM
    gate_col0 = BRANCH_W + 2 * KV_W

    def project_group(g0, after):
        g_rows = slice(g0 * BLOCK, g0 * BLOCK + g_len)
        xg = xs_buf[g_rows, :]
        if after is not None:
            nothing = pltpu.bitcast(after, jnp.int32) & zero_ref[...]
            nothing = jnp.concatenate([nothing] * (g_len // (2 * SUBLANES)), axis=0)
            nothing = jnp.concatenate([nothing] * (D_MODEL // LANES), axis=1)
            xg = pltpu.bitcast(pltpu.bitcast(xg, jnp.int32) | nothing, BF16)
        for piece in range(BRANCH_W // PROJ_COLS):
            c0 = piece * PROJ_COLS
            q = _dot(xg, w_ref[:, c0:c0 + PROJ_COLS]) * (HEAD_DIM ** -0.5)
            q_buf[g_rows, c0:c0 + PROJ_COLS] = q.astype(BF16)
        kv = _dot(xg, w_ref[:, BRANCH_W:BRANCH_W + 2 * KV_W])
        t_rows_g = slice(BLOCK + g0 * BLOCK, BLOCK + g0 * BLOCK + g_len)
        for tab, c0 in ((k_tab, 0), (v_tab, KV_W)):
            val = kv[:, c0:c0 + KV_W]
            swapped = pltpu.roll(val, HEAD_DIM, axis=1)
            zero = jnp.zeros_like(val)
            tab[0, 0, t_rows_g, :] = jnp.where(low, val, zero).astype(BF16)
            tab[0, 1, t_rows_g, :] = jnp.where(low, zero, swapped).astype(BF16)
            tab[1, 0, t_rows_g, :] = jnp.where(low, swapped, zero).astype(BF16)
            tab[1, 1, t_rows_g, :] = jnp.where(low, zero, val).astype(BF16)
        for piece in range(BRANCH_W // PROJ_COLS):
            c0 = piece * PROJ_COLS
            gate_buf[g_rows, c0:c0 + PROJ_COLS] = _dot(xg, w_ref[:, gate_col0 + c0:gate_col0 + c0 + PROJ_COLS])

    low128 = lax.broadcasted_iota(jnp.int32, (BLOCK, LANES), 1) < HEAD_DIM
    project_group(0, None)
    for g0 in range(0, n_blocks, ATTN_GROUP_BLOCKS):
        units = [(n, j) for n in range(g0, g0 + ATTN_GROUP_BLOCKS) for j in range(N_KV_HEADS)]

        for it, (n, j) in enumerate(units):
            r0 = n * BLOCK
            q_cols = j * PAIRS * LANES
            qs = jnp.concatenate(
                [q_buf[r0:r0 + BLOCK, q_cols + p * LANES:q_cols + (p + 1) * LANES] for p in range(PAIRS)], axis=0)
            k2 = jnp.concatenate([k_tab[j, 0, r0:r0 + 2 * BLOCK, :], k_tab[j, 1, r0:r0 + 2 * BLOCK, :]], axis=0)
            logits = lax.dot_general(qs, k2, (((1,), (1,)), ((), ())), preferred_element_type=F32)
            threshold = jnp.where(jnp.logical_and(i == 0, n == 0), 2, 1)
            for p in range(PAIRS):
                rows = slice(p * BLOCK, (p + 1) * BLOCK)
                row_max = []
                for par in range(2):
                    slabs = []
                    for half in range(2):
                        cols = slice((2 * par + half) * LANES, (2 * par + half + 1) * LANES)
                        s = logits[rows, cols] + bias_ref[j, rows, cols]
                        s = jnp.where(code_buf[rows, cols] >= threshold, s, NEG_INF)
                        s_buf[it, rows, cols] = s
                        slabs.append(s)
                    m = jnp.max(jnp.maximum(slabs[0], slabs[1]), axis=1, keepdims=True)
                    m = jnp.maximum(jnp.broadcast_to(m, (BLOCK, LANES)), sink_ref[j, par, rows, :])
                    m_buf[it, par, rows, :] = m
                    row_max.append(m)
                mc_buf[it, rows, :] = jnp.where(low128, row_max[0], row_max[1])

        if g0 + ATTN_GROUP_BLOCKS < n_blocks:
            project_group(g0 + ATTN_GROUP_BLOCKS, logits[0:SUBLANES, 0:LANES])

        for it, (n, j) in enumerate(units):
            for p in range(PAIRS):
                rows = slice(p * BLOCK, (p + 1) * BLOCK)
                for par in range(2):
                    m = m_buf[it, par, rows, :]
                    for half in range(2):
                        cols = slice((2 * par + half) * LANES, (2 * par + half + 1) * LANES)
                        p_buf[it, rows, cols] = jnp.exp(s_buf[it, rows, cols] - m).astype(BF16)

        for it, (n, j) in enumerate(units):
            r0 = n * BLOCK
            q_cols = j * PAIRS * LANES
            v2 = jnp.concatenate([v_tab[j, 0, r0:r0 + 2 * BLOCK, :], v_tab[j, 1, r0:r0 + 2 * BLOCK, :]], axis=0)
            o = _dot(p_buf[it], jnp.concatenate([v2, ones_buf[...]], axis=1))
            for p in range(PAIRS):
                rows = slice(p * BLOCK, (p + 1) * BLOCK)
                cols = slice(q_cols + p * LANES, q_cols + (p + 1) * LANES)
                den = o[rows, LANES:2 * LANES] + jnp.exp(sinkc_ref[j, rows, :] - mc_buf[it, rows, :])
                y = o[rows, 0:LANES] * (1.0 / den) * _silu(gate_buf[r0:r0 + BLOCK, cols])
                o_ref[r0:r0 + BLOCK, cols] = y.astype(BF16)

    for tab in (k_tab, v_tab):
        for j in range(N_KV_HEADS):
            for half in range(2):
                tab[j, half, 0:BLOCK, :] = tab[j, half, t_rows:t_rows + BLOCK, :]


CONV_COLUMNS = BRANCH_W // LANES
CONV_BLOCKS = BRANCH_W // PROJ_COLS


def _conv_stream_kernel(x_ref, w_ref, cw_ref, cb_ref, lng_ref, lnb_ref, zero_ref, o_ref,
                        xs_buf, glu_buf, gate_buf, shift_buf, y_buf):
    i = pl.program_id(0)
    t_rows = x_ref.shape[0]

    @pl.when(i == 0)
    def _():
        for c in range(CONV_COLUMNS):
            glu_buf[c, 0:CONV_HALO, :] = jnp.zeros((CONV_HALO, LANES), F32)

    xs_buf[...] = x_ref[...].astype(BF16)
    first = CONV_HALO - (CONV_K - 1)
    span = t_rows + CONV_HALO - SUBLANES

    def glu_block(n):
        gate_val = _dot(xs_buf[...], w_ref[CONV_BLOCKS + n])
        glu = _dot(xs_buf[...], w_ref[n]) * _sigmoid(gate_val)
        glu_buf[2 * n, CONV_HALO:CONV_HALO + t_rows, :] = glu[:, 0:LANES]
        glu_buf[2 * n + 1, CONV_HALO:CONV_HALO + t_rows, :] = glu[:, LANES:2 * LANES]
        return gate_val[t_rows - SUBLANES:t_rows, 0:LANES]

    def gate_block(n):
        gate = _dot(xs_buf[...], w_ref[2 * CONV_BLOCKS + n])
        gate_buf[2 * n] = gate[:, 0:LANES]
        gate_buf[2 * n + 1] = gate[:, LANES:2 * LANES]
        return gate[t_rows - SUBLANES:t_rows, 0:LANES]

    def conv_block(n, token, second_trigger):
        for half in range(2):
            c = 2 * n + half
            if half == 1:
                token = token + second_trigger()
            for r in range(1, SUBLANES):
                shift_buf[half, r, 0:span, :] = glu_buf[c, r:r + span, :]
            bias = jnp.broadcast_to(cb_ref[c], (STREAM_ROWS, LANES))
            for r0 in range(0, t_rows, STREAM_ROWS):
                nothing = pltpu.bitcast(pltpu.bitcast(token, jnp.int32) & zero_ref[...], F32)
                acc = bias
                for k in range(CONV_K):
                    a8 = (first + k) // SUBLANES * SUBLANES
                    r = (first + k) % SUBLANES
                    if r == 0:
                        src = glu_buf[c, r0 + a8:r0 + a8 + STREAM_ROWS, :]
                    else:
                        src = shift_buf[half, r, r0 + a8:r0 + a8 + STREAM_ROWS, :]
                    tap = jnp.concatenate([cw_ref[c, k] + nothing] * (STREAM_ROWS // SUBLANES), axis=0)
                    acc = acc + tap * src
                y_buf[c, r0:r0 + STREAM_ROWS, :] = acc

    for n in range(CONV_BLOCKS):
        token = glu_block(n)
        if n == 0:
            conv_block(n, token, lambda: token)
        else:
            conv_block(n, token, lambda n=n: gate_block(n - 1))
    gate_block(CONV_BLOCKS - 1)

    for r0 in range(0, t_rows, ROW_CHUNK):
        rows = slice(r0, r0 + ROW_CHUNK)
        ys = [y_buf[c, rows, :] for c in range(CONV_COLUMNS)]
        total = ys[0]
        for y in ys[1:]:
            total = total + y
        mu = jnp.sum(total, axis=1, keepdims=True) * (1.0 / BRANCH_W)
        ds = [y - mu for y in ys]
        sq = ds[0] * ds[0]
        for d in ds[1:]:
            sq = sq + d * d
        rstd = lax.rsqrt(jnp.sum(sq, axis=1, keepdims=True) * (1.0 / BRANCH_W) + LN_EPS)
        for c in range(CONV_COLUMNS):
            y = _silu(ds[c] * rstd * lng_ref[c] + lnb_ref[c]) * _silu(gate_buf[c, rows, :])
            o_ref[rows, c * LANES:(c + 1) * LANES] = y.astype(BF16)

    for c in range(CONV_COLUMNS):
        glu_buf[c, 0:CONV_HALO, :] = glu_buf[c, t_rows:t_rows + CONV_HALO, :]


def _merge_kernel(x_ref, ya_ref, yb_ref, yc_ref, yd_ref, wga_ref, wgb_ref, wgc_ref, wgd_ref, wb_ref, o_ref):
    xb = x_ref[...].astype(BF16)
    acc = None
    branches = ((ya_ref, wga_ref), (yb_ref, wgb_ref), (yc_ref, wgc_ref), (yd_ref, wgd_ref))
    for br, (y_ref, wg_ref) in enumerate(branches):
        gate = _sigmoid(_dot(xb, wg_ref[...]))
        term = gate * _dot(y_ref[...], wb_ref[br])
        acc = term if acc is None else acc + term
    o_ref[...] = acc.astype(BF16)


def _out_kernel(m_ref, w_ref, x_ref, g_ref, b_ref, zero_ref, o_ref, ob_ref, z_buf):
    t_rows = x_ref.shape[0]
    half_rows = t_rows // 2
    chunk = ROW_CHUNK // 2
    tokens = []
    for h in range(2):
        rows = slice(h * half_rows, (h + 1) * half_rows)
        z = _dot(m_ref[rows, :], w_ref[...])
        z_buf[rows, :] = z
        tokens.append(z[half_rows - SUBLANES:half_rows, 0:LANES])
    g = g_ref[...]
    b = b_ref[...]
    for h in range(2):
        nothing = pltpu.bitcast(pltpu.bitcast(tokens[h], jnp.int32) & zero_ref[...], F32)
        nothing = jnp.concatenate([nothing] * (chunk // SUBLANES), axis=0)
        nothing = jnp.concatenate([nothing] * (D_MODEL // LANES), axis=1)
        for r0 in range(h * half_rows, (h + 1) * half_rows, chunk):
            rows = slice(r0, r0 + chunk)
            y = _layer_norm_rows(ALPHA * x_ref[rows, :] + (z_buf[rows, :] + nothing), g, b)
            o_ref[rows, :] = y
            ob_ref[rows, :] = y.astype(BF16)


def _resident(shape):
    zeros = (0,) * len(shape)
    return pl.BlockSpec(shape, lambda *_: zeros, pipeline_mode=pl.Buffered(1))


def _nbytes(shape, dtype):
    return math.prod(d for d in shape if d is not None) * jnp.dtype(dtype).itemsize


def _call(kernel_fn, name, grid, inputs, outputs, scratch, value_bytes):
    arrays, in_specs = zip(*inputs)
    out_shape, out_specs = zip(*outputs)
    windows = [(spec, a.dtype) for a, spec in inputs] + [(spec, o.dtype) for o, spec in outputs]
    need = value_bytes + sum(_nbytes(v.shape, v.dtype) for v in scratch)
    for spec, dtype in windows:
        buffers = spec.pipeline_mode.buffer_count if spec.pipeline_mode is not None else 2
        need += buffers * _nbytes(spec.block_shape, dtype)
    assert need <= V7X_VMEM_BYTES, (name, need)
    return pl.pallas_call(
        kernel_fn,
        grid=grid,
        in_specs=list(in_specs),
        out_specs=list(out_specs) if len(out_specs) > 1 else out_specs[0],
        out_shape=list(out_shape) if len(out_shape) > 1 else out_shape[0],
        scratch_shapes=scratch,
        compiler_params=pltpu.CompilerParams(dimension_semantics=("arbitrary",) * len(grid),
                                             vmem_limit_bytes=need),
        name=name,
    )(*arrays)


def _row_tile(width, rows=SEQ_TILE):
    return pl.BlockSpec((rows, width), lambda i: (i, 0))


def _layer_weight(w_all, layer):
    idx = (layer,) + (0,) * (w_all.ndim - 1)
    return pl.BlockSpec((None,) + w_all.shape[1:], lambda *_: idx, pipeline_mode=pl.Buffered(1))


def _branch_call(kernel_fn, name, seq, x_in, w_all, layer, operands, scratch, value_tiles=2):
    value_bytes = value_tiles * _nbytes((SEQ_TILE, BRANCH_W), F32)
    inputs = [(x_in, _row_tile(D_MODEL)), (w_all, _layer_weight(w_all, layer))]
    inputs += [(op, _resident(op.shape)) for op in operands]
    outputs = [(jax.ShapeDtypeStruct((seq, BRANCH_W), BF16), _row_tile(BRANCH_W))]
    return _call(kernel_fn, name, (seq // SEQ_TILE,), inputs, outputs, scratch, value_bytes)


def _layer(x, x_in, band, w_parts, layer, pool_w, pool_scale, sgu_ln_g, sgu_ln_b, sgu_w, sgu_b, sinks,
           conv_w, conv_b, conv_ln_g, conv_ln_b, w_branch16, w_out16, ln_g, ln_b):
    seq = x.shape[0]
    t = SEQ_TILE
    w_pool, w_sgu, w_attn, w_gates, w_conv = w_parts
    row = lambda v: v.reshape(1, -1).astype(F32)

    y_a = _branch_call(
        _pool_kernel, "branch_pool", seq, x_in, w_pool, layer,
        [pool_w.astype(BF16), row(pool_scale)],
        [pltpu.VMEM((t + POOL_HALO, BRANCH_W), F32), pltpu.VMEM((t, BRANCH_W), F32),
         pltpu.VMEM((t, BRANCH_W), BF16)])

    sgu_bias = jnp.repeat(jnp.transpose(sgu_b).astype(F32), SGU_HD, axis=1)
    y_b = _branch_call(
        _sgu_kernel, "branch_sgu", seq, x_in, w_sgu, layer,
        [row(sgu_ln_g), row(sgu_ln_b), sgu_w.astype(F32), sgu_bias],
        [pltpu.VMEM((t, BRANCH_W), F32), pltpu.VMEM((t, BRANCH_W), F32),
         pltpu.VMEM((t, BRANCH_W), F32), pltpu.VMEM((t, BRANCH_W), BF16)])

    sink_hp = sinks.astype(F32).reshape(N_KV_HEADS, PAIRS, 2)
    sink_rows = jnp.repeat(jnp.transpose(sink_hp, (0, 2, 1)), BLOCK, axis=2)
    sink_rows = jnp.broadcast_to(sink_rows[..., None], (N_KV_HEADS, 2, PQ, LANES))
    sink_lanes = jnp.repeat(jnp.repeat(sink_hp, BLOCK, axis=1), HEAD_DIM, axis=2)
    n_units = N_KV_HEADS * ATTN_GROUP_BLOCKS
    y_c = _branch_call(
        _attn_kernel, "branch_attn", seq, x_in, w_attn, layer,
        [band, sink_rows, sink_lanes, jnp.zeros((SUBLANES, LANES), jnp.int32)],
        [pltpu.VMEM((t, D_MODEL), BF16), pltpu.VMEM((t, BRANCH_W), BF16), pltpu.VMEM((t, BRANCH_W), F32),
         pltpu.VMEM((N_KV_HEADS, 2, t + BLOCK, LANES), BF16),
         pltpu.VMEM((N_KV_HEADS, 2, t + BLOCK, LANES), BF16),
         pltpu.VMEM((PQ, 4 * BLOCK), jnp.int32),
         pltpu.VMEM((4 * BLOCK, LANES), BF16),
         pltpu.VMEM((n_units, PQ, 4 * BLOCK), F32),
         pltpu.VMEM((n_units, PQ, 4 * BLOCK), BF16),
         pltpu.VMEM((n_units, 2, PQ, LANES), F32),
         pltpu.VMEM((n_units, PQ, LANES), F32)])

    per_col = lambda v: v.astype(F32).reshape(CONV_COLUMNS, 1, LANES)
    taps = jnp.transpose(conv_w.astype(F32).reshape(CONV_K, CONV_COLUMNS, LANES), (1, 0, 2))
    taps = jnp.broadcast_to(taps[:, :, None, :], (CONV_COLUMNS, CONV_K, SUBLANES, LANES))
    y_d = _branch_call(
        _conv_stream_kernel, "branch_conv", seq, x_in, w_conv, layer,
        [taps, per_col(conv_b), per_col(conv_ln_g), per_col(conv_ln_b), jnp.zeros((SUBLANES, LANES), jnp.int32)],
        [pltpu.VMEM((t, D_MODEL), BF16),
         pltpu.VMEM((CONV_COLUMNS, t + CONV_HALO, LANES), F32),
         pltpu.VMEM((CONV_COLUMNS, t, LANES), F32),
         pltpu.VMEM((2, SUBLANES, t + CONV_HALO - SUBLANES, LANES), F32),
         pltpu.VMEM((CONV_COLUMNS, t, LANES), F32)], value_tiles=4)

    n_cb = D_MODEL // MERGE_COLS
    tm = MERGE_TILE
    x_spec = pl.BlockSpec((tm, D_MODEL), lambda c, i: (i, 0))
    y_spec = pl.BlockSpec((tm, BRANCH_W), lambda c, i: (i, 0))
    wg_specs = [pl.BlockSpec((None, D_MODEL, MERGE_COLS),
                             functools.partial(lambda br, c, i: (layer, 0, br * n_cb + c), br))
                for br in range(N_BRANCH)]
    wb_spec = pl.BlockSpec((None, N_BRANCH, BRANCH_W, MERGE_COLS), lambda c, i: (layer, 0, 0, c))
    merged = _call(
        _merge_kernel, "merge", (n_cb, seq // tm),
        [(x_in, x_spec)] + [(y, y_spec) for y in (y_a, y_b, y_c, y_d)]
        + [(w_gates, spec) for spec in wg_specs] + [(w_branch16, wb_spec)],
        [(jax.ShapeDtypeStruct((seq, D_MODEL), BF16), pl.BlockSpec((tm, MERGE_COLS), lambda c, i: (i, c)))],
        [], value_bytes=4 * _nbytes((tm, MERGE_COLS), F32))

    x_next, x_next_bf = _call(
        _out_kernel, "out_norm", (seq // t,),
        [(merged, _row_tile(D_MODEL)), (w_out16, _layer_weight(w_out16, layer)), (x, _row_tile(D_MODEL)),
         (row(ln_g), _resident((1, D_MODEL))), (row(ln_b), _resident((1, D_MODEL))),
         (jnp.zeros((SUBLANES, LANES), jnp.int32), _resident((SUBLANES, LANES)))],
        [(jax.ShapeDtypeStruct((seq, D_MODEL), F32), _row_tile(D_MODEL)),
         (jax.ShapeDtypeStruct((seq, D_MODEL), BF16), _row_tile(D_MODEL))],
        [pltpu.VMEM((t, D_MODEL), F32)], value_bytes=_nbytes((t, D_MODEL), F32))
    return x_next, x_next_bf


def kernel(x, w_in, pool_w, pool_scale, sgu_ln_g, sgu_ln_b, sgu_w, sgu_b, attn_sinks, rel_bias, conv_w,
           conv_b, conv_ln_g, conv_ln_b, w_branch, w_out, ln_g, ln_b):
    batch, seq, d_model = x.shape
    assert d_model == D_MODEL and w_in.shape == (DEPTH, D_MODEL, D_IN)
    assert seq % SEQ_TILE == 0 and SEQ_TILE % (BLOCK * ATTN_GROUP_BLOCKS) == 0

    band = _band_bias(rel_bias).reshape(N_KV_HEADS, PAIRS, 2, BLOCK, 2 * BLOCK)
    band = jnp.transpose(band, (0, 1, 3, 2, 4)).reshape(N_KV_HEADS, PQ, 4 * BLOCK)

    w_parts = _cast_w_in(w_in.astype(F32))
    w_branch16 = w_branch.astype(BF16)
    w_out16 = w_out.astype(BF16)
    outs = []
    for bi in range(batch):
        xc = x[bi].astype(F32)
        x_in = xc
        for l in range(DEPTH):
            xc, x_in = _layer(xc, x_in, band, w_parts, l, pool_w[l], pool_scale[l], sgu_ln_g[l], sgu_ln_b[l],
                              sgu_w[l], sgu_b[l], attn_sinks[l], conv_w[l], conv_b[l], conv_ln_g[l],
                              conv_ln_b[l], w_branch16, w_out16, ln_g[l], ln_b[l])
        outs.append(xc.astype(x.dtype)[None])
    return outs[0] if batch == 1 else jnp.concatenate(outs, axis=0)
```

```python
import functools
import math

import jax
import jax.numpy as jnp
from jax import lax
from jax.experimental import pallas as pl
from jax.experimental.pallas import tpu as pltpu

F32 = jnp.float32
BF16 = jnp.bfloat16

D_MODEL = 2048
DEPTH = 2
N_BRANCH = 4
BRANCH_W = 1024
POOL_WINDOWS = (2, 4, 8, 16)
POOL_GW = BRANCH_W // len(POOL_WINDOWS)
SGU_HEADS = 8
SGU_HD = BRANCH_W // SGU_HEADS
CHUNK = 128
N_Q_HEADS = 16
N_KV_HEADS = 2
Q_PER_KV = N_Q_HEADS // N_KV_HEADS
HEAD_DIM = 64
KV_W = N_KV_HEADS * HEAD_DIM
WINDOW = 128
BLOCK = 128
NUM_BUCKETS = 32
MAX_DISTANCE = 128
CONV_K = 31
ALPHA = (2 * DEPTH) ** 0.25
LN_EPS = 1e-5
NEG_INF = -1e30

OFF_A = 0
OFF_B = OFF_A + 2 * BRANCH_W
OFF_C = OFF_B + 3 * BRANCH_W
OFF_D = OFF_C + 2 * BRANCH_W + 2 * KV_W
OFF_G = OFF_D + 3 * BRANCH_W
D_IN = OFF_G + N_BRANCH * D_MODEL

LANES = 128
SUBLANES = 8
SEQ_TILE = 512
POOL_HALO = 16
CONV_HALO = 32
ROW_CHUNK = 64
STREAM_ROWS = 512
PROJ_COLS = 256
MERGE_COLS = 512
PAIRS = Q_PER_KV // 2
PQ = PAIRS * BLOCK
ATTN_GROUP_BLOCKS = 2
MERGE_TILE = 512
V7X_VMEM_BYTES = 64 * 1024 * 1024


def _sigmoid(x):
    return 0.5 * jnp.tanh(0.5 * x) + 0.5


def _silu(x):
    h = 0.5 * x
    return h * jnp.tanh(h) + h


def _layer_norm_rows(v, g, b):
    mu = jnp.mean(v, axis=-1, keepdims=True)
    d = v - mu
    var = jnp.mean(d * d, axis=-1, keepdims=True)
    return d * lax.rsqrt(var + LN_EPS) * g + b


def _dot(a, b):
    return jnp.dot(a, b, preferred_element_type=F32)


def _t5_bucket(n):
    max_exact = NUM_BUCKETS // 2
    nf = jnp.maximum(n, 1).astype(F32)
    large = max_exact + (jnp.log(nf / max_exact) / math.log(MAX_DISTANCE / max_exact)
                         * (NUM_BUCKETS - max_exact)).astype(jnp.int32)
    large = jnp.minimum(large, NUM_BUCKETS - 1)
    return jnp.where(n < max_exact, n, large)


def _band_bucket_ids():
    i = jnp.arange(BLOCK)[:, None]
    j = jnp.arange(2 * BLOCK)[None, :]
    return _t5_bucket(jnp.clip(i + BLOCK - j, 0, WINDOW - 1)).astype(jnp.int32)


def _bias_kernel(bucket_ref, table_ref, o_ref):
    h = pl.program_id(0)
    bucket = bucket_ref[...]
    acc = jnp.zeros(bucket.shape, F32)
    for b in range(NUM_BUCKETS):
        acc = jnp.where(bucket == b, table_ref[b, h], acc)
    o_ref[0] = acc


def _band_bias(rel_bias):
    return pl.pallas_call(
        _bias_kernel,
        grid=(N_Q_HEADS,),
        in_specs=[pl.BlockSpec((BLOCK, 2 * BLOCK), lambda h: (0, 0)),
                  pl.BlockSpec(memory_space=pltpu.SMEM)],
        out_specs=pl.BlockSpec((1, BLOCK, 2 * BLOCK), lambda h: (h, 0, 0)),
        out_shape=jax.ShapeDtypeStruct((N_Q_HEADS, BLOCK, 2 * BLOCK), F32),
        name="band_bias",
    )(_band_bucket_ids(), rel_bias.astype(F32))


W_IN_SPLITS = ((OFF_A, OFF_B), (OFF_B, OFF_C), (OFF_C, OFF_D), (OFF_G, D_IN))
CONV_W_CHUNKS = (OFF_G - OFF_D) // PROJ_COLS
CAST_ROWS = 128


def _cast_w_in_kernel(w_ref, oa_ref, ob_ref, oc_ref, og_ref, od_ref):
    for (a, b), o_ref in zip(W_IN_SPLITS, (oa_ref, ob_ref, oc_ref, og_ref)):
        o_ref[...] = w_ref[:, a:b].astype(BF16)
    for j in range(CONV_W_CHUNKS):
        od_ref[j] = w_ref[:, OFF_D + j * PROJ_COLS:OFF_D + (j + 1) * PROJ_COLS].astype(BF16)


def _cast_w_in(w_in):
    widths = [b - a for a, b in W_IN_SPLITS]
    outputs = [(jax.ShapeDtypeStruct((DEPTH, D_MODEL, w), BF16),
                pl.BlockSpec((None, CAST_ROWS, w), lambda l, r: (l, r, 0))) for w in widths]
    outputs.append((jax.ShapeDtypeStruct((DEPTH, CONV_W_CHUNKS, D_MODEL, PROJ_COLS), BF16),
                    pl.BlockSpec((None, CONV_W_CHUNKS, CAST_ROWS, PROJ_COLS), lambda l, r: (l, 0, r, 0))))
    return _call(_cast_w_in_kernel, "cast_w_in", (DEPTH, D_MODEL // CAST_ROWS),
                 [(w_in, pl.BlockSpec((None, CAST_ROWS, D_IN), lambda l, r: (l, r, 0)))],
                 outputs, [], value_bytes=0)


def _pool_kernel(x_ref, w_ref, pw_ref, sc_ref, o_ref, abuf, gate_buf, mix_buf):
    i = pl.program_id(0)
    t_rows = x_ref.shape[0]

    @pl.when(i == 0)
    def _():
        abuf[0:POOL_HALO, :] = jnp.zeros((POOL_HALO, BRANCH_W), F32)

    xb = x_ref[...].astype(BF16)
    abuf[POOL_HALO:POOL_HALO + t_rows, :] = _dot(xb, w_ref[:, 0:BRANCH_W])
    gate_buf[...] = _dot(xb, w_ref[:, BRANCH_W:2 * BRANCH_W])

    for g, win in enumerate(POOL_WINDOWS):
        c0 = g * POOL_GW
        for r0 in range(0, t_rows, ROW_CHUNK):
            base = POOL_HALO + r0
            cur = abuf[base:base + ROW_CHUNK, c0:c0 + POOL_GW]
            acc = cur
            for s in range(1, win):
                acc = acc + abuf[base - s:base - s + ROW_CHUNK, c0:c0 + POOL_GW]
            t = i * t_rows + r0 + lax.broadcasted_iota(jnp.int32, (ROW_CHUNK, POOL_GW), 0)
            cnt = jnp.minimum(t + 1, win).astype(F32)
            mix_buf[r0:r0 + ROW_CHUNK, c0:c0 + POOL_GW] = (acc / cnt - cur).astype(BF16)

    for g in range(len(POOL_WINDOWS)):
        c0 = g * POOL_GW
        y = _dot(mix_buf[:, c0:c0 + POOL_GW], pw_ref[g])
        y = y * sc_ref[:, c0:c0 + POOL_GW] * _silu(gate_buf[:, c0:c0 + POOL_GW])
        o_ref[:, c0:c0 + POOL_GW] = y.astype(BF16)

    abuf[0:POOL_HALO, :] = abuf[t_rows:t_rows + POOL_HALO, :]


def _sgu_kernel(x_ref, w_ref, lng_ref, lnb_ref, sw_ref, sb_ref, o_ref,
                u_buf, v_buf, gate_buf, vn_buf):
    t_rows = x_ref.shape[0]
    xb = x_ref[...].astype(BF16)
    u_buf[...] = _dot(xb, w_ref[:, 0:BRANCH_W])
    v_buf[...] = _dot(xb, w_ref[:, BRANCH_W:2 * BRANCH_W])
    gate_buf[...] = _dot(xb, w_ref[:, 2 * BRANCH_W:3 * BRANCH_W])

    g = lng_ref[...]
    b = lnb_ref[...]
    for r0 in range(0, t_rows, ROW_CHUNK):
        vn = _layer_norm_rows(v_buf[r0:r0 + ROW_CHUNK, :], g, b)
        vn_buf[r0:r0 + ROW_CHUNK, :] = vn.astype(BF16)

    row = lax.broadcasted_iota(jnp.int32, (CHUNK, CHUNK), 0)
    col = lax.broadcasted_iota(jnp.int32, (CHUNK, CHUNK), 1)
    causal = row >= col
    for h in range(SGU_HEADS):
        c0 = h * SGU_HD
        w_h = jnp.where(causal, sw_ref[h], 0.0).astype(BF16)
        bias_h = sb_ref[:, c0:c0 + SGU_HD]
        for r0 in range(0, t_rows, CHUNK):
            sp = _dot(w_h, vn_buf[r0:r0 + CHUNK, c0:c0 + SGU_HD]) + bias_h
            y = u_buf[r0:r0 + CHUNK, c0:c0 + SGU_HD] * sp * _silu(gate_buf[r0:r0 + CHUNK, c0:c0 + SGU_HD])
            o_ref[r0:r0 + CHUNK, c0:c0 + SGU_HD] = y.astype(BF16)


def _attn_kernel(x_ref, w_ref, bias_ref, sink_ref, sinkc_ref, zero_ref, o_ref,
                 xs_buf, q_buf, gate_buf, k_tab, v_tab, code_buf, ones_buf, s_buf, p_buf, m_buf, mc_buf):
    i = pl.program_id(0)
    t_rows = x_ref.shape[0]
    n_blocks = t_rows // BLOCK

    @pl.when(i == 0)
    def _():
        zero = jnp.zeros((BLOCK, LANES), BF16)
        for j in range(N_KV_HEADS):
            for half in range(2):
                k_tab[j, half, 0:BLOCK, :] = zero
                v_tab[j, half, 0:BLOCK, :] = zero
        row = lax.broadcasted_iota(jnp.int32, (PQ, 2 * 2 * BLOCK), 0)
        col = lax.broadcasted_iota(jnp.int32, (PQ, 2 * 2 * BLOCK), 1)
        key = col & (2 * BLOCK - 1)
        u = key - (row & (BLOCK - 1)) - 1
        in_window = (u >= 0) & (u < WINDOW)
        code_buf[...] = jnp.where(in_window, jnp.where(key >= BLOCK, 2, 1), 0)
        krow = lax.broadcasted_iota(jnp.int32, (4 * BLOCK, LANES), 0)
        klane = lax.broadcasted_iota(jnp.int32, (4 * BLOCK, LANES), 1)
        ones_buf[...] = jnp.where((krow >= 2 * BLOCK) == (klane >= HEAD_DIM), 1.0, 0.0).astype(BF16)

    g_len = ATTN_GROUP_BLOCKS * BLOCK
    xs_buf[...] = x_ref[...].astype(BF16)
    low = lax.broadcasted_iota(jnp.int32, (g_len, LANES), 1) < HEAD_DIM
    gate_col0 = BRANCH_W + 2 * KV_W

    def project_group(g0, after):
        g_rows = slice(g0 * BLOCK, g0 * BLOCK + g_len)
        xg = xs_buf[g_rows, :]
        if after is not None:
            nothing = pltpu.bitcast(after, jnp.int32) & zero_ref[...]
            nothing = jnp.concatenate([nothing] * (g_len // (2 * SUBLANES)), axis=0)
            nothing = jnp.concatenate([nothing] * (D_MODEL // LANES), axis=1)
            xg = pltpu.bitcast(pltpu.bitcast(xg, jnp.int32) | nothing, BF16)
        for piece in range(BRANCH_W // PROJ_COLS):
            c0 = piece * PROJ_COLS
            q = _dot(xg, w_ref[:, c0:c0 + PROJ_COLS]) * (HEAD_DIM ** -0.5)
            q_buf[g_rows, c0:c0 + PROJ_COLS] = q.astype(BF16)
        kv = _dot(xg, w_ref[:, BRANCH_W:BRANCH_W + 2 * KV_W])
        t_rows_g = slice(BLOCK + g0 * BLOCK, BLOCK + g0 * BLOCK + g_len)
        for tab, c0 in ((k_tab, 0), (v_tab, KV_W)):
            val = kv[:, c0:c0 + KV_W]
            swapped = pltpu.roll(val, HEAD_DIM, axis=1)
            zero = jnp.zeros_like(val)
            tab[0, 0, t_rows_g, :] = jnp.where(low, val, zero).astype(BF16)
            tab[0, 1, t_rows_g, :] = jnp.where(low, zero, swapped).astype(BF16)
            tab[1, 0, t_rows_g, :] = jnp.where(low, swapped, zero).astype(BF16)
            tab[1, 1, t_rows_g, :] = jnp.where(low, zero, val).astype(BF16)
        for piece in range(BRANCH_W // PROJ_COLS):
            c0 = piece * PROJ_COLS
            gate_buf[g_rows, c0:c0 + PROJ_COLS] = _dot(xg, w_ref[:, gate_col0 + c0:gate_col0 + c0 + PROJ_COLS])

    low128 = lax.broadcasted_iota(jnp.int32, (BLOCK, LANES), 1) < HEAD_DIM
    project_group(0, None)
    for g0 in range(0, n_blocks, ATTN_GROUP_BLOCKS):
        units = [(n, j) for n in range(g0, g0 + ATTN_GROUP_BLOCKS) for j in range(N_KV_HEADS)]

        for it, (n, j) in enumerate(units):
            r0 = n * BLOCK
            q_cols = j * PAIRS * LANES
            qs = jnp.concatenate(
                [q_buf[r0:r0 + BLOCK, q_cols + p * LANES:q_cols + (p + 1) * LANES] for p in range(PAIRS)], axis=0)
            k2 = jnp.concatenate([k_tab[j, 0, r0:r0 + 2 * BLOCK, :], k_tab[j, 1, r0:r0 + 2 * BLOCK, :]], axis=0)
            logits = lax.dot_general(qs, k2, (((1,), (1,)), ((), ())), preferred_element_type=F32)
            threshold = jnp.where(jnp.logical_and(i == 0, n == 0), 2, 1)
            for p in range(PAIRS):
                rows = slice(p * BLOCK, (p + 1) * BLOCK)
                row_max = []
                for par in range(2):
                    slabs = []
                    for half in range(2):
                        cols = slice((2 * par + half) * LANES, (2 * par + half + 1) * LANES)
                        s = logits[rows, cols] + bias_ref[j, rows, cols]
                        s = jnp.where(code_buf[rows, cols] >= threshold, s, NEG_INF)
                        s_buf[it, rows, cols] = s
                        slabs.append(s)
                    m = jnp.max(jnp.maximum(slabs[0], slabs[1]), axis=1, keepdims=True)
                    m = jnp.maximum(jnp.broadcast_to(m, (BLOCK, LANES)), sink_ref[j, par, rows, :])
                    m_buf[it, par, rows, :] = m
                    row_max.append(m)
                mc_buf[it, rows, :] = jnp.where(low128, row_max[0], row_max[1])

        if g0 + ATTN_GROUP_BLOCKS < n_blocks:
            project_group(g0 + ATTN_GROUP_BLOCKS, logits[0:SUBLANES, 0:LANES])

        for it, (n, j) in enumerate(units):
            for p in range(PAIRS):
                rows = slice(p * BLOCK, (p + 1) * BLOCK)
                for par in range(2):
                    m = m_buf[it, par, rows, :]
                    for half in range(2):
                        cols = slice((2 * par + half) * LANES, (2 * par + half + 1) * LANES)
                        p_buf[it, rows, cols] = jnp.exp(s_buf[it, rows, cols] - m).astype(BF16)

        for it, (n, j) in enumerate(units):
            r0 = n * BLOCK
            q_cols = j * PAIRS * LANES
            v2 = jnp.concatenate([v_tab[j, 0, r0:r0 + 2 * BLOCK, :], v_tab[j, 1, r0:r0 + 2 * BLOCK, :]], axis=0)
            o = _dot(p_buf[it], jnp.concatenate([v2, ones_buf[...]], axis=1))
            for p in range(PAIRS):
                rows = slice(p * BLOCK, (p + 1) * BLOCK)
                cols = slice(q_cols + p * LANES, q_cols + (p + 1) * LANES)
                den = o[rows, LANES:2 * LANES] + jnp.exp(sinkc_ref[j, rows, :] - mc_buf[it, rows, :])
                y = o[rows, 0:LANES] * (1.0 / den) * _silu(gate_buf[r0:r0 + BLOCK, cols])
                o_ref[r0:r0 + BLOCK, cols] = y.astype(BF16)

    for tab in (k_tab, v_tab):
        for j in range(N_KV_HEADS):
            for half in range(2):
                tab[j, half, 0:BLOCK, :] = tab[j, half, t_rows:t_rows + BLOCK, :]


CONV_COLUMNS = BRANCH_W // LANES
CONV_BLOCKS = BRANCH_W // PROJ_COLS


def _conv_stream_kernel(x_ref, w_ref, cw_ref, cb_ref, lng_ref, lnb_ref, zero_ref, o_ref,
                        xs_buf, glu_buf, gate_buf, shift_buf, y_buf):
    i = pl.program_id(0)
    t_rows = x_ref.shape[0]

    @pl.when(i == 0)
    def _():
        for c in range(CONV_COLUMNS):
            glu_buf[c, 0:CONV_HALO, :] = jnp.zeros((CONV_HALO, LANES), F32)

    xs_buf[...] = x_ref[...].astype(BF16)
    first = CONV_HALO - (CONV_K - 1)
    span = t_rows + CONV_HALO - SUBLANES

    def glu_block(n):
        gate_val = _dot(xs_buf[...], w_ref[CONV_BLOCKS + n])
        glu = _dot(xs_buf[...], w_ref[n]) * _sigmoid(gate_val)
        glu_buf[2 * n, CONV_HALO:CONV_HALO + t_rows, :] = glu[:, 0:LANES]
        glu_buf[2 * n + 1, CONV_HALO:CONV_HALO + t_rows, :] = glu[:, LANES:2 * LANES]
        return gate_val[t_rows - SUBLANES:t_rows, 0:LANES]

    def gate_block(n):
        gate = _dot(xs_buf[...], w_ref[2 * CONV_BLOCKS + n])
        gate_buf[2 * n] = gate[:, 0:LANES]
        gate_buf[2 * n + 1] = gate[:, LANES:2 * LANES]
        return gate[t_rows - SUBLANES:t_rows, 0:LANES]

    def conv_block(n, token, second_trigger):
        for half in range(2):
            c = 2 * n + half
            if half == 1:
                token = token + second_trigger()
            for r in range(1, SUBLANES):
                shift_buf[half, r, 0:span, :] = glu_buf[c, r:r + span, :]
            bias = jnp.broadcast_to(cb_ref[c], (STREAM_ROWS, LANES))
            for r0 in range(0, t_rows, STREAM_ROWS):
                nothing = pltpu.bitcast(pltpu.bitcast(token, jnp.int32) & zero_ref[...], F32)
                acc = bias
                for k in range(CONV_K):
                    a8 = (first + k) // SUBLANES * SUBLANES
                    r = (first + k) % SUBLANES
                    if r == 0:
                        src = glu_buf[c, r0 + a8:r0 + a8 + STREAM_ROWS, :]
                    else:
                        src = shift_buf[half, r, r0 + a8:r0 + a8 + STREAM_ROWS, :]
                    tap = jnp.concatenate([cw_ref[c, k] + nothing] * (STREAM_ROWS // SUBLANES), axis=0)
                    acc = acc + tap * src
                y_buf[c, r0:r0 + STREAM_ROWS, :] = acc

    for n in range(CONV_BLOCKS):
        token = glu_block(n)
        if n == 0:
            conv_block(n, token, lambda: token)
        else:
            conv_block(n, token, lambda n=n: gate_block(n - 1))
    gate_block(CONV_BLOCKS - 1)

    for r0 in range(0, t_rows, ROW_CHUNK):
        rows = slice(r0, r0 + ROW_CHUNK)
        ys = [y_buf[c, rows, :] for c in range(CONV_COLUMNS)]
        total = ys[0]
        for y in ys[1:]:
            total = total + y
        mu = jnp.sum(total, axis=1, keepdims=True) * (1.0 / BRANCH_W)
        ds = [y - mu for y in ys]
        sq = ds[0] * ds[0]
        for d in ds[1:]:
            sq = sq + d * d
        rstd = lax.rsqrt(jnp.sum(sq, axis=1, keepdims=True) * (1.0 / BRANCH_W) + LN_EPS)
        for c in range(CONV_COLUMNS):
            y = _silu(ds[c] * rstd * lng_ref[c] + lnb_ref[c]) * _silu(gate_buf[c, rows, :])
            o_ref[rows, c * LANES:(c + 1) * LANES] = y.astype(BF16)

    for c in range(CONV_COLUMNS):
        glu_buf[c, 0:CONV_HALO, :] = glu_buf[c, t_rows:t_rows + CONV_HALO, :]


def _merge_kernel(x_ref, ya_ref, yb_ref, yc_ref, yd_ref, wga_ref, wgb_ref, wgc_ref, wgd_ref, wb_ref, o_ref):
    xb = x_ref[...].astype(BF16)
    acc = None
    branches = ((ya_ref, wga_ref), (yb_ref, wgb_ref), (yc_ref, wgc_ref), (yd_ref, wgd_ref))
    for br, (y_ref, wg_ref) in enumerate(branches):
        gate = _sigmoid(_dot(xb, wg_ref[...]))
        term = gate * _dot(y_ref[...], wb_ref[br])
        acc = term if acc is None else acc + term
    o_ref[...] = acc.astype(BF16)


def _out_kernel(m_ref, w_ref, x_ref, g_ref, b_ref, zero_ref, o_ref, ob_ref, z_buf):
    t_rows = x_ref.shape[0]
    half_rows = t_rows // 2
    chunk = ROW_CHUNK // 2
    tokens = []
    for h in range(2):
        rows = slice(h * half_rows, (h + 1) * half_rows)
        z = _dot(m_ref[rows, :], w_ref[...])
        z_buf[rows, :] = z
        tokens.append(z[half_rows - SUBLANES:half_rows, 0:LANES])
    g = g_ref[...]
    b = b_ref[...]
    for h in range(2):
        nothing = pltpu.bitcast(pltpu.bitcast(tokens[h], jnp.int32) & zero_ref[...], F32)
        nothing = jnp.concatenate([nothing] * (chunk // SUBLANES), axis=0)
        nothing = jnp.concatenate([nothing] * (D_MODEL // LANES), axis=1)
        for r0 in range(h * half_rows, (h + 1) * half_rows, chunk):
            rows = slice(r0, r0 + chunk)
            y = _layer_norm_rows(ALPHA * x_ref[rows, :] + (z_buf[rows, :] + nothing), g, b)
            o_ref[rows, :] = y
            ob_ref[rows, :] = y.astype(BF16)


def _resident(shape):
    zeros = (0,) * len(shape)
    return pl.BlockSpec(shape, lambda *_: zeros, pipeline_mode=pl.Buffered(1))


def _nbytes(shape, dtype):
    return math.prod(d for d in shape if d is not None) * jnp.dtype(dtype).itemsize


def _call(kernel_fn, name, grid, inputs, outputs, scratch, value_bytes):
    arrays, in_specs = zip(*inputs)
    out_shape, out_specs = zip(*outputs)
    windows = [(spec, a.dtype) for a, spec in inputs] + [(spec, o.dtype) for o, spec in outputs]
    need = value_bytes + sum(_nbytes(v.shape, v.dtype) for v in scratch)
    for spec, dtype in windows:
        buffers = spec.pipeline_mode.buffer_count if spec.pipeline_mode is not None else 2
        need += buffers * _nbytes(spec.block_shape, dtype)
    assert need <= V7X_VMEM_BYTES, (name, need)
    return pl.pallas_call(
        kernel_fn,
        grid=grid,
        in_specs=list(in_specs),
        out_specs=list(out_specs) if len(out_specs) > 1 else out_specs[0],
        out_shape=list(out_shape) if len(out_shape) > 1 else out_shape[0],
        scratch_shapes=scratch,
        compiler_params=pltpu.CompilerParams(dimension_semantics=("arbitrary",) * len(grid),
                                             vmem_limit_bytes=need),
        name=name,
    )(*arrays)


def _row_tile(width, rows=SEQ_TILE):
    return pl.BlockSpec((rows, width), lambda i: (i, 0))


def _layer_weight(w_all, layer):
    idx = (layer,) + (0,) * (w_all.ndim - 1)
    return pl.BlockSpec((None,) + w_all.shape[1:], lambda *_: idx, pipeline_mode=pl.Buffered(1))


def _branch_call(kernel_fn, name, seq, x_in, w_all, layer, operands, scratch, value_tiles=2):
    value_bytes = value_tiles * _nbytes((SEQ_TILE, BRANCH_W), F32)
    inputs = [(x_in, _row_tile(D_MODEL)), (w_all, _layer_weight(w_all, layer))]
    inputs += [(op, _resident(op.shape)) for op in operands]
    outputs = [(jax.ShapeDtypeStruct((seq, BRANCH_W), BF16), _row_tile(BRANCH_W))]
    return _call(kernel_fn, name, (seq // SEQ_TILE,), inputs, outputs, scratch, value_bytes)


def _layer(x, x_in, band, w_parts, layer, pool_w, pool_scale, sgu_ln_g, sgu_ln_b, sgu_w, sgu_b, sinks,
           conv_w, conv_b, conv_ln_g, conv_ln_b, w_branch16, w_out16, ln_g, ln_b):
    seq = x.shape[0]
    t = SEQ_TILE
    w_pool, w_sgu, w_attn, w_gates, w_conv = w_parts
    row = lambda v: v.reshape(1, -1).astype(F32)

    y_a = _branch_call(
        _pool_kernel, "branch_pool", seq, x_in, w_pool, layer,
        [pool_w.astype(BF16), row(pool_scale)],
        [pltpu.VMEM((t + POOL_HALO, BRANCH_W), F32), pltpu.VMEM((t, BRANCH_W), F32),
         pltpu.VMEM((t, BRANCH_W), BF16)])

    sgu_bias = jnp.repeat(jnp.transpose(sgu_b).astype(F32), SGU_HD, axis=1)
    y_b = _branch_call(
        _sgu_kernel, "branch_sgu", seq, x_in, w_sgu, layer,
        [row(sgu_ln_g), row(sgu_ln_b), sgu_w.astype(F32), sgu_bias],
        [pltpu.VMEM((t, BRANCH_W), F32), pltpu.VMEM((t, BRANCH_W), F32),
         pltpu.VMEM((t, BRANCH_W), F32), pltpu.VMEM((t, BRANCH_W), BF16)])

    sink_hp = sinks.astype(F32).reshape(N_KV_HEADS, PAIRS, 2)
    sink_rows = jnp.repeat(jnp.transpose(sink_hp, (0, 2, 1)), BLOCK, axis=2)
    sink_rows = jnp.broadcast_to(sink_rows[..., None], (N_KV_HEADS, 2, PQ, LANES))
    sink_lanes = jnp.repeat(jnp.repeat(sink_hp, BLOCK, axis=1), HEAD_DIM, axis=2)
    n_units = N_KV_HEADS * ATTN_GROUP_BLOCKS
    y_c = _branch_call(
        _attn_kernel, "branch_attn", seq, x_in, w_attn, layer,
        [band, sink_rows, sink_lanes, jnp.zeros((SUBLANES, LANES), jnp.int32)],
        [pltpu.VMEM((t, D_MODEL), BF16), pltpu.VMEM((t, BRANCH_W), BF16), pltpu.VMEM((t, BRANCH_W), F32),
         pltpu.VMEM((N_KV_HEADS, 2, t + BLOCK, LANES), BF16),
         pltpu.VMEM((N_KV_HEADS, 2, t + BLOCK, LANES), BF16),
         pltpu.VMEM((PQ, 4 * BLOCK), jnp.int32),
         pltpu.VMEM((4 * BLOCK, LANES), BF16),
         pltpu.VMEM((n_units, PQ, 4 * BLOCK), F32),
         pltpu.VMEM((n_units, PQ, 4 * BLOCK), BF16),
         pltpu.VMEM((n_units, 2, PQ, LANES), F32),
         pltpu.VMEM((n_units, PQ, LANES), F32)])

    per_col = lambda v: v.astype(F32).reshape(CONV_COLUMNS, 1, LANES)
    taps = jnp.transpose(conv_w.astype(F32).reshape(CONV_K, CONV_COLUMNS, LANES), (1, 0, 2))
    taps = jnp.broadcast_to(taps[:, :, None, :], (CONV_COLUMNS, CONV_K, SUBLANES, LANES))
    y_d = _branch_call(
        _conv_stream_kernel, "branch_conv", seq, x_in, w_conv, layer,
        [taps, per_col(conv_b), per_col(conv_ln_g), per_col(conv_ln_b), jnp.zeros((SUBLANES, LANES), jnp.int32)],
        [pltpu.VMEM((t, D_MODEL), BF16),
         pltpu.VMEM((CONV_COLUMNS, t + CONV_HALO, LANES), F32),
         pltpu.VMEM((CONV_COLUMNS, t, LANES), F32),
         pltpu.VMEM((2, SUBLANES, t + CONV_HALO - SUBLANES, LANES), F32),
         pltpu.VMEM((CONV_COLUMNS, t, LANES), F32)], value_tiles=4)

    n_cb = D_MODEL // MERGE_COLS
    tm = MERGE_TILE
    x_spec = pl.BlockSpec((tm, D_MODEL), lambda c, i: (i, 0))
    y_spec = pl.BlockSpec((tm, BRANCH_W), lambda c, i: (i, 0))
    wg_specs = [pl.BlockSpec((None, D_MODEL, MERGE_COLS),
                             functools.partial(lambda br, c, i: (layer, 0, br * n_cb + c), br))
                for br in range(N_BRANCH)]
    wb_spec = pl.BlockSpec((None, N_BRANCH, BRANCH_W, MERGE_COLS), lambda c, i: (layer, 0, 0, c))
    merged = _call(
        _merge_kernel, "merge", (n_cb, seq // tm),
        [(x_in, x_spec)] + [(y, y_spec) for y in (y_a, y_b, y_c, y_d)]
        + [(w_gates, spec) for spec in wg_specs] + [(w_branch16, wb_spec)],
        [(jax.ShapeDtypeStruct((seq, D_MODEL), BF16), pl.BlockSpec((tm, MERGE_COLS), lambda c, i: (i, c)))],
        [], value_bytes=4 * _nbytes((tm, MERGE_COLS), F32))

    x_next, x_next_bf = _call(
        _out_kernel, "out_norm", (seq // t,),
        [(merged, _row_tile(D_MODEL)), (w_out16, _layer_weight(w_out16, layer)), (x, _row_tile(D_MODEL)),
         (row(ln_g), _resident((1, D_MODEL))), (row(ln_b), _resident((1, D_MODEL))),
         (jnp.zeros((SUBLANES, LANES), jnp.int32), _resident((SUBLANES, LANES)))],
        [(jax.ShapeDtypeStruct((seq, D_MODEL), F32), _row_tile(D_MODEL)),
         (jax.ShapeDtypeStruct((seq, D_MODEL), BF16), _row_tile(D_MODEL))],
        [pltpu.VMEM((t, D_MODEL), F32)], value_bytes=_nbytes((t, D_MODEL), F32))
    return x_next, x_next_bf


def kernel(x, w_in, pool_w, pool_scale, sgu_ln_g, sgu_ln_b, sgu_w, sgu_b, attn_sinks, rel_bias, conv_w,
           conv_b, conv_ln_g, conv_ln_b, w_branch, w_out, ln_g, ln_b):
    batch, seq, d_model = x.shape
    assert d_model == D_MODEL and w_in.shape == (DEPTH, D_MODEL, D_IN)
    assert seq % SEQ_TILE == 0 and SEQ_TILE % (BLOCK * ATTN_GROUP_BLOCKS) == 0

    band = _band_bias(rel_bias).reshape(N_KV_HEADS, PAIRS, 2, BLOCK, 2 * BLOCK)
    band = jnp.transpose(band, (0, 1, 3, 2, 4)).reshape(N_KV_HEADS, PQ, 4 * BLOCK)

    w_parts = _cast_w_in(w_in.astype(F32))
    w_branch16 = w_branch.astype(BF16)
    w_out16 = w_out.astype(BF16)
    outs = []
    for bi in range(batch):
        xc = x[bi].astype(F32)
        x_in = xc
        for l in range(DEPTH):
            xc, x_in = _layer(xc, x_in, band, w_parts, l, pool_w[l], pool_scale[l], sgu_ln_g[l], sgu_ln_b[l],
                              sgu_w[l], sgu_b[l], attn_sinks[l], conv_w[l], conv_b[l], conv_ln_g[l],
                              conv_ln_b[l], w_branch16, w_out16, ln_g[l], ln_b[l])
        outs.append(xc.astype(x.dtype)[None])
    return outs[0] if batch == 1 else jnp.concatenate(outs, axis=0)
```

```python
import functools
import math

import jax
import jax.numpy as jnp
from jax import lax
from jax.experimental import pallas as pl
from jax.experimental.pallas import tpu as pltpu

F32 = jnp.float32
BF16 = jnp.bfloat16

D_MODEL = 2048
DEPTH = 2
N_BRANCH = 4
BRANCH_W = 1024
POOL_WINDOWS = (2, 4, 8, 16)
POOL_GW = BRANCH_W // len(POOL_WINDOWS)
SGU_HEADS = 8
SGU_HD = BRANCH_W // SGU_HEADS
CHUNK = 128
N_Q_HEADS = 16
N_KV_HEADS = 2
Q_PER_KV = N_Q_HEADS // N_KV_HEADS
HEAD_DIM = 64
KV_W = N_KV_HEADS * HEAD_DIM
WINDOW = 128
BLOCK = 128
NUM_BUCKETS = 32
MAX_DISTANCE = 128
CONV_K = 31
ALPHA = (2 * DEPTH) ** 0.25
LN_EPS = 1e-5
NEG_INF = -1e30

OFF_A = 0
OFF_B = OFF_A + 2 * BRANCH_W
OFF_C = OFF_B + 3 * BRANCH_W
OFF_D = OFF_C + 2 * BRANCH_W + 2 * KV_W
OFF_G = OFF_D + 3 * BRANCH_W
D_IN = OFF_G + N_BRANCH * D_MODEL

LANES = 128
SUBLANES = 8
SEQ_TILE = 512
POOL_HALO = 16
CONV_HALO = 32
ROW_CHUNK = 64
STREAM_ROWS = 512
PROJ_COLS = 256
MERGE_COLS = 512
PAIRS = Q_PER_KV // 2
PQ = PAIRS * BLOCK
ATTN_GROUP_BLOCKS = 2
MERGE_TILE = 512
V7X_VMEM_BYTES = 64 * 1024 * 1024


def _sigmoid(x):
    return 0.5 * jnp.tanh(0.5 * x) + 0.5


def _silu(x):
    return x * _sigmoid(x)


def _layer_norm_rows(v, g, b):
    mu = jnp.mean(v, axis=-1, keepdims=True)
    d = v - mu
    var = jnp.mean(d * d, axis=-1, keepdims=True)
    return d * lax.rsqrt(var + LN_EPS) * g + b


def _dot(a, b):
    return jnp.dot(a, b, preferred_element_type=F32)


def _t5_bucket(n):
    max_exact = NUM_BUCKETS // 2
    nf = jnp.maximum(n, 1).astype(F32)
    large = max_exact + (jnp.log(nf / max_exact) / math.log(MAX_DISTANCE / max_exact)
                         * (NUM_BUCKETS - max_exact)).astype(jnp.int32)
    large = jnp.minimum(large, NUM_BUCKETS - 1)
    return jnp.where(n < max_exact, n, large)


def _band_bucket_ids():
    i = jnp.arange(BLOCK)[:, None]
    j = jnp.arange(2 * BLOCK)[None, :]
    return _t5_bucket(jnp.clip(i + BLOCK - j, 0, WINDOW - 1)).astype(jnp.int32)


def _bias_kernel(bucket_ref, table_ref, o_ref):
    h = pl.program_id(0)
    bucket = bucket_ref[...]
    acc = jnp.zeros(bucket.shape, F32)
    for b in range(NUM_BUCKETS):
        acc = jnp.where(bucket == b, table_ref[b, h], acc)
    o_ref[0] = acc


def _band_bias(rel_bias):
    return pl.pallas_call(
        _bias_kernel,
        grid=(N_Q_HEADS,),
        in_specs=[pl.BlockSpec((BLOCK, 2 * BLOCK), lambda h: (0, 0)),
                  pl.BlockSpec(memory_space=pltpu.SMEM)],
        out_specs=pl.BlockSpec((1, BLOCK, 2 * BLOCK), lambda h: (h, 0, 0)),
        out_shape=jax.ShapeDtypeStruct((N_Q_HEADS, BLOCK, 2 * BLOCK), F32),
        name="band_bias",
    )(_band_bucket_ids(), rel_bias.astype(F32))


W_IN_SPLITS = ((OFF_A, OFF_B), (OFF_B, OFF_C), (OFF_C, OFF_D), (OFF_G, D_IN))
CONV_W_CHUNKS = (OFF_G - OFF_D) // PROJ_COLS
CAST_ROWS = 128


def _cast_w_in_kernel(w_ref, oa_ref, ob_ref, oc_ref, og_ref, od_ref):
    for (a, b), o_ref in zip(W_IN_SPLITS, (oa_ref, ob_ref, oc_ref, og_ref)):
        o_ref[...] = w_ref[:, a:b].astype(BF16)
    for j in range(CONV_W_CHUNKS):
        od_ref[j] = w_ref[:, OFF_D + j * PROJ_COLS:OFF_D + (j + 1) * PROJ_COLS].astype(BF16)


def _cast_w_in(w_in):
    widths = [b - a for a, b in W_IN_SPLITS]
    outputs = [(jax.ShapeDtypeStruct((DEPTH, D_MODEL, w), BF16),
                pl.BlockSpec((None, CAST_ROWS, w), lambda l, r: (l, r, 0))) for w in widths]
    outputs.append((jax.ShapeDtypeStruct((DEPTH, CONV_W_CHUNKS, D_MODEL, PROJ_COLS), BF16),
                    pl.BlockSpec((None, CONV_W_CHUNKS, CAST_ROWS, PROJ_COLS), lambda l, r: (l, 0, r, 0))))
    return _call(_cast_w_in_kernel, "cast_w_in", (DEPTH, D_MODEL // CAST_ROWS),
                 [(w_in, pl.BlockSpec((None, CAST_ROWS, D_IN), lambda l, r: (l, r, 0)))],
                 outputs, [], value_bytes=0)


def _pool_kernel(x_ref, w_ref, pw_ref, sc_ref, o_ref, abuf, gate_buf, mix_buf):
    i = pl.program_id(0)
    t_rows = x_ref.shape[0]

    @pl.when(i == 0)
    def _():
        abuf[0:POOL_HALO, :] = jnp.zeros((POOL_HALO, BRANCH_W), F32)

    xb = x_ref[...].astype(BF16)
    abuf[POOL_HALO:POOL_HALO + t_rows, :] = _dot(xb, w_ref[:, 0:BRANCH_W])
    gate_buf[...] = _dot(xb, w_ref[:, BRANCH_W:2 * BRANCH_W])

    for g, win in enumerate(POOL_WINDOWS):
        c0 = g * POOL_GW
        for r0 in range(0, t_rows, ROW_CHUNK):
            base = POOL_HALO + r0
            cur = abuf[base:base + ROW_CHUNK, c0:c0 + POOL_GW]
            acc = cur
            for s in range(1, win):
                acc = acc + abuf[base - s:base - s + ROW_CHUNK, c0:c0 + POOL_GW]
            t = i * t_rows + r0 + lax.broadcasted_iota(jnp.int32, (ROW_CHUNK, POOL_GW), 0)
            cnt = jnp.minimum(t + 1, win).astype(F32)
            mix_buf[r0:r0 + ROW_CHUNK, c0:c0 + POOL_GW] = (acc / cnt - cur).astype(BF16)

    for g in range(len(POOL_WINDOWS)):
        c0 = g * POOL_GW
        y = _dot(mix_buf[:, c0:c0 + POOL_GW], pw_ref[g])
        y = y * sc_ref[:, c0:c0 + POOL_GW] * _silu(gate_buf[:, c0:c0 + POOL_GW])
        o_ref[:, c0:c0 + POOL_GW] = y.astype(BF16)

    abuf[0:POOL_HALO, :] = abuf[t_rows:t_rows + POOL_HALO, :]


def _sgu_kernel(x_ref, w_ref, lng_ref, lnb_ref, sw_ref, sb_ref, o_ref,
                u_buf, v_buf, gate_buf, vn_buf):
    t_rows = x_ref.shape[0]
    xb = x_ref[...].astype(BF16)
    u_buf[...] = _dot(xb, w_ref[:, 0:BRANCH_W])
    v_buf[...] = _dot(xb, w_ref[:, BRANCH_W:2 * BRANCH_W])
    gate_buf[...] = _dot(xb, w_ref[:, 2 * BRANCH_W:3 * BRANCH_W])

    g = lng_ref[...]
    b = lnb_ref[...]
    for r0 in range(0, t_rows, ROW_CHUNK):
        vn = _layer_norm_rows(v_buf[r0:r0 + ROW_CHUNK, :], g, b)
        vn_buf[r0:r0 + ROW_CHUNK, :] = vn.astype(BF16)

    row = lax.broadcasted_iota(jnp.int32, (CHUNK, CHUNK), 0)
    col = lax.broadcasted_iota(jnp.int32, (CHUNK, CHUNK), 1)
    causal = row >= col
    for h in range(SGU_HEADS):
        c0 = h * SGU_HD
        w_h = jnp.where(causal, sw_ref[h], 0.0).astype(BF16)
        bias_h = sb_ref[:, c0:c0 + SGU_HD]
        for r0 in range(0, t_rows, CHUNK):
            sp = _dot(w_h, vn_buf[r0:r0 + CHUNK, c0:c0 + SGU_HD]) + bias_h
            y = u_buf[r0:r0 + CHUNK, c0:c0 + SGU_HD] * sp * _silu(gate_buf[r0:r0 + CHUNK, c0:c0 + SGU_HD])
            o_ref[r0:r0 + CHUNK, c0:c0 + SGU_HD] = y.astype(BF16)


def _attn_kernel(x_ref, w_ref, bias_ref, sink_ref, sinkc_ref, zero_ref, o_ref,
                 xs_buf, q_buf, gate_buf, k_tab, v_tab, code_buf, ones_buf, s_buf, p_buf, m_buf, mc_buf):
    i = pl.program_id(0)
    t_rows = x_ref.shape[0]
    n_blocks = t_rows // BLOCK

    @pl.when(i == 0)
    def _():
        zero = jnp.zeros((BLOCK, LANES), BF16)
        for j in range(N_KV_HEADS):
            for half in range(2):
                k_tab[j, half, 0:BLOCK, :] = zero
                v_tab[j, half, 0:BLOCK, :] = zero
        row = lax.broadcasted_iota(jnp.int32, (PQ, 2 * 2 * BLOCK), 0)
        col = lax.broadcasted_iota(jnp.int32, (PQ, 2 * 2 * BLOCK), 1)
        key = col & (2 * BLOCK - 1)
        u = key - (row & (BLOCK - 1)) - 1
        in_window = (u >= 0) & (u < WINDOW)
        code_buf[...] = jnp.where(in_window, jnp.where(key >= BLOCK, 2, 1), 0)
        krow = lax.broadcasted_iota(jnp.int32, (4 * BLOCK, LANES), 0)
        klane = lax.broadcasted_iota(jnp.int32, (4 * BLOCK, LANES), 1)
        ones_buf[...] = jnp.where((krow >= 2 * BLOCK) == (klane >= HEAD_DIM), 1.0, 0.0).astype(BF16)

    g_len = ATTN_GROUP_BLOCKS * BLOCK
    xs_buf[...] = x_ref[...].astype(BF16)
    low = lax.broadcasted_iota(jnp.int32, (g_len, LANES), 1) < HEAD_DIM
    gate_col0 = BRANCH_W + 2 * KV_W

    def project_group(g0, after):
        g_rows = slice(g0 * BLOCK, g0 * BLOCK + g_len)
        xg = xs_buf[g_rows, :]
        if after is not None:
            nothing = pltpu.bitcast(after, jnp.int32) & zero_ref[...]
            nothing = jnp.concatenate([nothing] * (g_len // (2 * SUBLANES)), axis=0)
            nothing = jnp.concatenate([nothing] * (D_MODEL // LANES), axis=1)
            xg = pltpu.bitcast(pltpu.bitcast(xg, jnp.int32) | nothing, BF16)
        for piece in range(BRANCH_W // PROJ_COLS):
            c0 = piece * PROJ_COLS
            q = _dot(xg, w_ref[:, c0:c0 + PROJ_COLS]) * (HEAD_DIM ** -0.5)
            q_buf[g_rows, c0:c0 + PROJ_COLS] = q.astype(BF16)
        kv = _dot(xg, w_ref[:, BRANCH_W:BRANCH_W + 2 * KV_W])
        t_rows_g = slice(BLOCK + g0 * BLOCK, BLOCK + g0 * BLOCK + g_len)
        for tab, c0 in ((k_tab, 0), (v_tab, KV_W)):
            val = kv[:, c0:c0 + KV_W]
            swapped = pltpu.roll(val, HEAD_DIM, axis=1)
            zero = jnp.zeros_like(val)
            tab[0, 0, t_rows_g, :] = jnp.where(low, val, zero).astype(BF16)
            tab[0, 1, t_rows_g, :] = jnp.where(low, zero, swapped).astype(BF16)
            tab[1, 0, t_rows_g, :] = jnp.where(low, swapped, zero).astype(BF16)
            tab[1, 1, t_rows_g, :] = jnp.where(low, zero, val).astype(BF16)
        for piece in range(BRANCH_W // PROJ_COLS):
            c0 = piece * PROJ_COLS
            gate_buf[g_rows, c0:c0 + PROJ_COLS] = _dot(xg, w_ref[:, gate_col0 + c0:gate_col0 + c0 + PROJ_COLS])

    low128 = lax.broadcasted_iota(jnp.int32, (BLOCK, LANES), 1) < HEAD_DIM
    project_group(0, None)
    for g0 in range(0, n_blocks, ATTN_GROUP_BLOCKS):
        units = [(n, j) for n in range(g0, g0 + ATTN_GROUP_BLOCKS) for j in range(N_KV_HEADS)]

        for it, (n, j) in enumerate(units):
            r0 = n * BLOCK
            q_cols = j * PAIRS * LANES
            qs = jnp.concatenate(
                [q_buf[r0:r0 + BLOCK, q_cols + p * LANES:q_cols + (p + 1) * LANES] for p in range(PAIRS)], axis=0)
            k2 = jnp.concatenate([k_tab[j, 0, r0:r0 + 2 * BLOCK, :], k_tab[j, 1, r0:r0 + 2 * BLOCK, :]], axis=0)
            logits = lax.dot_general(qs, k2, (((1,), (1,)), ((), ())), preferred_element_type=F32)
            threshold = jnp.where(jnp.logical_and(i == 0, n == 0), 2, 1)
            for p in range(PAIRS):
                rows = slice(p * BLOCK, (p + 1) * BLOCK)
                row_max = []
                for par in range(2):
                    slabs = []
                    for half in range(2):
                        cols = slice((2 * par + half) * LANES, (2 * par + half + 1) * LANES)
                        s = logits[rows, cols] + bias_ref[j, rows, cols]
                        s = jnp.where(code_buf[rows, cols] >= threshold, s, NEG_INF)
                        s_buf[it, rows, cols] = s
                        slabs.append(s)
                    m = jnp.max(jnp.maximum(slabs[0], slabs[1]), axis=1, keepdims=True)
                    m = jnp.maximum(jnp.broadcast_to(m, (BLOCK, LANES)), sink_ref[j, par, rows, :])
                    m_buf[it, par, rows, :] = m
                    row_max.append(m)
                mc_buf[it, rows, :] = jnp.where(low128, row_max[0], row_max[1])

        if g0 + ATTN_GROUP_BLOCKS < n_blocks:
            project_group(g0 + ATTN_GROUP_BLOCKS, logits[0:SUBLANES, 0:LANES])

        for it, (n, j) in enumerate(units):
            for p in range(PAIRS):
                rows = slice(p * BLOCK, (p + 1) * BLOCK)
                for par in range(2):
                    m = m_buf[it, par, rows, :]
                    for half in range(2):
                        cols = slice((2 * par + half) * LANES, (2 * par + half + 1) * LANES)
                        p_buf[it, rows, cols] = jnp.exp(s_buf[it, rows, cols] - m).astype(BF16)

        for it, (n, j) in enumerate(units):
            r0 = n * BLOCK
            q_cols = j * PAIRS * LANES
            v2 = jnp.concatenate([v_tab[j, 0, r0:r0 + 2 * BLOCK, :], v_tab[j, 1, r0:r0 + 2 * BLOCK, :]], axis=0)
            o = _dot(p_buf[it], jnp.concatenate([v2, ones_buf[...]], axis=1))
            for p in range(PAIRS):
                rows = slice(p * BLOCK, (p + 1) * BLOCK)
                cols = slice(q_cols + p * LANES, q_cols + (p + 1) * LANES)
                den = o[rows, LANES:2 * LANES] + jnp.exp(sinkc_ref[j, rows, :] - mc_buf[it, rows, :])
                y = o[rows, 0:LANES] * (1.0 / den) * _silu(gate_buf[r0:r0 + BLOCK, cols])
                o_ref[r0:r0 + BLOCK, cols] = y.astype(BF16)

    for tab in (k_tab, v_tab):
        for j in range(N_KV_HEADS):
            for half in range(2):
                tab[j, half, 0:BLOCK, :] = tab[j, half, t_rows:t_rows + BLOCK, :]


CONV_COLUMNS = BRANCH_W // LANES
CONV_BLOCKS = BRANCH_W // PROJ_COLS


def _conv_stream_kernel(x_ref, w_ref, cw_ref, cb_ref, lng_ref, lnb_ref, zero_ref, o_ref,
                        xs_buf, glu_buf, gate_buf, shift_buf, y_buf):
    i = pl.program_id(0)
    t_rows = x_ref.shape[0]

    @pl.when(i == 0)
    def _():
        for c in range(CONV_COLUMNS):
            glu_buf[c, 0:CONV_HALO, :] = jnp.zeros((CONV_HALO, LANES), F32)

    xs_buf[...] = x_ref[...].astype(BF16)
    first = CONV_HALO - (CONV_K - 1)
    span = t_rows + CONV_HALO - SUBLANES

    def glu_block(n):
        gate_val = _dot(xs_buf[...], w_ref[CONV_BLOCKS + n])
        glu = _dot(xs_buf[...], w_ref[n]) * _sigmoid(gate_val)
        glu_buf[2 * n, CONV_HALO:CONV_HALO + t_rows, :] = glu[:, 0:LANES]
        glu_buf[2 * n + 1, CONV_HALO:CONV_HALO + t_rows, :] = glu[:, LANES:2 * LANES]
        return gate_val[t_rows - SUBLANES:t_rows, 0:LANES]

    def gate_block(n):
        gate = _dot(xs_buf[...], w_ref[2 * CONV_BLOCKS + n])
        gate_buf[2 * n] = gate[:, 0:LANES]
        gate_buf[2 * n + 1] = gate[:, LANES:2 * LANES]
        return gate[t_rows - SUBLANES:t_rows, 0:LANES]

    def conv_block(n, token, second_trigger):
        for half in range(2):
            c = 2 * n + half
            if half == 1:
                token = token + second_trigger()
            for r in range(1, SUBLANES):
                shift_buf[half, r, 0:span, :] = glu_buf[c, r:r + span, :]
            bias = jnp.broadcast_to(cb_ref[c], (STREAM_ROWS, LANES))
            for r0 in range(0, t_rows, STREAM_ROWS):
                nothing = pltpu.bitcast(pltpu.bitcast(token, jnp.int32) & zero_ref[...], F32)
                acc = bias
                for k in range(CONV_K):
                    a8 = (first + k) // SUBLANES * SUBLANES
                    r = (first + k) % SUBLANES
                    if r == 0:
                        src = glu_buf[c, r0 + a8:r0 + a8 + STREAM_ROWS, :]
                    else:
                        src = shift_buf[half, r, r0 + a8:r0 + a8 + STREAM_ROWS, :]
                    tap = jnp.concatenate([cw_ref[c, k] + nothing] * (STREAM_ROWS // SUBLANES), axis=0)
                    acc = acc + tap * src
                y_buf[c, r0:r0 + STREAM_ROWS, :] = acc

    for n in range(CONV_BLOCKS):
        token = glu_block(n)
        if n == 0:
            conv_block(n, token, lambda: token)
        else:
            conv_block(n, token, lambda n=n: gate_block(n - 1))
    gate_block(CONV_BLOCKS - 1)

    for r0 in range(0, t_rows, ROW_CHUNK // 2):
        rows = slice(r0, r0 + ROW_CHUNK // 2)
        ys = [y_buf[c, rows, :] for c in range(CONV_COLUMNS)]
        total = ys[0]
        for y in ys[1:]:
            total = total + y
        mu = jnp.sum(total, axis=1, keepdims=True) * (1.0 / BRANCH_W)
        ds = [y - mu for y in ys]
        sq = ds[0] * ds[0]
        for d in ds[1:]:
            sq = sq + d * d
        rstd = lax.rsqrt(jnp.sum(sq, axis=1, keepdims=True) * (1.0 / BRANCH_W) + LN_EPS)
        for c in range(CONV_COLUMNS):
            y = _silu(ds[c] * rstd * lng_ref[c] + lnb_ref[c]) * _silu(gate_buf[c, rows, :])
            o_ref[rows, c * LANES:(c + 1) * LANES] = y.astype(BF16)

    for c in range(CONV_COLUMNS):
        glu_buf[c, 0:CONV_HALO, :] = glu_buf[c, t_rows:t_rows + CONV_HALO, :]


def _merge_kernel(x_ref, ya_ref, yb_ref, yc_ref, yd_ref, wga_ref, wgb_ref, wgc_ref, wgd_ref, wb_ref, o_ref):
    xb = x_ref[...].astype(BF16)
    acc = None
    branches = ((ya_ref, wga_ref), (yb_ref, wgb_ref), (yc_ref, wgc_ref), (yd_ref, wgd_ref))
    for br, (y_ref, wg_ref) in enumerate(branches):
        gate = _sigmoid(_dot(xb, wg_ref[...]))
        term = gate * _dot(y_ref[...], wb_ref[br])
        acc = term if acc is None else acc + term
    o_ref[...] = acc.astype(BF16)


def _out_kernel(m_ref, w_ref, x_ref, g_ref, b_ref, zero_ref, o_ref, ob_ref, z_buf):
    t_rows = x_ref.shape[0]
    half_rows = t_rows // 2
    chunk = ROW_CHUNK // 2
    tokens = []
    for h in range(2):
        rows = slice(h * half_rows, (h + 1) * half_rows)
        z = _dot(m_ref[rows, :], w_ref[...])
        z_buf[rows, :] = z
        tokens.append(z[half_rows - SUBLANES:half_rows, 0:LANES])
    g = g_ref[...]
    b = b_ref[...]
    for h in range(2):
        nothing = pltpu.bitcast(pltpu.bitcast(tokens[h], jnp.int32) & zero_ref[...], F32)
        nothing = jnp.concatenate([nothing] * (chunk // SUBLANES), axis=0)
        nothing = jnp.concatenate([nothing] * (D_MODEL // LANES), axis=1)
        for r0 in range(h * half_rows, (h + 1) * half_rows, chunk):
            rows = slice(r0, r0 + chunk)
            y = _layer_norm_rows(ALPHA * x_ref[rows, :] + (z_buf[rows, :] + nothing), g, b)
            o_ref[rows, :] = y
            ob_ref[rows, :] = y.astype(BF16)


def _resident(shape):
    zeros = (0,) * len(shape)
    return pl.BlockSpec(shape, lambda *_: zeros, pipeline_mode=pl.Buffered(1))


def _nbytes(shape, dtype):
    return math.prod(d for d in shape if d is not None) * jnp.dtype(dtype).itemsize


def _call(kernel_fn, name, grid, inputs, outputs, scratch, value_bytes):
    arrays, in_specs = zip(*inputs)
    out_shape, out_specs = zip(*outputs)
    windows = [(spec, a.dtype) for a, spec in inputs] + [(spec, o.dtype) for o, spec in outputs]
    need = value_bytes + sum(_nbytes(v.shape, v.dtype) for v in scratch)
    for spec, dtype in windows:
        buffers = spec.pipeline_mode.buffer_count if spec.pipeline_mode is not None else 2
        need += buffers * _nbytes(spec.block_shape, dtype)
    assert need <= V7X_VMEM_BYTES, (name, need)
    return pl.pallas_call(
        kernel_fn,
        grid=grid,
        in_specs=list(in_specs),
        out_specs=list(out_specs) if len(out_specs) > 1 else out_specs[0],
        out_shape=list(out_shape) if len(out_shape) > 1 else out_shape[0],
        scratch_shapes=scratch,
        compiler_params=pltpu.CompilerParams(dimension_semantics=("arbitrary",) * len(grid),
                                             vmem_limit_bytes=need),
        name=name,
    )(*arrays)


def _row_tile(width, rows=SEQ_TILE):
    return pl.BlockSpec((rows, width), lambda i: (i, 0))


def _layer_weight(w_all, layer):
    idx = (layer,) + (0,) * (w_all.ndim - 1)
    return pl.BlockSpec((None,) + w_all.shape[1:], lambda *_: idx, pipeline_mode=pl.Buffered(1))


def _branch_call(kernel_fn, name, seq, x_in, w_all, layer, operands, scratch, value_tiles=2):
    value_bytes = value_tiles * _nbytes((SEQ_TILE, BRANCH_W), F32)
    inputs = [(x_in, _row_tile(D_MODEL)), (w_all, _layer_weight(w_all, layer))]
    inputs += [(op, _resident(op.shape)) for op in operands]
    outputs = [(jax.ShapeDtypeStruct((seq, BRANCH_W), BF16), _row_tile(BRANCH_W))]
    return _call(kernel_fn, name, (seq // SEQ_TILE,), inputs, outputs, scratch, value_bytes)


def _layer(x, x_in, band, w_parts, layer, pool_w, pool_scale, sgu_ln_g, sgu_ln_b, sgu_w, sgu_b, sinks,
           conv_w, conv_b, conv_ln_g, conv_ln_b, w_branch16, w_out16, ln_g, ln_b):
    seq = x.shape[0]
    t = SEQ_TILE
    w_pool, w_sgu, w_attn, w_gates, w_conv = w_parts
    row = lambda v: v.reshape(1, -1).astype(F32)

    y_a = _branch_call(
        _pool_kernel, "branch_pool", seq, x_in, w_pool, layer,
        [pool_w.astype(BF16), row(pool_scale)],
        [pltpu.VMEM((t + POOL_HALO, BRANCH_W), F32), pltpu.VMEM((t, BRANCH_W), F32),
         pltpu.VMEM((t, BRANCH_W), BF16)])

    sgu_bias = jnp.repeat(jnp.transpose(sgu_b).astype(F32), SGU_HD, axis=1)
    y_b = _branch_call(
        _sgu_kernel, "branch_sgu", seq, x_in, w_sgu, layer,
        [row(sgu_ln_g), row(sgu_ln_b), sgu_w.astype(F32), sgu_bias],
        [pltpu.VMEM((t, BRANCH_W), F32), pltpu.VMEM((t, BRANCH_W), F32),
         pltpu.VMEM((t, BRANCH_W), F32), pltpu.VMEM((t, BRANCH_W), BF16)])

    sink_hp = sinks.astype(F32).reshape(N_KV_HEADS, PAIRS, 2)
    sink_rows = jnp.repeat(jnp.transpose(sink_hp, (0, 2, 1)), BLOCK, axis=2)
    sink_rows = jnp.broadcast_to(sink_rows[..., None], (N_KV_HEADS, 2, PQ, LANES))
    sink_lanes = jnp.repeat(jnp.repeat(sink_hp, BLOCK, axis=1), HEAD_DIM, axis=2)
    n_units = N_KV_HEADS * ATTN_GROUP_BLOCKS
    y_c = _branch_call(
        _attn_kernel, "branch_attn", seq, x_in, w_attn, layer,
        [band, sink_rows, sink_lanes, jnp.zeros((SUBLANES, LANES), jnp.int32)],
        [pltpu.VMEM((t, D_MODEL), BF16), pltpu.VMEM((t, BRANCH_W), BF16), pltpu.VMEM((t, BRANCH_W), F32),
         pltpu.VMEM((N_KV_HEADS, 2, t + BLOCK, LANES), BF16),
         pltpu.VMEM((N_KV_HEADS, 2, t + BLOCK, LANES), BF16),
         pltpu.VMEM((PQ, 4 * BLOCK), jnp.int32),
         pltpu.VMEM((4 * BLOCK, LANES), BF16),
         pltpu.VMEM((n_units, PQ, 4 * BLOCK), F32),
         pltpu.VMEM((n_units, PQ, 4 * BLOCK), BF16),
         pltpu.VMEM((n_units, 2, PQ, LANES), F32),
         pltpu.VMEM((n_units, PQ, LANES), F32)])

    per_col = lambda v: v.astype(F32).reshape(CONV_COLUMNS, 1, LANES)
    taps = jnp.transpose(conv_w.astype(F32).reshape(CONV_K, CONV_COLUMNS, LANES), (1, 0, 2))
    taps = jnp.broadcast_to(taps[:, :, None, :], (CONV_COLUMNS, CONV_K, SUBLANES, LANES))
    y_d = _branch_call(
        _conv_stream_kernel, "branch_conv", seq, x_in, w_conv, layer,
        [taps, per_col(conv_b), per_col(conv_ln_g), per_col(conv_ln_b), jnp.zeros((SUBLANES, LANES), jnp.int32)],
        [pltpu.VMEM((t, D_MODEL), BF16),
         pltpu.VMEM((CONV_COLUMNS, t + CONV_HALO, LANES), F32),
         pltpu.VMEM((CONV_COLUMNS, t, LANES), F32),
         pltpu.VMEM((2, SUBLANES, t + CONV_HALO - SUBLANES, LANES), F32),
         pltpu.VMEM((CONV_COLUMNS, t, LANES), F32)], value_tiles=4)

    n_cb = D_MODEL // MERGE_COLS
    tm = MERGE_TILE
    x_spec = pl.BlockSpec((tm, D_MODEL), lambda c, i: (i, 0))
    y_spec = pl.BlockSpec((tm, BRANCH_W), lambda c, i: (i, 0))
    wg_specs = [pl.BlockSpec((None, D_MODEL, MERGE_COLS),
                             functools.partial(lambda br, c, i: (layer, 0, br * n_cb + c), br))
                for br in range(N_BRANCH)]
    wb_spec = pl.BlockSpec((None, N_BRANCH, BRANCH_W, MERGE_COLS), lambda c, i: (layer, 0, 0, c))
    merged = _call(
        _merge_kernel, "merge", (n_cb, seq // tm),
        [(x_in, x_spec)] + [(y, y_spec) for y in (y_a, y_b, y_c, y_d)]
        + [(w_gates, spec) for spec in wg_specs] + [(w_branch16, wb_spec)],
        [(jax.ShapeDtypeStruct((seq, D_MODEL), BF16), pl.BlockSpec((tm, MERGE_COLS), lambda c, i: (i, c)))],
        [], value_bytes=4 * _nbytes((tm, MERGE_COLS), F32))

    x_next, x_next_bf = _call(
        _out_kernel, "out_norm", (seq // t,),
        [(merged, _row_tile(D_MODEL)), (w_out16, _layer_weight(w_out16, layer)), (x, _row_tile(D_MODEL)),
         (row(ln_g), _resident((1, D_MODEL))), (row(ln_b), _resident((1, D_MODEL))),
         (jnp.zeros((SUBLANES, LANES), jnp.int32), _resident((SUBLANES, LANES)))],
        [(jax.ShapeDtypeStruct((seq, D_MODEL), F32), _row_tile(D_MODEL)),
         (jax.ShapeDtypeStruct((seq, D_MODEL), BF16), _row_tile(D_MODEL))],
        [pltpu.VMEM((t, D_MODEL), F32)], value_bytes=_nbytes((t, D_MODEL), F32))
    return x_next, x_next_bf


def kernel(x, w_in, pool_w, pool_scale, sgu_ln_g, sgu_ln_b, sgu_w, sgu_b, attn_sinks, rel_bias, conv_w,
           conv_b, conv_ln_g, conv_ln_b, w_branch, w_out, ln_g, ln_b):
    batch, seq, d_model = x.shape
    assert d_model == D_MODEL and w_in.shape == (DEPTH, D_MODEL, D_IN)
    assert seq % SEQ_TILE == 0 and SEQ_TILE % (BLOCK * ATTN_GROUP_BLOCKS) == 0

    band = _band_bias(rel_bias).reshape(N_KV_HEADS, PAIRS, 2, BLOCK, 2 * BLOCK)
    band = jnp.transpose(band, (0, 1, 3, 2, 4)).reshape(N_KV_HEADS, PQ, 4 * BLOCK)

    w_parts = _cast_w_in(w_in.astype(F32))
    w_branch16 = w_branch.astype(BF16)
    w_out16 = w_out.astype(BF16)
    outs = []
    for bi in range(batch):
        xc = x[bi].astype(F32)
        x_in = xc
        for l in range(DEPTH):
            xc, x_in = _layer(xc, x_in, band, w_parts, l, pool_w[l], pool_scale[l], sgu_ln_g[l], sgu_ln_b[l],
                              sgu_w[l], sgu_b[l], attn_sinks[l], conv_w[l], conv_b[l], conv_ln_g[l],
                              conv_ln_b[l], w_branch16, w_out16, ln_g[l], ln_b[l])
        outs.append(xc.astype(x.dtype)[None])
    return outs[0] if batch == 1 else jnp.concatenate(outs, axis=0)
```

```python
import functools
import math

import jax
import jax.numpy as jnp
from jax import lax
from jax.experimental import pallas as pl
from jax.experimental.pallas import tpu as pltpu

F32 = jnp.float32
BF16 = jnp.bfloat16

D_MODEL = 2048
DEPTH = 2
N_BRANCH = 4
BRANCH_W = 1024
POOL_WINDOWS = (2, 4, 8, 16)
POOL_GW = BRANCH_W // len(POOL_WINDOWS)
SGU_HEADS = 8
SGU_HD = BRANCH_W // SGU_HEADS
CHUNK = 128
N_Q_HEADS = 16
N_KV_HEADS = 2
Q_PER_KV = N_Q_HEADS // N_KV_HEADS
HEAD_DIM = 64
KV_W = N_KV_HEADS * HEAD_DIM
WINDOW = 128
BLOCK = 128
NUM_BUCKETS = 32
MAX_DISTANCE = 128
CONV_K = 31
ALPHA = (2 * DEPTH) ** 0.25
LN_EPS = 1e-5
NEG_INF = -1e30

OFF_A = 0
OFF_B = OFF_A + 2 * BRANCH_W
OFF_C = OFF_B + 3 * BRANCH_W
OFF_D = OFF_C + 2 * BRANCH_W + 2 * KV_W
OFF_G = OFF_D + 3 * BRANCH_W
D_IN = OFF_G + N_BRANCH * D_MODEL

LANES = 128
SUBLANES = 8
SEQ_TILE = 512
POOL_HALO = 16
CONV_HALO = 32
ROW_CHUNK = 64
STREAM_ROWS = 512
PROJ_COLS = 256
MERGE_COLS = 512
PAIRS = Q_PER_KV // 2
PQ = PAIRS * BLOCK
ATTN_GROUP_BLOCKS = 2
MERGE_TILE = 512
V7X_VMEM_BYTES = 64 * 1024 * 1024


def _sigmoid(x):
    return 0.5 * jnp.tanh(0.5 * x) + 0.5


def _silu(x):
    return x * _sigmoid(x)


def _layer_norm_rows(v, g, b):
    mu = jnp.mean(v, axis=-1, keepdims=True)
    d = v - mu
    var = jnp.mean(d * d, axis=-1, keepdims=True)
    return d * lax.rsqrt(var + LN_EPS) * g + b


def _dot(a, b):
    return jnp.dot(a, b, preferred_element_type=F32)


def _t5_bucket(n):
    max_exact = NUM_BUCKETS // 2
    nf = jnp.maximum(n, 1).astype(F32)
    large = max_exact + (jnp.log(nf / max_exact) / math.log(MAX_DISTANCE / max_exact)
                         * (NUM_BUCKETS - max_exact)).astype(jnp.int32)
    large = jnp.minimum(large, NUM_BUCKETS - 1)
    return jnp.where(n < max_exact, n, large)


def _band_bucket_ids():
    i = jnp.arange(BLOCK)[:, None]
    j = jnp.arange(2 * BLOCK)[None, :]
    return _t5_bucket(jnp.clip(i + BLOCK - j, 0, WINDOW - 1)).astype(jnp.int32)


def _bias_kernel(bucket_ref, table_ref, o_ref):
    h = pl.program_id(0)
    bucket = bucket_ref[...]
    acc = jnp.zeros(bucket.shape, F32)
    for b in range(NUM_BUCKETS):
        acc = jnp.where(bucket == b, table_ref[b, h], acc)
    o_ref[0] = acc


def _band_bias(rel_bias):
    return pl.pallas_call(
        _bias_kernel,
        grid=(N_Q_HEADS,),
        in_specs=[pl.BlockSpec((BLOCK, 2 * BLOCK), lambda h: (0, 0)),
                  pl.BlockSpec(memory_space=pltpu.SMEM)],
        out_specs=pl.BlockSpec((1, BLOCK, 2 * BLOCK), lambda h: (h, 0, 0)),
        out_shape=jax.ShapeDtypeStruct((N_Q_HEADS, BLOCK, 2 * BLOCK), F32),
        name="band_bias",
    )(_band_bucket_ids(), rel_bias.astype(F32))


W_IN_SPLITS = ((OFF_A, OFF_B), (OFF_B, OFF_C), (OFF_C, OFF_D), (OFF_G, D_IN))
CONV_W_CHUNKS = (OFF_G - OFF_D) // PROJ_COLS
CAST_ROWS = 128


def _cast_w_in_kernel(w_ref, oa_ref, ob_ref, oc_ref, og_ref, od_ref):
    for (a, b), o_ref in zip(W_IN_SPLITS, (oa_ref, ob_ref, oc_ref, og_ref)):
        o_ref[...] = w_ref[:, a:b].astype(BF16)
    for j in range(CONV_W_CHUNKS):
        od_ref[j] = w_ref[:, OFF_D + j * PROJ_COLS:OFF_D + (j + 1) * PROJ_COLS].astype(BF16)


def _cast_w_in(w_in):
    widths = [b - a for a, b in W_IN_SPLITS]
    outputs = [(jax.ShapeDtypeStruct((DEPTH, D_MODEL, w), BF16),
                pl.BlockSpec((None, CAST_ROWS, w), lambda l, r: (l, r, 0))) for w in widths]
    outputs.append((jax.ShapeDtypeStruct((DEPTH, CONV_W_CHUNKS, D_MODEL, PROJ_COLS), BF16),
                    pl.BlockSpec((None, CONV_W_CHUNKS, CAST_ROWS, PROJ_COLS), lambda l, r: (l, 0, r, 0))))
    return _call(_cast_w_in_kernel, "cast_w_in", (DEPTH, D_MODEL // CAST_ROWS),
                 [(w_in, pl.BlockSpec((None, CAST_ROWS, D_IN), lambda l, r: (l, r, 0)))],
                 outputs, [], value_bytes=0)


def _pool_kernel(x_ref, w_ref, pw_ref, sc_ref, o_ref, abuf, gate_buf, mix_buf):
    i = pl.program_id(0)
    t_rows = x_ref.shape[0]

    @pl.when(i == 0)
    def _():
        abuf[0:POOL_HALO, :] = jnp.zeros((POOL_HALO, BRANCH_W), F32)

    xb = x_ref[...].astype(BF16)
    abuf[POOL_HALO:POOL_HALO + t_rows, :] = _dot(xb, w_ref[:, 0:BRANCH_W])
    gate_buf[...] = _dot(xb, w_ref[:, BRANCH_W:2 * BRANCH_W])

    for g, win in enumerate(POOL_WINDOWS):
        c0 = g * POOL_GW
        for r0 in range(0, t_rows, ROW_CHUNK):
            base = POOL_HALO + r0
            cur = abuf[base:base + ROW_CHUNK, c0:c0 + POOL_GW]
            acc = cur
            for s in range(1, win):
                acc = acc + abuf[base - s:base - s + ROW_CHUNK, c0:c0 + POOL_GW]
            t = i * t_rows + r0 + lax.broadcasted_iota(jnp.int32, (ROW_CHUNK, POOL_GW), 0)
            cnt = jnp.minimum(t + 1, win).astype(F32)
            mix_buf[r0:r0 + ROW_CHUNK, c0:c0 + POOL_GW] = (acc / cnt - cur).astype(BF16)

    for g in range(len(POOL_WINDOWS)):
        c0 = g * POOL_GW
        y = _dot(mix_buf[:, c0:c0 + POOL_GW], pw_ref[g])
        y = y * sc_ref[:, c0:c0 + POOL_GW] * _silu(gate_buf[:, c0:c0 + POOL_GW])
        o_ref[:, c0:c0 + POOL_GW] = y.astype(BF16)

    abuf[0:POOL_HALO, :] = abuf[t_rows:t_rows + POOL_HALO, :]


def _sgu_kernel(x_ref, w_ref, lng_ref, lnb_ref, sw_ref, sb_ref, o_ref,
                u_buf, v_buf, gate_buf, vn_buf):
    t_rows = x_ref.shape[0]
    xb = x_ref[...].astype(BF16)
    u_buf[...] = _dot(xb, w_ref[:, 0:BRANCH_W])
    v_buf[...] = _dot(xb, w_ref[:, BRANCH_W:2 * BRANCH_W])
    gate_buf[...] = _dot(xb, w_ref[:, 2 * BRANCH_W:3 * BRANCH_W])

    g = lng_ref[...]
    b = lnb_ref[...]
    for r0 in range(0, t_rows, ROW_CHUNK):
        vn = _layer_norm_rows(v_buf[r0:r0 + ROW_CHUNK, :], g, b)
        vn_buf[r0:r0 + ROW_CHUNK, :] = vn.astype(BF16)

    row = lax.broadcasted_iota(jnp.int32, (CHUNK, CHUNK), 0)
    col = lax.broadcasted_iota(jnp.int32, (CHUNK, CHUNK), 1)
    causal = row >= col
    for h in range(SGU_HEADS):
        c0 = h * SGU_HD
        w_h = jnp.where(causal, sw_ref[h], 0.0).astype(BF16)
        bias_h = sb_ref[:, c0:c0 + SGU_HD]
        for r0 in range(0, t_rows, CHUNK):
            sp = _dot(w_h, vn_buf[r0:r0 + CHUNK, c0:c0 + SGU_HD]) + bias_h
            y = u_buf[r0:r0 + CHUNK, c0:c0 + SGU_HD] * sp * _silu(gate_buf[r0:r0 + CHUNK, c0:c0 + SGU_HD])
            o_ref[r0:r0 + CHUNK, c0:c0 + SGU_HD] = y.astype(BF16)


def _attn_kernel(x_ref, w_ref, bias_ref, sink_ref, sinkc_ref, zero_ref, o_ref,
                 xs_buf, q_buf, gate_buf, k_tab, v_tab, code_buf, ones_buf, s_buf, p_buf, m_buf, mc_buf):
    i = pl.program_id(0)
    t_rows = x_ref.shape[0]
    n_blocks = t_rows // BLOCK

    @pl.when(i == 0)
    def _():
        zero = jnp.zeros((BLOCK, LANES), BF16)
        for j in range(N_KV_HEADS):
            for half in range(2):
                k_tab[j, half, 0:BLOCK, :] = zero
                v_tab[j, half, 0:BLOCK, :] = zero
        row = lax.broadcasted_iota(jnp.int32, (PQ, 2 * 2 * BLOCK), 0)
        col = lax.broadcasted_iota(jnp.int32, (PQ, 2 * 2 * BLOCK), 1)
        key = col & (2 * BLOCK - 1)
        u = key - (row & (BLOCK - 1)) - 1
        in_window = (u >= 0) & (u < WINDOW)
        code_buf[...] = jnp.where(in_window, jnp.where(key >= BLOCK, 2, 1), 0)
        krow = lax.broadcasted_iota(jnp.int32, (4 * BLOCK, LANES), 0)
        klane = lax.broadcasted_iota(jnp.int32, (4 * BLOCK, LANES), 1)
        ones_buf[...] = jnp.where((krow >= 2 * BLOCK) == (klane >= HEAD_DIM), 1.0, 0.0).astype(BF16)

    g_len = ATTN_GROUP_BLOCKS * BLOCK
    xs_buf[...] = x_ref[...].astype(BF16)
    low = lax.broadcasted_iota(jnp.int32, (g_len, LANES), 1) < HEAD_DIM
    gate_col0 = BRANCH_W + 2 * KV_W

    def project_group(g0, after, qkv=True, gate=True):
        g_rows = slice(g0 * BLOCK, g0 * BLOCK + g_len)
        xg = xs_buf[g_rows, :]
        if after is not None:
            nothing = pltpu.bitcast(after, jnp.int32) & zero_ref[...]
            nothing = jnp.concatenate([nothing] * (g_len // (2 * SUBLANES)), axis=0)
            nothing = jnp.concatenate([nothing] * (D_MODEL // LANES), axis=1)
            xg = pltpu.bitcast(pltpu.bitcast(xg, jnp.int32) | nothing, BF16)
        if qkv:
            for piece in range(BRANCH_W // PROJ_COLS):
                c0 = piece * PROJ_COLS
                q = _dot(xg, w_ref[:, c0:c0 + PROJ_COLS]) * (HEAD_DIM ** -0.5)
                q_buf[g_rows, c0:c0 + PROJ_COLS] = q.astype(BF16)
            kv = _dot(xg, w_ref[:, BRANCH_W:BRANCH_W + 2 * KV_W])
            t_rows_g = slice(BLOCK + g0 * BLOCK, BLOCK + g0 * BLOCK + g_len)
            for tab, c0 in ((k_tab, 0), (v_tab, KV_W)):
                val = kv[:, c0:c0 + KV_W]
                swapped = pltpu.roll(val, HEAD_DIM, axis=1)
                zero = jnp.zeros_like(val)
                tab[0, 0, t_rows_g, :] = jnp.where(low, val, zero).astype(BF16)
                tab[0, 1, t_rows_g, :] = jnp.where(low, zero, swapped).astype(BF16)
                tab[1, 0, t_rows_g, :] = jnp.where(low, swapped, zero).astype(BF16)
                tab[1, 1, t_rows_g, :] = jnp.where(low, zero, val).astype(BF16)
        if gate:
            for piece in range(BRANCH_W // PROJ_COLS):
                c0 = piece * PROJ_COLS
                gate_buf[g_rows, c0:c0 + PROJ_COLS] = _dot(xg, w_ref[:, gate_col0 + c0:gate_col0 + c0 + PROJ_COLS])

    low128 = lax.broadcasted_iota(jnp.int32, (BLOCK, LANES), 1) < HEAD_DIM
    project_group(0, None)
    for g0 in range(0, n_blocks, ATTN_GROUP_BLOCKS):
        units = [(n, j) for n in range(g0, g0 + ATTN_GROUP_BLOCKS) for j in range(N_KV_HEADS)]

        for it, (n, j) in enumerate(units):
            r0 = n * BLOCK
            q_cols = j * PAIRS * LANES
            qs = jnp.concatenate(
                [q_buf[r0:r0 + BLOCK, q_cols + p * LANES:q_cols + (p + 1) * LANES] for p in range(PAIRS)], axis=0)
            k2 = jnp.concatenate([k_tab[j, 0, r0:r0 + 2 * BLOCK, :], k_tab[j, 1, r0:r0 + 2 * BLOCK, :]], axis=0)
            logits = lax.dot_general(qs, k2, (((1,), (1,)), ((), ())), preferred_element_type=F32)
            threshold = jnp.where(jnp.logical_and(i == 0, n == 0), 2, 1)
            for p in range(PAIRS):
                rows = slice(p * BLOCK, (p + 1) * BLOCK)
                row_max = []
                for par in range(2):
                    slabs = []
                    for half in range(2):
                        cols = slice((2 * par + half) * LANES, (2 * par + half + 1) * LANES)
                        s = logits[rows, cols] + bias_ref[j, rows, cols]
                        s = jnp.where(code_buf[rows, cols] >= threshold, s, NEG_INF)
                        s_buf[it, rows, cols] = s
                        slabs.append(s)
                    m = jnp.max(jnp.maximum(slabs[0], slabs[1]), axis=1, keepdims=True)
                    m = jnp.maximum(jnp.broadcast_to(m, (BLOCK, LANES)), sink_ref[j, par, rows, :])
                    m_buf[it, par, rows, :] = m
                    row_max.append(m)
                mc_buf[it, rows, :] = jnp.where(low128, row_max[0], row_max[1])

        after = logits[0:SUBLANES, 0:LANES]
        nxt = g0 + ATTN_GROUP_BLOCKS
        if nxt < n_blocks:
            project_group(nxt, after, gate=nxt + ATTN_GROUP_BLOCKS < n_blocks)
        else:
            project_group(g0, after, qkv=False)

        for it, (n, j) in enumerate(units):
            for p in range(PAIRS):
                rows = slice(p * BLOCK, (p + 1) * BLOCK)
                for par in range(2):
                    m = m_buf[it, par, rows, :]
                    for half in range(2):
                        cols = slice((2 * par + half) * LANES, (2 * par + half + 1) * LANES)
                        p_buf[it, rows, cols] = jnp.exp(s_buf[it, rows, cols] - m).astype(BF16)

        for it, (n, j) in enumerate(units):
            r0 = n * BLOCK
            q_cols = j * PAIRS * LANES
            v2 = jnp.concatenate([v_tab[j, 0, r0:r0 + 2 * BLOCK, :], v_tab[j, 1, r0:r0 + 2 * BLOCK, :]], axis=0)
            o = _dot(p_buf[it], jnp.concatenate([v2, ones_buf[...]], axis=1))
            for p in range(PAIRS):
                rows = slice(p * BLOCK, (p + 1) * BLOCK)
                cols = slice(q_cols + p * LANES, q_cols + (p + 1) * LANES)
                den = o[rows, LANES:2 * LANES] + jnp.exp(sinkc_ref[j, rows, :] - mc_buf[it, rows, :])
                y = o[rows, 0:LANES] * (1.0 / den) * _silu(gate_buf[r0:r0 + BLOCK, cols])
                o_ref[r0:r0 + BLOCK, cols] = y.astype(BF16)

    for tab in (k_tab, v_tab):
        for j in range(N_KV_HEADS):
            for half in range(2):
                tab[j, half, 0:BLOCK, :] = tab[j, half, t_rows:t_rows + BLOCK, :]


CONV_COLUMNS = BRANCH_W // LANES
CONV_BLOCKS = BRANCH_W // PROJ_COLS


def _conv_stream_kernel(x_ref, w_ref, cw_ref, cb_ref, lng_ref, lnb_ref, zero_ref, o_ref,
                        xs_buf, glu_buf, gate_buf, shift_buf, y_buf):
    i = pl.program_id(0)
    t_rows = x_ref.shape[0]

    @pl.when(i == 0)
    def _():
        for c in range(CONV_COLUMNS):
            glu_buf[c, 0:CONV_HALO, :] = jnp.zeros((CONV_HALO, LANES), F32)

    xs_buf[...] = x_ref[...].astype(BF16)
    first = CONV_HALO - (CONV_K - 1)
    span = t_rows + CONV_HALO - SUBLANES

    def glu_block(n):
        gate_val = _dot(xs_buf[...], w_ref[CONV_BLOCKS + n])
        glu = _dot(xs_buf[...], w_ref[n]) * _sigmoid(gate_val)
        glu_buf[2 * n, CONV_HALO:CONV_HALO + t_rows, :] = glu[:, 0:LANES]
        glu_buf[2 * n + 1, CONV_HALO:CONV_HALO + t_rows, :] = glu[:, LANES:2 * LANES]
        return gate_val[t_rows - SUBLANES:t_rows, 0:LANES]

    def gate_block(n):
        gate = _dot(xs_buf[...], w_ref[2 * CONV_BLOCKS + n])
        gate_buf[2 * n] = gate[:, 0:LANES]
        gate_buf[2 * n + 1] = gate[:, LANES:2 * LANES]
        return gate[t_rows - SUBLANES:t_rows, 0:LANES]

    def conv_block(n, token, second_trigger):
        for half in range(2):
            c = 2 * n + half
            if half == 1:
                token = token + second_trigger()
            for r in range(1, SUBLANES):
                shift_buf[half, r, 0:span, :] = glu_buf[c, r:r + span, :]
            bias = jnp.broadcast_to(cb_ref[c], (STREAM_ROWS, LANES))
            for r0 in range(0, t_rows, STREAM_ROWS):
                nothing = pltpu.bitcast(pltpu.bitcast(token, jnp.int32) & zero_ref[...], F32)
                acc = bias
                for k in range(CONV_K):
                    a8 = (first + k) // SUBLANES * SUBLANES
                    r = (first + k) % SUBLANES
                    if r == 0:
                        src = glu_buf[c, r0 + a8:r0 + a8 + STREAM_ROWS, :]
                    else:
                        src = shift_buf[half, r, r0 + a8:r0 + a8 + STREAM_ROWS, :]
                    tap = jnp.concatenate([cw_ref[c, k] + nothing] * (STREAM_ROWS // SUBLANES), axis=0)
                    acc = acc + tap * src
                y_buf[c, r0:r0 + STREAM_ROWS, :] = acc

    for n in range(CONV_BLOCKS):
        token = glu_block(n)
        if n == 0:
            conv_block(n, token, lambda: token)
        else:
            conv_block(n, token, lambda n=n: gate_block(n - 1))
    gate_block(CONV_BLOCKS - 1)

    for r0 in range(0, t_rows, ROW_CHUNK):
        rows = slice(r0, r0 + ROW_CHUNK)
        ys = [y_buf[c, rows, :] for c in range(CONV_COLUMNS)]
        total = ys[0]
        for y in ys[1:]:
            total = total + y
        mu = jnp.sum(total, axis=1, keepdims=True) * (1.0 / BRANCH_W)
        ds = [y - mu for y in ys]
        sq = ds[0] * ds[0]
        for d in ds[1:]:
            sq = sq + d * d
        rstd = lax.rsqrt(jnp.sum(sq, axis=1, keepdims=True) * (1.0 / BRANCH_W) + LN_EPS)
        for c in range(CONV_COLUMNS):
            y = _silu(ds[c] * rstd * lng_ref[c] + lnb_ref[c]) * _silu(gate_buf[c, rows, :])
            o_ref[rows, c * LANES:(c + 1) * LANES] = y.astype(BF16)

    for c in range(CONV_COLUMNS):
        glu_buf[c, 0:CONV_HALO, :] = glu_buf[c, t_rows:t_rows + CONV_HALO, :]


def _merge_kernel(x_ref, ya_ref, yb_ref, yc_ref, yd_ref, wga_ref, wgb_ref, wgc_ref, wgd_ref, wb_ref, o_ref):
    xb = x_ref[...].astype(BF16)
    acc = None
    branches = ((ya_ref, wga_ref), (yb_ref, wgb_ref), (yc_ref, wgc_ref), (yd_ref, wgd_ref))
    for br, (y_ref, wg_ref) in enumerate(branches):
        gate = _sigmoid(_dot(xb, wg_ref[...]))
        term = gate * _dot(y_ref[...], wb_ref[br])
        acc = term if acc is None else acc + term
    o_ref[...] = acc.astype(BF16)


def _out_kernel(m_ref, w_ref, x_ref, g_ref, b_ref, zero_ref, o_ref, ob_ref, z_buf):
    t_rows = x_ref.shape[0]
    half_rows = t_rows // 2
    chunk = ROW_CHUNK // 2
    tokens = []
    for h in range(2):
        rows = slice(h * half_rows, (h + 1) * half_rows)
        z = _dot(m_ref[rows, :], w_ref[...])
        z_buf[rows, :] = z
        tokens.append(z[half_rows - SUBLANES:half_rows, 0:LANES])
    g = g_ref[...]
    b = b_ref[...]
    for h in range(2):
        nothing = pltpu.bitcast(pltpu.bitcast(tokens[h], jnp.int32) & zero_ref[...], F32)
        nothing = jnp.concatenate([nothing] * (chunk // SUBLANES), axis=0)
        nothing = jnp.concatenate([nothing] * (D_MODEL // LANES), axis=1)
        for r0 in range(h * half_rows, (h + 1) * half_rows, chunk):
            rows = slice(r0, r0 + chunk)
            y = _layer_norm_rows(ALPHA * x_ref[rows, :] + (z_buf[rows, :] + nothing), g, b)
            o_ref[rows, :] = y
            ob_ref[rows, :] = y.astype(BF16)


def _resident(shape):
    zeros = (0,) * len(shape)
    return pl.BlockSpec(shape, lambda *_: zeros, pipeline_mode=pl.Buffered(1))


def _nbytes(shape, dtype):
    return math.prod(d for d in shape if d is not None) * jnp.dtype(dtype).itemsize


def _call(kernel_fn, name, grid, inputs, outputs, scratch, value_bytes):
    arrays, in_specs = zip(*inputs)
    out_shape, out_specs = zip(*outputs)
    windows = [(spec, a.dtype) for a, spec in inputs] + [(spec, o.dtype) for o, spec in outputs]
    need = value_bytes + sum(_nbytes(v.shape, v.dtype) for v in scratch)
    for spec, dtype in windows:
        buffers = spec.pipeline_mode.buffer_count if spec.pipeline_mode is not None else 2
        need += buffers * _nbytes(spec.block_shape, dtype)
    assert need <= V7X_VMEM_BYTES, (name, need)
    return pl.pallas_call(
        kernel_fn,
        grid=grid,
        in_specs=list(in_specs),
        out_specs=list(out_specs) if len(out_specs) > 1 else out_specs[0],
        out_shape=list(out_shape) if len(out_shape) > 1 else out_shape[0],
        scratch_shapes=scratch,
        compiler_params=pltpu.CompilerParams(dimension_semantics=("arbitrary",) * len(grid),
                                             vmem_limit_bytes=need),
        name=name,
    )(*arrays)


def _row_tile(width, rows=SEQ_TILE):
    return pl.BlockSpec((rows, width), lambda i: (i, 0))


def _layer_weight(w_all, layer):
    idx = (layer,) + (0,) * (w_all.ndim - 1)
    return pl.BlockSpec((None,) + w_all.shape[1:], lambda *_: idx, pipeline_mode=pl.Buffered(1))


def _branch_call(kernel_fn, name, seq, x_in, w_all, layer, operands, scratch, value_tiles=2):
    value_bytes = value_tiles * _nbytes((SEQ_TILE, BRANCH_W), F32)
    inputs = [(x_in, _row_tile(D_MODEL)), (w_all, _layer_weight(w_all, layer))]
    inputs += [(op, _resident(op.shape)) for op in operands]
    outputs = [(jax.ShapeDtypeStruct((seq, BRANCH_W), BF16), _row_tile(BRANCH_W))]
    return _call(kernel_fn, name, (seq // SEQ_TILE,), inputs, outputs, scratch, value_bytes)


def _layer(x, x_in, band, w_parts, layer, pool_w, pool_scale, sgu_ln_g, sgu_ln_b, sgu_w, sgu_b, sinks,
           conv_w, conv_b, conv_ln_g, conv_ln_b, w_branch16, w_out16, ln_g, ln_b):
    seq = x.shape[0]
    t = SEQ_TILE
    w_pool, w_sgu, w_attn, w_gates, w_conv = w_parts
    row = lambda v: v.reshape(1, -1).astype(F32)

    y_a = _branch_call(
        _pool_kernel, "branch_pool", seq, x_in, w_pool, layer,
        [pool_w.astype(BF16), row(pool_scale)],
        [pltpu.VMEM((t + POOL_HALO, BRANCH_W), F32), pltpu.VMEM((t, BRANCH_W), F32),
         pltpu.VMEM((t, BRANCH_W), BF16)])

    sgu_bias = jnp.repeat(jnp.transpose(sgu_b).astype(F32), SGU_HD, axis=1)
    y_b = _branch_call(
        _sgu_kernel, "branch_sgu", seq, x_in, w_sgu, layer,
        [row(sgu_ln_g), row(sgu_ln_b), sgu_w.astype(F32), sgu_bias],
        [pltpu.VMEM((t, BRANCH_W), F32), pltpu.VMEM((t, BRANCH_W), F32),
         pltpu.VMEM((t, BRANCH_W), F32), pltpu.VMEM((t, BRANCH_W), BF16)])

    sink_hp = sinks.astype(F32).reshape(N_KV_HEADS, PAIRS, 2)
    sink_rows = jnp.repeat(jnp.transpose(sink_hp, (0, 2, 1)), BLOCK, axis=2)
    sink_rows = jnp.broadcast_to(sink_rows[..., None], (N_KV_HEADS, 2, PQ, LANES))
    sink_lanes = jnp.repeat(jnp.repeat(sink_hp, BLOCK, axis=1), HEAD_DIM, axis=2)
    n_units = N_KV_HEADS * ATTN_GROUP_BLOCKS
    y_c = _branch_call(
        _attn_kernel, "branch_attn", seq, x_in, w_attn, layer,
        [band, sink_rows, sink_lanes, jnp.zeros((SUBLANES, LANES), jnp.int32)],
        [pltpu.VMEM((t, D_MODEL), BF16), pltpu.VMEM((t, BRANCH_W), BF16), pltpu.VMEM((t, BRANCH_W), F32),
         pltpu.VMEM((N_KV_HEADS, 2, t + BLOCK, LANES), BF16),
         pltpu.VMEM((N_KV_HEADS, 2, t + BLOCK, LANES), BF16),
         pltpu.VMEM((PQ, 4 * BLOCK), jnp.int32),
         pltpu.VMEM((4 * BLOCK, LANES), BF16),
         pltpu.VMEM((n_units, PQ, 4 * BLOCK), F32),
         pltpu.VMEM((n_units, PQ, 4 * BLOCK), BF16),
         pltpu.VMEM((n_units, 2, PQ, LANES), F32),
         pltpu.VMEM((n_units, PQ, LANES), F32)])

    per_col = lambda v: v.astype(F32).reshape(CONV_COLUMNS, 1, LANES)
    taps = jnp.transpose(conv_w.astype(F32).reshape(CONV_K, CONV_COLUMNS, LANES), (1, 0, 2))
    taps = jnp.broadcast_to(taps[:, :, None, :], (CONV_COLUMNS, CONV_K, SUBLANES, LANES))
    y_d = _branch_call(
        _conv_stream_kernel, "branch_conv", seq, x_in, w_conv, layer,
        [taps, per_col(conv_b), per_col(conv_ln_g), per_col(conv_ln_b), jnp.zeros((SUBLANES, LANES), jnp.int32)],
        [pltpu.VMEM((t, D_MODEL), BF16),
         pltpu.VMEM((CONV_COLUMNS, t + CONV_HALO, LANES), F32),
         pltpu.VMEM((CONV_COLUMNS, t, LANES), F32),
         pltpu.VMEM((2, SUBLANES, t + CONV_HALO - SUBLANES, LANES), F32),
         pltpu.VMEM((CONV_COLUMNS, t, LANES), F32)], value_tiles=4)

    n_cb = D_MODEL // MERGE_COLS
    tm = MERGE_TILE
    x_spec = pl.BlockSpec((tm, D_MODEL), lambda c, i: (i, 0))
    y_spec = pl.BlockSpec((tm, BRANCH_W), lambda c, i: (i, 0))
    wg_specs = [pl.BlockSpec((None, D_MODEL, MERGE_COLS),
                             functools.partial(lambda br, c, i: (layer, 0, br * n_cb + c), br))
                for br in range(N_BRANCH)]
    wb_spec = pl.BlockSpec((None, N_BRANCH, BRANCH_W, MERGE_COLS), lambda c, i: (layer, 0, 0, c))
    merged = _call(
        _merge_kernel, "merge", (n_cb, seq // tm),
        [(x_in, x_spec)] + [(y, y_spec) for y in (y_a, y_b, y_c, y_d)]
        + [(w_gates, spec) for spec in wg_specs] + [(w_branch16, wb_spec)],
        [(jax.ShapeDtypeStruct((seq, D_MODEL), BF16), pl.BlockSpec((tm, MERGE_COLS), lambda c, i: (i, c)))],
        [], value_bytes=4 * _nbytes((tm, MERGE_COLS), F32))

    x_next, x_next_bf = _call(
        _out_kernel, "out_norm", (seq // t,),
        [(merged, _row_tile(D_MODEL)), (w_out16, _layer_weight(w_out16, layer)), (x, _row_tile(D_MODEL)),
         (row(ln_g), _resident((1, D_MODEL))), (row(ln_b), _resident((1, D_MODEL))),
         (jnp.zeros((SUBLANES, LANES), jnp.int32), _resident((SUBLANES, LANES)))],
        [(jax.ShapeDtypeStruct((seq, D_MODEL), F32), _row_tile(D_MODEL)),
         (jax.ShapeDtypeStruct((seq, D_MODEL), BF16), _row_tile(D_MODEL))],
        [pltpu.VMEM((t, D_MODEL), F32)], value_bytes=_nbytes((t, D_MODEL), F32))
    return x_next, x_next_bf


def kernel(x, w_in, pool_w, pool_scale, sgu_ln_g, sgu_ln_b, sgu_w, sgu_b, attn_sinks, rel_bias, conv_w,
           conv_b, conv_ln_g, conv_ln_b, w_branch, w_out, ln_g, ln_b):
    batch, seq, d_model = x.shape
    assert d_model == D_MODEL and w_in.shape == (DEPTH, D_MODEL, D_IN)
    assert seq % SEQ_TILE == 0 and SEQ_TILE % (BLOCK * ATTN_GROUP_BLOCKS) == 0

    band = _band_bias(rel_bias).reshape(N_KV_HEADS, PAIRS, 2, BLOCK, 2 * BLOCK)
    band = jnp.transpose(band, (0, 1, 3, 2, 4)).reshape(N_KV_HEADS, PQ, 4 * BLOCK)

    w_parts = _cast_w_in(w_in.astype(F32))
    w_branch16 = w_branch.astype(BF16)
    w_out16 = w_out.astype(BF16)
    outs = []
    for bi in range(batch):
        xc = x[bi].astype(F32)
        x_in = xc
        for l in range(DEPTH):
            xc, x_in = _layer(xc, x_in, band, w_parts, l, pool_w[l], pool_scale[l], sgu_ln_g[l], sgu_ln_b[l],
                              sgu_w[l], sgu_b[l], attn_sinks[l], conv_w[l], conv_b[l], conv_ln_g[l],
                              conv_ln_b[l], w_branch16, w_out16, ln_g[l], ln_b[l])
        outs.append(xc.astype(x.dtype)[None])
    return outs[0] if batch == 1 else jnp.concatenate(outs, axis=0)
```

```python
import functools
import math

import jax
import jax.numpy as jnp
from jax import lax
from jax.experimental import pallas as pl
from jax.experimental.pallas import tpu as pltpu

F32 = jnp.float32
BF16 = jnp.bfloat16

D_MODEL = 2048
DEPTH = 2
N_BRANCH = 4
BRANCH_W = 1024
POOL_WINDOWS = (2, 4, 8, 16)
POOL_GW = BRANCH_W // len(POOL_WINDOWS)
SGU_HEADS = 8
SGU_HD = BRANCH_W // SGU_HEADS
CHUNK = 128
N_Q_HEADS = 16
N_KV_HEADS = 2
Q_PER_KV = N_Q_HEADS // N_KV_HEADS
HEAD_DIM = 64
KV_W = N_KV_HEADS * HEAD_DIM
WINDOW = 128
BLOCK = 128
NUM_BUCKETS = 32
MAX_DISTANCE = 128
CONV_K = 31
ALPHA = (2 * DEPTH) ** 0.25
LN_EPS = 1e-5
NEG_INF = -1e30

OFF_A = 0
OFF_B = OFF_A + 2 * BRANCH_W
OFF_C = OFF_B + 3 * BRANCH_W
OFF_D = OFF_C + 2 * BRANCH_W + 2 * KV_W
OFF_G = OFF_D + 3 * BRANCH_W
D_IN = OFF_G + N_BRANCH * D_MODEL

LANES = 128
SUBLANES = 8
SEQ_TILE = 512
POOL_HALO = 16
CONV_HALO = 32
ROW_CHUNK = 64
STREAM_ROWS = 512
PROJ_COLS = 256
MERGE_COLS = 512
PAIRS = Q_PER_KV // 2
PQ = PAIRS * BLOCK
ATTN_GROUP_BLOCKS = 2
MERGE_TILE = 512
V7X_VMEM_BYTES = 64 * 1024 * 1024


def _sigmoid(x):
    return 0.5 * jnp.tanh(0.5 * x) + 0.5


def _silu(x):
    return x * _sigmoid(x)


def _layer_norm_rows(v, g, b):
    mu = jnp.mean(v, axis=-1, keepdims=True)
    d = v - mu
    var = jnp.mean(d * d, axis=-1, keepdims=True)
    return d * lax.rsqrt(var + LN_EPS) * g + b


def _dot(a, b):
    return jnp.dot(a, b, preferred_element_type=F32)


def _t5_bucket(n):
    max_exact = NUM_BUCKETS // 2
    nf = jnp.maximum(n, 1).astype(F32)
    large = max_exact + (jnp.log(nf / max_exact) / math.log(MAX_DISTANCE / max_exact)
                         * (NUM_BUCKETS - max_exact)).astype(jnp.int32)
    large = jnp.minimum(large, NUM_BUCKETS - 1)
    return jnp.where(n < max_exact, n, large)


def _band_bucket_ids():
    i = jnp.arange(BLOCK)[:, None]
    j = jnp.arange(2 * BLOCK)[None, :]
    return _t5_bucket(jnp.clip(i + BLOCK - j, 0, WINDOW - 1)).astype(jnp.int32)


def _bias_kernel(bucket_ref, table_ref, o_ref):
    h = pl.program_id(0)
    bucket = bucket_ref[...]
    acc = jnp.zeros(bucket.shape, F32)
    for b in range(NUM_BUCKETS):
        acc = jnp.where(bucket == b, table_ref[b, h], acc)
    o_ref[0] = acc


def _band_bias(rel_bias):
    return pl.pallas_call(
        _bias_kernel,
        grid=(N_Q_HEADS,),
        in_specs=[pl.BlockSpec((BLOCK, 2 * BLOCK), lambda h: (0, 0)),
                  pl.BlockSpec(memory_space=pltpu.SMEM)],
        out_specs=pl.BlockSpec((1, BLOCK, 2 * BLOCK), lambda h: (h, 0, 0)),
        out_shape=jax.ShapeDtypeStruct((N_Q_HEADS, BLOCK, 2 * BLOCK), F32),
        name="band_bias",
    )(_band_bucket_ids(), rel_bias.astype(F32))


W_IN_SPLITS = ((OFF_A, OFF_B), (OFF_B, OFF_C), (OFF_C, OFF_D), (OFF_G, D_IN))
CONV_W_CHUNKS = (OFF_G - OFF_D) // PROJ_COLS
CAST_ROWS = 128


def _cast_w_in_kernel(w_ref, oa_ref, ob_ref, oc_ref, og_ref, od_ref):
    for (a, b), o_ref in zip(W_IN_SPLITS, (oa_ref, ob_ref, oc_ref, og_ref)):
        o_ref[...] = w_ref[:, a:b].astype(BF16)
    for j in range(CONV_W_CHUNKS):
        od_ref[j] = w_ref[:, OFF_D + j * PROJ_COLS:OFF_D + (j + 1) * PROJ_COLS].astype(BF16)


def _cast_w_in(w_in):
    widths = [b - a for a, b in W_IN_SPLITS]
    outputs = [(jax.ShapeDtypeStruct((DEPTH, D_MODEL, w), BF16),
                pl.BlockSpec((None, CAST_ROWS, w), lambda l, r: (l, r, 0))) for w in widths]
    outputs.append((jax.ShapeDtypeStruct((DEPTH, CONV_W_CHUNKS, D_MODEL, PROJ_COLS), BF16),
                    pl.BlockSpec((None, CONV_W_CHUNKS, CAST_ROWS, PROJ_COLS), lambda l, r: (l, 0, r, 0))))
    return _call(_cast_w_in_kernel, "cast_w_in", (DEPTH, D_MODEL // CAST_ROWS),
                 [(w_in, pl.BlockSpec((None, CAST_ROWS, D_IN), lambda l, r: (l, r, 0)))],
                 outputs, [], value_bytes=0)


def _pool_kernel(x_ref, w_ref, pw_ref, sc_ref, o_ref, abuf, gate_buf, mix_buf):
    i = pl.program_id(0)
    t_rows = x_ref.shape[0]

    @pl.when(i == 0)
    def _():
        abuf[0:POOL_HALO, :] = jnp.zeros((POOL_HALO, BRANCH_W), F32)

    xb = x_ref[...].astype(BF16)
    abuf[POOL_HALO:POOL_HALO + t_rows, :] = _dot(xb, w_ref[:, 0:BRANCH_W])
    gate_buf[...] = _dot(xb, w_ref[:, BRANCH_W:2 * BRANCH_W])

    for g, win in enumerate(POOL_WINDOWS):
        c0 = g * POOL_GW
        for r0 in range(0, t_rows, ROW_CHUNK):
            base = POOL_HALO + r0
            cur = abuf[base:base + ROW_CHUNK, c0:c0 + POOL_GW]
            acc = cur
            for s in range(1, win):
                acc = acc + abuf[base - s:base - s + ROW_CHUNK, c0:c0 + POOL_GW]
            t = i * t_rows + r0 + lax.broadcasted_iota(jnp.int32, (ROW_CHUNK, POOL_GW), 0)
            cnt = jnp.minimum(t + 1, win).astype(F32)
            mix_buf[r0:r0 + ROW_CHUNK, c0:c0 + POOL_GW] = (acc / cnt - cur).astype(BF16)

    for g in range(len(POOL_WINDOWS)):
        c0 = g * POOL_GW
        y = _dot(mix_buf[:, c0:c0 + POOL_GW], pw_ref[g])
        y = y * sc_ref[:, c0:c0 + POOL_GW] * _silu(gate_buf[:, c0:c0 + POOL_GW])
        o_ref[:, c0:c0 + POOL_GW] = y.astype(BF16)

    abuf[0:POOL_HALO, :] = abuf[t_rows:t_rows + POOL_HALO, :]


def _sgu_kernel(x_ref, w_ref, lng_ref, lnb_ref, sw_ref, sb_ref, o_ref,
                u_buf, v_buf, gate_buf, vn_buf):
    t_rows = x_ref.shape[0]
    xb = x_ref[...].astype(BF16)
    u_buf[...] = _dot(xb, w_ref[:, 0:BRANCH_W])
    v_buf[...] = _dot(xb, w_ref[:, BRANCH_W:2 * BRANCH_W])
    gate_buf[...] = _dot(xb, w_ref[:, 2 * BRANCH_W:3 * BRANCH_W])

    g = lng_ref[...]
    b = lnb_ref[...]
    for r0 in range(0, t_rows, ROW_CHUNK):
        vn = _layer_norm_rows(v_buf[r0:r0 + ROW_CHUNK, :], g, b)
        vn_buf[r0:r0 + ROW_CHUNK, :] = vn.astype(BF16)

    row = lax.broadcasted_iota(jnp.int32, (CHUNK, CHUNK), 0)
    col = lax.broadcasted_iota(jnp.int32, (CHUNK, CHUNK), 1)
    causal = row >= col
    for h in range(SGU_HEADS):
        c0 = h * SGU_HD
        w_h = jnp.where(causal, sw_ref[h], 0.0).astype(BF16)
        bias_h = sb_ref[:, c0:c0 + SGU_HD]
        for r0 in range(0, t_rows, CHUNK):
            sp = _dot(w_h, vn_buf[r0:r0 + CHUNK, c0:c0 + SGU_HD]) + bias_h
            y = u_buf[r0:r0 + CHUNK, c0:c0 + SGU_HD] * sp * _silu(gate_buf[r0:r0 + CHUNK, c0:c0 + SGU_HD])
            o_ref[r0:r0 + CHUNK, c0:c0 + SGU_HD] = y.astype(BF16)


def _attn_kernel(x_ref, w_ref, bias_ref, sink_ref, sinkc_ref, zero_ref, o_ref,
                 xs_buf, q_buf, gate_buf, k_tab, v_tab, code_buf, ones_buf, s_buf, p_buf, m_buf, mc_buf):
    i = pl.program_id(0)
    t_rows = x_ref.shape[0]
    n_blocks = t_rows // BLOCK

    @pl.when(i == 0)
    def _():
        zero = jnp.zeros((BLOCK, LANES), BF16)
        for j in range(N_KV_HEADS):
            for half in range(2):
                k_tab[j, half, 0:BLOCK, :] = zero
                v_tab[j, half, 0:BLOCK, :] = zero
        row = lax.broadcasted_iota(jnp.int32, (PQ, 2 * 2 * BLOCK), 0)
        col = lax.broadcasted_iota(jnp.int32, (PQ, 2 * 2 * BLOCK), 1)
        key = col & (2 * BLOCK - 1)
        u = key - (row & (BLOCK - 1)) - 1
        in_window = (u >= 0) & (u < WINDOW)
        code_buf[...] = jnp.where(in_window, jnp.where(key >= BLOCK, 2, 1), 0)
        krow = lax.broadcasted_iota(jnp.int32, (4 * BLOCK, LANES), 0)
        klane = lax.broadcasted_iota(jnp.int32, (4 * BLOCK, LANES), 1)
        ones_buf[...] = jnp.where((krow >= 2 * BLOCK) == (klane >= HEAD_DIM), 1.0, 0.0).astype(BF16)

    g_len = ATTN_GROUP_BLOCKS * BLOCK
    xs_buf[...] = x_ref[...].astype(BF16)
    low = lax.broadcasted_iota(jnp.int32, (g_len, LANES), 1) < HEAD_DIM
    gate_col0 = BRANCH_W + 2 * KV_W

    def project_group(g0, after):
        g_rows = slice(g0 * BLOCK, g0 * BLOCK + g_len)
        xg = xs_buf[g_rows, :]
        if after is not None:
            nothing = pltpu.bitcast(after, jnp.int32) & zero_ref[...]
            nothing = jnp.concatenate([nothing] * (g_len // (2 * SUBLANES)), axis=0)
            nothing = jnp.concatenate([nothing] * (D_MODEL // LANES), axis=1)
            xg = pltpu.bitcast(pltpu.bitcast(xg, jnp.int32) | nothing, BF16)
        for piece in range(BRANCH_W // PROJ_COLS):
            c0 = piece * PROJ_COLS
            q = _dot(xg, w_ref[:, c0:c0 + PROJ_COLS]) * (HEAD_DIM ** -0.5)
            q_buf[g_rows, c0:c0 + PROJ_COLS] = q.astype(BF16)
        kv = _dot(xg, w_ref[:, BRANCH_W:BRANCH_W + 2 * KV_W])
        t_rows_g = slice(BLOCK + g0 * BLOCK, BLOCK + g0 * BLOCK + g_len)
        for tab, c0 in ((k_tab, 0), (v_tab, KV_W)):
            val = kv[:, c0:c0 + KV_W]
            swapped = pltpu.roll(val, HEAD_DIM, axis=1)
            zero = jnp.zeros_like(val)
            tab[0, 0, t_rows_g, :] = jnp.where(low, val, zero).astype(BF16)
            tab[0, 1, t_rows_g, :] = jnp.where(low, zero, swapped).astype(BF16)
            tab[1, 0, t_rows_g, :] = jnp.where(low, swapped, zero).astype(BF16)
            tab[1, 1, t_rows_g, :] = jnp.where(low, zero, val).astype(BF16)
        for piece in range(BRANCH_W // PROJ_COLS):
            c0 = piece * PROJ_COLS
            gate_buf[g_rows, c0:c0 + PROJ_COLS] = _dot(xg, w_ref[:, gate_col0 + c0:gate_col0 + c0 + PROJ_COLS])

    low128 = lax.broadcasted_iota(jnp.int32, (BLOCK, LANES), 1) < HEAD_DIM
    project_group(0, None)
    for g0 in range(0, n_blocks, ATTN_GROUP_BLOCKS):
        units = [(n, j) for n in range(g0, g0 + ATTN_GROUP_BLOCKS) for j in range(N_KV_HEADS)]

        for it, (n, j) in enumerate(units):
            r0 = n * BLOCK
            q_cols = j * PAIRS * LANES
            qs = jnp.concatenate(
                [q_buf[r0:r0 + BLOCK, q_cols + p * LANES:q_cols + (p + 1) * LANES] for p in range(PAIRS)], axis=0)
            k2 = jnp.concatenate([k_tab[j, 0, r0:r0 + 2 * BLOCK, :], k_tab[j, 1, r0:r0 + 2 * BLOCK, :]], axis=0)
            logits = lax.dot_general(qs, k2, (((1,), (1,)), ((), ())), preferred_element_type=F32)
            threshold = jnp.where(jnp.logical_and(i == 0, n == 0), 2, 1)
            for p in range(PAIRS):
                rows = slice(p * BLOCK, (p + 1) * BLOCK)
                row_max = []
                for par in range(2):
                    slabs = []
                    for half in range(2):
                        cols = slice((2 * par + half) * LANES, (2 * par + half + 1) * LANES)
                        s = logits[rows, cols] + bias_ref[j, rows, cols]
                        s = jnp.where(code_buf[rows, cols] >= threshold, s, NEG_INF)
                        s_buf[it, rows, cols] = s
                        slabs.append(s)
                    m = jnp.max(jnp.maximum(slabs[0], slabs[1]), axis=1, keepdims=True)
                    m = jnp.maximum(jnp.broadcast_to(m, (BLOCK, LANES)), sink_ref[j, par, rows, :])
                    m_buf[it, par, rows, :] = m
                    row_max.append(m)
                mc_buf[it, rows, :] = jnp.where(low128, row_max[0], row_max[1])

        if g0 + ATTN_GROUP_BLOCKS < n_blocks:
            project_group(g0 + ATTN_GROUP_BLOCKS, logits[0:SUBLANES, 0:LANES])

        for it, (n, j) in enumerate(units):
            for p in range(PAIRS):
                rows = slice(p * BLOCK, (p + 1) * BLOCK)
                for par in range(2):
                    m = m_buf[it, par, rows, :]
                    for half in range(2):
                        cols = slice((2 * par + half) * LANES, (2 * par + half + 1) * LANES)
                        p_buf[it, rows, cols] = jnp.exp(s_buf[it, rows, cols] - m).astype(BF16)

        for it, (n, j) in enumerate(units):
            r0 = n * BLOCK
            q_cols = j * PAIRS * LANES
            v2 = jnp.concatenate([v_tab[j, 0, r0:r0 + 2 * BLOCK, :], v_tab[j, 1, r0:r0 + 2 * BLOCK, :]], axis=0)
            o = _dot(p_buf[it], jnp.concatenate([v2, ones_buf[...]], axis=1))
            for p in range(PAIRS):
                rows = slice(p * BLOCK, (p + 1) * BLOCK)
                cols = slice(q_cols + p * LANES, q_cols + (p + 1) * LANES)
                den = o[rows, LANES:2 * LANES] + jnp.exp(sinkc_ref[j, rows, :] - mc_buf[it, rows, :])
                y = o[rows, 0:LANES] * (1.0 / den) * _silu(gate_buf[r0:r0 + BLOCK, cols])
                o_ref[r0:r0 + BLOCK, cols] = y.astype(BF16)

    for tab in (k_tab, v_tab):
        for j in range(N_KV_HEADS):
            for half in range(2):
                tab[j, half, 0:BLOCK, :] = tab[j, half, t_rows:t_rows + BLOCK, :]


CONV_COLUMNS = BRANCH_W // LANES
CONV_BLOCKS = BRANCH_W // PROJ_COLS


def _conv_stream_kernel(x_ref, w_ref, cw_ref, cb_ref, lng_ref, lnb_ref, zero_ref, o_ref,
                        xs_buf, glu_buf, gate_buf, shift_buf, y_buf):
    i = pl.program_id(0)
    t_rows = x_ref.shape[0]

    @pl.when(i == 0)
    def _():
        for c in range(CONV_COLUMNS):
            glu_buf[c, 0:CONV_HALO, :] = jnp.zeros((CONV_HALO, LANES), F32)

    xs_buf[...] = x_ref[...].astype(BF16)
    first = CONV_HALO - (CONV_K - 1)
    span = t_rows + CONV_HALO - SUBLANES

    def glu_block(n):
        gate_val = _dot(xs_buf[...], w_ref[CONV_BLOCKS + n])
        glu = _dot(xs_buf[...], w_ref[n]) * _sigmoid(gate_val)
        glu_buf[2 * n, CONV_HALO:CONV_HALO + t_rows, :] = glu[:, 0:LANES]
        glu_buf[2 * n + 1, CONV_HALO:CONV_HALO + t_rows, :] = glu[:, LANES:2 * LANES]
        return gate_val[t_rows - SUBLANES:t_rows, 0:LANES]

    def gate_block(n):
        gate = _dot(xs_buf[...], w_ref[2 * CONV_BLOCKS + n])
        gate_buf[2 * n] = gate[:, 0:LANES]
        gate_buf[2 * n + 1] = gate[:, LANES:2 * LANES]
        return gate[t_rows - SUBLANES:t_rows, 0:LANES]

    def conv_block(n, token, second_trigger):
        for half in range(2):
            c = 2 * n + half
            if half == 1:
                token = token + second_trigger()
            for r in range(1, SUBLANES):
                shift_buf[half, r, 0:span, :] = glu_buf[c, r:r + span, :]
            bias = jnp.broadcast_to(cb_ref[c], (STREAM_ROWS, LANES))
            for r0 in range(0, t_rows, STREAM_ROWS):
                nothing = pltpu.bitcast(pltpu.bitcast(token, jnp.int32) & zero_ref[...], F32)
                acc = bias
                for k in range(CONV_K):
                    a8 = (first + k) // SUBLANES * SUBLANES
                    r = (first + k) % SUBLANES
                    if r == 0:
                        src = glu_buf[c, r0 + a8:r0 + a8 + STREAM_ROWS, :]
                    else:
                        src = shift_buf[half, r, r0 + a8:r0 + a8 + STREAM_ROWS, :]
                    tap = jnp.concatenate([cw_ref[c, k] + nothing] * (STREAM_ROWS // SUBLANES), axis=0)
                    acc = acc + tap * src
                y_buf[c, r0:r0 + STREAM_ROWS, :] = acc

    for n in range(CONV_BLOCKS):
        token = glu_block(n)
        if n == 0:
            conv_block(n, token, lambda: token)
        else:
            conv_block(n, token, lambda n=n: gate_block(n - 1))
    gate_block(CONV_BLOCKS - 1)

    for r0 in range(0, t_rows, ROW_CHUNK):
        rows = slice(r0, r0 + ROW_CHUNK)
        ys = [y_buf[c, rows, :] for c in range(CONV_COLUMNS)]
        total = ys[0]
        for y in ys[1:]:
            total = total + y
        mu = jnp.sum(total, axis=1, keepdims=True) * (1.0 / BRANCH_W)
        ds = [y - mu for y in ys]
        sq = ds[0] * ds[0]
        for d in ds[1:]:
            sq = sq + d * d
        rstd = lax.rsqrt(jnp.sum(sq, axis=1, keepdims=True) * (1.0 / BRANCH_W) + LN_EPS)
        for c in range(CONV_COLUMNS):
            y = _silu(ds[c] * rstd * lng_ref[c] + lnb_ref[c]) * _silu(gate_buf[c, rows, :])
            o_ref[rows, c * LANES:(c + 1) * LANES] = y.astype(BF16)

    for c in range(CONV_COLUMNS):
        glu_buf[c, 0:CONV_HALO, :] = glu_buf[c, t_rows:t_rows + CONV_HALO, :]


def _merge_kernel(x_ref, ya_ref, yb_ref, yc_ref, yd_ref, wga_ref, wgb_ref, wgc_ref, wgd_ref, wb_ref, o_ref):
    xb = x_ref[...].astype(BF16)
    acc = None
    branches = ((ya_ref, wga_ref), (yb_ref, wgb_ref), (yc_ref, wgc_ref), (yd_ref, wgd_ref))
    for br, (y_ref, wg_ref) in enumerate(branches):
        gate = _sigmoid(_dot(xb, wg_ref[...]))
        term = gate * _dot(y_ref[...], wb_ref[br])
        acc = term if acc is None else acc + term
    o_ref[...] = acc.astype(BF16)


def _out_kernel(m_ref, w_ref, x_ref, g_ref, b_ref, zero_ref, o_ref, ob_ref, z_buf):
    t_rows = x_ref.shape[0]
    half_rows = t_rows // 2
    chunk = ROW_CHUNK // 2
    tokens = []
    for h in range(2):
        rows = slice(h * half_rows, (h + 1) * half_rows)
        z = _dot(m_ref[rows, :], w_ref[...])
        z_buf[rows, :] = z
        tokens.append(z[half_rows - SUBLANES:half_rows, 0:LANES])
    g = g_ref[...]
    b = b_ref[...]
    for h in range(2):
        nothing = pltpu.bitcast(pltpu.bitcast(tokens[h], jnp.int32) & zero_ref[...], F32)
        nothing = jnp.concatenate([nothing] * (chunk // SUBLANES), axis=0)
        nothing = jnp.concatenate([nothing] * (D_MODEL // LANES), axis=1)
        for r0 in range(h * half_rows, (h + 1) * half_rows, chunk):
            rows = slice(r0, r0 + chunk)
            y = _layer_norm_rows(ALPHA * x_ref[rows, :] + (z_buf[rows, :] + nothing), g, b)
            o_ref[rows, :] = y
            ob_ref[rows, :] = y.astype(BF16)


def _resident(shape):
    zeros = (0,) * len(shape)
    return pl.BlockSpec(shape, lambda *_: zeros, pipeline_mode=pl.Buffered(1))


def _nbytes(shape, dtype):
    return math.prod(d for d in shape if d is not None) * jnp.dtype(dtype).itemsize


def _call(kernel_fn, name, grid, inputs, outputs, scratch, value_bytes):
    arrays, in_specs = zip(*inputs)
    out_shape, out_specs = zip(*outputs)
    windows = [(spec, a.dtype) for a, spec in inputs] + [(spec, o.dtype) for o, spec in outputs]
    need = value_bytes + sum(_nbytes(v.shape, v.dtype) for v in scratch)
    for spec, dtype in windows:
        buffers = spec.pipeline_mode.buffer_count if spec.pipeline_mode is not None else 2
        need += buffers * _nbytes(spec.block_shape, dtype)
    assert need <= V7X_VMEM_BYTES, (name, need)
    return pl.pallas_call(
        kernel_fn,
        grid=grid,
        in_specs=list(in_specs),
        out_specs=list(out_specs) if len(out_specs) > 1 else out_specs[0],
        out_shape=list(out_shape) if len(out_shape) > 1 else out_shape[0],
        scratch_shapes=scratch,
        compiler_params=pltpu.CompilerParams(dimension_semantics=("arbitrary",) * len(grid),
                                             vmem_limit_bytes=need),
        name=name,
    )(*arrays)


def _row_tile(width, rows=SEQ_TILE):
    return pl.BlockSpec((rows, width), lambda i: (i, 0))


def _layer_weight(w_all, layer):
    idx = (layer,) + (0,) * (w_all.ndim - 1)
    return pl.BlockSpec((None,) + w_all.shape[1:], lambda *_: idx, pipeline_mode=pl.Buffered(1))


def _branch_call(kernel_fn, name, seq, x_in, w_all, layer, operands, scratch, value_tiles=2):
    value_bytes = value_tiles * _nbytes((SEQ_TILE, BRANCH_W), F32)
    inputs = [(x_in, _row_tile(D_MODEL)), (w_all, _layer_weight(w_all, layer))]
    inputs += [(op, _resident(op.shape)) for op in operands]
    outputs = [(jax.ShapeDtypeStruct((seq, BRANCH_W), BF16), _row_tile(BRANCH_W))]
    return _call(kernel_fn, name, (seq // SEQ_TILE,), inputs, outputs, scratch, value_bytes)


def _layer(x, x_in, band, w_parts, layer, pool_w, pool_scale, sgu_ln_g, sgu_ln_b, sgu_w, sgu_b, sinks,
           conv_w, conv_b, conv_ln_g, conv_ln_b, w_branch16, w_out16, ln_g, ln_b):
    seq = x.shape[0]
    t = SEQ_TILE
    w_pool, w_sgu, w_attn, w_gates, w_conv = w_parts
    row = lambda v: v.reshape(1, -1).astype(F32)

    y_a = _branch_call(
        _pool_kernel, "branch_pool", seq, x_in, w_pool, layer,
        [pool_w.astype(BF16), row(pool_scale)],
        [pltpu.VMEM((t + POOL_HALO, BRANCH_W), F32), pltpu.VMEM((t, BRANCH_W), F32),
         pltpu.VMEM((t, BRANCH_W), BF16)])

    sgu_bias = jnp.repeat(jnp.transpose(sgu_b).astype(F32), SGU_HD, axis=1)
    y_b = _branch_call(
        _sgu_kernel, "branch_sgu", seq, x_in, w_sgu, layer,
        [row(sgu_ln_g), row(sgu_ln_b), sgu_w.astype(F32), sgu_bias],
        [pltpu.VMEM((t, BRANCH_W), F32), pltpu.VMEM((t, BRANCH_W), F32),
         pltpu.VMEM((t, BRANCH_W), F32), pltpu.VMEM((t, BRANCH_W), BF16)])

    sink_hp = sinks.astype(F32).reshape(N_KV_HEADS, PAIRS, 2)
    sink_rows = jnp.repeat(jnp.transpose(sink_hp, (0, 2, 1)), BLOCK, axis=2)
    sink_rows = jnp.broadcast_to(sink_rows[..., None], (N_KV_HEADS, 2, PQ, LANES))
    sink_lanes = jnp.repeat(jnp.repeat(sink_hp, BLOCK, axis=1), HEAD_DIM, axis=2)
    n_units = N_KV_HEADS * ATTN_GROUP_BLOCKS
    y_c = _branch_call(
        _attn_kernel, "branch_attn", seq, x_in, w_attn, layer,
        [band, sink_rows, sink_lanes, jnp.zeros((SUBLANES, LANES), jnp.int32)],
        [pltpu.VMEM((t, D_MODEL), BF16), pltpu.VMEM((t, BRANCH_W), BF16), pltpu.VMEM((t, BRANCH_W), F32),
         pltpu.VMEM((N_KV_HEADS, 2, t + BLOCK, LANES), BF16),
         pltpu.VMEM((N_KV_HEADS, 2, t + BLOCK, LANES), BF16),
         pltpu.VMEM((PQ, 4 * BLOCK), jnp.int32),
         pltpu.VMEM((4 * BLOCK, LANES), BF16),
         pltpu.VMEM((n_units, PQ, 4 * BLOCK), F32),
         pltpu.VMEM((n_units, PQ, 4 * BLOCK), BF16),
         pltpu.VMEM((n_units, 2, PQ, LANES), F32),
         pltpu.VMEM((n_units, PQ, LANES), F32)])

    per_col = lambda v: v.astype(F32).reshape(CONV_COLUMNS, 1, LANES)
    taps = jnp.transpose(conv_w.astype(F32).reshape(CONV_K, CONV_COLUMNS, LANES), (1, 0, 2))
    taps = jnp.broadcast_to(taps[:, :, None, :], (CONV_COLUMNS, CONV_K, SUBLANES, LANES))
    y_d = _branch_call(
        _conv_stream_kernel, "branch_conv", seq, x_in, w_conv, layer,
        [taps, per_col(conv_b), per_col(conv_ln_g), per_col(conv_ln_b), jnp.zeros((SUBLANES, LANES), jnp.int32)],
        [pltpu.VMEM((t, D_MODEL), BF16),
         pltpu.VMEM((CONV_COLUMNS, t + CONV_HALO, LANES), F32),
         pltpu.VMEM((CONV_COLUMNS, t, LANES), F32),
         pltpu.VMEM((2, SUBLANES, t + CONV_HALO - SUBLANES, LANES), F32),
         pltpu.VMEM((CONV_COLUMNS, t, LANES), F32)], value_tiles=4)

    n_cb = D_MODEL // MERGE_COLS
    tm = MERGE_TILE
    x_spec = pl.BlockSpec((tm, D_MODEL), lambda c, i: (i, 0))
    y_spec = pl.BlockSpec((tm, BRANCH_W), lambda c, i: (i, 0))
    wg_specs = [pl.BlockSpec((None, D_MODEL, MERGE_COLS),
                             functools.partial(lambda br, c, i: (layer, 0, br * n_cb + c), br))
                for br in range(N_BRANCH)]
    wb_spec = pl.BlockSpec((None, N_BRANCH, BRANCH_W, MERGE_COLS), lambda c, i: (layer, 0, 0, c))
    merged = _call(
        _merge_kernel, "merge", (n_cb, seq // tm),
        [(x_in, x_spec)] + [(y, y_spec) for y in (y_a, y_b, y_c, y_d)]
        + [(w_gates, spec) for spec in wg_specs] + [(w_branch16, wb_spec)],
        [(jax.ShapeDtypeStruct((seq, D_MODEL), BF16), pl.BlockSpec((tm, MERGE_COLS), lambda c, i: (i, c)))],
        [], value_bytes=4 * _nbytes((tm, MERGE_COLS), F32))

    x_next, x_next_bf = _call(
        _out_kernel, "out_norm", (seq // t,),
        [(merged, _row_tile(D_MODEL)), (w_out16, _layer_weight(w_out16, layer)), (x, _row_tile(D_MODEL)),
         (row(ln_g), _resident((1, D_MODEL))), (row(ln_b), _resident((1, D_MODEL))),
         (jnp.zeros((SUBLANES, LANES), jnp.int32), _resident((SUBLANES, LANES)))],
        [(jax.ShapeDtypeStruct((seq, D_MODEL), F32), _row_tile(D_MODEL)),
         (jax.ShapeDtypeStruct((seq, D_MODEL), BF16), _row_tile(D_MODEL))],
        [pltpu.VMEM((t, D_MODEL), F32)], value_bytes=_nbytes((t, D_MODEL), F32))
    return x_next, x_next_bf


def kernel(x, w_in, pool_w, pool_scale, sgu_ln_g, sgu_ln_b, sgu_w, sgu_b, attn_sinks, rel_bias, conv_w,
           conv_b, conv_ln_g, conv_ln_b, w_branch, w_out, ln_g, ln_b):
    batch, seq, d_model = x.shape
    assert d_model == D_MODEL and w_in.shape == (DEPTH, D_MODEL, D_IN)
    assert seq % SEQ_TILE == 0 and SEQ_TILE % (BLOCK * ATTN_GROUP_BLOCKS) == 0

    band = _band_bias(rel_bias).reshape(N_KV_HEADS, PAIRS, 2, BLOCK, 2 * BLOCK)
    band = jnp.transpose(band, (0, 1, 3, 2, 4)).reshape(N_KV_HEADS, PQ, 4 * BLOCK)

    w_parts = _cast_w_in(w_in.astype(F32))
    w_branch16 = w_branch.astype(BF16)
    w_out16 = w_out.astype(BF16)
    outs = []
    for bi in range(batch):
        xc = x[bi].astype(F32)
        x_in = xc
        for l in range(DEPTH):
            xc, x_in = _layer(xc, x_in, band, w_parts, l, pool_w[l], pool_scale[l], sgu_ln_g[l], sgu_ln_b[l],
                              sgu_w[l], sgu_b[l], attn_sinks[l], conv_w[l], conv_b[l], conv_ln_g[l],
                              conv_ln_b[l], w_branch16, w_out16, ln_g[l], ln_b[l])
        outs.append(xc.astype(x.dtype)[None])
    return outs[0] if batch == 1 else jnp.concatenate(outs, axis=0)
```

```python
import functools
import math

import jax
import jax.numpy as jnp
from jax import lax
from jax.experimental import pallas as pl
from jax.experimental.pallas import tpu as pltpu

F32 = jnp.float32
BF16 = jnp.bfloat16

D_MODEL = 2048
DEPTH = 2
N_BRANCH = 4
BRANCH_W = 1024
POOL_WINDOWS = (2, 4, 8, 16)
POOL_GW = BRANCH_W // len(POOL_WINDOWS)
SGU_HEADS = 8
SGU_HD = BRANCH_W // SGU_HEADS
CHUNK = 128
N_Q_HEADS = 16
N_KV_HEADS = 2
Q_PER_KV = N_Q_HEADS // N_KV_HEADS
HEAD_DIM = 64
KV_W = N_KV_HEADS * HEAD_DIM
WINDOW = 128
BLOCK = 128
NUM_BUCKETS = 32
MAX_DISTANCE = 128
CONV_K = 31
ALPHA = (2 * DEPTH) ** 0.25
LN_EPS = 1e-5
NEG_INF = -1e30

OFF_A = 0
OFF_B = OFF_A + 2 * BRANCH_W
OFF_C = OFF_B + 3 * BRANCH_W
OFF_D = OFF_C + 2 * BRANCH_W + 2 * KV_W
OFF_G = OFF_D + 3 * BRANCH_W
D_IN = OFF_G + N_BRANCH * D_MODEL

LANES = 128
SUBLANES = 8
SEQ_TILE = 512
POOL_HALO = 16
CONV_HALO = 32
ROW_CHUNK = 64
STREAM_ROWS = 512
PROJ_COLS = 256
MERGE_COLS = 512
PAIRS = Q_PER_KV // 2
PQ = PAIRS * BLOCK
ATTN_GROUP_BLOCKS = 2
MERGE_TILE = 512
V7X_VMEM_BYTES = 64 * 1024 * 1024


def _sigmoid(x):
    return 0.5 * jnp.tanh(0.5 * x) + 0.5


def _silu(x):
    return x * _sigmoid(x)


def _layer_norm_rows(v, g, b):
    mu = jnp.mean(v, axis=-1, keepdims=True)
    d = v - mu
    var = jnp.mean(d * d, axis=-1, keepdims=True)
    return d * lax.rsqrt(var + LN_EPS) * g + b


def _dot(a, b):
    return jnp.dot(a, b, preferred_element_type=F32)


def _t5_bucket(n):
    max_exact = NUM_BUCKETS // 2
    nf = jnp.maximum(n, 1).astype(F32)
    large = max_exact + (jnp.log(nf / max_exact) / math.log(MAX_DISTANCE / max_exact)
                         * (NUM_BUCKETS - max_exact)).astype(jnp.int32)
    large = jnp.minimum(large, NUM_BUCKETS - 1)
    return jnp.where(n < max_exact, n, large)


def _band_bucket_ids():
    i = jnp.arange(BLOCK)[:, None]
    j = jnp.arange(2 * BLOCK)[None, :]
    return _t5_bucket(jnp.clip(i + BLOCK - j, 0, WINDOW - 1)).astype(jnp.int32)


def _bias_kernel(bucket_ref, table_ref, o_ref):
    h = pl.program_id(0)
    bucket = bucket_ref[...]
    acc = jnp.zeros(bucket.shape, F32)
    for b in range(NUM_BUCKETS):
        acc = jnp.where(bucket == b, table_ref[b, h], acc)
    o_ref[0] = acc


def _band_bias(rel_bias):
    return pl.pallas_call(
        _bias_kernel,
        grid=(N_Q_HEADS,),
        in_specs=[pl.BlockSpec((BLOCK, 2 * BLOCK), lambda h: (0, 0)),
                  pl.BlockSpec(memory_space=pltpu.SMEM)],
        out_specs=pl.BlockSpec((1, BLOCK, 2 * BLOCK), lambda h: (h, 0, 0)),
        out_shape=jax.ShapeDtypeStruct((N_Q_HEADS, BLOCK, 2 * BLOCK), F32),
        name="band_bias",
    )(_band_bucket_ids(), rel_bias.astype(F32))


W_IN_SPLITS = ((OFF_A, OFF_B), (OFF_B, OFF_C), (OFF_C, OFF_D), (OFF_G, D_IN))
CONV_W_CHUNKS = (OFF_G - OFF_D) // PROJ_COLS
CAST_ROWS = 128


def _cast_w_in_kernel(w_ref, oa_ref, ob_ref, oc_ref, og_ref, od_ref):
    for (a, b), o_ref in zip(W_IN_SPLITS, (oa_ref, ob_ref, oc_ref, og_ref)):
        o_ref[...] = w_ref[:, a:b].astype(BF16)
    for j in range(CONV_W_CHUNKS):
        od_ref[j] = w_ref[:, OFF_D + j * PROJ_COLS:OFF_D + (j + 1) * PROJ_COLS].astype(BF16)


def _cast_w_in(w_in):
    widths = [b - a for a, b in W_IN_SPLITS]
    outputs = [(jax.ShapeDtypeStruct((DEPTH, D_MODEL, w), BF16),
                pl.BlockSpec((None, CAST_ROWS, w), lambda l, r: (l, r, 0))) for w in widths]
    outputs.append((jax.ShapeDtypeStruct((DEPTH, CONV_W_CHUNKS, D_MODEL, PROJ_COLS), BF16),
                    pl.BlockSpec((None, CONV_W_CHUNKS, CAST_ROWS, PROJ_COLS), lambda l, r: (l, 0, r, 0))))
    return _call(_cast_w_in_kernel, "cast_w_in", (DEPTH, D_MODEL // CAST_ROWS),
                 [(w_in, pl.BlockSpec((None, CAST_ROWS, D_IN), lambda l, r: (l, r, 0)))],
                 outputs, [], value_bytes=0)


def _pool_kernel(x_ref, w_ref, pw_ref, sc_ref, o_ref, abuf, gate_buf, mix_buf):
    i = pl.program_id(0)
    t_rows = x_ref.shape[0]

    @pl.when(i == 0)
    def _():
        abuf[0:POOL_HALO, :] = jnp.zeros((POOL_HALO, BRANCH_W), F32)

    xb = x_ref[...].astype(BF16)
    abuf[POOL_HALO:POOL_HALO + t_rows, :] = _dot(xb, w_ref[:, 0:BRANCH_W])
    gate_buf[...] = _dot(xb, w_ref[:, BRANCH_W:2 * BRANCH_W])

    for g, win in enumerate(POOL_WINDOWS):
        c0 = g * POOL_GW
        for r0 in range(0, t_rows, ROW_CHUNK):
            base = POOL_HALO + r0
            cur = abuf[base:base + ROW_CHUNK, c0:c0 + POOL_GW]
            acc = cur
            for s in range(1, win):
                acc = acc + abuf[base - s:base - s + ROW_CHUNK, c0:c0 + POOL_GW]
            t = i * t_rows + r0 + lax.broadcasted_iota(jnp.int32, (ROW_CHUNK, POOL_GW), 0)
            cnt = jnp.minimum(t + 1, win).astype(F32)
            mix_buf[r0:r0 + ROW_CHUNK, c0:c0 + POOL_GW] = (acc / cnt - cur).astype(BF16)

    for g in range(len(POOL_WINDOWS)):
        c0 = g * POOL_GW
        y = _dot(mix_buf[:, c0:c0 + POOL_GW], pw_ref[g])
        y = y * sc_ref[:, c0:c0 + POOL_GW] * _silu(gate_buf[:, c0:c0 + POOL_GW])
        o_ref[:, c0:c0 + POOL_GW] = y.astype(BF16)

    abuf[0:POOL_HALO, :] = abuf[t_rows:t_rows + POOL_HALO, :]


def _sgu_kernel(x_ref, w_ref, lng_ref, lnb_ref, sw_ref, sb_ref, o_ref,
                u_buf, v_buf, gate_buf, vn_buf):
    t_rows = x_ref.shape[0]
    xb = x_ref[...].astype(BF16)
    u_buf[...] = _dot(xb, w_ref[:, 0:BRANCH_W])
    v_buf[...] = _dot(xb, w_ref[:, BRANCH_W:2 * BRANCH_W])
    gate_buf[...] = _dot(xb, w_ref[:, 2 * BRANCH_W:3 * BRANCH_W])

    g = lng_ref[...]
    b = lnb_ref[...]
    for r0 in range(0, t_rows, ROW_CHUNK):
        vn = _layer_norm_rows(v_buf[r0:r0 + ROW_CHUNK, :], g, b)
        vn_buf[r0:r0 + ROW_CHUNK, :] = vn.astype(BF16)

    row = lax.broadcasted_iota(jnp.int32, (CHUNK, CHUNK), 0)
    col = lax.broadcasted_iota(jnp.int32, (CHUNK, CHUNK), 1)
    causal = row >= col
    for h in range(SGU_HEADS):
        c0 = h * SGU_HD
        w_h = jnp.where(causal, sw_ref[h], 0.0).astype(BF16)
        bias_h = sb_ref[:, c0:c0 + SGU_HD]
        for r0 in range(0, t_rows, CHUNK):
            sp = _dot(w_h, vn_buf[r0:r0 + CHUNK, c0:c0 + SGU_HD]) + bias_h
            y = u_buf[r0:r0 + CHUNK, c0:c0 + SGU_HD] * sp * _silu(gate_buf[r0:r0 + CHUNK, c0:c0 + SGU_HD])
            o_ref[r0:r0 + CHUNK, c0:c0 + SGU_HD] = y.astype(BF16)


def _pool_sgu_kernel(x_ref, wp_ref, ws_ref, pw_ref, sc_ref, lng_ref, lnb_ref, sw_ref, sb_ref, oa_ref, ob_ref,
                     abuf, pool_gate_buf, mix_buf, u_buf, v_buf, gate_buf, vn_buf):
    _pool_kernel(x_ref, wp_ref, pw_ref, sc_ref, oa_ref, abuf, pool_gate_buf, mix_buf)
    _sgu_kernel(x_ref, ws_ref, lng_ref, lnb_ref, sw_ref, sb_ref, ob_ref, u_buf, v_buf, gate_buf, vn_buf)


def _attn_kernel(x_ref, w_ref, bias_ref, sink_ref, sinkc_ref, zero_ref, o_ref,
                 xs_buf, q_buf, gate_buf, k_tab, v_tab, code_buf, ones_buf, s_buf, p_buf, m_buf, mc_buf):
    i = pl.program_id(0)
    t_rows = x_ref.shape[0]
    n_blocks = t_rows // BLOCK

    @pl.when(i == 0)
    def _():
        zero = jnp.zeros((BLOCK, LANES), BF16)
        for j in range(N_KV_HEADS):
            for half in range(2):
                k_tab[j, half, 0:BLOCK, :] = zero
                v_tab[j, half, 0:BLOCK, :] = zero
        row = lax.broadcasted_iota(jnp.int32, (PQ, 2 * 2 * BLOCK), 0)
        col = lax.broadcasted_iota(jnp.int32, (PQ, 2 * 2 * BLOCK), 1)
        key = col & (2 * BLOCK - 1)
        u = key - (row & (BLOCK - 1)) - 1
        in_window = (u >= 0) & (u < WINDOW)
        code_buf[...] = jnp.where(in_window, jnp.where(key >= BLOCK, 2, 1), 0)
        krow = lax.broadcasted_iota(jnp.int32, (4 * BLOCK, LANES), 0)
        klane = lax.broadcasted_iota(jnp.int32, (4 * BLOCK, LANES), 1)
        ones_buf[...] = jnp.where((krow >= 2 * BLOCK) == (klane >= HEAD_DIM), 1.0, 0.0).astype(BF16)

    g_len = ATTN_GROUP_BLOCKS * BLOCK
    xs_buf[...] = x_ref[...].astype(BF16)
    low = lax.broadcasted_iota(jnp.int32, (g_len, LANES), 1) < HEAD_DIM
    gate_col0 = BRANCH_W + 2 * KV_W

    def project_group(g0, after):
        g_rows = slice(g0 * BLOCK, g0 * BLOCK + g_len)
        xg = xs_buf[g_rows, :]
        if after is not None:
            nothing = pltpu.bitcast(after, jnp.int32) & zero_ref[...]
            nothing = jnp.concatenate([nothing] * (g_len // (2 * SUBLANES)), axis=0)
            nothing = jnp.concatenate([nothing] * (D_MODEL // LANES), axis=1)
            xg = pltpu.bitcast(pltpu.bitcast(xg, jnp.int32) | nothing, BF16)
        for piece in range(BRANCH_W // PROJ_COLS):
            c0 = piece * PROJ_COLS
            q = _dot(xg, w_ref[:, c0:c0 + PROJ_COLS]) * (HEAD_DIM ** -0.5)
            q_buf[g_rows, c0:c0 + PROJ_COLS] = q.astype(BF16)
        kv = _dot(xg, w_ref[:, BRANCH_W:BRANCH_W + 2 * KV_W])
        t_rows_g = slice(BLOCK + g0 * BLOCK, BLOCK + g0 * BLOCK + g_len)
        for tab, c0 in ((k_tab, 0), (v_tab, KV_W)):
            val = kv[:, c0:c0 + KV_W]
            swapped = pltpu.roll(val, HEAD_DIM, axis=1)
            zero = jnp.zeros_like(val)
            tab[0, 0, t_rows_g, :] = jnp.where(low, val, zero).astype(BF16)
            tab[0, 1, t_rows_g, :] = jnp.where(low, zero, swapped).astype(BF16)
            tab[1, 0, t_rows_g, :] = jnp.where(low, swapped, zero).astype(BF16)
            tab[1, 1, t_rows_g, :] = jnp.where(low, zero, val).astype(BF16)
        for piece in range(BRANCH_W // PROJ_COLS):
            c0 = piece * PROJ_COLS
            gate_buf[g_rows, c0:c0 + PROJ_COLS] = _dot(xg, w_ref[:, gate_col0 + c0:gate_col0 + c0 + PROJ_COLS])

    low128 = lax.broadcasted_iota(jnp.int32, (BLOCK, LANES), 1) < HEAD_DIM
    project_group(0, None)
    for g0 in range(0, n_blocks, ATTN_GROUP_BLOCKS):
        units = [(n, j) for n in range(g0, g0 + ATTN_GROUP_BLOCKS) for j in range(N_KV_HEADS)]

        for it, (n, j) in enumerate(units):
            r0 = n * BLOCK
            q_cols = j * PAIRS * LANES
            qs = jnp.concatenate(
                [q_buf[r0:r0 + BLOCK, q_cols + p * LANES:q_cols + (p + 1) * LANES] for p in range(PAIRS)], axis=0)
            k2 = jnp.concatenate([k_tab[j, 0, r0:r0 + 2 * BLOCK, :], k_tab[j, 1, r0:r0 + 2 * BLOCK, :]], axis=0)
            logits = lax.dot_general(qs, k2, (((1,), (1,)), ((), ())), preferred_element_type=F32)
            threshold = jnp.where(jnp.logical_and(i == 0, n == 0), 2, 1)
            for p in range(PAIRS):
                rows = slice(p * BLOCK, (p + 1) * BLOCK)
                row_max = []
                for par in range(2):
                    slabs = []
                    for half in range(2):
                        cols = slice((2 * par + half) * LANES, (2 * par + half + 1) * LANES)
                        s = logits[rows, cols] + bias_ref[j, rows, cols]
                        s = jnp.where(code_buf[rows, cols] >= threshold, s, NEG_INF)
                        s_buf[it, rows, cols] = s
                        slabs.append(s)
                    m = jnp.max(jnp.maximum(slabs[0], slabs[1]), axis=1, keepdims=True)
                    m = jnp.maximum(jnp.broadcast_to(m, (BLOCK, LANES)), sink_ref[j, par, rows, :])
                    m_buf[it, par, rows, :] = m
                    row_max.append(m)
                mc_buf[it, rows, :] = jnp.where(low128, row_max[0], row_max[1])

        if g0 + ATTN_GROUP_BLOCKS < n_blocks:
            project_group(g0 + ATTN_GROUP_BLOCKS, logits[0:SUBLANES, 0:LANES])

        for it, (n, j) in enumerate(units):
            for p in range(PAIRS):
                rows = slice(p * BLOCK, (p + 1) * BLOCK)
                for par in range(2):
                    m = m_buf[it, par, rows, :]
                    for half in range(2):
                        cols = slice((2 * par + half) * LANES, (2 * par + half + 1) * LANES)
                        p_buf[it, rows, cols] = jnp.exp(s_buf[it, rows, cols] - m).astype(BF16)

        for it, (n, j) in enumerate(units):
            r0 = n * BLOCK
            q_cols = j * PAIRS * LANES
            v2 = jnp.concatenate([v_tab[j, 0, r0:r0 + 2 * BLOCK, :], v_tab[j, 1, r0:r0 + 2 * BLOCK, :]], axis=0)
            o = _dot(p_buf[it], jnp.concatenate([v2, ones_buf[...]], axis=1))
            for p in range(PAIRS):
                rows = slice(p * BLOCK, (p + 1) * BLOCK)
                cols = slice(q_cols + p * LANES, q_cols + (p + 1) * LANES)
                den = o[rows, LANES:2 * LANES] + jnp.exp(sinkc_ref[j, rows, :] - mc_buf[it, rows, :])
                y = o[rows, 0:LANES] * (1.0 / den) * _silu(gate_buf[r0:r0 + BLOCK, cols])
                o_ref[r0:r0 + BLOCK, cols] = y.astype(BF16)

    for tab in (k_tab, v_tab):
        for j in range(N_KV_HEADS):
            for half in range(2):
                tab[j, half, 0:BLOCK, :] = tab[j, half, t_rows:t_rows + BLOCK, :]


CONV_COLUMNS = BRANCH_W // LANES
CONV_BLOCKS = BRANCH_W // PROJ_COLS


def _conv_stream_kernel(x_ref, w_ref, cw_ref, cb_ref, lng_ref, lnb_ref, zero_ref, o_ref,
                        xs_buf, glu_buf, gate_buf, shift_buf, y_buf):
    i = pl.program_id(0)
    t_rows = x_ref.shape[0]

    @pl.when(i == 0)
    def _():
        for c in range(CONV_COLUMNS):
            glu_buf[c, 0:CONV_HALO, :] = jnp.zeros((CONV_HALO, LANES), F32)

    xs_buf[...] = x_ref[...].astype(BF16)
    first = CONV_HALO - (CONV_K - 1)
    span = t_rows + CONV_HALO - SUBLANES

    def glu_block(n):
        gate_val = _dot(xs_buf[...], w_ref[CONV_BLOCKS + n])
        glu = _dot(xs_buf[...], w_ref[n]) * _sigmoid(gate_val)
        glu_buf[2 * n, CONV_HALO:CONV_HALO + t_rows, :] = glu[:, 0:LANES]
        glu_buf[2 * n + 1, CONV_HALO:CONV_HALO + t_rows, :] = glu[:, LANES:2 * LANES]
        return gate_val[t_rows - SUBLANES:t_rows, 0:LANES]

    def gate_block(n):
        gate = _dot(xs_buf[...], w_ref[2 * CONV_BLOCKS + n])
        gate_buf[2 * n] = gate[:, 0:LANES]
        gate_buf[2 * n + 1] = gate[:, LANES:2 * LANES]
        return gate[t_rows - SUBLANES:t_rows, 0:LANES]

    def conv_block(n, token, second_trigger):
        for half in range(2):
            c = 2 * n + half
            if half == 1:
                token = token + second_trigger()
            for r in range(1, SUBLANES):
                shift_buf[half, r, 0:span, :] = glu_buf[c, r:r + span, :]
            bias = jnp.broadcast_to(cb_ref[c], (STREAM_ROWS, LANES))
            for r0 in range(0, t_rows, STREAM_ROWS):
                nothing = pltpu.bitcast(pltpu.bitcast(token, jnp.int32) & zero_ref[...], F32)
                acc = bias
                for k in range(CONV_K):
                    a8 = (first + k) // SUBLANES * SUBLANES
                    r = (first + k) % SUBLANES
                    if r == 0:
                        src = glu_buf[c, r0 + a8:r0 + a8 + STREAM_ROWS, :]
                    else:
                        src = shift_buf[half, r, r0 + a8:r0 + a8 + STREAM_ROWS, :]
                    tap = jnp.concatenate([cw_ref[c, k] + nothing] * (STREAM_ROWS // SUBLANES), axis=0)
                    acc = acc + tap * src
                y_buf[c, r0:r0 + STREAM_ROWS, :] = acc

    for n in range(CONV_BLOCKS):
        token = glu_block(n)
        if n == 0:
            conv_block(n, token, lambda: token)
        else:
            conv_block(n, token, lambda n=n: gate_block(n - 1))
    gate_block(CONV_BLOCKS - 1)

    for r0 in range(0, t_rows, ROW_CHUNK):
        rows = slice(r0, r0 + ROW_CHUNK)
        ys = [y_buf[c, rows, :] for c in range(CONV_COLUMNS)]
        total = ys[0]
        for y in ys[1:]:
            total = total + y
        mu = jnp.sum(total, axis=1, keepdims=True) * (1.0 / BRANCH_W)
        ds = [y - mu for y in ys]
        sq = ds[0] * ds[0]
        for d in ds[1:]:
            sq = sq + d * d
        rstd = lax.rsqrt(jnp.sum(sq, axis=1, keepdims=True) * (1.0 / BRANCH_W) + LN_EPS)
        for c in range(CONV_COLUMNS):
            y = _silu(ds[c] * rstd * lng_ref[c] + lnb_ref[c]) * _silu(gate_buf[c, rows, :])
            o_ref[rows, c * LANES:(c + 1) * LANES] = y.astype(BF16)

    for c in range(CONV_COLUMNS):
        glu_buf[c, 0:CONV_HALO, :] = glu_buf[c, t_rows:t_rows + CONV_HALO, :]


def _merge_kernel(x_ref, ya_ref, yb_ref, yc_ref, yd_ref, wga_ref, wgb_ref, wgc_ref, wgd_ref, wb_ref, o_ref):
    xb = x_ref[...].astype(BF16)
    acc = None
    branches = ((ya_ref, wga_ref), (yb_ref, wgb_ref), (yc_ref, wgc_ref), (yd_ref, wgd_ref))
    for br, (y_ref, wg_ref) in enumerate(branches):
        gate = _sigmoid(_dot(xb, wg_ref[...]))
        term = gate * _dot(y_ref[...], wb_ref[br])
        acc = term if acc is None else acc + term
    o_ref[...] = acc.astype(BF16)


def _out_kernel(m_ref, w_ref, x_ref, g_ref, b_ref, zero_ref, o_ref, ob_ref, z_buf):
    t_rows = x_ref.shape[0]
    half_rows = t_rows // 2
    chunk = ROW_CHUNK // 2
    tokens = []
    for h in range(2):
        rows = slice(h * half_rows, (h + 1) * half_rows)
        z = _dot(m_ref[rows, :], w_ref[...])
        z_buf[rows, :] = z
        tokens.append(z[half_rows - SUBLANES:half_rows, 0:LANES])
    g = g_ref[...]
    b = b_ref[...]
    for h in range(2):
        nothing = pltpu.bitcast(pltpu.bitcast(tokens[h], jnp.int32) & zero_ref[...], F32)
        nothing = jnp.concatenate([nothing] * (chunk // SUBLANES), axis=0)
        nothing = jnp.concatenate([nothing] * (D_MODEL // LANES), axis=1)
        for r0 in range(h * half_rows, (h + 1) * half_rows, chunk):
            rows = slice(r0, r0 + chunk)
            y = _layer_norm_rows(ALPHA * x_ref[rows, :] + (z_buf[rows, :] + nothing), g, b)
            o_ref[rows, :] = y
            ob_ref[rows, :] = y.astype(BF16)


def _resident(shape):
    zeros = (0,) * len(shape)
    return pl.BlockSpec(shape, lambda *_: zeros, pipeline_mode=pl.Buffered(1))


def _nbytes(shape, dtype):
    return math.prod(d for d in shape if d is not None) * jnp.dtype(dtype).itemsize


def _call(kernel_fn, name, grid, inputs, outputs, scratch, value_bytes):
    arrays, in_specs = zip(*inputs)
    out_shape, out_specs = zip(*outputs)
    windows = [(spec, a.dtype) for a, spec in inputs] + [(spec, o.dtype) for o, spec in outputs]
    need = value_bytes + sum(_nbytes(v.shape, v.dtype) for v in scratch)
    for spec, dtype in windows:
        buffers = spec.pipeline_mode.buffer_count if spec.pipeline_mode is not None else 2
        need += buffers * _nbytes(spec.block_shape, dtype)
    assert need <= V7X_VMEM_BYTES, (name, need)
    return pl.pallas_call(
        kernel_fn,
        grid=grid,
        in_specs=list(in_specs),
        out_specs=list(out_specs) if len(out_specs) > 1 else out_specs[0],
        out_shape=list(out_shape) if len(out_shape) > 1 else out_shape[0],
        scratch_shapes=scratch,
        compiler_params=pltpu.CompilerParams(dimension_semantics=("arbitrary",) * len(grid),
                                             vmem_limit_bytes=need),
        name=name,
    )(*arrays)


def _row_tile(width, rows=SEQ_TILE):
    return pl.BlockSpec((rows, width), lambda i: (i, 0))


def _layer_weight(w_all, layer):
    idx = (layer,) + (0,) * (w_all.ndim - 1)
    return pl.BlockSpec((None,) + w_all.shape[1:], lambda *_: idx, pipeline_mode=pl.Buffered(1))


def _branch_call(kernel_fn, name, seq, x_in, w_all, layer, operands, scratch, value_tiles=2):
    value_bytes = value_tiles * _nbytes((SEQ_TILE, BRANCH_W), F32)
    inputs = [(x_in, _row_tile(D_MODEL)), (w_all, _layer_weight(w_all, layer))]
    inputs += [(op, _resident(op.shape)) for op in operands]
    outputs = [(jax.ShapeDtypeStruct((seq, BRANCH_W), BF16), _row_tile(BRANCH_W))]
    return _call(kernel_fn, name, (seq // SEQ_TILE,), inputs, outputs, scratch, value_bytes)


def _layer(x, x_in, band, w_parts, layer, pool_w, pool_scale, sgu_ln_g, sgu_ln_b, sgu_w, sgu_b, sinks,
           conv_w, conv_b, conv_ln_g, conv_ln_b, w_branch16, w_out16, ln_g, ln_b):
    seq = x.shape[0]
    t = SEQ_TILE
    w_pool, w_sgu, w_attn, w_gates, w_conv = w_parts
    row = lambda v: v.reshape(1, -1).astype(F32)

    sgu_bias = jnp.repeat(jnp.transpose(sgu_b).astype(F32), SGU_HD, axis=1)
    operands = [pool_w.astype(BF16), row(pool_scale), row(sgu_ln_g), row(sgu_ln_b), sgu_w.astype(F32), sgu_bias]
    branch_out = (jax.ShapeDtypeStruct((seq, BRANCH_W), BF16), _row_tile(BRANCH_W))
    y_a, y_b = _call(
        _pool_sgu_kernel, "branch_pool_sgu", (seq // t,),
        [(x_in, _row_tile(D_MODEL)), (w_pool, _layer_weight(w_pool, layer)), (w_sgu, _layer_weight(w_sgu, layer))]
        + [(op, _resident(op.shape)) for op in operands],
        [branch_out, branch_out],
        [pltpu.VMEM((t + POOL_HALO, BRANCH_W), F32), pltpu.VMEM((t, BRANCH_W), F32),
         pltpu.VMEM((t, BRANCH_W), BF16),
         pltpu.VMEM((t, BRANCH_W), F32), pltpu.VMEM((t, BRANCH_W), F32),
         pltpu.VMEM((t, BRANCH_W), F32), pltpu.VMEM((t, BRANCH_W), BF16)],
        value_bytes=4 * _nbytes((t, BRANCH_W), F32))

    sink_hp = sinks.astype(F32).reshape(N_KV_HEADS, PAIRS, 2)
    sink_rows = jnp.repeat(jnp.transpose(sink_hp, (0, 2, 1)), BLOCK, axis=2)
    sink_rows = jnp.broadcast_to(sink_rows[..., None], (N_KV_HEADS, 2, PQ, LANES))
    sink_lanes = jnp.repeat(jnp.repeat(sink_hp, BLOCK, axis=1), HEAD_DIM, axis=2)
    n_units = N_KV_HEADS * ATTN_GROUP_BLOCKS
    y_c = _branch_call(
        _attn_kernel, "branch_attn", seq, x_in, w_attn, layer,
        [band, sink_rows, sink_lanes, jnp.zeros((SUBLANES, LANES), jnp.int32)],
        [pltpu.VMEM((t, D_MODEL), BF16), pltpu.VMEM((t, BRANCH_W), BF16), pltpu.VMEM((t, BRANCH_W), F32),
         pltpu.VMEM((N_KV_HEADS, 2, t + BLOCK, LANES), BF16),
         pltpu.VMEM((N_KV_HEADS, 2, t + BLOCK, LANES), BF16),
         pltpu.VMEM((PQ, 4 * BLOCK), jnp.int32),
         pltpu.VMEM((4 * BLOCK, LANES), BF16),
         pltpu.VMEM((n_units, PQ, 4 * BLOCK), F32),
         pltpu.VMEM((n_units, PQ, 4 * BLOCK), BF16),
         pltpu.VMEM((n_units, 2, PQ, LANES), F32),
         pltpu.VMEM((n_units, PQ, LANES), F32)])

    per_col = lambda v: v.astype(F32).reshape(CONV_COLUMNS, 1, LANES)
    taps = jnp.transpose(conv_w.astype(F32).reshape(CONV_K, CONV_COLUMNS, LANES), (1, 0, 2))
    taps = jnp.broadcast_to(taps[:, :, None, :], (CONV_COLUMNS, CONV_K, SUBLANES, LANES))
    y_d = _branch_call(
        _conv_stream_kernel, "branch_conv", seq, x_in, w_conv, layer,
        [taps, per_col(conv_b), per_col(conv_ln_g), per_col(conv_ln_b), jnp.zeros((SUBLANES, LANES), jnp.int32)],
        [pltpu.VMEM((t, D_MODEL), BF16),
         pltpu.VMEM((CONV_COLUMNS, t + CONV_HALO, LANES), F32),
         pltpu.VMEM((CONV_COLUMNS, t, LANES), F32),
         pltpu.VMEM((2, SUBLANES, t + CONV_HALO - SUBLANES, LANES), F32),
         pltpu.VMEM((CONV_COLUMNS, t, LANES), F32)], value_tiles=4)

    n_cb = D_MODEL // MERGE_COLS
    tm = MERGE_TILE
    x_spec = pl.BlockSpec((tm, D_MODEL), lambda c, i: (i, 0))
    y_spec = pl.BlockSpec((tm, BRANCH_W), lambda c, i: (i, 0))
    wg_specs = [pl.BlockSpec((None, D_MODEL, MERGE_COLS),
                             functools.partial(lambda br, c, i: (layer, 0, br * n_cb + c), br))
                for br in range(N_BRANCH)]
    wb_spec = pl.BlockSpec((None, N_BRANCH, BRANCH_W, MERGE_COLS), lambda c, i: (layer, 0, 0, c))
    merged = _call(
        _merge_kernel, "merge", (n_cb, seq // tm),
        [(x_in, x_spec)] + [(y, y_spec) for y in (y_a, y_b, y_c, y_d)]
        + [(w_gates, spec) for spec in wg_specs] + [(w_branch16, wb_spec)],
        [(jax.ShapeDtypeStruct((seq, D_MODEL), BF16), pl.BlockSpec((tm, MERGE_COLS), lambda c, i: (i, c)))],
        [], value_bytes=4 * _nbytes((tm, MERGE_COLS), F32))

    x_next, x_next_bf = _call(
        _out_kernel, "out_norm", (seq // t,),
        [(merged, _row_tile(D_MODEL)), (w_out16, _layer_weight(w_out16, layer)), (x, _row_tile(D_MODEL)),
         (row(ln_g), _resident((1, D_MODEL))), (row(ln_b), _resident((1, D_MODEL))),
         (jnp.zeros((SUBLANES, LANES), jnp.int32), _resident((SUBLANES, LANES)))],
        [(jax.ShapeDtypeStruct((seq, D_MODEL), F32), _row_tile(D_MODEL)),
         (jax.ShapeDtypeStruct((seq, D_MODEL), BF16), _row_tile(D_MODEL))],
        [pltpu.VMEM((t, D_MODEL), F32)], value_bytes=_nbytes((t, D_MODEL), F32))
    return x_next, x_next_bf


def kernel(x, w_in, pool_w, pool_scale, sgu_ln_g, sgu_ln_b, sgu_w, sgu_b, attn_sinks, rel_bias, conv_w,
           conv_b, conv_ln_g, conv_ln_b, w_branch, w_out, ln_g, ln_b):
    batch, seq, d_model = x.shape
    assert d_model == D_MODEL and w_in.shape == (DEPTH, D_MODEL, D_IN)
    assert seq % SEQ_TILE == 0 and SEQ_TILE % (BLOCK * ATTN_GROUP_BLOCKS) == 0

    band = _band_bias(rel_bias).reshape(N_KV_HEADS, PAIRS, 2, BLOCK, 2 * BLOCK)
    band = jnp.transpose(band, (0, 1, 3, 2, 4)).reshape(N_KV_HEADS, PQ, 4 * BLOCK)

    w_parts = _cast_w_in(w_in.astype(F32))
    w_branch16 = w_branch.astype(BF16)
    w_out16 = w_out.astype(BF16)
    outs = []
    for bi in range(batch):
        xc = x[bi].astype(F32)
        x_in = xc
        for l in range(DEPTH):
            xc, x_in = _layer(xc, x_in, band, w_parts, l, pool_w[l], pool_scale[l], sgu_ln_g[l], sgu_ln_b[l],
                              sgu_w[l], sgu_b[l], attn_sinks[l], conv_w[l], conv_b[l], conv_ln_g[l],
                              conv_ln_b[l], w_branch16, w_out16, ln_g[l], ln_b[l])
        outs.append(xc.astype(x.dtype)[None])
    return outs[0] if batch == 1 else jnp.concatenate(outs, axis=0)
```

```python
import functools
import math

import jax
import jax.numpy as jnp
from jax import lax
from jax.experimental import pallas as pl
from jax.experimental.pallas import tpu as pltpu

F32 = jnp.float32
BF16 = jnp.bfloat16

D_MODEL = 2048
DEPTH = 2
N_BRANCH = 4
BRANCH_W = 1024
POOL_WINDOWS = (2, 4, 8, 16)
POOL_GW = BRANCH_W // len(POOL_WINDOWS)
SGU_HEADS = 8
SGU_HD = BRANCH_W // SGU_HEADS
CHUNK = 128
N_Q_HEADS = 16
N_KV_HEADS = 2
Q_PER_KV = N_Q_HEADS // N_KV_HEADS
HEAD_DIM = 64
KV_W = N_KV_HEADS * HEAD_DIM
WINDOW = 128
BLOCK = 128
NUM_BUCKETS = 32
MAX_DISTANCE = 128
CONV_K = 31
ALPHA = (2 * DEPTH) ** 0.25
LN_EPS = 1e-5
NEG_INF = -1e30

OFF_A = 0
OFF_B = OFF_A + 2 * BRANCH_W
OFF_C = OFF_B + 3 * BRANCH_W
OFF_D = OFF_C + 2 * BRANCH_W + 2 * KV_W
OFF_G = OFF_D + 3 * BRANCH_W
D_IN = OFF_G + N_BRANCH * D_MODEL

LANES = 128
SUBLANES = 8
SEQ_TILE = 512
POOL_HALO = 16
CONV_HALO = 32
ROW_CHUNK = 64
STREAM_ROWS = 512
PROJ_COLS = 256
MERGE_COLS = 512
PAIRS = Q_PER_KV // 2
PQ = PAIRS * BLOCK
ATTN_GROUP_BLOCKS = 2
MERGE_TILE = 512
V7X_VMEM_BYTES = 64 * 1024 * 1024


def _sigmoid(x):
    return 0.5 * jnp.tanh(0.5 * x) + 0.5


def _silu(x):
    return x * _sigmoid(x)


def _layer_norm_rows(v, g, b):
    mu = jnp.mean(v, axis=-1, keepdims=True)
    d = v - mu
    var = jnp.mean(d * d, axis=-1, keepdims=True)
    return d * lax.rsqrt(var + LN_EPS) * g + b


def _dot(a, b):
    return jnp.dot(a, b, preferred_element_type=F32)


def _t5_bucket(n):
    max_exact = NUM_BUCKETS // 2
    nf = jnp.maximum(n, 1).astype(F32)
    large = max_exact + (jnp.log(nf / max_exact) / math.log(MAX_DISTANCE / max_exact)
                         * (NUM_BUCKETS - max_exact)).astype(jnp.int32)
    large = jnp.minimum(large, NUM_BUCKETS - 1)
    return jnp.where(n < max_exact, n, large)


def _band_bucket_ids():
    i = jnp.arange(BLOCK)[:, None]
    j = jnp.arange(2 * BLOCK)[None, :]
    return _t5_bucket(jnp.clip(i + BLOCK - j, 0, WINDOW - 1)).astype(jnp.int32)


def _bias_kernel(bucket_ref, table_ref, o_ref):
    h = pl.program_id(0)
    bucket = bucket_ref[...]
    acc = jnp.zeros(bucket.shape, F32)
    for b in range(NUM_BUCKETS):
        acc = jnp.where(bucket == b, table_ref[b, h], acc)
    o_ref[0] = acc


def _band_bias(rel_bias):
    return pl.pallas_call(
        _bias_kernel,
        grid=(N_Q_HEADS,),
        in_specs=[pl.BlockSpec((BLOCK, 2 * BLOCK), lambda h: (0, 0)),
                  pl.BlockSpec(memory_space=pltpu.SMEM)],
        out_specs=pl.BlockSpec((1, BLOCK, 2 * BLOCK), lambda h: (h, 0, 0)),
        out_shape=jax.ShapeDtypeStruct((N_Q_HEADS, BLOCK, 2 * BLOCK), F32),
        name="band_bias",
    )(_band_bucket_ids(), rel_bias.astype(F32))


W_IN_SPLITS = ((OFF_A, OFF_B), (OFF_B, OFF_C), (OFF_C, OFF_D), (OFF_G, D_IN))
CONV_W_CHUNKS = (OFF_G - OFF_D) // PROJ_COLS
CAST_ROWS = 128


def _cast_w_in_kernel(w_ref, oa_ref, ob_ref, oc_ref, og_ref, od_ref):
    for (a, b), o_ref in zip(W_IN_SPLITS, (oa_ref, ob_ref, oc_ref, og_ref)):
        o_ref[...] = w_ref[:, a:b].astype(BF16)
    for j in range(CONV_W_CHUNKS):
        od_ref[j] = w_ref[:, OFF_D + j * PROJ_COLS:OFF_D + (j + 1) * PROJ_COLS].astype(BF16)


def _cast_w_in(w_in):
    widths = [b - a for a, b in W_IN_SPLITS]
    outputs = [(jax.ShapeDtypeStruct((DEPTH, D_MODEL, w), BF16),
                pl.BlockSpec((None, CAST_ROWS, w), lambda l, r: (l, r, 0))) for w in widths]
    outputs.append((jax.ShapeDtypeStruct((DEPTH, CONV_W_CHUNKS, D_MODEL, PROJ_COLS), BF16),
                    pl.BlockSpec((None, CONV_W_CHUNKS, CAST_ROWS, PROJ_COLS), lambda l, r: (l, 0, r, 0))))
    return _call(_cast_w_in_kernel, "cast_w_in", (DEPTH, D_MODEL // CAST_ROWS),
                 [(w_in, pl.BlockSpec((None, CAST_ROWS, D_IN), lambda l, r: (l, r, 0)))],
                 outputs, [], value_bytes=0)


def _pool_kernel(x_ref, w_ref, pw_ref, sc_ref, o_ref, abuf, gate_buf, mix_buf):
    i = pl.program_id(0)
    t_rows = x_ref.shape[0]

    @pl.when(i == 0)
    def _():
        abuf[0:POOL_HALO, :] = jnp.zeros((POOL_HALO, BRANCH_W), F32)

    xb = x_ref[...].astype(BF16)
    abuf[POOL_HALO:POOL_HALO + t_rows, :] = _dot(xb, w_ref[:, 0:BRANCH_W])
    gate_buf[...] = _dot(xb, w_ref[:, BRANCH_W:2 * BRANCH_W])

    for g, win in enumerate(POOL_WINDOWS):
        c0 = g * POOL_GW
        for r0 in range(0, t_rows, ROW_CHUNK):
            base = POOL_HALO + r0
            cur = abuf[base:base + ROW_CHUNK, c0:c0 + POOL_GW]
            acc = cur
            for s in range(1, win):
                acc = acc + abuf[base - s:base - s + ROW_CHUNK, c0:c0 + POOL_GW]
            t = i * t_rows + r0 + lax.broadcasted_iota(jnp.int32, (ROW_CHUNK, POOL_GW), 0)
            cnt = jnp.minimum(t + 1, win).astype(F32)
            mix_buf[r0:r0 + ROW_CHUNK, c0:c0 + POOL_GW] = (acc / cnt - cur).astype(BF16)

    for g in range(len(POOL_WINDOWS)):
        c0 = g * POOL_GW
        y = _dot(mix_buf[:, c0:c0 + POOL_GW], pw_ref[g])
        y = y * sc_ref[:, c0:c0 + POOL_GW] * _silu(gate_buf[:, c0:c0 + POOL_GW])
        o_ref[:, c0:c0 + POOL_GW] = y.astype(BF16)

    abuf[0:POOL_HALO, :] = abuf[t_rows:t_rows + POOL_HALO, :]


def _sgu_kernel(x_ref, w_ref, lng_ref, lnb_ref, sw_ref, sb_ref, o_ref,
                u_buf, v_buf, gate_buf, vn_buf):
    t_rows = x_ref.shape[0]
    xb = x_ref[...].astype(BF16)
    u_buf[...] = _dot(xb, w_ref[:, 0:BRANCH_W])
    v_buf[...] = _dot(xb, w_ref[:, BRANCH_W:2 * BRANCH_W])
    gate_buf[...] = _dot(xb, w_ref[:, 2 * BRANCH_W:3 * BRANCH_W])

    g = lng_ref[...]
    b = lnb_ref[...]
    for r0 in range(0, t_rows, ROW_CHUNK):
        vn = _layer_norm_rows(v_buf[r0:r0 + ROW_CHUNK, :], g, b)
        vn_buf[r0:r0 + ROW_CHUNK, :] = vn.astype(BF16)

    row = lax.broadcasted_iota(jnp.int32, (CHUNK, CHUNK), 0)
    col = lax.broadcasted_iota(jnp.int32, (CHUNK, CHUNK), 1)
    causal = row >= col
    for h in range(SGU_HEADS):
        c0 = h * SGU_HD
        w_h = jnp.where(causal, sw_ref[h], 0.0).astype(BF16)
        bias_h = sb_ref[:, c0:c0 + SGU_HD]
        for r0 in range(0, t_rows, CHUNK):
            sp = _dot(w_h, vn_buf[r0:r0 + CHUNK, c0:c0 + SGU_HD]) + bias_h
            y = u_buf[r0:r0 + CHUNK, c0:c0 + SGU_HD] * sp * _silu(gate_buf[r0:r0 + CHUNK, c0:c0 + SGU_HD])
            o_ref[r0:r0 + CHUNK, c0:c0 + SGU_HD] = y.astype(BF16)


def _pool_sgu_kernel(x_ref, wp_ref, ws_ref, pw_ref, sc_ref, lng_ref, lnb_ref, sw_ref, sb_ref, oa_ref, ob_ref,
                     abuf, pool_gate_buf, mix_buf, u_buf, v_buf, gate_buf, vn_buf):
    _pool_kernel(x_ref, wp_ref, pw_ref, sc_ref, oa_ref, abuf, pool_gate_buf, mix_buf)
    _sgu_kernel(x_ref, ws_ref, lng_ref, lnb_ref, sw_ref, sb_ref, ob_ref, u_buf, v_buf, gate_buf, vn_buf)


def _attn_kernel(x_ref, w_ref, bias_ref, sink_ref, sinkc_ref, zero_ref, o_ref,
                 xs_buf, q_buf, gate_buf, k_tab, v_tab, code_buf, ones_buf, s_buf, p_buf, m_buf, mc_buf):
    i = pl.program_id(0)
    t_rows = x_ref.shape[0]
    n_blocks = t_rows // BLOCK

    @pl.when(i == 0)
    def _():
        zero = jnp.zeros((BLOCK, LANES), BF16)
        for j in range(N_KV_HEADS):
            for half in range(2):
                k_tab[j, half, 0:BLOCK, :] = zero
                v_tab[j, half, 0:BLOCK, :] = zero
        row = lax.broadcasted_iota(jnp.int32, (PQ, 2 * 2 * BLOCK), 0)
        col = lax.broadcasted_iota(jnp.int32, (PQ, 2 * 2 * BLOCK), 1)
        key = col & (2 * BLOCK - 1)
        u = key - (row & (BLOCK - 1)) - 1
        in_window = (u >= 0) & (u < WINDOW)
        code_buf[...] = jnp.where(in_window, jnp.where(key >= BLOCK, 2, 1), 0)
        krow = lax.broadcasted_iota(jnp.int32, (4 * BLOCK, LANES), 0)
        klane = lax.broadcasted_iota(jnp.int32, (4 * BLOCK, LANES), 1)
        ones_buf[...] = jnp.where((krow >= 2 * BLOCK) == (klane >= HEAD_DIM), 1.0, 0.0).astype(BF16)

    g_len = ATTN_GROUP_BLOCKS * BLOCK
    xs_buf[...] = x_ref[...].astype(BF16)
    low = lax.broadcasted_iota(jnp.int32, (g_len, LANES), 1) < HEAD_DIM
    gate_col0 = BRANCH_W + 2 * KV_W

    def project_group(g0, after):
        g_rows = slice(g0 * BLOCK, g0 * BLOCK + g_len)
        xg = xs_buf[g_rows, :]
        if after is not None:
            nothing = pltpu.bitcast(after, jnp.int32) & zero_ref[...]
            nothing = jnp.concatenate([nothing] * (g_len // (2 * SUBLANES)), axis=0)
            nothing = jnp.concatenate([nothing] * (D_MODEL // LANES), axis=1)
            xg = pltpu.bitcast(pltpu.bitcast(xg, jnp.int32) | nothing, BF16)
        for piece in range(BRANCH_W // PROJ_COLS):
            c0 = piece * PROJ_COLS
            q = _dot(xg, w_ref[:, c0:c0 + PROJ_COLS]) * (HEAD_DIM ** -0.5)
            q_buf[g_rows, c0:c0 + PROJ_COLS] = q.astype(BF16)
        kv = _dot(xg, w_ref[:, BRANCH_W:BRANCH_W + 2 * KV_W])
        t_rows_g = slice(BLOCK + g0 * BLOCK, BLOCK + g0 * BLOCK + g_len)
        for tab, c0 in ((k_tab, 0), (v_tab, KV_W)):
            val = kv[:, c0:c0 + KV_W]
            swapped = pltpu.roll(val, HEAD_DIM, axis=1)
            zero = jnp.zeros_like(val)
            tab[0, 0, t_rows_g, :] = jnp.where(low, val, zero).astype(BF16)
            tab[0, 1, t_rows_g, :] = jnp.where(low, zero, swapped).astype(BF16)
            tab[1, 0, t_rows_g, :] = jnp.where(low, swapped, zero).astype(BF16)
            tab[1, 1, t_rows_g, :] = jnp.where(low, zero, val).astype(BF16)
        for piece in range(BRANCH_W // PROJ_COLS):
            c0 = piece * PROJ_COLS
            gate_buf[g_rows, c0:c0 + PROJ_COLS] = _dot(xg, w_ref[:, gate_col0 + c0:gate_col0 + c0 + PROJ_COLS])

    low128 = lax.broadcasted_iota(jnp.int32, (BLOCK, LANES), 1) < HEAD_DIM
    project_group(0, None)
    for g0 in range(0, n_blocks, ATTN_GROUP_BLOCKS):
        units = [(n, j) for n in range(g0, g0 + ATTN_GROUP_BLOCKS) for j in range(N_KV_HEADS)]

        for it, (n, j) in enumerate(units):
            r0 = n * BLOCK
            q_cols = j * PAIRS * LANES
            qs = jnp.concatenate(
                [q_buf[r0:r0 + BLOCK, q_cols + p * LANES:q_cols + (p + 1) * LANES] for p in range(PAIRS)], axis=0)
            k2 = jnp.concatenate([k_tab[j, 0, r0:r0 + 2 * BLOCK, :], k_tab[j, 1, r0:r0 + 2 * BLOCK, :]], axis=0)
            logits = lax.dot_general(qs, k2, (((1,), (1,)), ((), ())), preferred_element_type=F32)
            threshold = jnp.where(jnp.logical_and(i == 0, n == 0), 2, 1)
            for p in range(PAIRS):
                rows = slice(p * BLOCK, (p + 1) * BLOCK)
                row_max = []
                for par in range(2):
                    slabs = []
                    for half in range(2):
                        cols = slice((2 * par + half) * LANES, (2 * par + half + 1) * LANES)
                        s = logits[rows, cols] + bias_ref[j, rows, cols]
                        s = jnp.where(code_buf[rows, cols] >= threshold, s, NEG_INF)
                        s_buf[it, rows, cols] = s
                        slabs.append(s)
                    m = jnp.max(jnp.maximum(slabs[0], slabs[1]), axis=1, keepdims=True)
                    m = jnp.maximum(jnp.broadcast_to(m, (BLOCK, LANES)), sink_ref[j, par, rows, :])
                    m_buf[it, par, rows, :] = m
                    row_max.append(m)
                mc_buf[it, rows, :] = jnp.where(low128, row_max[0], row_max[1])

        if g0 + ATTN_GROUP_BLOCKS < n_blocks:
            project_group(g0 + ATTN_GROUP_BLOCKS, logits[0:SUBLANES, 0:LANES])

        for it, (n, j) in enumerate(units):
            for p in range(PAIRS):
                rows = slice(p * BLOCK, (p + 1) * BLOCK)
                for par in range(2):
                    m = m_buf[it, par, rows, :]
                    for half in range(2):
                        cols = slice((2 * par + half) * LANES, (2 * par + half + 1) * LANES)
                        p_buf[it, rows, cols] = jnp.exp(s_buf[it, rows, cols] - m).astype(BF16)

        for it, (n, j) in enumerate(units):
            r0 = n * BLOCK
            q_cols = j * PAIRS * LANES
            v2 = jnp.concatenate([v_tab[j, 0, r0:r0 + 2 * BLOCK, :], v_tab[j, 1, r0:r0 + 2 * BLOCK, :]], axis=0)
            o = _dot(p_buf[it], jnp.concatenate([v2, ones_buf[...]], axis=1))
            for p in range(PAIRS):
                rows = slice(p * BLOCK, (p + 1) * BLOCK)
                cols = slice(q_cols + p * LANES, q_cols + (p + 1) * LANES)
                den = o[rows, LANES:2 * LANES] + jnp.exp(sinkc_ref[j, rows, :] - mc_buf[it, rows, :])
                y = o[rows, 0:LANES] * (1.0 / den) * _silu(gate_buf[r0:r0 + BLOCK, cols])
                o_ref[r0:r0 + BLOCK, cols] = y.astype(BF16)

    for tab in (k_tab, v_tab):
        for j in range(N_KV_HEADS):
            for half in range(2):
                tab[j, half, 0:BLOCK, :] = tab[j, half, t_rows:t_rows + BLOCK, :]


CONV_COLUMNS = BRANCH_W // LANES
CONV_BLOCKS = BRANCH_W // PROJ_COLS


def _conv_stream_kernel(x_ref, w_ref, cw_ref, cb_ref, lng_ref, lnb_ref, zero_ref, o_ref,
                        xs_buf, glu_buf, gate_buf, shift_buf, y_buf):
    i = pl.program_id(0)
    t_rows = x_ref.shape[0]

    @pl.when(i == 0)
    def _():
        for c in range(CONV_COLUMNS):
            glu_buf[c, 0:CONV_HALO, :] = jnp.zeros((CONV_HALO, LANES), F32)

    xs_buf[...] = x_ref[...].astype(BF16)
    first = CONV_HALO - (CONV_K - 1)
    span = t_rows + CONV_HALO - SUBLANES

    def glu_block(n):
        gate_val = _dot(xs_buf[...], w_ref[CONV_BLOCKS + n])
        glu = _dot(xs_buf[...], w_ref[n]) * _sigmoid(gate_val)
        glu_buf[2 * n, CONV_HALO:CONV_HALO + t_rows, :] = glu[:, 0:LANES]
        glu_buf[2 * n + 1, CONV_HALO:CONV_HALO + t_rows, :] = glu[:, LANES:2 * LANES]
        return gate_val[t_rows - SUBLANES:t_rows, 0:LANES]

    def gate_block(n):
        gate = _dot(xs_buf[...], w_ref[2 * CONV_BLOCKS + n])
        gate_buf[2 * n] = gate[:, 0:LANES]
        gate_buf[2 * n + 1] = gate[:, LANES:2 * LANES]
        return gate[t_rows - SUBLANES:t_rows, 0:LANES]

    def conv_block(n, token, second_trigger):
        for half in range(2):
            c = 2 * n + half
            if half == 1:
                token = token + second_trigger()
            for r in range(1, SUBLANES):
                shift_buf[half, r, 0:span, :] = glu_buf[c, r:r + span, :]
            bias = jnp.broadcast_to(cb_ref[c], (STREAM_ROWS, LANES))
            for r0 in range(0, t_rows, STREAM_ROWS):
                nothing = pltpu.bitcast(pltpu.bitcast(token, jnp.int32) & zero_ref[...], F32)
                acc = bias
                for k in range(CONV_K):
                    a8 = (first + k) // SUBLANES * SUBLANES
                    r = (first + k) % SUBLANES
                    if r == 0:
                        src = glu_buf[c, r0 + a8:r0 + a8 + STREAM_ROWS, :]
                    else:
                        src = shift_buf[half, r, r0 + a8:r0 + a8 + STREAM_ROWS, :]
                    tap = jnp.concatenate([cw_ref[c, k] + nothing] * (STREAM_ROWS // SUBLANES), axis=0)
                    acc = acc + tap * src
                y_buf[c, r0:r0 + STREAM_ROWS, :] = acc

    for n in range(CONV_BLOCKS):
        token = glu_block(n)
        if n == 0:
            conv_block(n, token, lambda: token)
        else:
            conv_block(n, token, lambda n=n: gate_block(n - 1))
    gate_block(CONV_BLOCKS - 1)

    for r0 in range(0, t_rows, ROW_CHUNK):
        rows = slice(r0, r0 + ROW_CHUNK)
        ys = [y_buf[c, rows, :] for c in range(CONV_COLUMNS)]
        total = ys[0]
        for y in ys[1:]:
            total = total + y
        mu = jnp.sum(total, axis=1, keepdims=True) * (1.0 / BRANCH_W)
        ds = [y - mu for y in ys]
        sq = ds[0] * ds[0]
        for d in ds[1:]:
            sq = sq + d * d
        rstd = lax.rsqrt(jnp.sum(sq, axis=1, keepdims=True) * (1.0 / BRANCH_W) + LN_EPS)
        for c in range(CONV_COLUMNS):
            y = _silu(ds[c] * rstd * lng_ref[c] + lnb_ref[c]) * _silu(gate_buf[c, rows, :])
            o_ref[rows, c * LANES:(c + 1) * LANES] = y.astype(BF16)

    for c in range(CONV_COLUMNS):
        glu_buf[c, 0:CONV_HALO, :] = glu_buf[c, t_rows:t_rows + CONV_HALO, :]


def _merge_kernel(x_ref, ya_ref, yb_ref, yc_ref, yd_ref, wga_ref, wgb_ref, wgc_ref, wgd_ref, wb_ref, o_ref):
    xb = x_ref[...].astype(BF16)
    acc = None
    branches = ((ya_ref, wga_ref), (yb_ref, wgb_ref), (yc_ref, wgc_ref), (yd_ref, wgd_ref))
    for br, (y_ref, wg_ref) in enumerate(branches):
        gate = _sigmoid(_dot(xb, wg_ref[...]))
        term = gate * _dot(y_ref[...], wb_ref[br])
        acc = term if acc is None else acc + term
    o_ref[...] = acc.astype(BF16)


def _out_kernel(emit_bf16, m_ref, w_ref, x_ref, g_ref, b_ref, zero_ref, o_ref, *rest):
    z_buf = rest[-1]
    t_rows = x_ref.shape[0]
    half_rows = t_rows // 2
    chunk = ROW_CHUNK // 2
    tokens = []
    for h in range(2):
        rows = slice(h * half_rows, (h + 1) * half_rows)
        z = _dot(m_ref[rows, :], w_ref[...])
        z_buf[rows, :] = z
        tokens.append(z[half_rows - SUBLANES:half_rows, 0:LANES])
    g = g_ref[...]
    b = b_ref[...]
    for h in range(2):
        nothing = pltpu.bitcast(pltpu.bitcast(tokens[h], jnp.int32) & zero_ref[...], F32)
        nothing = jnp.concatenate([nothing] * (chunk // SUBLANES), axis=0)
        nothing = jnp.concatenate([nothing] * (D_MODEL // LANES), axis=1)
        for r0 in range(h * half_rows, (h + 1) * half_rows, chunk):
            rows = slice(r0, r0 + chunk)
            y = _layer_norm_rows(ALPHA * x_ref[rows, :] + (z_buf[rows, :] + nothing), g, b)
            o_ref[rows, :] = y
            if emit_bf16:
                rest[0][rows, :] = y.astype(BF16)


def _resident(shape):
    zeros = (0,) * len(shape)
    return pl.BlockSpec(shape, lambda *_: zeros, pipeline_mode=pl.Buffered(1))


def _nbytes(shape, dtype):
    return math.prod(d for d in shape if d is not None) * jnp.dtype(dtype).itemsize


def _call(kernel_fn, name, grid, inputs, outputs, scratch, value_bytes):
    arrays, in_specs = zip(*inputs)
    out_shape, out_specs = zip(*outputs)
    windows = [(spec, a.dtype) for a, spec in inputs] + [(spec, o.dtype) for o, spec in outputs]
    need = value_bytes + sum(_nbytes(v.shape, v.dtype) for v in scratch)
    for spec, dtype in windows:
        buffers = spec.pipeline_mode.buffer_count if spec.pipeline_mode is not None else 2
        need += buffers * _nbytes(spec.block_shape, dtype)
    assert need <= V7X_VMEM_BYTES, (name, need)
    return pl.pallas_call(
        kernel_fn,
        grid=grid,
        in_specs=list(in_specs),
        out_specs=list(out_specs) if len(out_specs) > 1 else out_specs[0],
        out_shape=list(out_shape) if len(out_shape) > 1 else out_shape[0],
        scratch_shapes=scratch,
        compiler_params=pltpu.CompilerParams(dimension_semantics=("arbitrary",) * len(grid),
                                             vmem_limit_bytes=need),
        name=name,
    )(*arrays)


def _row_tile(width, rows=SEQ_TILE):
    return pl.BlockSpec((rows, width), lambda i: (i, 0))


def _layer_weight(w_all, layer):
    idx = (layer,) + (0,) * (w_all.ndim - 1)
    return pl.BlockSpec((None,) + w_all.shape[1:], lambda *_: idx, pipeline_mode=pl.Buffered(1))


def _branch_call(kernel_fn, name, seq, x_in, w_all, layer, operands, scratch, value_tiles=2):
    value_bytes = value_tiles * _nbytes((SEQ_TILE, BRANCH_W), F32)
    inputs = [(x_in, _row_tile(D_MODEL)), (w_all, _layer_weight(w_all, layer))]
    inputs += [(op, _resident(op.shape)) for op in operands]
    outputs = [(jax.ShapeDtypeStruct((seq, BRANCH_W), BF16), _row_tile(BRANCH_W))]
    return _call(kernel_fn, name, (seq // SEQ_TILE,), inputs, outputs, scratch, value_bytes)


def _layer(x, x_in, band, w_parts, layer, pool_w, pool_scale, sgu_ln_g, sgu_ln_b, sgu_w, sgu_b, sinks,
           conv_w, conv_b, conv_ln_g, conv_ln_b, w_branch16, w_out16, ln_g, ln_b):
    seq = x.shape[0]
    t = SEQ_TILE
    w_pool, w_sgu, w_attn, w_gates, w_conv = w_parts
    row = lambda v: v.reshape(1, -1).astype(F32)

    sgu_bias = jnp.repeat(jnp.transpose(sgu_b).astype(F32), SGU_HD, axis=1)
    operands = [pool_w.astype(BF16), row(pool_scale), row(sgu_ln_g), row(sgu_ln_b), sgu_w.astype(F32), sgu_bias]
    branch_out = (jax.ShapeDtypeStruct((seq, BRANCH_W), BF16), _row_tile(BRANCH_W))
    y_a, y_b = _call(
        _pool_sgu_kernel, "branch_pool_sgu", (seq // t,),
        [(x_in, _row_tile(D_MODEL)), (w_pool, _layer_weight(w_pool, layer)), (w_sgu, _layer_weight(w_sgu, layer))]
        + [(op, _resident(op.shape)) for op in operands],
        [branch_out, branch_out],
        [pltpu.VMEM((t + POOL_HALO, BRANCH_W), F32), pltpu.VMEM((t, BRANCH_W), F32),
         pltpu.VMEM((t, BRANCH_W), BF16),
         pltpu.VMEM((t, BRANCH_W), F32), pltpu.VMEM((t, BRANCH_W), F32),
         pltpu.VMEM((t, BRANCH_W), F32), pltpu.VMEM((t, BRANCH_W), BF16)],
        value_bytes=4 * _nbytes((t, BRANCH_W), F32))

    sink_hp = sinks.astype(F32).reshape(N_KV_HEADS, PAIRS, 2)
    sink_rows = jnp.repeat(jnp.transpose(sink_hp, (0, 2, 1)), BLOCK, axis=2)
    sink_rows = jnp.broadcast_to(sink_rows[..., None], (N_KV_HEADS, 2, PQ, LANES))
    sink_lanes = jnp.repeat(jnp.repeat(sink_hp, BLOCK, axis=1), HEAD_DIM, axis=2)
    n_units = N_KV_HEADS * ATTN_GROUP_BLOCKS
    y_c = _branch_call(
        _attn_kernel, "branch_attn", seq, x_in, w_attn, layer,
        [band, sink_rows, sink_lanes, jnp.zeros((SUBLANES, LANES), jnp.int32)],
        [pltpu.VMEM((t, D_MODEL), BF16), pltpu.VMEM((t, BRANCH_W), BF16), pltpu.VMEM((t, BRANCH_W), F32),
         pltpu.VMEM((N_KV_HEADS, 2, t + BLOCK, LANES), BF16),
         pltpu.VMEM((N_KV_HEADS, 2, t + BLOCK, LANES), BF16),
         pltpu.VMEM((PQ, 4 * BLOCK), jnp.int32),
         pltpu.VMEM((4 * BLOCK, LANES), BF16),
         pltpu.VMEM((n_units, PQ, 4 * BLOCK), F32),
         pltpu.VMEM((n_units, PQ, 4 * BLOCK), BF16),
         pltpu.VMEM((n_units, 2, PQ, LANES), F32),
         pltpu.VMEM((n_units, PQ, LANES), F32)])

    per_col = lambda v: v.astype(F32).reshape(CONV_COLUMNS, 1, LANES)
    taps = jnp.transpose(conv_w.astype(F32).reshape(CONV_K, CONV_COLUMNS, LANES), (1, 0, 2))
    taps = jnp.broadcast_to(taps[:, :, None, :], (CONV_COLUMNS, CONV_K, SUBLANES, LANES))
    y_d = _branch_call(
        _conv_stream_kernel, "branch_conv", seq, x_in, w_conv, layer,
        [taps, per_col(conv_b), per_col(conv_ln_g), per_col(conv_ln_b), jnp.zeros((SUBLANES, LANES), jnp.int32)],
        [pltpu.VMEM((t, D_MODEL), BF16),
         pltpu.VMEM((CONV_COLUMNS, t + CONV_HALO, LANES), F32),
         pltpu.VMEM((CONV_COLUMNS, t, LANES), F32),
         pltpu.VMEM((2, SUBLANES, t + CONV_HALO - SUBLANES, LANES), F32),
         pltpu.VMEM((CONV_COLUMNS, t, LANES), F32)], value_tiles=4)

    n_cb = D_MODEL // MERGE_COLS
    tm = MERGE_TILE
    x_spec = pl.BlockSpec((tm, D_MODEL), lambda c, i: (i, 0))
    y_spec = pl.BlockSpec((tm, BRANCH_W), lambda c, i: (i, 0))
    wg_specs = [pl.BlockSpec((None, D_MODEL, MERGE_COLS),
                             functools.partial(lambda br, c, i: (layer, 0, br * n_cb + c), br))
                for br in range(N_BRANCH)]
    wb_spec = pl.BlockSpec((None, N_BRANCH, BRANCH_W, MERGE_COLS), lambda c, i: (layer, 0, 0, c))
    merged = _call(
        _merge_kernel, "merge", (n_cb, seq // tm),
        [(x_in, x_spec)] + [(y, y_spec) for y in (y_a, y_b, y_c, y_d)]
        + [(w_gates, spec) for spec in wg_specs] + [(w_branch16, wb_spec)],
        [(jax.ShapeDtypeStruct((seq, D_MODEL), BF16), pl.BlockSpec((tm, MERGE_COLS), lambda c, i: (i, c)))],
        [], value_bytes=4 * _nbytes((tm, MERGE_COLS), F32))

    emit_bf16 = layer + 1 < DEPTH
    outs = _call(
        functools.partial(_out_kernel, emit_bf16), "out_norm", (seq // t,),
        [(merged, _row_tile(D_MODEL)), (w_out16, _layer_weight(w_out16, layer)), (x, _row_tile(D_MODEL)),
         (row(ln_g), _resident((1, D_MODEL))), (row(ln_b), _resident((1, D_MODEL))),
         (jnp.zeros((SUBLANES, LANES), jnp.int32), _resident((SUBLANES, LANES)))],
        [(jax.ShapeDtypeStruct((seq, D_MODEL), F32), _row_tile(D_MODEL))]
        + [(jax.ShapeDtypeStruct((seq, D_MODEL), BF16), _row_tile(D_MODEL))] * emit_bf16,
        [pltpu.VMEM((t, D_MODEL), F32)], value_bytes=_nbytes((t, D_MODEL), F32))
    return (outs[0], outs[1]) if emit_bf16 else (outs, None)


def kernel(x, w_in, pool_w, pool_scale, sgu_ln_g, sgu_ln_b, sgu_w, sgu_b, attn_sinks, rel_bias, conv_w,
           conv_b, conv_ln_g, conv_ln_b, w_branch, w_out, ln_g, ln_b):
    batch, seq, d_model = x.shape
    assert d_model == D_MODEL and w_in.shape == (DEPTH, D_MODEL, D_IN)
    assert seq % SEQ_TILE == 0 and SEQ_TILE % (BLOCK * ATTN_GROUP_BLOCKS) == 0

    band = _band_bias(rel_bias).reshape(N_KV_HEADS, PAIRS, 2, BLOCK, 2 * BLOCK)
    band = jnp.transpose(band, (0, 1, 3, 2, 4)).reshape(N_KV_HEADS, PQ, 4 * BLOCK)

    w_parts = _cast_w_in(w_in.astype(F32))
    w_branch16 = w_branch.astype(BF16)
    w_out16 = w_out.astype(BF16)
    outs = []
    for bi in range(batch):
        xc = x[bi].astype(F32)
        x_in = xc
        for l in range(DEPTH):
            xc, x_in = _layer(xc, x_in, band, w_parts, l, pool_w[l], pool_scale[l], sgu_ln_g[l], sgu_ln_b[l],
                              sgu_w[l], sgu_b[l], attn_sinks[l], conv_w[l], conv_b[l], conv_ln_g[l],
                              conv_ln_b[l], w_branch16, w_out16, ln_g[l], ln_b[l])
        outs.append(xc.astype(x.dtype)[None])
    return outs[0] if batch == 1 else jnp.concatenate(outs, axis=0)
```
